```python
import jax, jax.numpy as jnp
from jax import lax
import numpy as np

D_MODEL = 1024
BATCH = 8
SEQ = 2048
DEPTH = 4
DEC_BATCH = 8
DEC_SEQ = 64
PAST_LEN = 1024

CHUNK = 64
N_MIXERS = 3
N_A = (DEPTH + 2) // 3
N_B = (DEPTH + 1) // 3
N_C = DEPTH // 3
NORM_EPS = 1e-6
ROPE_THETA = 500000.0
Q_BLOCK = 128

A_HEADS = 16
A_HEAD_DIM = 64
A_KV_HEADS = 2
A_ROT = A_HEAD_DIM // 4
IDX_HEADS = 8
IDX_DIM = 64
IDX_ROT = IDX_DIM // 4
TOPK_MAX = 256
A_O_Q = A_HEADS * A_HEAD_DIM
A_O_K = A_O_Q + A_KV_HEADS * A_HEAD_DIM
A_O_V = A_O_K + A_KV_HEADS * A_HEAD_DIM
A_O_QI = A_O_V + IDX_HEADS * IDX_DIM
A_O_KI = A_O_QI + IDX_DIM
A_IN = A_O_KI + IDX_HEADS

B_HEAD_DIM = 64
B_HEADS = D_MODEL // B_HEAD_DIM
B_DECAY_LORA = 64
B_ICL_LORA = 64
B_GATE_LORA = 128
B_GN_EPS = 64e-5

C_HEADS = 16
C_NOPE = 64
C_ROPE = 32
C_V = 64
C_Q_RANK = 512
C_KV_RANK = 256
C_IN = C_Q_RANK + C_KV_RANK + C_ROPE

D_FF = 2816
CONV_W = 3

kernel_name = 'hybrid_dsa_rwkv7_mla_convffn_step'

F32 = jnp.float32


def _rmsnorm(x, g):
    xf = x.astype(F32)
    xf = xf * lax.rsqrt(jnp.mean(xf * xf, axis=-1, keepdims=True) + NORM_EPS)
    return xf.astype(x.dtype) * g


def _rope(x, pos, rot):
    half = rot // 2
    inv = ROPE_THETA ** (-jnp.arange(half, dtype=F32) / half)
    ang = pos.astype(F32)[:, None] * inv[None, :]
    shape = (ang.shape[0],) + (1,) * (x.ndim - 3) + (half,)
    cos = jnp.cos(ang).reshape(shape).astype(x.dtype)
    sin = jnp.sin(ang).reshape(shape).astype(x.dtype)
    x1 = x[..., :half]
    x2 = x[..., half:rot]
    return jnp.concatenate([x1 * cos - x2 * sin, x1 * sin + x2 * cos, x[..., rot:]], axis=-1)


def _query_blocks(fn, qs, pos):
    T = pos.shape[0]
    qb = Q_BLOCK if T % Q_BLOCK == 0 else T
    nb = T // qb
    split = lambda a: jnp.moveaxis(a.reshape((a.shape[0], nb, qb) + a.shape[2:]), 1, 0)
    out = lax.map(lambda args: fn(*args), tuple(split(a) for a in qs) + (pos.reshape(nb, qb),))
    out = jnp.moveaxis(out, 0, 1)
    return out.reshape((out.shape[0], T) + out.shape[3:])


def _chunk_limit(pos_q):
    return (pos_q // CHUNK + 1) * CHUNK


def _dsa_mixer(h, pos, past_k, past_v, past_ki, w_in, w_out):
    Bn, T, _ = h.shape
    q, k, v, qi, ki, wi = jnp.split(h @ w_in, [A_O_Q, A_O_K, A_O_V, A_O_QI, A_O_KI], axis=-1)
    q = _rope(q.reshape(Bn, T, A_HEADS, A_HEAD_DIM), pos, A_ROT)
    k = _rope(k.reshape(Bn, T, A_KV_HEADS, A_HEAD_DIM), pos, A_ROT)
    v = v.reshape(Bn, T, A_KV_HEADS, A_HEAD_DIM)
    qi = _rope(qi.reshape(Bn, T, IDX_HEADS, IDX_DIM), pos, IDX_ROT)
    ki = _rope(ki, pos, IDX_ROT)
    k_all = jnp.concatenate([past_k, k], axis=1)
    v_all = jnp.concatenate([past_v, v], axis=1)
    ki_all = jnp.concatenate([past_ki, ki], axis=1).astype(F32)
    L = k_all.shape[1]
    top = min(TOPK_MAX, L // 4)
    group = A_HEADS // A_KV_HEADS
    scale = A_HEAD_DIM ** -0.5
    iscale = (IDX_HEADS * IDX_DIM) ** -0.5
    key_pos = jnp.arange(L)
    gather = jax.vmap(lambda rows, idx: rows[idx])

    def block(qb, qib, wib, pb):
        n = pb.shape[0]
        dots = jax.nn.relu(jnp.einsum('bthd,bsd->bths', qib.astype(F32), ki_all))
        score = jnp.einsum('bth,bths->bts', wib.astype(F32), dots) * iscale
        limit = _chunk_limit(pb)
        score = jnp.where((key_pos[None, :] < limit[:, None])[None], score, -jnp.inf)
        _, sel = lax.top_k(score, top)
        valid = sel < limit[None, :, None]
        kg = gather(k_all, sel)
        vg = gather(v_all, sel)
        qg = qb.reshape(Bn, n, A_KV_HEADS, group, A_HEAD_DIM)
        logits = jnp.einsum('btkgd,btskd->btkgs', qg, kg).astype(F32) * scale
        logits = jnp.where(valid[:, :, None, None, :], logits, -jnp.inf)
        prob = jax.nn.softmax(logits, axis=-1).astype(vg.dtype)
        o = jnp.einsum('btkgs,btskd->btkgd', prob, vg)
        return o.reshape(Bn, n, A_HEADS * A_HEAD_DIM)

    o = _query_blocks(block, (q, qi, wi), pos)
    return o @ w_out, k, v, ki


def _wkv7_scan(S0, r, decay, k, v, a_vec, b_vec):
    def step(S, inp):
        rt, wt, kt, vt, at, bt = inp
        sa = jnp.einsum('bhij,bhj->bhi', S, at)
        S = S * wt[:, :, None, :] + sa[..., None] * bt[:, :, None, :] + vt[..., None] * kt[:, :, None, :]
        return S, jnp.einsum('bhij,bhj->bhi', S, rt)
    xs = tuple(jnp.moveaxis(t, 1, 0) for t in (r, decay, k, v, a_vec, b_vec))
    S, ys = lax.scan(step, S0, xs)
    return jnp.moveaxis(ys, 0, 1), S


def _rwkv7_mixer(h, shift_prev, S0, mu, w_rkv, w0, w1, w2, a0, a1, a2, g1, g2, k_k, k_a, r_k, ln_w, ln_b, w_out):
    Bn, T, D = h.shape
    h_prev = jnp.concatenate([shift_prev[:, None, :], h[:, :-1]], axis=1)
    xx = h_prev - h
    xm = h[:, :, None, :] + xx[:, :, None, :] * mu
    xr, xw, xk, xv, xa, xg = [xm[:, :, i] for i in range(6)]
    r = xr @ w_rkv[0]
    k = xk @ w_rkv[1]
    v = xv @ w_rkv[2]
    w_log = -jax.nn.softplus(-(w0 + jnp.tanh(xw @ w1) @ w2)) - 0.5
    decay = jnp.exp(-jnp.exp(w_log.astype(F32)))
    a = jax.nn.sigmoid(a0 + (xa @ a1) @ a2)
    g = jax.nn.sigmoid(xg @ g1) @ g2
    hs = lambda t: t.reshape(Bn, T, B_HEADS, B_HEAD_DIM)
    kk = hs(k * k_k).astype(F32)
    kk = kk / jnp.maximum(jnp.sqrt(jnp.sum(kk * kk, axis=-1, keepdims=True)), 1e-12)
    k = k * (1 + (a - 1) * k_a)
    a_h = hs(a).astype(F32)
    y, S = _wkv7_scan(S0.astype(F32), hs(r).astype(F32), hs(decay), hs(k).astype(F32),
                      hs(v).astype(F32), -kk, kk * a_h)
    mean = jnp.mean(y, axis=-1, keepdims=True)
    var = jnp.mean(jnp.square(y - mean), axis=-1, keepdims=True)
    y = ((y - mean) * lax.rsqrt(var + B_GN_EPS)).reshape(Bn, T, D).astype(h.dtype) * ln_w + ln_b
    bonus = jnp.sum(hs(r) * hs(k) * r_k, axis=-1, keepdims=True) * hs(v)
    y = y + bonus.reshape(Bn, T, D)
    return (y * g) @ w_out, h[:, -1], S.astype(h.dtype)


def _mla_mixer(h, pos, past_lat, past_rope, w_in, g_q, g_kv, w_uq, w_ukv, w_out):
    Bn, T, _ = h.shape
    cq, ckv, kpe = jnp.split(h @ w_in, [C_Q_RANK, C_Q_RANK + C_KV_RANK], axis=-1)
    q = (_rmsnorm(cq, g_q) @ w_uq).reshape(Bn, T, C_HEADS, C_NOPE + C_ROPE)
    q_nope = q[..., :C_NOPE]
    q_pe = _rope(q[..., C_NOPE:], pos, C_ROPE)
    lat = _rmsnorm(ckv, g_kv)
    kpe = _rope(kpe, pos, C_ROPE)
    lat_all = jnp.concatenate([past_lat, lat], axis=1)
    kpe_all = jnp.concatenate([past_rope, kpe], axis=1)
    L = lat_all.shape[1]
    kv = (lat_all @ w_ukv).reshape(Bn, L, C_HEADS, C_NOPE + C_V)
    k_nope = kv[..., :C_NOPE]
    v = kv[..., C_NOPE:]
    scale = (C_NOPE + C_ROPE) ** -0.5
    key_pos = jnp.arange(L)

    def block(qn, qp, pb):
        logits = (jnp.einsum('bthd,bshd->bhts', qn, k_nope)
                  + jnp.einsum('bthr,bsr->bhts', qp, kpe_all)).astype(F32) * scale
        adm = key_pos[None, :] < _chunk_limit(pb)[:, None]
        logits = jnp.where(adm[None, None], logits, -jnp.inf)
        prob = jax.nn.softmax(logits, axis=-1).astype(v.dtype)
        return jnp.einsum('bhts,bshd->bthd', prob, v).reshape(Bn, pb.shape[0], C_HEADS * C_V)

    o = _query_blocks(block, (q_nope, q_pe), pos)
    return o @ w_out, lat, kpe


def _conv_ffn(h, conv_prev, w_up, w_conv, b_conv, w_down):
    T = h.shape[1]
    u = h @ w_up
    u_pad = jnp.concatenate([conv_prev, u], axis=1)
    c = b_conv + sum(u_pad[:, j:j + T] * w_conv[j] for j in range(CONV_W))
    gate, val = jnp.split(c, 2, axis=-1)
    return (jax.nn.silu(gate) * val) @ w_down, u_pad[:, T:]


def _trunk(x, pos, st, w):
    new = {name: [] for name in ('a_k', 'a_v', 'a_idx', 'b_wkv', 'b_shift', 'c_lat', 'c_rope', 'ffn')}
    for i in range(DEPTH):
        j = i // N_MIXERS
        kind = i % N_MIXERS
        hn = _rmsnorm(x, w['n_mix'][i])
        if kind == 0:
            out, k, v, ki = _dsa_mixer(hn, pos, st['a_k'][j], st['a_v'][j], st['a_idx'][j],
                                       w['a_w_in'][j], w['a_w_out'][j])
            new['a_k'].append(k)
            new['a_v'].append(v)
            new['a_idx'].append(ki)
        elif kind == 1:
            out, shift, S = _rwkv7_mixer(hn, st['b_shift'][j], st['b_wkv'][j], w['b_mu'][j], w['b_w_rkv'][j],
                                         w['b_w0'][j], w['b_w1'][j], w['b_w2'][j], w['b_a0'][j], w['b_a1'][j],
                                         w['b_a2'][j], w['b_g1'][j], w['b_g2'][j], w['b_k_k'][j], w['b_k_a'][j],
                                         w['b_r_k'][j], w['b_ln_w'][j], w['b_ln_b'][j], w['b_w_out'][j])
            new['b_shift'].append(shift)
            new['b_wkv'].append(S)
        else:
            out, lat, kpe = _mla_mixer(hn, pos, st['c_lat'][j], st['c_rope'][j], w['c_w_in'][j], w['c_g_q'][j],
                                       w['c_g_kv'][j], w['c_w_uq'][j], w['c_w_ukv'][j], w['c_w_out'][j])
            new['c_lat'].append(lat)
            new['c_rope'].append(kpe)
        x = x + out
        out, cbuf = _conv_ffn(_rmsnorm(x, w['n_ffn'][i]), st['ffn'][i], w['f_w_up'][i], w['f_w_conv'][i],
                              w['f_b_conv'][i], w['f_w_down'][i])
        new['ffn'].append(cbuf)
        x = x + out
    y = _rmsnorm(x, w['n_final'])
    return y, {name: jnp.stack(rows, axis=0) for name, rows in new.items()}


def setup_inputs(seed: int = 0) -> dict:
    key = jax.random.key(seed)
    ks = list(jax.random.split(key, 48))

    def nrm(shape, scale):
        return jax.random.normal(ks.pop(), shape, F32) * scale

    def unif(shape, lo, hi):
        return jax.random.uniform(ks.pop(), shape, F32, lo, hi)

    D = D_MODEL
    F2 = 2 * D_FF
    return {
        'x_prompt': nrm((BATCH, SEQ, D), 1.0),
        'x_sample': nrm((DEC_BATCH, DEC_SEQ, D), 1.0),
        'cache_a_k': nrm((N_A, DEC_BATCH, PAST_LEN, A_KV_HEADS, A_HEAD_DIM), 1.0),
        'cache_a_v': nrm((N_A, DEC_BATCH, PAST_LEN, A_KV_HEADS, A_HEAD_DIM), 1.0),
        'cache_a_idx': nrm((N_A, DEC_BATCH, PAST_LEN, IDX_DIM), 1.0),
        'state_b_wkv': nrm((N_B, DEC_BATCH, B_HEADS, B_HEAD_DIM, B_HEAD_DIM), 0.1),
        'state_b_shift': nrm((N_B, DEC_BATCH, D), 1.0),
        'cache_c_latent': nrm((N_C, DEC_BATCH, PAST_LEN, C_KV_RANK), 1.0),
        'cache_c_rope': nrm((N_C, DEC_BATCH, PAST_LEN, C_ROPE), 1.0),
        'state_ffn_conv': nrm((DEPTH, DEC_BATCH, CONV_W - 1, F2), 1.0),
        'n_mix': 1.0 + nrm((DEPTH, D), 0.02),
        'n_ffn': 1.0 + nrm((DEPTH, D), 0.02),
        'n_final': 1.0 + nrm((D,), 0.02),
        'a_w_in': nrm((N_A, D, A_IN), D ** -0.5),
        'a_w_out': nrm((N_A, A_HEADS * A_HEAD_DIM, D), (A_HEADS * A_HEAD_DIM) ** -0.5),
        'b_mu': unif((N_B, 6, D), 0.0, 1.0),
        'b_w_rkv': nrm((N_B, 3, D, D), D ** -0.5),
        'b_w0': unif((N_B, D), -5.0, 1.0),
        'b_w1': nrm((N_B, D, B_DECAY_LORA), D ** -0.5),
        'b_w2': nrm((N_B, B_DECAY_LORA, D), 0.1 * B_DECAY_LORA ** -0.5),
        'b_a0': nrm((N_B, D), 0.1),
        'b_a1': nrm((N_B, D, B_ICL_LORA), D ** -0.5),
        'b_a2': nrm((N_B, B_ICL_LORA, D), 0.1 * B_ICL_LORA ** -0.5),
        'b_g1': nrm((N_B, D, B_GATE_LORA), D ** -0.5),
        'b_g2': nrm((N_B, B_GATE_LORA, D), B_GATE_LORA ** -0.5),
        'b_k_k': 0.85 + nrm((N_B, D), 0.02),
        'b_k_a': 1.0 + nrm((N_B, D), 0.02),
        'b_r_k': nrm((N_B, B_HEADS, B_HEAD_DIM), 0.1),
        'b_ln_w': 1.0 + nrm((N_B, D), 0.02),
        'b_ln_b': nrm((N_B, D), 0.01),
        'b_w_out': nrm((N_B, D, D), D ** -0.5),
        'c_w_in': nrm((N_C, D, C_IN), D ** -0.5),
        'c_g_q': 1.0 + nrm((N_C, C_Q_RANK), 0.02),
        'c_g_kv': 1.0 + nrm((N_C, C_KV_RANK), 0.02),
        'c_w_uq': nrm((N_C, C_Q_RANK, C_HEADS * (C_NOPE + C_ROPE)), C_Q_RANK ** -0.5),
        'c_w_ukv': nrm((N_C, C_KV_RANK, C_HEADS * (C_NOPE + C_V)), C_KV_RANK ** -0.5),
        'c_w_out': nrm((N_C, C_HEADS * C_V, D), (C_HEADS * C_V) ** -0.5),
        'f_w_up': nrm((DEPTH, D, F2), D ** -0.5),
        'f_w_conv': nrm((DEPTH, CONV_W, F2), CONV_W ** -0.5),
        'f_b_conv': nrm((DEPTH, F2), 0.01),
        'f_w_down': nrm((DEPTH, D_FF, D), D_FF ** -0.5),
    }


def reference(x_prompt, x_sample, cache_a_k, cache_a_v, cache_a_idx, state_b_wkv, state_b_shift,
              cache_c_latent, cache_c_rope, state_ffn_conv, n_mix, n_ffn, n_final, a_w_in, a_w_out,
              b_mu, b_w_rkv, b_w0, b_w1, b_w2, b_a0, b_a1, b_a2, b_g1, b_g2, b_k_k, b_k_a, b_r_k,
              b_ln_w, b_ln_b, b_w_out, c_w_in, c_g_q, c_g_kv, c_w_uq, c_w_ukv, c_w_out,
              f_w_up, f_w_conv, f_b_conv, f_w_down):
    w = dict(n_mix=n_mix, n_ffn=n_ffn, n_final=n_final, a_w_in=a_w_in, a_w_out=a_w_out,
             b_mu=b_mu, b_w_rkv=b_w_rkv, b_w0=b_w0, b_w1=b_w1, b_w2=b_w2, b_a0=b_a0, b_a1=b_a1,
             b_a2=b_a2, b_g1=b_g1, b_g2=b_g2, b_k_k=b_k_k, b_k_a=b_k_a, b_r_k=b_r_k,
             b_ln_w=b_ln_w, b_ln_b=b_ln_b, b_w_out=b_w_out, c_w_in=c_w_in, c_g_q=c_g_q,
             c_g_kv=c_g_kv, c_w_uq=c_w_uq, c_w_ukv=c_w_ukv, c_w_out=c_w_out,
             f_w_up=f_w_up, f_w_conv=f_w_conv, f_b_conv=f_b_conv, f_w_down=f_w_down)
    Bp, Tp, D = x_prompt.shape
    dt = x_prompt.dtype
    st_prompt = dict(
        a_k=jnp.zeros((N_A, Bp, 0, A_KV_HEADS, A_HEAD_DIM), dt),
        a_v=jnp.zeros((N_A, Bp, 0, A_KV_HEADS, A_HEAD_DIM), dt),
        a_idx=jnp.zeros((N_A, Bp, 0, IDX_DIM), dt),
        b_wkv=jnp.zeros((N_B, Bp, B_HEADS, B_HEAD_DIM, B_HEAD_DIM), dt),
        b_shift=jnp.zeros((N_B, Bp, D), dt),
        c_lat=jnp.zeros((N_C, Bp, 0, C_KV_RANK), dt),
        c_rope=jnp.zeros((N_C, Bp, 0, C_ROPE), dt),
        ffn=jnp.zeros((DEPTH, Bp, CONV_W - 1, 2 * D_FF), dt))
    st_sample = dict(a_k=cache_a_k, a_v=cache_a_v, a_idx=cache_a_idx, b_wkv=state_b_wkv,
                     b_shift=state_b_shift, c_lat=cache_c_latent, c_rope=cache_c_rope,
                     ffn=state_ffn_conv)
    pos_prompt = jnp.arange(Tp, dtype=jnp.int32)
    pos_sample = cache_a_k.shape[2] + jnp.arange(x_sample.shape[1], dtype=jnp.int32)
    y_prompt, sp = _trunk(x_prompt, pos_prompt, st_prompt, w)
    y_sample, ss = _trunk(x_sample, pos_sample, st_sample, w)
    return (y_prompt, y_sample,
            sp['a_k'], ss['a_k'], sp['a_v'], ss['a_v'], sp['a_idx'], ss['a_idx'],
            sp['b_wkv'], ss['b_wkv'], sp['b_shift'], ss['b_shift'],
            sp['c_lat'], ss['c_lat'], sp['c_rope'], ss['c_rope'],
            sp['ffn'], ss['ffn'])
```

```python
import functools

import jax
import jax.numpy as jnp
from jax import lax
from jax.experimental import pallas as pl
from jax.experimental.pallas import tpu as pltpu

F32 = jnp.float32
BF16 = jnp.bfloat16

D_MODEL = 1024
DEPTH = 4
CHUNK = 64
N_MIXERS = 3
NORM_EPS = 1e-6
ROPE_THETA = 500000.0
A_HEADS, A_HEAD_DIM, A_KV_HEADS = 16, 64, 2
A_ROT = A_HEAD_DIM // 4
IDX_HEADS, IDX_DIM = 8, 64
IDX_ROT = IDX_DIM // 4
TOPK_MAX = 256
A_O_Q = A_HEADS * A_HEAD_DIM
A_O_K = A_O_Q + A_KV_HEADS * A_HEAD_DIM
A_O_V = A_O_K + A_KV_HEADS * A_HEAD_DIM
A_O_QI = A_O_V + IDX_HEADS * IDX_DIM
A_O_KI = A_O_QI + IDX_DIM
A_IN = A_O_KI + IDX_HEADS
B_HEAD_DIM = 64
B_HEADS = D_MODEL // B_HEAD_DIM
B_GN_EPS = 64e-5
C_HEADS, C_NOPE, C_ROPE, C_V = 16, 64, 32, 64
C_Q_RANK, C_KV_RANK = 512, 256
D_FF = 2816
CONV_W = 3

LANE = 128
SUBLANE_BF16 = 16
VMEM_LIMIT = 56 * 1024 * 1024
NEG_INF = float("-inf")
LOG2E = 1.4426950408889634


def _round_up(n, m):
    return (n + m - 1) // m * m


def _row_tile(M, pref):
    t = min(pref, M)
    while M % t:
        t //= 2
    return t


def _cparams(sem):
    return pltpu.CompilerParams(dimension_semantics=sem, vmem_limit_bytes=VMEM_LIMIT)


def _mm_kernel(*refs, has_norm, has_res):
    a_ref, w_ref = refs[0], refs[1]
    i = 2
    g_ref = r_ref = None
    if has_norm:
        g_ref = refs[i]
        i += 1
    if has_res:
        r_ref = refs[i]
        i += 1
    o_ref = refs[i]
    a = a_ref[...]
    if has_norm:
        af = a.astype(F32)
        a = af * lax.rsqrt(jnp.mean(af * af, axis=-1, keepdims=True) + NORM_EPS) * g_ref[...]
    acc = jnp.dot(a.astype(BF16), w_ref[...], preferred_element_type=F32)
    if has_res:
        acc = acc + r_ref[...]
    o_ref[...] = acc.astype(o_ref.dtype)


def _mm(a, w, *, norm_g=None, residual=None, out_dtype=F32, tm=512):
    M, K = a.shape
    N = w.shape[1]
    tm = _row_tile(M, tm)
    assert M % tm == 0 and N % LANE == 0
    ins = [a, w]
    specs = [pl.BlockSpec((tm, K), lambda i: (i, 0)), pl.BlockSpec((K, N), lambda i: (0, 0))]
    if norm_g is not None:
        ins.append(norm_g.reshape(1, K).astype(F32))
        specs.append(pl.BlockSpec((1, K), lambda i: (0, 0)))
    if residual is not None:
        ins.append(residual)
        specs.append(pl.BlockSpec((tm, N), lambda i: (i, 0)))
    return pl.pallas_call(
        functools.partial(_mm_kernel, has_norm=norm_g is not None, has_res=residual is not None),
        grid=(M // tm,),
        in_specs=specs,
        out_specs=pl.BlockSpec((tm, N), lambda i: (i, 0)),
        out_shape=jax.ShapeDtypeStruct((M, N), out_dtype),
        compiler_params=_cparams(("parallel",)),
        name="mm",
    )(*ins)


def _norm_kernel(x_ref, g_ref, o_ref):
    xf = x_ref[...]
    o_ref[...] = xf * lax.rsqrt(jnp.mean(xf * xf, axis=-1, keepdims=True) + NORM_EPS) * g_ref[...]


def _norm(x, g, tm=512):
    M, K = x.shape
    tm = _row_tile(M, tm)
    return pl.pallas_call(
        _norm_kernel,
        grid=(M // tm,),
        in_specs=[pl.BlockSpec((tm, K), lambda i: (i, 0)), pl.BlockSpec((1, K), lambda i: (0, 0))],
        out_specs=pl.BlockSpec((tm, K), lambda i: (i, 0)),
        out_shape=jax.ShapeDtypeStruct((M, K), F32),
        compiler_params=_cparams(("parallel",)),
        name="rmsnorm",
    )(x, g.reshape(1, K))


def _count(cond):
    return jnp.sum(jnp.where(cond, 1.0, 0.0), axis=-1, keepdims=True)


def _topk_select(score, top, key_idx):
    L = score.shape[1]
    kf = float(top)
    c0 = _count(score >= 0.0)
    neg = c0 < kf
    y = jnp.where(neg, -score, score)
    kp = jnp.where(neg, kf, float(L) - kf + 1.0)
    zero = jnp.zeros_like(c0)
    e_cur = zero
    t_cur = zero
    for b in range(7, -1, -1):
        step = 2 ** b
        cand = jnp.where(e_cur == 0.0, 2.0 ** (step - 127), t_cur * (2.0 ** step if step < 128 else 1.0))
        ok = _count(y < cand) < kp
        e_cur = jnp.where(ok, e_cur + float(step), e_cur)
        t_cur = jnp.where(ok, cand, t_cur)
    t_pow = t_cur
    for j in range(1, 24):
        cand = t_cur + t_pow * (2.0 ** -j)
        ok = _count(y < cand) < kp
        t_cur = jnp.where(ok, cand, t_cur)
    thr = jnp.where(neg, -t_cur, t_cur)
    gt = score > thr
    eq = score == thr
    need = kf - _count(gt)
    eqf = jnp.where(eq, 1.0, 0.0)
    n_eq = jnp.sum(eqf, axis=-1, keepdims=True)

    def index_cut():
        c_cur = zero
        nbits = max(1, (L - 1).bit_length())
        for b in range(nbits - 1, -1, -1):
            cand = c_cur + float(2 ** b)
            ok = jnp.sum(jnp.where(key_idx < cand, eqf, 0.0), axis=-1, keepdims=True) < need
            c_cur = jnp.where(ok, cand, c_cur)
        return c_cur

    any_split = jnp.max(jnp.where(n_eq > need, 1.0, 0.0)) > 0.0
    c_cut = lax.cond(any_split, index_cut, lambda: jnp.full_like(zero, float(L)))
    return gt | (eq & (key_idx <= c_cut))


def _attn_kernel(*refs, n_heads, group, dq, dv, tq, pos0, top, indexer, key_counts):
    if indexer:
        q_ref, k_ref, v_ref, qi_ref, wi_ref, ki_ref, o_ref = refs
    else:
        q_ref, k_ref, v_ref, o_ref = refs
    qb = pl.program_id(1)
    first = pos0 + qb * tq

    def body(L):
        row = lax.broadcasted_iota(jnp.int32, (tq, 1), 0) + first
        limit = (row & ~(CHUNK - 1)) + CHUNK
        key_i = lax.broadcasted_iota(jnp.int32, (tq, L), 1)
        valid = key_i < limit
        if indexer:
            ki = ki_ref[0, :L, :]
            qi = qi_ref[0]
            wi = wi_ref[0]
            score = jnp.zeros((tq, L), F32)
            for h in range(IDX_HEADS):
                d = lax.dot_general(qi[:, h * IDX_DIM:(h + 1) * IDX_DIM], ki, (((1,), (1,)), ((), ())),
                                    preferred_element_type=F32)
                score = score + wi[:, h:h + 1] * jnp.maximum(d, 0.0)
            score = jnp.where(valid, score, NEG_INF)
            sel = _topk_select(score, top, key_i.astype(F32))
            valid = sel & valid
        bias = jnp.where(valid, 0.0, NEG_INF)
        q = q_ref[0]
        kk = k_ref[0, :L, :]
        vv = v_ref[0, :L, :]
        outs = []
        for h in range(n_heads):
            g = h // group
            logits = lax.dot_general(q[:, h * dq:(h + 1) * dq], kk[:, g * dq:(g + 1) * dq],
                                     (((1,), (1,)), ((), ())), preferred_element_type=F32) + bias
            m = jnp.max(logits, axis=-1, keepdims=True)
            p = jnp.exp2(logits - m)
            s = jnp.sum(p, axis=-1, keepdims=True)
            o = jnp.dot(p.astype(BF16), vv[:, g * dv:(g + 1) * dv], preferred_element_type=F32)
            outs.append(o / s)
        o_ref[0] = jnp.concatenate(outs, axis=-1).astype(o_ref.dtype)

    if len(key_counts) == 1:
        body(key_counts[0])
    else:
        last_limit = ((first + tq - 1) & ~(CHUNK - 1)) + CHUNK
        lo = 0
        for L in key_counts:
            pl.when((last_limit > lo) & (last_limit <= L))(functools.partial(body, L))
            lo = L


ATTN_KEY_STEP = 512


def _attn(q, k, v, *, n_heads, group, dq, dv, pos0, tq, idx=None, top=0):
    B, T, _ = q.shape
    L = k.shape[1]
    assert T % tq == 0 and L % LANE == 0
    need = sorted({min(L, _round_up(_round_up(pos0 + (i + 1) * tq, CHUNK), ATTN_KEY_STEP)) for i in range(T // tq)})
    assert need[-1] == L or _round_up(pos0 + T, CHUNK) <= need[-1]
    assert idx is None or top <= need[0]
    ins = [q, k, v]
    specs = [pl.BlockSpec((1, tq, q.shape[2]), lambda b, i: (b, i, 0)),
             pl.BlockSpec((1, L, k.shape[2]), lambda b, i: (b, 0, 0)),
             pl.BlockSpec((1, L, v.shape[2]), lambda b, i: (b, 0, 0))]
    if idx is not None:
        qi, wi, ki = idx
        ins += [qi, wi, ki]
        specs += [pl.BlockSpec((1, tq, qi.shape[2]), lambda b, i: (b, i, 0)),
                  pl.BlockSpec((1, tq, wi.shape[2]), lambda b, i: (b, i, 0)),
                  pl.BlockSpec((1, L, ki.shape[2]), lambda b, i: (b, 0, 0))]
    return pl.pallas_call(
        functools.partial(_attn_kernel, n_heads=n_heads, group=group, dq=dq, dv=dv, tq=tq, pos0=pos0,
                          top=top, indexer=idx is not None, key_counts=tuple(need)),
        grid=(B, T // tq),
        in_specs=specs,
        out_specs=pl.BlockSpec((1, tq, n_heads * dv), lambda b, i: (b, i, 0)),
        out_shape=jax.ShapeDtypeStruct((B, T, n_heads * dv), BF16),
        compiler_params=_cparams(("parallel", "parallel")),
        name="dsa_attn" if idx is not None else "mla_attn",
    )(*ins)


def _rwkv_proj_kernel(h_ref, hp_ref, mu_ref, wr_ref, wk_ref, wv_ref, w1_ref, w2_ref, a1_ref, a2_ref,
                      g1_ref, g2_ref, r_ref, k_ref, v_ref, wl_ref, al_ref, g_ref):
    h = h_ref[...]
    xx = hp_ref[...] - h
    mu = mu_ref[...]

    def mix(i):
        return (h + xx * mu[i:i + 1, :]).astype(BF16)

    dot = lambda a, w_: jnp.dot(a, w_[...], preferred_element_type=F32)
    r_ref[...] = dot(mix(0), wr_ref)
    wl_ref[...] = dot(jnp.tanh(dot(mix(1), w1_ref)).astype(BF16), w2_ref)
    k_ref[...] = dot(mix(2), wk_ref)
    v_ref[...] = dot(mix(3), wv_ref)
    al_ref[...] = dot(dot(mix(4), a1_ref).astype(BF16), a2_ref)
    g_ref[...] = dot(jax.nn.sigmoid(dot(mix(5), g1_ref)).astype(BF16), g2_ref)


def _rwkv_proj(h, hp, mu, wr, wk, wv, w1, w2, a1, a2, g1, g2, tm=512):
    M, D = h.shape
    tm = _row_tile(M, tm)
    row = pl.BlockSpec((tm, D), lambda i: (i, 0))
    full = lambda a: pl.BlockSpec(a.shape, lambda i: (0, 0))
    ws = [wr, wk, wv, w1, w2, a1, a2, g1, g2]
    return pl.pallas_call(
        _rwkv_proj_kernel,
        grid=(M // tm,),
        in_specs=[row, row, full(mu)] + [full(a) for a in ws],
        out_specs=[row] * 6,
        out_shape=[jax.ShapeDtypeStruct((M, D), F32)] * 6,
        compiler_params=_cparams(("parallel",)),
        name="rwkv_proj",
    )(h, hp, mu, *ws)


WKV_SUB = 8
WKV_NB = 4
WKV_BW = 2 * LANE


def _wkv_kernel(rp_ref, w_ref, k_ref, v_ref, a_ref, b_ref, s0_ref, bo_ref, dg_ref, y_ref, sT_ref, S, VB,
                *, tc, nb):
    c = pl.program_id(1)
    n = B_HEAD_DIM
    D = D_MODEL
    bw = WKV_BW
    nt = D // bw
    seg = nt * n

    @pl.when(c == 0)
    def _():
        S[...] = s0_ref[...]

    def tiles(x):
        return jnp.concatenate([x[:, j * bw:(j + 1) * bw] for j in range(nt)], axis=0)

    def untile(x):
        return jnp.concatenate([x[j * n:(j + 1) * n, :] for j in range(nt)], axis=-1)

    def block(sc, carry):
        base = pl.multiple_of(sc * WKV_SUB, WKV_SUB)
        rows = pl.ds(base, WKV_SUB)
        bo = bo_ref[...]
        dg = dg_ref[...]
        ins = []
        for bi in range(nb):
            r8, w8, k8, v8, a8, b8 = (x[bi, rows, :] for x in (rp_ref, w_ref, k_ref, v_ref, a_ref, b_ref))
            ins.append((r8, w8, k8, a8, b8))
            vd = jnp.concatenate([tiles((dg * v8[u:u + 1, :]).astype(BF16)) for u in range(WKV_SUB)], axis=0)
            VB[bi] = jnp.dot(vd, bo, preferred_element_type=F32)
        ys = [[] for _ in range(nb)]
        for u in range(WKV_SUB):
            for bi in range(nb):
                r8, w8, k8, a8, b8 = ins[bi]
                s = S[bi]
                pa = (s * a8[u:u + 1, :]).astype(BF16)
                pr = (s * r8[u:u + 1, :]).astype(BF16)
                rr = jnp.dot(jnp.concatenate([tiles(pa), tiles(pr)], axis=0), bo, preferred_element_type=F32)
                sa = untile(rr[:seg])
                yb = untile(rr[seg:])
                S[bi] = (s * w8[u:u + 1, :] + sa * b8[u:u + 1, :]
                         + untile(VB[bi, u * seg:(u + 1) * seg, :]) * k8[u:u + 1, :])
                ys[bi].append(jnp.sum(yb * dg, axis=0, keepdims=True))
        for bi in range(nb):
            y_ref[bi, rows, :] = jnp.concatenate(ys[bi], axis=0)
        return carry

    lax.fori_loop(0, tc // WKV_SUB, block, 0)

    @pl.when(c == pl.num_programs(1) - 1)
    def _():
        sT_ref[...] = S[...]


def _wkv(r, w, k, v, a, b, s0):
    B, T, D = r.shape
    n = B_HEAD_DIM
    H = B_HEADS
    heads = lambda t: t.reshape(B, T, H, n)
    head_dot = lambda x, y_: jnp.broadcast_to(jnp.sum(heads(x * y_), axis=-1, keepdims=True),
                                              (B, T, H, n)).reshape(B, T, D)
    rp = w * r + a * head_dot(b, r)
    nb = WKV_NB if B % WKV_NB == 0 else 1
    tc = _row_tile(T, 128)
    assert tc % WKV_SUB == 0
    bw = WKV_BW
    s0t = jnp.transpose(s0, (0, 2, 1, 3)).reshape(B, n, D)
    blk = jnp.arange(bw) // n
    bo = (blk[:, None] == blk[None, :]).astype(BF16)
    dg = (jnp.arange(n)[:, None] == (jnp.arange(D) % n)[None, :]).astype(F32)
    seq = pl.BlockSpec((nb, tc, D), lambda bi, c: (bi, c, 0))
    st = pl.BlockSpec((nb, n, D), lambda bi, c: (bi, 0, 0))
    const = lambda a_: pl.BlockSpec(a_.shape, lambda bi, c: (0, 0))
    y, sT = pl.pallas_call(
        functools.partial(_wkv_kernel, tc=tc, nb=nb),
        grid=(B // nb, T // tc),
        in_specs=[seq] * 6 + [st, const(bo), const(dg)],
        out_specs=[seq, st],
        out_shape=[jax.ShapeDtypeStruct((B, T, D), F32), jax.ShapeDtypeStruct((B, n, D), F32)],
        scratch_shapes=[pltpu.VMEM((nb, n, D), F32), pltpu.VMEM((nb, WKV_SUB * (D // bw) * n, bw), F32)],
        compiler_params=_cparams(("parallel", "arbitrary")),
        name="wkv",
    )(rp, w, k, v, a, b, s0t, bo, dg)
    return y + v * head_dot(k, r), jnp.transpose(sT.reshape(B, n, H, n), (0, 2, 1, 3))


FFN_HALO = SUBLANE_BF16


def _ffn_kernel(x_ref, xh_ref, g_ref, wug_ref, wuv_ref, cg_ref, cv_ref, pg_ref, pv_ref, wd_ref,
                o_ref, hn, ug, uv, acc, *, tm):
    i = pl.program_id(1)
    f = pl.program_id(2)
    H = FFN_HALO

    def norm(xf):
        return xf * lax.rsqrt(jnp.mean(xf * xf, axis=-1, keepdims=True) + NORM_EPS) * g_ref[...]

    @pl.when(f == 0)
    def _():
        halo = jnp.where(i > 0, norm(xh_ref[0]), 0.0)
        hn[0:H, :] = halo.astype(BF16)
        hn[H:H + tm, :] = norm(x_ref[0]).astype(BF16)
        acc[...] = jnp.zeros_like(acc)

    h = hn[...]
    ug[...] = jnp.dot(h, wug_ref[...], preferred_element_type=F32)
    uv[...] = jnp.dot(h, wuv_ref[...], preferred_element_type=F32)

    @pl.when(i == 0)
    def _():
        ug[H - 2:H, :] = pg_ref[0]
        uv[H - 2:H, :] = pv_ref[0]

    def conv(u, c_ref):
        cw = c_ref[...]
        return (cw[3:4, :] + u[H - 2:H - 2 + tm, :] * cw[0:1, :] + u[H - 1:H - 1 + tm, :] * cw[1:2, :]
                + u[H:H + tm, :] * cw[2:3, :])

    gate = conv(ug, cg_ref)
    val = conv(uv, cv_ref)
    act = (gate * jax.nn.sigmoid(gate) * val).astype(BF16)
    acc[...] += jnp.dot(act, wd_ref[...], preferred_element_type=F32)

    @pl.when(f == pl.num_programs(2) - 1)
    def _():
        o_ref[0] = x_ref[0] + acc[...]


def _ffn(x, g, wug, wuv, cg, cv, prev, wd, *, tm, tf):
    B, T, D = x.shape
    F = D_FF
    H = FFN_HALO
    tm = min(tm, T)
    assert T % tm == 0 and tm % H == 0 and F % tf == 0
    nh = tm // H
    pg, pv = prev[:, :, :F], prev[:, :, F:]
    return pl.pallas_call(
        functools.partial(_ffn_kernel, tm=tm),
        grid=(B, T // tm, F // tf),
        in_specs=[
            pl.BlockSpec((1, tm, D), lambda b, i, f: (b, i, 0)),
            pl.BlockSpec((1, H, D), lambda b, i, f: (b, jnp.maximum(i * nh - 1, 0), 0)),
            pl.BlockSpec((1, D), lambda b, i, f: (0, 0)),
            pl.BlockSpec((D, tf), lambda b, i, f: (0, f)),
            pl.BlockSpec((D, tf), lambda b, i, f: (0, f)),
            pl.BlockSpec((8, tf), lambda b, i, f: (0, f)),
            pl.BlockSpec((8, tf), lambda b, i, f: (0, f)),
            pl.BlockSpec((1, 2, tf), lambda b, i, f: (b, 0, f)),
            pl.BlockSpec((1, 2, tf), lambda b, i, f: (b, 0, f)),
            pl.BlockSpec((tf, D), lambda b, i, f: (f, 0)),
        ],
        out_specs=pl.BlockSpec((1, tm, D), lambda b, i, f: (b, i, 0)),
        out_shape=jax.ShapeDtypeStruct((B, T, D), F32),
        scratch_shapes=[pltpu.VMEM((tm + H, D), BF16), pltpu.VMEM((tm + H, tf), F32),
                        pltpu.VMEM((tm + H, tf), F32), pltpu.VMEM((tm, D), F32)],
        compiler_params=_cparams(("parallel", "arbitrary", "arbitrary")),
        name="conv_ffn",
    )(x, x, g.reshape(1, D), wug, wuv, cg, cv, pg, pv, wd)


def _rope(x, pos, rot):
    half = rot // 2
    inv = ROPE_THETA ** (-jnp.arange(half, dtype=F32) / half)
    ang = pos.astype(F32)[:, None] * inv[None, :]
    shape = (ang.shape[0],) + (1,) * (x.ndim - 3) + (half,)
    cos = jnp.cos(ang).reshape(shape)
    sin = jnp.sin(ang).reshape(shape)
    x1 = x[..., :half]
    x2 = x[..., half:rot]
    return jnp.concatenate([x1 * cos - x2 * sin, x1 * sin + x2 * cos, x[..., rot:]], axis=-1)


def _pad_cols(w, n):
    return jnp.pad(w, ((0, 0), (0, n - w.shape[1])))


def _pad_keys(a, L):
    return jnp.pad(a, ((0, 0), (0, L - a.shape[1])) + ((0, 0),) * (a.ndim - 2))


def _dsa_layer(x, pos, past_k, past_v, past_ki, g, w_in, w_out):
    B, T, D = x.shape
    n_in = _round_up(A_IN, LANE)
    proj = _mm(x.reshape(B * T, D), _pad_cols(w_in, n_in).astype(BF16), norm_g=g).reshape(B, T, n_in)
    q = _rope(proj[..., :A_O_Q].reshape(B, T, A_HEADS, A_HEAD_DIM), pos, A_ROT)
    k = _rope(proj[..., A_O_Q:A_O_K].reshape(B, T, A_KV_HEADS, A_HEAD_DIM), pos, A_ROT)
    v = proj[..., A_O_K:A_O_V].reshape(B, T, A_KV_HEADS, A_HEAD_DIM)
    qi = _rope(proj[..., A_O_V:A_O_QI].reshape(B, T, IDX_HEADS, IDX_DIM), pos, IDX_ROT)
    ki = _rope(proj[..., A_O_QI:A_O_KI], pos, IDX_ROT)
    wi = proj[..., A_O_KI:A_IN] * ((IDX_HEADS * IDX_DIM) ** -0.5)
    P = past_k.shape[1]
    L = P + T
    Lp = _round_up(L, LANE)
    flat = lambda a: a.reshape(a.shape[0], a.shape[1], a.shape[2] * a.shape[3])
    k_all = _pad_keys(jnp.concatenate([flat(past_k), flat(k)], axis=1), Lp).astype(BF16)
    v_all = _pad_keys(jnp.concatenate([flat(past_v), flat(v)], axis=1), Lp).astype(BF16)
    ki_all = _pad_keys(jnp.concatenate([past_ki, ki], axis=1), Lp).astype(BF16)
    top = min(TOPK_MAX, L // 4)
    o = _attn((flat(q) * (A_HEAD_DIM ** -0.5 * LOG2E)).astype(BF16), k_all, v_all, n_heads=A_HEADS,
              group=A_HEADS // A_KV_HEADS, dq=A_HEAD_DIM, dv=A_HEAD_DIM, pos0=P, tq=min(128, T),
              idx=(flat(qi).astype(BF16), wi, ki_all), top=top)
    x = _mm(o.reshape(B * T, -1), w_out.astype(BF16), residual=x.reshape(B * T, D)).reshape(B, T, D)
    return x, k, v, ki


def _rwkv_layer(x, shift_prev, S0, g, mu, w_rkv, w0, w1, w2, a0, a1, a2, g1, g2, k_k, k_a, r_k, ln_w, ln_b,
                w_out):
    B, T, D = x.shape
    h = _norm(x.reshape(B * T, D), g).reshape(B, T, D)
    hp = jnp.concatenate([shift_prev[:, None, :], h[:, :-1]], axis=1)
    bf = lambda a: a.astype(BF16)
    r, k, v, wl, al, gate = _rwkv_proj(h.reshape(B * T, D), hp.reshape(B * T, D), mu, bf(w_rkv[0]), bf(w_rkv[1]),
                                       bf(w_rkv[2]), bf(w1), bf(w2), bf(a1), bf(a2), bf(g1), bf(g2))
    w_log = -jax.nn.softplus(-(w0 + wl)) - 0.5
    decay = jnp.exp(-jnp.exp(w_log))
    a = jax.nn.sigmoid(a0 + al)
    hs = lambda t: t.reshape(B, T, B_HEADS, B_HEAD_DIM)
    kk = hs(k * k_k)
    kk = kk / jnp.maximum(jnp.sqrt(jnp.sum(kk * kk, axis=-1, keepdims=True)), 1e-12)
    kk = kk.reshape(B * T, D)
    k = k * (1 + (a - 1) * k_a)
    sq = lambda t: t.reshape(B, T, D)
    y, S = _wkv(sq(r), sq(decay), sq(k), sq(v), sq(-kk), sq(kk * a), S0)
    y = hs(y)
    mean = jnp.mean(y, axis=-1, keepdims=True)
    var = jnp.mean(jnp.square(y - mean), axis=-1, keepdims=True)
    y = ((y - mean) * lax.rsqrt(var + B_GN_EPS)).reshape(B * T, D) * ln_w + ln_b
    bonus = jnp.sum(hs(r) * hs(k) * r_k, axis=-1, keepdims=True) * hs(v)
    y = y + bonus.reshape(B * T, D)
    x = _mm(y * gate, bf(w_out), residual=x.reshape(B * T, D)).reshape(B, T, D)
    return x, h[:, -1], S


def _mla_layer(x, pos, past_lat, past_rope, g, w_in, g_q, g_kv, w_uq, w_ukv, w_out):
    B, T, D = x.shape
    c_in = C_Q_RANK + C_KV_RANK + C_ROPE
    n_in = _round_up(c_in, LANE)
    proj = _mm(x.reshape(B * T, D), _pad_cols(w_in, n_in).astype(BF16), norm_g=g)
    cq, ckv, kpe = proj[:, :C_Q_RANK], proj[:, C_Q_RANK:C_Q_RANK + C_KV_RANK], proj[:, C_Q_RANK + C_KV_RANK:c_in]
    q = _mm(cq, w_uq.astype(BF16), norm_g=g_q).reshape(B, T, C_HEADS, C_NOPE + C_ROPE)
    q_pe = _rope(q[..., C_NOPE:], pos, C_ROPE)
    lat = _norm(ckv, g_kv).reshape(B, T, C_KV_RANK)
    kpe = _rope(kpe.reshape(B, T, C_ROPE), pos, C_ROPE)
    P = past_lat.shape[1]
    L = P + T
    Lp = _round_up(L, LANE)
    lat_all = _pad_keys(jnp.concatenate([past_lat, lat], axis=1), Lp)
    kpe_all = _pad_keys(jnp.concatenate([past_rope, kpe], axis=1), Lp)
    kv = _mm(lat_all.reshape(B * Lp, C_KV_RANK), w_ukv.astype(BF16), out_dtype=BF16)
    kv = kv.reshape(B, Lp, C_HEADS, C_NOPE + C_V)
    dq = LANE
    zq = jnp.zeros((B, T, C_HEADS, dq - C_NOPE - C_ROPE), F32)
    scale = (C_NOPE + C_ROPE) ** -0.5 * LOG2E
    q_cat = (jnp.concatenate([q[..., :C_NOPE], q_pe, zq], axis=-1) * scale).astype(BF16).reshape(B, T, -1)
    zk = jnp.zeros((B, Lp, C_HEADS, dq - C_NOPE - C_ROPE), BF16)
    kpe_b = jnp.broadcast_to(kpe_all.astype(BF16)[:, :, None, :], (B, Lp, C_HEADS, C_ROPE))
    k_cat = jnp.concatenate([kv[..., :C_NOPE], kpe_b, zk], axis=-1).reshape(B, Lp, -1)
    v_all = kv[..., C_NOPE:].reshape(B, Lp, -1)
    o = _attn(q_cat, k_cat, v_all, n_heads=C_HEADS, group=1, dq=dq, dv=C_V, pos0=P, tq=min(128, T))
    x = _mm(o.reshape(B * T, -1), w_out.astype(BF16), residual=x.reshape(B * T, D)).reshape(B, T, D)
    return x, lat, kpe


def _ffn_layer(x, prev, g, w_up, w_conv, b_conv, w_down):
    F = D_FF
    B, T, D = x.shape
    taps = jnp.concatenate([w_conv, b_conv[None, :], jnp.zeros((8 - CONV_W - 1, 2 * F), F32)], axis=0)
    w_up = w_up.astype(BF16)
    out = _ffn(x, g, w_up[:, :F], w_up[:, F:], taps[:, :F], taps[:, F:], prev, w_down.astype(BF16),
               tm=512, tf=F // 2)
    assert T >= 8
    u_last = _mm(x[:, T - 8:].reshape(B * 8, D), w_up, norm_g=g).reshape(B, 8, 2 * F)
    return out, u_last[:, 8 - (CONV_W - 1):]


def _trunk(x, pos0, st, w):
    B, T, D = x.shape
    pos = pos0 + jnp.arange(T, dtype=jnp.int32)
    new = {name: [] for name in ('a_k', 'a_v', 'a_idx', 'b_wkv', 'b_shift', 'c_lat', 'c_rope', 'ffn')}
    for i in range(DEPTH):
        j = i // N_MIXERS
        kind = i % N_MIXERS
        if kind == 0:
            x, k, v, ki = _dsa_layer(x, pos, st['a_k'][j], st['a_v'][j], st['a_idx'][j], w['n_mix'][i],
                                     w['a_w_in'][j], w['a_w_out'][j])
            new['a_k'].append(k)
            new['a_v'].append(v)
            new['a_idx'].append(ki)
        elif kind == 1:
            x, shift, S = _rwkv_layer(x, st['b_shift'][j], st['b_wkv'][j], w['n_mix'][i], w['b_mu'][j],
                                      w['b_w_rkv'][j], w['b_w0'][j], w['b_w1'][j], w['b_w2'][j], w['b_a0'][j],
                                      w['b_a1'][j], w['b_a2'][j], w['b_g1'][j], w['b_g2'][j], w['b_k_k'][j],
                                      w['b_k_a'][j], w['b_r_k'][j], w['b_ln_w'][j], w['b_ln_b'][j],
                                      w['b_w_out'][j])
            new['b_shift'].append(shift)
            new['b_wkv'].append(S)
        else:
            x, lat, kpe = _mla_layer(x, pos, st['c_lat'][j], st['c_rope'][j], w['n_mix'][i], w['c_w_in'][j],
                                     w['c_g_q'][j], w['c_g_kv'][j], w['c_w_uq'][j], w['c_w_ukv'][j],
                                     w['c_w_out'][j])
            new['c_lat'].append(lat)
            new['c_rope'].append(kpe)
        x, cbuf = _ffn_layer(x, st['ffn'][i], w['n_ffn'][i], w['f_w_up'][i], w['f_w_conv'][i],
                             w['f_b_conv'][i], w['f_w_down'][i])
        new['ffn'].append(cbuf)
    y = _norm(x.reshape(B * T, D), w['n_final']).reshape(B, T, D)
    return y, {name: jnp.stack(rows, axis=0) for name, rows in new.items()}


def kernel(x_prompt, x_sample, cache_a_k, cache_a_v, cache_a_idx, state_b_wkv, state_b_shift,
           cache_c_latent, cache_c_rope, state_ffn_conv, n_mix, n_ffn, n_final, a_w_in, a_w_out,
           b_mu, b_w_rkv, b_w0, b_w1, b_w2, b_a0, b_a1, b_a2, b_g1, b_g2, b_k_k, b_k_a, b_r_k,
           b_ln_w, b_ln_b, b_w_out, c_w_in, c_g_q, c_g_kv, c_w_uq, c_w_ukv, c_w_out,
           f_w_up, f_w_conv, f_b_conv, f_w_down):
    w = dict(n_mix=n_mix, n_ffn=n_ffn, n_final=n_final, a_w_in=a_w_in, a_w_out=a_w_out,
             b_mu=b_mu, b_w_rkv=b_w_rkv, b_w0=b_w0, b_w1=b_w1, b_w2=b_w2, b_a0=b_a0, b_a1=b_a1,
             b_a2=b_a2, b_g1=b_g1, b_g2=b_g2, b_k_k=b_k_k, b_k_a=b_k_a, b_r_k=b_r_k,
             b_ln_w=b_ln_w, b_ln_b=b_ln_b, b_w_out=b_w_out, c_w_in=c_w_in, c_g_q=c_g_q,
             c_g_kv=c_g_kv, c_w_uq=c_w_uq, c_w_ukv=c_w_ukv, c_w_out=c_w_out,
             f_w_up=f_w_up, f_w_conv=f_w_conv, f_b_conv=f_b_conv, f_w_down=f_w_down)
    Bp, Tp, D = x_prompt.shape
    n_a, n_b, n_c = cache_a_k.shape[0], state_b_wkv.shape[0], cache_c_latent.shape[0]
    st_prompt = dict(
        a_k=jnp.zeros((n_a, Bp, 0, A_KV_HEADS, A_HEAD_DIM), F32),
        a_v=jnp.zeros((n_a, Bp, 0, A_KV_HEADS, A_HEAD_DIM), F32),
        a_idx=jnp.zeros((n_a, Bp, 0, IDX_DIM), F32),
        b_wkv=jnp.zeros((n_b, Bp, B_HEADS, B_HEAD_DIM, B_HEAD_DIM), F32),
        b_shift=jnp.zeros((n_b, Bp, D), F32),
        c_lat=jnp.zeros((n_c, Bp, 0, C_KV_RANK), F32),
        c_rope=jnp.zeros((n_c, Bp, 0, C_ROPE), F32),
        ffn=jnp.zeros((DEPTH, Bp, CONV_W - 1, 2 * D_FF), F32))
    st_sample = dict(a_k=cache_a_k, a_v=cache_a_v, a_idx=cache_a_idx, b_wkv=state_b_wkv,
                     b_shift=state_b_shift, c_lat=cache_c_latent, c_rope=cache_c_rope,
                     ffn=state_ffn_conv)
    y_prompt, sp = _trunk(x_prompt, 0, st_prompt, w)
    y_sample, ss = _trunk(x_sample, cache_a_k.shape[2], st_sample, w)
    return (y_prompt, y_sample,
            sp['a_k'], ss['a_k'], sp['a_v'], ss['a_v'], sp['a_idx'], ss['a_idx'],
            sp['b_wkv'], ss['b_wkv'], sp['b_shift'], ss['b_shift'],
            sp['c_lat'], ss['c_lat'], sp['c_rope'], ss['c_rope'],
            sp['ffn'], ss['ffn'])
```

```python
import functools

import jax
import jax.numpy as jnp
from jax import lax
from jax.experimental import pallas as pl
from jax.experimental.pallas import tpu as pltpu

F32 = jnp.float32
BF16 = jnp.bfloat16

D_MODEL = 1024
DEPTH = 4
CHUNK = 64
N_MIXERS = 3
NORM_EPS = 1e-6
ROPE_THETA = 500000.0
A_HEADS, A_HEAD_DIM, A_KV_HEADS = 16, 64, 2
A_ROT = A_HEAD_DIM // 4
IDX_HEADS, IDX_DIM = 8, 64
IDX_ROT = IDX_DIM // 4
TOPK_MAX = 256
A_O_Q = A_HEADS * A_HEAD_DIM
A_O_K = A_O_Q + A_KV_HEADS * A_HEAD_DIM
A_O_V = A_O_K + A_KV_HEADS * A_HEAD_DIM
A_O_QI = A_O_V + IDX_HEADS * IDX_DIM
A_O_KI = A_O_QI + IDX_DIM
A_IN = A_O_KI + IDX_HEADS
B_HEAD_DIM = 64
B_HEADS = D_MODEL // B_HEAD_DIM
B_GN_EPS = 64e-5
C_HEADS, C_NOPE, C_ROPE, C_V = 16, 64, 32, 64
C_Q_RANK, C_KV_RANK = 512, 256
D_FF = 2816
CONV_W = 3

LANE = 128
SUBLANE_BF16 = 16
VMEM_LIMIT = 56 * 1024 * 1024
NEG_INF = float("-inf")
LOG2E = 1.4426950408889634


def _round_up(n, m):
    return (n + m - 1) // m * m


def _row_tile(M, pref):
    t = min(pref, M)
    while M % t:
        t //= 2
    return t


def _cparams(sem):
    return pltpu.CompilerParams(dimension_semantics=sem, vmem_limit_bytes=VMEM_LIMIT)


def _mm_kernel(*refs, has_norm, has_res):
    a_ref, w_ref = refs[0], refs[1]
    i = 2
    g_ref = r_ref = None
    if has_norm:
        g_ref = refs[i]
        i += 1
    if has_res:
        r_ref = refs[i]
        i += 1
    o_ref = refs[i]
    a = a_ref[...]
    if has_norm:
        af = a.astype(F32)
        a = af * lax.rsqrt(jnp.mean(af * af, axis=-1, keepdims=True) + NORM_EPS) * g_ref[...]
    acc = jnp.dot(a.astype(BF16), w_ref[...], preferred_element_type=F32)
    if has_res:
        acc = acc + r_ref[...]
    o_ref[...] = acc.astype(o_ref.dtype)


def _mm(a, w, *, norm_g=None, residual=None, out_dtype=F32, tm=512):
    M, K = a.shape
    N = w.shape[1]
    tm = _row_tile(M, tm)
    assert M % tm == 0 and N % LANE == 0
    ins = [a, w]
    specs = [pl.BlockSpec((tm, K), lambda i: (i, 0)), pl.BlockSpec((K, N), lambda i: (0, 0))]
    if norm_g is not None:
        ins.append(norm_g.reshape(1, K).astype(F32))
        specs.append(pl.BlockSpec((1, K), lambda i: (0, 0)))
    if residual is not None:
        ins.append(residual)
        specs.append(pl.BlockSpec((tm, N), lambda i: (i, 0)))
    return pl.pallas_call(
        functools.partial(_mm_kernel, has_norm=norm_g is not None, has_res=residual is not None),
        grid=(M // tm,),
        in_specs=specs,
        out_specs=pl.BlockSpec((tm, N), lambda i: (i, 0)),
        out_shape=jax.ShapeDtypeStruct((M, N), out_dtype),
        compiler_params=_cparams(("parallel",)),
        name="mm",
    )(*ins)


def _norm_kernel(x_ref, g_ref, o_ref):
    xf = x_ref[...]
    o_ref[...] = xf * lax.rsqrt(jnp.mean(xf * xf, axis=-1, keepdims=True) + NORM_EPS) * g_ref[...]


def _norm(x, g, tm=512):
    M, K = x.shape
    tm = _row_tile(M, tm)
    return pl.pallas_call(
        _norm_kernel,
        grid=(M // tm,),
        in_specs=[pl.BlockSpec((tm, K), lambda i: (i, 0)), pl.BlockSpec((1, K), lambda i: (0, 0))],
        out_specs=pl.BlockSpec((tm, K), lambda i: (i, 0)),
        out_shape=jax.ShapeDtypeStruct((M, K), F32),
        compiler_params=_cparams(("parallel",)),
        name="rmsnorm",
    )(x, g.reshape(1, K))


def _count(cond):
    return jnp.sum(jnp.where(cond, 1.0, 0.0), axis=-1, keepdims=True)


def _topk_select(score, top, key_idx):
    L = score.shape[1]
    kf = float(top)
    c0 = _count(score >= 0.0)
    neg = c0 < kf
    y = jnp.where(neg, -score, score)
    kp = jnp.where(neg, kf, float(L) - kf + 1.0)
    zero = jnp.zeros_like(c0)
    e_cur = zero
    t_cur = zero
    for b in range(7, -1, -1):
        step = 2 ** b
        cand = jnp.where(e_cur == 0.0, 2.0 ** (step - 127), t_cur * (2.0 ** step if step < 128 else 1.0))
        ok = _count(y < cand) < kp
        e_cur = jnp.where(ok, e_cur + float(step), e_cur)
        t_cur = jnp.where(ok, cand, t_cur)
    t_pow = t_cur
    for j in range(1, 24):
        cand = t_cur + t_pow * (2.0 ** -j)
        ok = _count(y < cand) < kp
        t_cur = jnp.where(ok, cand, t_cur)
    thr = jnp.where(neg, -t_cur, t_cur)
    gt = score > thr
    eq = score == thr
    need = kf - _count(gt)
    eqf = jnp.where(eq, 1.0, 0.0)
    n_eq = jnp.sum(eqf, axis=-1, keepdims=True)

    def index_cut():
        c_cur = zero
        nbits = max(1, (L - 1).bit_length())
        for b in range(nbits - 1, -1, -1):
            cand = c_cur + float(2 ** b)
            ok = jnp.sum(jnp.where(key_idx < cand, eqf, 0.0), axis=-1, keepdims=True) < need
            c_cur = jnp.where(ok, cand, c_cur)
        return c_cur

    any_split = jnp.max(jnp.where(n_eq > need, 1.0, 0.0)) > 0.0
    c_cut = lax.cond(any_split, index_cut, lambda: jnp.full_like(zero, float(L)))
    return gt | (eq & (key_idx <= c_cut))


def _attn_kernel(*refs, n_heads, group, dq, dv, tq, pos0, top, indexer, key_counts):
    if indexer:
        q_ref, k_ref, v_ref, qi_ref, wi_ref, ki_ref, o_ref = refs
    else:
        q_ref, k_ref, v_ref, o_ref = refs
    qb = pl.program_id(1)
    first = pos0 + qb * tq

    def body(L):
        row = lax.broadcasted_iota(jnp.int32, (tq, 1), 0) + first
        limit = (row & ~(CHUNK - 1)) + CHUNK
        key_i = lax.broadcasted_iota(jnp.int32, (tq, L), 1)
        valid = key_i < limit
        if indexer:
            ki = ki_ref[0, :L, :]
            qi = qi_ref[0]
            wi = wi_ref[0]
            score = jnp.zeros((tq, L), F32)
            for h in range(IDX_HEADS):
                d = lax.dot_general(qi[:, h * IDX_DIM:(h + 1) * IDX_DIM], ki, (((1,), (1,)), ((), ())),
                                    preferred_element_type=F32)
                score = score + wi[:, h:h + 1] * jnp.maximum(d, 0.0)
            score = jnp.where(valid, score, NEG_INF)
            sel = _topk_select(score, top, key_i.astype(F32))
            valid = sel & valid
        bias = jnp.where(valid, 0.0, NEG_INF)
        q = q_ref[0]
        kk = k_ref[0, :L, :]
        vv = v_ref[0, :L, :]
        outs = []
        for h in range(n_heads):
            g = h // group
            logits = lax.dot_general(q[:, h * dq:(h + 1) * dq], kk[:, g * dq:(g + 1) * dq],
                                     (((1,), (1,)), ((), ())), preferred_element_type=F32) + bias
            m = jnp.max(logits, axis=-1, keepdims=True)
            p = jnp.exp2(logits - m)
            s = jnp.sum(p, axis=-1, keepdims=True)
            o = jnp.dot(p.astype(BF16), vv[:, g * dv:(g + 1) * dv], preferred_element_type=F32)
            outs.append(o / s)
        o_ref[0] = jnp.concatenate(outs, axis=-1).astype(o_ref.dtype)

    if len(key_counts) == 1:
        body(key_counts[0])
    else:
        last_limit = ((first + tq - 1) & ~(CHUNK - 1)) + CHUNK
        lo = 0
        for L in key_counts:
            pl.when((last_limit > lo) & (last_limit <= L))(functools.partial(body, L))
            lo = L


ATTN_KEY_STEP = 512


def _attn(q, k, v, *, n_heads, group, dq, dv, pos0, tq, idx=None, top=0):
    B, T, _ = q.shape
    L = k.shape[1]
    assert T % tq == 0 and L % LANE == 0
    need = sorted({min(L, _round_up(_round_up(pos0 + (i + 1) * tq, CHUNK), ATTN_KEY_STEP)) for i in range(T // tq)})
    assert need[-1] == L or _round_up(pos0 + T, CHUNK) <= need[-1]
    assert idx is None or top <= need[0]
    ins = [q, k, v]
    specs = [pl.BlockSpec((1, tq, q.shape[2]), lambda b, i: (b, i, 0)),
             pl.BlockSpec((1, L, k.shape[2]), lambda b, i: (b, 0, 0)),
             pl.BlockSpec((1, L, v.shape[2]), lambda b, i: (b, 0, 0))]
    if idx is not None:
        qi, wi, ki = idx
        ins += [qi, wi, ki]
        specs += [pl.BlockSpec((1, tq, qi.shape[2]), lambda b, i: (b, i, 0)),
                  pl.BlockSpec((1, tq, wi.shape[2]), lambda b, i: (b, i, 0)),
                  pl.BlockSpec((1, L, ki.shape[2]), lambda b, i: (b, 0, 0))]
    return pl.pallas_call(
        functools.partial(_attn_kernel, n_heads=n_heads, group=group, dq=dq, dv=dv, tq=tq, pos0=pos0,
                          top=top, indexer=idx is not None, key_counts=tuple(need)),
        grid=(B, T // tq),
        in_specs=specs,
        out_specs=pl.BlockSpec((1, tq, n_heads * dv), lambda b, i: (b, i, 0)),
        out_shape=jax.ShapeDtypeStruct((B, T, n_heads * dv), BF16),
        compiler_params=_cparams(("parallel", "parallel")),
        name="dsa_attn" if idx is not None else "mla_attn",
    )(*ins)


ATTN_KB = 512
ATTN_KB_SHIFT = ATTN_KB.bit_length() - 1
SELECT_ALL = 1e9
ATTN_ROWS_PER_ITER = 1024


def _fold(x, op=jnp.add):
    acc = x[:, :LANE]
    for j in range(1, x.shape[1] // LANE):
        acc = op(acc, x[:, j * LANE:(j + 1) * LANE])
    return acc


def _topk_bias(tab_ref, qi_ref, wi_ref, ki_ref, SC, MS, *, nk, limit, lane_i, tq, top):
    kb = ATTN_KB
    qi = qi_ref[0]
    wi = wi_ref[0]
    zeros_l = jnp.zeros((tq, LANE), F32)
    zero = jnp.zeros((tq, 1), F32)
    kf = float(top)

    def lane_sum(body):
        acc = lax.fori_loop(0, nk, lambda j, a: a + _fold(body(j)), zeros_l)
        return jnp.sum(acc, axis=-1, keepdims=True)

    def ones_where(c):
        return jnp.where(c, 1.0, 0.0)

    def score_block(j):
        kij = ki_ref[0, pl.ds(pl.multiple_of(j * kb, kb), kb), :]
        sc = jnp.zeros((tq, kb), F32)
        for h in range(IDX_HEADS):
            d = lax.dot_general(qi[:, h * IDX_DIM:(h + 1) * IDX_DIM], kij, (((1,), (1,)), ((), ())),
                                preferred_element_type=F32)
            sc = sc + wi[:, h:h + 1] * jnp.maximum(d, 0.0)
        sc = jnp.where(lane_i + j * kb < limit, sc, NEG_INF)
        SC[j] = sc
        return ones_where(sc >= 0.0)

    c0 = lane_sum(score_block)
    neg = c0 < kf
    sgn = jnp.where(neg, -1.0, 1.0)
    kp = jnp.where(neg, kf, (nk * kb).astype(F32) - kf + 1.0)

    def flip(j, c):
        SC[j] = SC[j] * sgn
        return c

    lax.fori_loop(0, nk, flip, 0)

    def count_lt(cand):
        return lane_sum(lambda j: ones_where(SC[j] < cand))

    def exp_step(i, carry):
        e_cur, t_cur = carry
        cand = jnp.where(e_cur == 0.0, tab_ref[0, i], t_cur * tab_ref[1, i])
        ok = count_lt(cand) < kp
        return jnp.where(ok, e_cur + tab_ref[2, i], e_cur), jnp.where(ok, cand, t_cur)

    _, t_pow = lax.fori_loop(0, 8, exp_step, (zero, zero))

    def man_step(i, carry):
        t_cur, frac = carry
        cand = t_cur + frac
        ok = count_lt(cand) < kp
        return jnp.where(ok, cand, t_cur), frac * 0.5

    t_cur, _ = lax.fori_loop(0, 23, man_step, (t_pow, t_pow * 0.5))
    thr = t_cur * sgn

    def score(j):
        return SC[j] * sgn

    def key_idx(j):
        return (lane_i + j * kb).astype(F32)

    need = kf - lane_sum(lambda j: ones_where(score(j) > thr))
    n_eq = lane_sum(lambda j: ones_where(score(j) == thr))

    def index_cut():
        nbits = (SC.shape[0] * kb - 1).bit_length()

        def bit_step(i, carry):
            c_cur, bit = carry
            cand = c_cur + bit
            ok = lane_sum(lambda j: ones_where((score(j) == thr) & (key_idx(j) < cand))) < need
            return jnp.where(ok, cand, c_cur), bit * 0.5

        c_cur, _ = lax.fori_loop(0, nbits, bit_step, (zero, jnp.full((tq, 1), 2.0 ** (nbits - 1), F32)))
        return c_cur

    any_split = jnp.max(ones_where(n_eq > need)) > 0.0
    c_cut = lax.cond(any_split, index_cut, lambda: jnp.full((tq, 1), SELECT_ALL, F32))

    def write_bias(j, c):
        s = score(j)
        sel = (s > thr) | ((s == thr) & (key_idx(j) <= c_cut))
        MS[j] = jnp.where(sel & (lane_i + j * kb < limit), 0.0, NEG_INF)
        return c

    lax.fori_loop(0, nk, write_bias, 0)


def _attn2_kernel(*refs, n_kv, group, tq, pos0, top, indexer):
    if indexer:
        tab_ref, q_ref, k_ref, v_ref, qi_ref, wi_ref, ki_ref, o_ref, MS, LG, MACC, LACC, OACC, SC = refs
    else:
        q_ref, k_ref, v_ref, o_ref, MS, LG, MACC, LACC, OACC = refs
    kb = ATTN_KB
    first = pos0 + pl.program_id(1) * tq
    last_limit = ((first + tq - 1) & ~(CHUNK - 1)) + CHUNK
    nk = (last_limit + (kb - 1)) >> ATTN_KB_SHIFT
    row = lax.broadcasted_iota(jnp.int32, (tq, 1), 0) + first
    limit = (row & ~(CHUNK - 1)) + CHUNK
    lane_i = lax.broadcasted_iota(jnp.int32, (tq, kb), 1)

    if indexer:
        _topk_bias(tab_ref, qi_ref, wi_ref, ki_ref, SC, MS, nk=nk, limit=limit, lane_i=lane_i, tq=tq, top=top)
    else:
        def causal_bias(j, c):
            MS[j] = jnp.where(lane_i + j * kb < limit, 0.0, NEG_INF)
            return c

        lax.fori_loop(0, nk, causal_bias, 0)

    hu = LG.shape[0]

    def per_kv_heads(gi, c):
        heads = [gi * hu + u for u in range(hu)]
        qs = [q_ref[0, g, 0] for g in heads]
        MACC[...] = jnp.full(MACC.shape, NEG_INF, F32)
        LACC[...] = jnp.zeros(LACC.shape, F32)
        OACC[...] = jnp.zeros(OACC.shape, F32)

        def logits_block(j, c_):
            keys = pl.ds(pl.multiple_of(j * kb, kb), kb)
            bias = MS[j][None]
            for u, g in enumerate(heads):
                lg = lax.dot_general(qs[u], k_ref[0, g, keys, :], (((1,), (1,)), ((), ())),
                                     preferred_element_type=F32)
                lg = (lg.reshape(group, tq, kb) + bias).reshape(group * tq, kb)
                LG[u, j] = lg
                MACC[u] = jnp.maximum(MACC[u], _fold(lg, jnp.maximum))
            return c_

        lax.fori_loop(0, nk, logits_block, 0)
        ms = [jnp.max(MACC[u], axis=-1, keepdims=True) for u in range(hu)]

        def value_block(j, c_):
            keys = pl.ds(pl.multiple_of(j * kb, kb), kb)
            for u, g in enumerate(heads):
                p = jnp.exp2(LG[u, j] - ms[u])
                LACC[u] += _fold(p)
                OACC[u] += jnp.dot(p.astype(BF16), v_ref[0, g, keys, :], preferred_element_type=F32)
            return c_

        lax.fori_loop(0, nk, value_block, 0)
        for u, g in enumerate(heads):
            o_ref[0, g, 0] = (OACC[u] / jnp.sum(LACC[u], axis=-1, keepdims=True)).astype(o_ref.dtype)
        return c

    lax.fori_loop(0, n_kv // hu, per_kv_heads, 0)


def _attn2(q, k, v, *, pos0, tq, idx=None, top=0):
    B, T, n_kv, group, dq = q.shape
    L, dv = k.shape[1], v.shape[3]
    kb = ATTN_KB
    nq = T // tq
    assert T % tq == 0 and L % kb == 0 and (idx is None or top <= kb)
    qg = jnp.transpose(q.reshape(B, nq, tq, n_kv, group, dq), (0, 3, 1, 4, 2, 5)).reshape(B, n_kv, nq, group * tq, dq)
    kt = jnp.transpose(k, (0, 2, 1, 3))
    vt = jnp.transpose(v, (0, 2, 1, 3))
    ins = [qg, kt, vt]
    specs = [pl.BlockSpec((1, n_kv, 1, group * tq, dq), lambda b, i: (b, 0, i, 0, 0)),
             pl.BlockSpec((1, n_kv, L, dq), lambda b, i: (b, 0, 0, 0)),
             pl.BlockSpec((1, n_kv, L, dv), lambda b, i: (b, 0, 0, 0))]
    hu = max(1, min(n_kv, ATTN_ROWS_PER_ITER // (group * tq)))
    assert n_kv % hu == 0
    rows = group * tq
    scratch = [pltpu.VMEM((L // kb, tq, kb), F32), pltpu.VMEM((hu, L // kb, rows, kb), F32),
               pltpu.VMEM((hu, rows, LANE), F32), pltpu.VMEM((hu, rows, LANE), F32),
               pltpu.VMEM((hu, rows, dv), F32)]
    if idx is not None:
        qi, wi, ki = idx
        steps = [2 ** b for b in range(7, -1, -1)]
        tab = jnp.array([[2.0 ** (s - 127) for s in steps], [2.0 ** s if s < 128 else 1.0 for s in steps],
                         [float(s) for s in steps]], F32)
        ins = [tab] + ins + [qi, wi, ki]
        specs = ([pl.BlockSpec(memory_space=pltpu.SMEM)] + specs
                 + [pl.BlockSpec((1, tq, qi.shape[2]), lambda b, i: (b, i, 0)),
                    pl.BlockSpec((1, tq, wi.shape[2]), lambda b, i: (b, i, 0)),
                    pl.BlockSpec((1, L, ki.shape[2]), lambda b, i: (b, 0, 0))])
        scratch.append(pltpu.VMEM((L // kb, tq, kb), F32))
    o = pl.pallas_call(
        functools.partial(_attn2_kernel, n_kv=n_kv, group=group, tq=tq, pos0=pos0, top=top,
                          indexer=idx is not None),
        grid=(B, nq),
        in_specs=specs,
        out_specs=pl.BlockSpec((1, n_kv, 1, group * tq, dv), lambda b, i: (b, 0, i, 0, 0)),
        out_shape=jax.ShapeDtypeStruct((B, n_kv, nq, group * tq, dv), BF16),
        scratch_shapes=scratch,
        compiler_params=_cparams(("parallel", "parallel")),
        name="dsa_attn" if idx is not None else "mla_attn",
    )(*ins)
    return jnp.transpose(o.reshape(B, n_kv, nq, group, tq, dv), (0, 2, 4, 1, 3, 5)).reshape(B, T, n_kv, group, dv)


def _rwkv_proj_kernel(h_ref, hp_ref, mu_ref, wr_ref, wk_ref, wv_ref, w1_ref, w2_ref, a1_ref, a2_ref,
                      g1_ref, g2_ref, r_ref, k_ref, v_ref, wl_ref, al_ref, g_ref):
    h = h_ref[...]
    xx = hp_ref[...] - h
    mu = mu_ref[...]

    def mix(i):
        return (h + xx * mu[i:i + 1, :]).astype(BF16)

    dot = lambda a, w_: jnp.dot(a, w_[...], preferred_element_type=F32)
    r_ref[...] = dot(mix(0), wr_ref)
    wl_ref[...] = dot(jnp.tanh(dot(mix(1), w1_ref)).astype(BF16), w2_ref)
    k_ref[...] = dot(mix(2), wk_ref)
    v_ref[...] = dot(mix(3), wv_ref)
    al_ref[...] = dot(dot(mix(4), a1_ref).astype(BF16), a2_ref)
    g_ref[...] = dot(jax.nn.sigmoid(dot(mix(5), g1_ref)).astype(BF16), g2_ref)


def _rwkv_proj(h, hp, mu, wr, wk, wv, w1, w2, a1, a2, g1, g2, tm=512):
    M, D = h.shape
    tm = _row_tile(M, tm)
    row = pl.BlockSpec((tm, D), lambda i: (i, 0))
    full = lambda a: pl.BlockSpec(a.shape, lambda i: (0, 0))
    ws = [wr, wk, wv, w1, w2, a1, a2, g1, g2]
    return pl.pallas_call(
        _rwkv_proj_kernel,
        grid=(M // tm,),
        in_specs=[row, row, full(mu)] + [full(a) for a in ws],
        out_specs=[row] * 6,
        out_shape=[jax.ShapeDtypeStruct((M, D), F32)] * 6,
        compiler_params=_cparams(("parallel",)),
        name="rwkv_proj",
    )(h, hp, mu, *ws)


WKV_SUB = 8
WKV_NB = 4
WKV_BW = 2 * LANE


def _wkv_kernel(rp_ref, w_ref, k_ref, v_ref, a_ref, b_ref, s0_ref, bo_ref, dg_ref, y_ref, sT_ref, S, VB,
                *, tc, nb):
    c = pl.program_id(1)
    n = B_HEAD_DIM
    D = D_MODEL
    bw = WKV_BW
    nt = D // bw
    seg = nt * n

    @pl.when(c == 0)
    def _():
        S[...] = s0_ref[...]

    def tiles(x):
        return jnp.concatenate([x[:, j * bw:(j + 1) * bw] for j in range(nt)], axis=0)

    def untile(x):
        return jnp.concatenate([x[j * n:(j + 1) * n, :] for j in range(nt)], axis=-1)

    def block(sc, carry):
        base = pl.multiple_of(sc * WKV_SUB, WKV_SUB)
        rows = pl.ds(base, WKV_SUB)
        bo = bo_ref[...]
        dg = dg_ref[...]
        ins = []
        for bi in range(nb):
            r8, w8, k8, v8, a8, b8 = (x[bi, rows, :] for x in (rp_ref, w_ref, k_ref, v_ref, a_ref, b_ref))
            ins.append((r8, w8, k8, a8, b8))
            vd = jnp.concatenate([tiles((dg * v8[u:u + 1, :]).astype(BF16)) for u in range(WKV_SUB)], axis=0)
            VB[bi] = jnp.dot(vd, bo, preferred_element_type=F32)
        ys = [[] for _ in range(nb)]
        for u in range(WKV_SUB):
            for bi in range(nb):
                r8, w8, k8, a8, b8 = ins[bi]
                s = S[bi]
                pa = (s * a8[u:u + 1, :]).astype(BF16)
                pr = (s * r8[u:u + 1, :]).astype(BF16)
                rr = jnp.dot(jnp.concatenate([tiles(pa), tiles(pr)], axis=0), bo, preferred_element_type=F32)
                sa = untile(rr[:seg])
                yb = untile(rr[seg:])
                S[bi] = (s * w8[u:u + 1, :] + sa * b8[u:u + 1, :]
                         + untile(VB[bi, u * seg:(u + 1) * seg, :]) * k8[u:u + 1, :])
                ys[bi].append(jnp.sum(yb * dg, axis=0, keepdims=True))
        for bi in range(nb):
            y_ref[bi, rows, :] = jnp.concatenate(ys[bi], axis=0)
        return carry

    lax.fori_loop(0, tc // WKV_SUB, block, 0)

    @pl.when(c == pl.num_programs(1) - 1)
    def _():
        sT_ref[...] = S[...]


def _wkv(r, w, k, v, a, b, s0):
    B, T, D = r.shape
    n = B_HEAD_DIM
    H = B_HEADS
    heads = lambda t: t.reshape(B, T, H, n)
    head_dot = lambda x, y_: jnp.broadcast_to(jnp.sum(heads(x * y_), axis=-1, keepdims=True),
                                              (B, T, H, n)).reshape(B, T, D)
    rp = w * r + a * head_dot(b, r)
    nb = WKV_NB if B % WKV_NB == 0 else 1
    tc = _row_tile(T, 128)
    assert tc % WKV_SUB == 0
    bw = WKV_BW
    s0t = jnp.transpose(s0, (0, 2, 1, 3)).reshape(B, n, D)
    blk = jnp.arange(bw) // n
    bo = (blk[:, None] == blk[None, :]).astype(BF16)
    dg = (jnp.arange(n)[:, None] == (jnp.arange(D) % n)[None, :]).astype(F32)
    seq = pl.BlockSpec((nb, tc, D), lambda bi, c: (bi, c, 0))
    st = pl.BlockSpec((nb, n, D), lambda bi, c: (bi, 0, 0))
    const = lambda a_: pl.BlockSpec(a_.shape, lambda bi, c: (0, 0))
    y, sT = pl.pallas_call(
        functools.partial(_wkv_kernel, tc=tc, nb=nb),
        grid=(B // nb, T // tc),
        in_specs=[seq] * 6 + [st, const(bo), const(dg)],
        out_specs=[seq, st],
        out_shape=[jax.ShapeDtypeStruct((B, T, D), F32), jax.ShapeDtypeStruct((B, n, D), F32)],
        scratch_shapes=[pltpu.VMEM((nb, n, D), F32), pltpu.VMEM((nb, WKV_SUB * (D // bw) * n, bw), F32)],
        compiler_params=_cparams(("parallel", "arbitrary")),
        name="wkv",
    )(rp, w, k, v, a, b, s0t, bo, dg)
    return y + v * head_dot(k, r), jnp.transpose(sT.reshape(B, n, H, n), (0, 2, 1, 3))


FFN_HALO = SUBLANE_BF16


def _ffn_kernel(x_ref, xh_ref, g_ref, wug_ref, wuv_ref, cg_ref, cv_ref, pg_ref, pv_ref, wd_ref,
                o_ref, hn, ug, uv, acc, *, tm):
    i = pl.program_id(1)
    f = pl.program_id(2)
    H = FFN_HALO

    def norm(xf):
        return xf * lax.rsqrt(jnp.mean(xf * xf, axis=-1, keepdims=True) + NORM_EPS) * g_ref[...]

    @pl.when(f == 0)
    def _():
        halo = jnp.where(i > 0, norm(xh_ref[0]), 0.0)
        hn[0:H, :] = halo.astype(BF16)
        hn[H:H + tm, :] = norm(x_ref[0]).astype(BF16)
        acc[...] = jnp.zeros_like(acc)

    h = hn[...]
    ug[...] = jnp.dot(h, wug_ref[...], preferred_element_type=F32)
    uv[...] = jnp.dot(h, wuv_ref[...], preferred_element_type=F32)

    @pl.when(i == 0)
    def _():
        ug[H - 2:H, :] = pg_ref[0]
        uv[H - 2:H, :] = pv_ref[0]

    def conv(u, c_ref):
        cw = c_ref[...]
        return (cw[3:4, :] + u[H - 2:H - 2 + tm, :] * cw[0:1, :] + u[H - 1:H - 1 + tm, :] * cw[1:2, :]
                + u[H:H + tm, :] * cw[2:3, :])

    gate = conv(ug, cg_ref)
    val = conv(uv, cv_ref)
    act = (gate * jax.nn.sigmoid(gate) * val).astype(BF16)
    acc[...] += jnp.dot(act, wd_ref[...], preferred_element_type=F32)

    @pl.when(f == pl.num_programs(2) - 1)
    def _():
        o_ref[0] = x_ref[0] + acc[...]


def _ffn(x, g, wug, wuv, cg, cv, prev, wd, *, tm, tf):
    B, T, D = x.shape
    F = D_FF
    H = FFN_HALO
    tm = min(tm, T)
    assert T % tm == 0 and tm % H == 0 and F % tf == 0
    nh = tm // H
    pg, pv = prev[:, :, :F], prev[:, :, F:]
    return pl.pallas_call(
        functools.partial(_ffn_kernel, tm=tm),
        grid=(B, T // tm, F // tf),
        in_specs=[
            pl.BlockSpec((1, tm, D), lambda b, i, f: (b, i, 0)),
            pl.BlockSpec((1, H, D), lambda b, i, f: (b, jnp.maximum(i * nh - 1, 0), 0)),
            pl.BlockSpec((1, D), lambda b, i, f: (0, 0)),
            pl.BlockSpec((D, tf), lambda b, i, f: (0, f)),
            pl.BlockSpec((D, tf), lambda b, i, f: (0, f)),
            pl.BlockSpec((8, tf), lambda b, i, f: (0, f)),
            pl.BlockSpec((8, tf), lambda b, i, f: (0, f)),
            pl.BlockSpec((1, 2, tf), lambda b, i, f: (b, 0, f)),
            pl.BlockSpec((1, 2, tf), lambda b, i, f: (b, 0, f)),
            pl.BlockSpec((tf, D), lambda b, i, f: (f, 0)),
        ],
        out_specs=pl.BlockSpec((1, tm, D), lambda b, i, f: (b, i, 0)),
        out_shape=jax.ShapeDtypeStruct((B, T, D), F32),
        scratch_shapes=[pltpu.VMEM((tm + H, D), BF16), pltpu.VMEM((tm + H, tf), F32),
                        pltpu.VMEM((tm + H, tf), F32), pltpu.VMEM((tm, D), F32)],
        compiler_params=_cparams(("parallel", "arbitrary", "arbitrary")),
        name="conv_ffn",
    )(x, x, g.reshape(1, D), wug, wuv, cg, cv, pg, pv, wd)


def _rope(x, pos, rot):
    half = rot // 2
    inv = ROPE_THETA ** (-jnp.arange(half, dtype=F32) / half)
    ang = pos.astype(F32)[:, None] * inv[None, :]
    shape = (ang.shape[0],) + (1,) * (x.ndim - 3) + (half,)
    cos = jnp.cos(ang).reshape(shape)
    sin = jnp.sin(ang).reshape(shape)
    x1 = x[..., :half]
    x2 = x[..., half:rot]
    return jnp.concatenate([x1 * cos - x2 * sin, x1 * sin + x2 * cos, x[..., rot:]], axis=-1)


def _pad_cols(w, n):
    return jnp.pad(w, ((0, 0), (0, n - w.shape[1])))


def _pad_keys(a, L):
    return jnp.pad(a, ((0, 0), (0, L - a.shape[1])) + ((0, 0),) * (a.ndim - 2))


def _dsa_layer(x, pos, past_k, past_v, past_ki, g, w_in, w_out):
    B, T, D = x.shape
    n_in = _round_up(A_IN, LANE)
    proj = _mm(x.reshape(B * T, D), _pad_cols(w_in, n_in).astype(BF16), norm_g=g).reshape(B, T, n_in)
    q = _rope(proj[..., :A_O_Q].reshape(B, T, A_HEADS, A_HEAD_DIM), pos, A_ROT)
    k = _rope(proj[..., A_O_Q:A_O_K].reshape(B, T, A_KV_HEADS, A_HEAD_DIM), pos, A_ROT)
    v = proj[..., A_O_K:A_O_V].reshape(B, T, A_KV_HEADS, A_HEAD_DIM)
    qi = _rope(proj[..., A_O_V:A_O_QI].reshape(B, T, IDX_HEADS, IDX_DIM), pos, IDX_ROT)
    ki = _rope(proj[..., A_O_QI:A_O_KI], pos, IDX_ROT)
    wi = proj[..., A_O_KI:A_IN] * ((IDX_HEADS * IDX_DIM) ** -0.5)
    P = past_k.shape[1]
    L = P + T
    Lp = _round_up(L, ATTN_KB)
    k_all = _pad_keys(jnp.concatenate([past_k, k], axis=1), Lp).astype(BF16)
    v_all = _pad_keys(jnp.concatenate([past_v, v], axis=1), Lp).astype(BF16)
    ki_all = _pad_keys(jnp.concatenate([past_ki, ki], axis=1), Lp).astype(BF16)
    top = min(TOPK_MAX, L // 4)
    group = A_HEADS // A_KV_HEADS
    qs = (q * (A_HEAD_DIM ** -0.5 * LOG2E)).astype(BF16).reshape(B, T, A_KV_HEADS, group, A_HEAD_DIM)
    o = _attn2(qs, k_all, v_all, pos0=P, tq=min(256, T),
               idx=(qi.reshape(B, T, IDX_HEADS * IDX_DIM).astype(BF16), wi, ki_all), top=top)
    x = _mm(o.reshape(B * T, A_HEADS * A_HEAD_DIM), w_out.astype(BF16),
            residual=x.reshape(B * T, D)).reshape(B, T, D)
    return x, k, v, ki


def _rwkv_layer(x, shift_prev, S0, g, mu, w_rkv, w0, w1, w2, a0, a1, a2, g1, g2, k_k, k_a, r_k, ln_w, ln_b,
                w_out):
    B, T, D = x.shape
    h = _norm(x.reshape(B * T, D), g).reshape(B, T, D)
    hp = jnp.concatenate([shift_prev[:, None, :], h[:, :-1]], axis=1)
    bf = lambda a: a.astype(BF16)
    r, k, v, wl, al, gate = _rwkv_proj(h.reshape(B * T, D), hp.reshape(B * T, D), mu, bf(w_rkv[0]), bf(w_rkv[1]),
                                       bf(w_rkv[2]), bf(w1), bf(w2), bf(a1), bf(a2), bf(g1), bf(g2))
    w_log = -jax.nn.softplus(-(w0 + wl)) - 0.5
    decay = jnp.exp(-jnp.exp(w_log))
    a = jax.nn.sigmoid(a0 + al)
    hs = lambda t: t.reshape(B, T, B_HEADS, B_HEAD_DIM)
    kk = hs(k * k_k)
    kk = kk / jnp.maximum(jnp.sqrt(jnp.sum(kk * kk, axis=-1, keepdims=True)), 1e-12)
    kk = kk.reshape(B * T, D)
    k = k * (1 + (a - 1) * k_a)
    sq = lambda t: t.reshape(B, T, D)
    y, S = _wkv(sq(r), sq(decay), sq(k), sq(v), sq(-kk), sq(kk * a), S0)
    y = hs(y)
    mean = jnp.mean(y, axis=-1, keepdims=True)
    var = jnp.mean(jnp.square(y - mean), axis=-1, keepdims=True)
    y = ((y - mean) * lax.rsqrt(var + B_GN_EPS)).reshape(B * T, D) * ln_w + ln_b
    bonus = jnp.sum(hs(r) * hs(k) * r_k, axis=-1, keepdims=True) * hs(v)
    y = y + bonus.reshape(B * T, D)
    x = _mm(y * gate, bf(w_out), residual=x.reshape(B * T, D)).reshape(B, T, D)
    return x, h[:, -1], S


def _mla_layer(x, pos, past_lat, past_rope, g, w_in, g_q, g_kv, w_uq, w_ukv, w_out):
    B, T, D = x.shape
    c_in = C_Q_RANK + C_KV_RANK + C_ROPE
    n_in = _round_up(c_in, LANE)
    proj = _mm(x.reshape(B * T, D), _pad_cols(w_in, n_in).astype(BF16), norm_g=g)
    cq, ckv, kpe = proj[:, :C_Q_RANK], proj[:, C_Q_RANK:C_Q_RANK + C_KV_RANK], proj[:, C_Q_RANK + C_KV_RANK:c_in]
    q = _mm(cq, w_uq.astype(BF16), norm_g=g_q).reshape(B, T, C_HEADS, C_NOPE + C_ROPE)
    q_pe = _rope(q[..., C_NOPE:], pos, C_ROPE)
    lat = _norm(ckv, g_kv).reshape(B, T, C_KV_RANK)
    kpe = _rope(kpe.reshape(B, T, C_ROPE), pos, C_ROPE)
    P = past_lat.shape[1]
    L = P + T
    Lp = _round_up(L, ATTN_KB)
    lat_all = _pad_keys(jnp.concatenate([past_lat, lat], axis=1), Lp)
    kpe_all = _pad_keys(jnp.concatenate([past_rope, kpe], axis=1), Lp)
    kv = _mm(lat_all.reshape(B * Lp, C_KV_RANK), w_ukv.astype(BF16), out_dtype=BF16)
    kv = kv.reshape(B, Lp, C_HEADS, C_NOPE + C_V)
    dq = LANE
    zq = jnp.zeros((B, T, C_HEADS, dq - C_NOPE - C_ROPE), F32)
    scale = (C_NOPE + C_ROPE) ** -0.5 * LOG2E
    q_cat = (jnp.concatenate([q[..., :C_NOPE], q_pe, zq], axis=-1) * scale).astype(BF16)
    zk = jnp.zeros((B, Lp, C_HEADS, dq - C_NOPE - C_ROPE), BF16)
    kpe_b = jnp.broadcast_to(kpe_all.astype(BF16)[:, :, None, :], (B, Lp, C_HEADS, C_ROPE))
    k_cat = jnp.concatenate([kv[..., :C_NOPE], kpe_b, zk], axis=-1)
    o = _attn2(q_cat.reshape(B, T, C_HEADS, 1, dq), k_cat, kv[..., C_NOPE:], pos0=P, tq=min(256, T))
    x = _mm(o.reshape(B * T, C_HEADS * C_V), w_out.astype(BF16), residual=x.reshape(B * T, D)).reshape(B, T, D)
    return x, lat, kpe


def _ffn_layer(x, prev, g, w_up, w_conv, b_conv, w_down):
    F = D_FF
    B, T, D = x.shape
    taps = jnp.concatenate([w_conv, b_conv[None, :], jnp.zeros((8 - CONV_W - 1, 2 * F), F32)], axis=0)
    w_up = w_up.astype(BF16)
    out = _ffn(x, g, w_up[:, :F], w_up[:, F:], taps[:, :F], taps[:, F:], prev, w_down.astype(BF16),
               tm=512, tf=F // 2)
    assert T >= 8
    u_last = _mm(x[:, T - 8:].reshape(B * 8, D), w_up, norm_g=g).reshape(B, 8, 2 * F)
    return out, u_last[:, 8 - (CONV_W - 1):]


def _trunk(x, pos0, st, w):
    B, T, D = x.shape
    pos = pos0 + jnp.arange(T, dtype=jnp.int32)
    new = {name: [] for name in ('a_k', 'a_v', 'a_idx', 'b_wkv', 'b_shift', 'c_lat', 'c_rope', 'ffn')}
    for i in range(DEPTH):
        j = i // N_MIXERS
        kind = i % N_MIXERS
        if kind == 0:
            x, k, v, ki = _dsa_layer(x, pos, st['a_k'][j], st['a_v'][j], st['a_idx'][j], w['n_mix'][i],
                                     w['a_w_in'][j], w['a_w_out'][j])
            new['a_k'].append(k)
            new['a_v'].append(v)
            new['a_idx'].append(ki)
        elif kind == 1:
            x, shift, S = _rwkv_layer(x, st['b_shift'][j], st['b_wkv'][j], w['n_mix'][i], w['b_mu'][j],
                                      w['b_w_rkv'][j], w['b_w0'][j], w['b_w1'][j], w['b_w2'][j], w['b_a0'][j],
                                      w['b_a1'][j], w['b_a2'][j], w['b_g1'][j], w['b_g2'][j], w['b_k_k'][j],
                                      w['b_k_a'][j], w['b_r_k'][j], w['b_ln_w'][j], w['b_ln_b'][j],
                                      w['b_w_out'][j])
            new['b_shift'].append(shift)
            new['b_wkv'].append(S)
        else:
            x, lat, kpe = _mla_layer(x, pos, st['c_lat'][j], st['c_rope'][j], w['n_mix'][i], w['c_w_in'][j],
                                     w['c_g_q'][j], w['c_g_kv'][j], w['c_w_uq'][j], w['c_w_ukv'][j],
                                     w['c_w_out'][j])
            new['c_lat'].append(lat)
            new['c_rope'].append(kpe)
        x, cbuf = _ffn_layer(x, st['ffn'][i], w['n_ffn'][i], w['f_w_up'][i], w['f_w_conv'][i],
                             w['f_b_conv'][i], w['f_w_down'][i])
        new['ffn'].append(cbuf)
    y = _norm(x.reshape(B * T, D), w['n_final']).reshape(B, T, D)
    return y, {name: jnp.stack(rows, axis=0) for name, rows in new.items()}


def kernel(x_prompt, x_sample, cache_a_k, cache_a_v, cache_a_idx, state_b_wkv, state_b_shift,
           cache_c_latent, cache_c_rope, state_ffn_conv, n_mix, n_ffn, n_final, a_w_in, a_w_out,
           b_mu, b_w_rkv, b_w0, b_w1, b_w2, b_a0, b_a1, b_a2, b_g1, b_g2, b_k_k, b_k_a, b_r_k,
           b_ln_w, b_ln_b, b_w_out, c_w_in, c_g_q, c_g_kv, c_w_uq, c_w_ukv, c_w_out,
           f_w_up, f_w_conv, f_b_conv, f_w_down):
    w = dict(n_mix=n_mix, n_ffn=n_ffn, n_final=n_final, a_w_in=a_w_in, a_w_out=a_w_out,
             b_mu=b_mu, b_w_rkv=b_w_rkv, b_w0=b_w0, b_w1=b_w1, b_w2=b_w2, b_a0=b_a0, b_a1=b_a1,
             b_a2=b_a2, b_g1=b_g1, b_g2=b_g2, b_k_k=b_k_k, b_k_a=b_k_a, b_r_k=b_r_k,
             b_ln_w=b_ln_w, b_ln_b=b_ln_b, b_w_out=b_w_out, c_w_in=c_w_in, c_g_q=c_g_q,
             c_g_kv=c_g_kv, c_w_uq=c_w_uq, c_w_ukv=c_w_ukv, c_w_out=c_w_out,
             f_w_up=f_w_up, f_w_conv=f_w_conv, f_b_conv=f_b_conv, f_w_down=f_w_down)
    Bp, Tp, D = x_prompt.shape
    n_a, n_b, n_c = cache_a_k.shape[0], state_b_wkv.shape[0], cache_c_latent.shape[0]
    st_prompt = dict(
        a_k=jnp.zeros((n_a, Bp, 0, A_KV_HEADS, A_HEAD_DIM), F32),
        a_v=jnp.zeros((n_a, Bp, 0, A_KV_HEADS, A_HEAD_DIM), F32),
        a_idx=jnp.zeros((n_a, Bp, 0, IDX_DIM), F32),
        b_wkv=jnp.zeros((n_b, Bp, B_HEADS, B_HEAD_DIM, B_HEAD_DIM), F32),
        b_shift=jnp.zeros((n_b, Bp, D), F32),
        c_lat=jnp.zeros((n_c, Bp, 0, C_KV_RANK), F32),
        c_rope=jnp.zeros((n_c, Bp, 0, C_ROPE), F32),
        ffn=jnp.zeros((DEPTH, Bp, CONV_W - 1, 2 * D_FF), F32))
    st_sample = dict(a_k=cache_a_k, a_v=cache_a_v, a_idx=cache_a_idx, b_wkv=state_b_wkv,
                     b_shift=state_b_shift, c_lat=cache_c_latent, c_rope=cache_c_rope,
                     ffn=state_ffn_conv)
    y_prompt, sp = _trunk(x_prompt, 0, st_prompt, w)
    y_sample, ss = _trunk(x_sample, cache_a_k.shape[2], st_sample, w)
    return (y_prompt, y_sample,
            sp['a_k'], ss['a_k'], sp['a_v'], ss['a_v'], sp['a_idx'], ss['a_idx'],
            sp['b_wkv'], ss['b_wkv'], sp['b_shift'], ss['b_shift'],
            sp['c_lat'], ss['c_lat'], sp['c_rope'], ss['c_rope'],
            sp['ffn'], ss['ffn'])
```

```python
import functools

import jax
import jax.numpy as jnp
from jax import lax
from jax.experimental import pallas as pl
from jax.experimental.pallas import tpu as pltpu

F32 = jnp.float32
BF16 = jnp.bfloat16

D_MODEL = 1024
DEPTH = 4
CHUNK = 64
N_MIXERS = 3
NORM_EPS = 1e-6
ROPE_THETA = 500000.0
A_HEADS, A_HEAD_DIM, A_KV_HEADS = 16, 64, 2
A_ROT = A_HEAD_DIM // 4
IDX_HEADS, IDX_DIM = 8, 64
IDX_ROT = IDX_DIM // 4
TOPK_MAX = 256
A_O_Q = A_HEADS * A_HEAD_DIM
A_O_K = A_O_Q + A_KV_HEADS * A_HEAD_DIM
A_O_V = A_O_K + A_KV_HEADS * A_HEAD_DIM
A_O_QI = A_O_V + IDX_HEADS * IDX_DIM
A_O_KI = A_O_QI + IDX_DIM
A_IN = A_O_KI + IDX_HEADS
B_HEAD_DIM = 64
B_HEADS = D_MODEL // B_HEAD_DIM
B_GN_EPS = 64e-5
C_HEADS, C_NOPE, C_ROPE, C_V = 16, 64, 32, 64
C_Q_RANK, C_KV_RANK = 512, 256
D_FF = 2816
CONV_W = 3

LANE = 128
SUBLANE_BF16 = 16
VMEM_LIMIT = 56 * 1024 * 1024
NEG_INF = float("-inf")
LOG2E = 1.4426950408889634


def _round_up(n, m):
    return (n + m - 1) // m * m


def _row_tile(M, pref):
    t = min(pref, M)
    while M % t:
        t //= 2
    return t


def _cparams(sem):
    return pltpu.CompilerParams(dimension_semantics=sem, vmem_limit_bytes=VMEM_LIMIT)


def _mm_kernel(*refs, has_norm, has_res):
    a_ref, w_ref = refs[0], refs[1]
    i = 2
    g_ref = r_ref = None
    if has_norm:
        g_ref = refs[i]
        i += 1
    if has_res:
        r_ref = refs[i]
        i += 1
    o_ref = refs[i]
    a = a_ref[...]
    if has_norm:
        af = a.astype(F32)
        a = af * lax.rsqrt(jnp.mean(af * af, axis=-1, keepdims=True) + NORM_EPS) * g_ref[...]
    acc = jnp.dot(a.astype(BF16), w_ref[...], preferred_element_type=F32)
    if has_res:
        acc = acc + r_ref[...]
    o_ref[...] = acc.astype(o_ref.dtype)


def _mm(a, w, *, norm_g=None, residual=None, out_dtype=F32, tm=512):
    M, K = a.shape
    N = w.shape[1]
    tm = _row_tile(M, tm)
    assert M % tm == 0 and N % LANE == 0
    ins = [a, w]
    specs = [pl.BlockSpec((tm, K), lambda i: (i, 0)), pl.BlockSpec((K, N), lambda i: (0, 0))]
    if norm_g is not None:
        ins.append(norm_g.reshape(1, K).astype(F32))
        specs.append(pl.BlockSpec((1, K), lambda i: (0, 0)))
    if residual is not None:
        ins.append(residual)
        specs.append(pl.BlockSpec((tm, N), lambda i: (i, 0)))
    return pl.pallas_call(
        functools.partial(_mm_kernel, has_norm=norm_g is not None, has_res=residual is not None),
        grid=(M // tm,),
        in_specs=specs,
        out_specs=pl.BlockSpec((tm, N), lambda i: (i, 0)),
        out_shape=jax.ShapeDtypeStruct((M, N), out_dtype),
        compiler_params=_cparams(("parallel",)),
        name="mm",
    )(*ins)


def _norm_kernel(x_ref, g_ref, o_ref):
    xf = x_ref[...]
    o_ref[...] = xf * lax.rsqrt(jnp.mean(xf * xf, axis=-1, keepdims=True) + NORM_EPS) * g_ref[...]


def _norm(x, g, tm=512):
    M, K = x.shape
    tm = _row_tile(M, tm)
    return pl.pallas_call(
        _norm_kernel,
        grid=(M // tm,),
        in_specs=[pl.BlockSpec((tm, K), lambda i: (i, 0)), pl.BlockSpec((1, K), lambda i: (0, 0))],
        out_specs=pl.BlockSpec((tm, K), lambda i: (i, 0)),
        out_shape=jax.ShapeDtypeStruct((M, K), F32),
        compiler_params=_cparams(("parallel",)),
        name="rmsnorm",
    )(x, g.reshape(1, K))


def _count(cond):
    return jnp.sum(jnp.where(cond, 1.0, 0.0), axis=-1, keepdims=True)


def _topk_select(score, top, key_idx):
    L = score.shape[1]
    kf = float(top)
    c0 = _count(score >= 0.0)
    neg = c0 < kf
    y = jnp.where(neg, -score, score)
    kp = jnp.where(neg, kf, float(L) - kf + 1.0)
    zero = jnp.zeros_like(c0)
    e_cur = zero
    t_cur = zero
    for b in range(7, -1, -1):
        step = 2 ** b
        cand = jnp.where(e_cur == 0.0, 2.0 ** (step - 127), t_cur * (2.0 ** step if step < 128 else 1.0))
        ok = _count(y < cand) < kp
        e_cur = jnp.where(ok, e_cur + float(step), e_cur)
        t_cur = jnp.where(ok, cand, t_cur)
    t_pow = t_cur
    for j in range(1, 24):
        cand = t_cur + t_pow * (2.0 ** -j)
        ok = _count(y < cand) < kp
        t_cur = jnp.where(ok, cand, t_cur)
    thr = jnp.where(neg, -t_cur, t_cur)
    gt = score > thr
    eq = score == thr
    need = kf - _count(gt)
    eqf = jnp.where(eq, 1.0, 0.0)
    n_eq = jnp.sum(eqf, axis=-1, keepdims=True)

    def index_cut():
        c_cur = zero
        nbits = max(1, (L - 1).bit_length())
        for b in range(nbits - 1, -1, -1):
            cand = c_cur + float(2 ** b)
            ok = jnp.sum(jnp.where(key_idx < cand, eqf, 0.0), axis=-1, keepdims=True) < need
            c_cur = jnp.where(ok, cand, c_cur)
        return c_cur

    any_split = jnp.max(jnp.where(n_eq > need, 1.0, 0.0)) > 0.0
    c_cut = lax.cond(any_split, index_cut, lambda: jnp.full_like(zero, float(L)))
    return gt | (eq & (key_idx <= c_cut))


def _attn_kernel(*refs, n_heads, group, dq, dv, tq, pos0, top, indexer, key_counts):
    if indexer:
        q_ref, k_ref, v_ref, qi_ref, wi_ref, ki_ref, o_ref = refs
    else:
        q_ref, k_ref, v_ref, o_ref = refs
    qb = pl.program_id(1)
    first = pos0 + qb * tq

    def body(L):
        row = lax.broadcasted_iota(jnp.int32, (tq, 1), 0) + first
        limit = (row & ~(CHUNK - 1)) + CHUNK
        key_i = lax.broadcasted_iota(jnp.int32, (tq, L), 1)
        valid = key_i < limit
        if indexer:
            ki = ki_ref[0, :L, :]
            qi = qi_ref[0]
            wi = wi_ref[0]
            score = jnp.zeros((tq, L), F32)
            for h in range(IDX_HEADS):
                d = lax.dot_general(qi[:, h * IDX_DIM:(h + 1) * IDX_DIM], ki, (((1,), (1,)), ((), ())),
                                    preferred_element_type=F32)
                score = score + wi[:, h:h + 1] * jnp.maximum(d, 0.0)
            score = jnp.where(valid, score, NEG_INF)
            sel = _topk_select(score, top, key_i.astype(F32))
            valid = sel & valid
        bias = jnp.where(valid, 0.0, NEG_INF)
        q = q_ref[0]
        kk = k_ref[0, :L, :]
        vv = v_ref[0, :L, :]
        outs = []
        for h in range(n_heads):
            g = h // group
            logits = lax.dot_general(q[:, h * dq:(h + 1) * dq], kk[:, g * dq:(g + 1) * dq],
                                     (((1,), (1,)), ((), ())), preferred_element_type=F32) + bias
            m = jnp.max(logits, axis=-1, keepdims=True)
            p = jnp.exp2(logits - m)
            s = jnp.sum(p, axis=-1, keepdims=True)
            o = jnp.dot(p.astype(BF16), vv[:, g * dv:(g + 1) * dv], preferred_element_type=F32)
            outs.append(o / s)
        o_ref[0] = jnp.concatenate(outs, axis=-1).astype(o_ref.dtype)

    if len(key_counts) == 1:
        body(key_counts[0])
    else:
        last_limit = ((first + tq - 1) & ~(CHUNK - 1)) + CHUNK
        lo = 0
        for L in key_counts:
            pl.when((last_limit > lo) & (last_limit <= L))(functools.partial(body, L))
            lo = L


ATTN_KEY_STEP = 512


def _attn(q, k, v, *, n_heads, group, dq, dv, pos0, tq, idx=None, top=0):
    B, T, _ = q.shape
    L = k.shape[1]
    assert T % tq == 0 and L % LANE == 0
    need = sorted({min(L, _round_up(_round_up(pos0 + (i + 1) * tq, CHUNK), ATTN_KEY_STEP)) for i in range(T // tq)})
    assert need[-1] == L or _round_up(pos0 + T, CHUNK) <= need[-1]
    assert idx is None or top <= need[0]
    ins = [q, k, v]
    specs = [pl.BlockSpec((1, tq, q.shape[2]), lambda b, i: (b, i, 0)),
             pl.BlockSpec((1, L, k.shape[2]), lambda b, i: (b, 0, 0)),
             pl.BlockSpec((1, L, v.shape[2]), lambda b, i: (b, 0, 0))]
    if idx is not None:
        qi, wi, ki = idx
        ins += [qi, wi, ki]
        specs += [pl.BlockSpec((1, tq, qi.shape[2]), lambda b, i: (b, i, 0)),
                  pl.BlockSpec((1, tq, wi.shape[2]), lambda b, i: (b, i, 0)),
                  pl.BlockSpec((1, L, ki.shape[2]), lambda b, i: (b, 0, 0))]
    return pl.pallas_call(
        functools.partial(_attn_kernel, n_heads=n_heads, group=group, dq=dq, dv=dv, tq=tq, pos0=pos0,
                          top=top, indexer=idx is not None, key_counts=tuple(need)),
        grid=(B, T // tq),
        in_specs=specs,
        out_specs=pl.BlockSpec((1, tq, n_heads * dv), lambda b, i: (b, i, 0)),
        out_shape=jax.ShapeDtypeStruct((B, T, n_heads * dv), BF16),
        compiler_params=_cparams(("parallel", "parallel")),
        name="dsa_attn" if idx is not None else "mla_attn",
    )(*ins)


ATTN_KB = 512
ATTN_KB_SHIFT = ATTN_KB.bit_length() - 1
SELECT_ALL = 1e9
ATTN_ROWS_PER_ITER = 1024


def _fold(x, op=jnp.add):
    acc = x[:, :LANE]
    for j in range(1, x.shape[1] // LANE):
        acc = op(acc, x[:, j * LANE:(j + 1) * LANE])
    return acc


def _topk_bias(tab_ref, qi_ref, wi_ref, ki_ref, SC, MS, *, nk, limit, lane_i, tq, top, wi_off):
    kb = ATTN_KB
    qi = qi_ref[0]
    wi = wi_ref[0][:, wi_off:wi_off + IDX_HEADS]
    zeros_l = jnp.zeros((tq, LANE), F32)
    zero = jnp.zeros((tq, 1), F32)
    kf = float(top)

    def lane_sum(body):
        acc = lax.fori_loop(0, nk, lambda j, a: a + _fold(body(j)), zeros_l)
        return jnp.sum(acc, axis=-1, keepdims=True)

    def ones_where(c):
        return jnp.where(c, 1.0, 0.0)

    def score_block(j):
        kij = ki_ref[0, pl.ds(pl.multiple_of(j * kb, kb), kb), :]
        sc = jnp.zeros((tq, kb), F32)
        for h in range(IDX_HEADS):
            d = lax.dot_general(qi[:, h * IDX_DIM:(h + 1) * IDX_DIM], kij, (((1,), (1,)), ((), ())),
                                preferred_element_type=F32)
            sc = sc + wi[:, h:h + 1] * jnp.maximum(d, 0.0)
        sc = jnp.where(lane_i + j * kb < limit, sc, NEG_INF)
        SC[j] = sc
        return ones_where(sc >= 0.0)

    c0 = lane_sum(score_block)
    neg = c0 < kf
    sgn = jnp.where(neg, -1.0, 1.0)
    kp = jnp.where(neg, kf, (nk * kb).astype(F32) - kf + 1.0)

    def flip(j, c):
        SC[j] = SC[j] * sgn
        return c

    lax.fori_loop(0, nk, flip, 0)

    def count_lt(cand):
        return lane_sum(lambda j: ones_where(SC[j] < cand))

    def exp_step(i, carry):
        e_cur, t_cur = carry
        cand = jnp.where(e_cur == 0.0, tab_ref[0, i], t_cur * tab_ref[1, i])
        ok = count_lt(cand) < kp
        return jnp.where(ok, e_cur + tab_ref[2, i], e_cur), jnp.where(ok, cand, t_cur)

    _, t_pow = lax.fori_loop(0, 8, exp_step, (zero, zero))

    def man_step(i, carry):
        t_cur, frac = carry
        cand = t_cur + frac
        ok = count_lt(cand) < kp
        return jnp.where(ok, cand, t_cur), frac * 0.5

    t_cur, _ = lax.fori_loop(0, 23, man_step, (t_pow, t_pow * 0.5))
    thr = t_cur * sgn

    def score(j):
        return SC[j] * sgn

    def key_idx(j):
        return (lane_i + j * kb).astype(F32)

    need = kf - lane_sum(lambda j: ones_where(score(j) > thr))
    n_eq = lane_sum(lambda j: ones_where(score(j) == thr))

    def index_cut():
        nbits = (SC.shape[0] * kb - 1).bit_length()

        def bit_step(i, carry):
            c_cur, bit = carry
            cand = c_cur + bit
            ok = lane_sum(lambda j: ones_where((score(j) == thr) & (key_idx(j) < cand))) < need
            return jnp.where(ok, cand, c_cur), bit * 0.5

        c_cur, _ = lax.fori_loop(0, nbits, bit_step, (zero, jnp.full((tq, 1), 2.0 ** (nbits - 1), F32)))
        return c_cur

    any_split = jnp.max(ones_where(n_eq > need)) > 0.0
    c_cut = lax.cond(any_split, index_cut, lambda: jnp.full((tq, 1), SELECT_ALL, F32))

    def write_bias(j, c):
        s = score(j)
        sel = (s > thr) | ((s == thr) & (key_idx(j) <= c_cut))
        MS[j] = jnp.where(sel & (lane_i + j * kb < limit), 0.0, NEG_INF)
        return c

    lax.fori_loop(0, nk, write_bias, 0)


def _attn2_kernel(*refs, n_kv, group, tq, pos0, top, indexer, wi_off):
    if indexer:
        (tab_ref, q_ref, k_ref, v_ref, x_ref, wo_ref, qi_ref, wi_ref, ki_ref, xo_ref,
         MS, LG, MACC, LACC, OACC, OH, OALL, SC) = refs
    else:
        q_ref, k_ref, v_ref, x_ref, wo_ref, xo_ref, MS, LG, MACC, LACC, OACC, OH, OALL = refs
    kb = ATTN_KB
    first = pos0 + pl.program_id(1) * tq
    last_limit = ((first + tq - 1) & ~(CHUNK - 1)) + CHUNK
    nk = (last_limit + (kb - 1)) >> ATTN_KB_SHIFT
    row = lax.broadcasted_iota(jnp.int32, (tq, 1), 0) + first
    limit = (row & ~(CHUNK - 1)) + CHUNK
    lane_i = lax.broadcasted_iota(jnp.int32, (tq, kb), 1)

    if indexer:
        _topk_bias(tab_ref, qi_ref, wi_ref, ki_ref, SC, MS, nk=nk, limit=limit, lane_i=lane_i, tq=tq, top=top,
                   wi_off=wi_off)
    else:
        def causal_bias(j, c):
            MS[j] = jnp.where(lane_i + j * kb < limit, 0.0, NEG_INF)
            return c

        lax.fori_loop(0, nk, causal_bias, 0)

    hu = LG.shape[0]

    def per_kv_heads(gi, c):
        heads = [gi * hu + u for u in range(hu)]
        qs = [q_ref[0, g, 0] for g in heads]
        MACC[...] = jnp.full(MACC.shape, NEG_INF, F32)
        LACC[...] = jnp.zeros(LACC.shape, F32)
        OACC[...] = jnp.zeros(OACC.shape, F32)

        def logits_block(j, c_):
            keys = pl.ds(pl.multiple_of(j * kb, kb), kb)
            bias = MS[j][None]
            for u, g in enumerate(heads):
                lg = lax.dot_general(qs[u], k_ref[0, g, keys, :], (((1,), (1,)), ((), ())),
                                     preferred_element_type=F32)
                lg = (lg.reshape(group, tq, kb) + bias).reshape(group * tq, kb)
                LG[u, j] = lg
                MACC[u] = jnp.maximum(MACC[u], _fold(lg, jnp.maximum))
            return c_

        lax.fori_loop(0, nk, logits_block, 0)
        ms = [jnp.max(MACC[u], axis=-1, keepdims=True) for u in range(hu)]

        def value_block(j, c_):
            keys = pl.ds(pl.multiple_of(j * kb, kb), kb)
            for u, g in enumerate(heads):
                p = jnp.exp2(LG[u, j] - ms[u])
                LACC[u] += _fold(p)
                OACC[u] += jnp.dot(p.astype(BF16), v_ref[0, g, keys, :], preferred_element_type=F32)
            return c_

        lax.fori_loop(0, nk, value_block, 0)
        for u, g in enumerate(heads):
            OH[g] = (OACC[u] / jnp.sum(LACC[u], axis=-1, keepdims=True)).astype(OH.dtype)
        return c

    lax.fori_loop(0, n_kv // hu, per_kv_heads, 0)
    dv = OH.shape[2]
    for h in range(n_kv * group):
        OALL[:, h * dv:(h + 1) * dv] = OH[h // group, (h % group) * tq:(h % group + 1) * tq, :]
    xo_ref[0] = x_ref[0] + jnp.dot(OALL[...], wo_ref[...], preferred_element_type=F32)


def _head_major(a, tq, group):
    B, T, H, d = a.shape
    n_kv = H // group
    return jnp.transpose(a.reshape(B, T // tq, tq, n_kv, group, d), (0, 3, 1, 4, 2, 5)).reshape(
        B, n_kv, T // tq, group * tq, d)


def _attn2(qg, kt, vt, x, w_out, *, pos0, tq, group, idx=None, top=0, wi_off=0):
    B, n_kv, nq, rows, dq = qg.shape
    L, dv = kt.shape[2], vt.shape[3]
    T, D = x.shape[1], x.shape[2]
    kb = ATTN_KB
    assert nq * tq == T and rows == group * tq and L % kb == 0 and (idx is None or top <= kb)
    ins = [qg, kt, vt, x, w_out]
    specs = [pl.BlockSpec((1, n_kv, 1, rows, dq), lambda b, i: (b, 0, i, 0, 0)),
             pl.BlockSpec((1, n_kv, L, dq), lambda b, i: (b, 0, 0, 0)),
             pl.BlockSpec((1, n_kv, L, dv), lambda b, i: (b, 0, 0, 0)),
             pl.BlockSpec((1, tq, D), lambda b, i: (b, i, 0)),
             pl.BlockSpec(w_out.shape, lambda b, i: (0, 0))]
    hu = max(1, min(n_kv, ATTN_ROWS_PER_ITER // rows))
    assert n_kv % hu == 0
    scratch = [pltpu.VMEM((L // kb, tq, kb), F32), pltpu.VMEM((hu, L // kb, rows, kb), F32),
               pltpu.VMEM((hu, rows, LANE), F32), pltpu.VMEM((hu, rows, LANE), F32),
               pltpu.VMEM((hu, rows, dv), F32), pltpu.VMEM((n_kv, rows, dv), BF16),
               pltpu.VMEM((tq, n_kv * group * dv), BF16)]
    if idx is not None:
        qi, wi, ki = idx
        steps = [2 ** b for b in range(7, -1, -1)]
        tab = jnp.array([[2.0 ** (s - 127) for s in steps], [2.0 ** s if s < 128 else 1.0 for s in steps],
                         [float(s) for s in steps]], F32)
        ins = [tab] + ins + [qi, wi, ki]
        specs = ([pl.BlockSpec(memory_space=pltpu.SMEM)] + specs
                 + [pl.BlockSpec((1, tq, qi.shape[2]), lambda b, i: (b, i, 0)),
                    pl.BlockSpec((1, tq, wi.shape[2]), lambda b, i: (b, i, 0)),
                    pl.BlockSpec((1, L, ki.shape[2]), lambda b, i: (b, 0, 0))])
        scratch.append(pltpu.VMEM((L // kb, tq, kb), F32))
    return pl.pallas_call(
        functools.partial(_attn2_kernel, n_kv=n_kv, group=group, tq=tq, pos0=pos0, top=top,
                          indexer=idx is not None, wi_off=wi_off),
        grid=(B, nq),
        in_specs=specs,
        out_specs=pl.BlockSpec((1, tq, D), lambda b, i: (b, i, 0)),
        out_shape=jax.ShapeDtypeStruct((B, T, D), F32),
        scratch_shapes=scratch,
        compiler_params=_cparams(("parallel", "parallel")),
        name="dsa_attn" if idx is not None else "mla_attn",
    )(*ins)


def _rope_tables(pos, dh, rot, offset=0):
    half = rot // 2
    inv = ROPE_THETA ** (-jnp.arange(half, dtype=F32) / half)
    ang = pos.astype(F32)[:, None] * inv[None, :]
    cos, sin = jnp.cos(ang), jnp.sin(ang)
    T = pos.shape[0]
    pad = lambda n, v: jnp.full((T, n), v, F32)
    zh = pad(half, 0.0)
    lo, hi = offset, dh - offset - rot
    c = jnp.concatenate([pad(lo, 1.0), cos, cos, pad(hi, 1.0)], axis=1)
    s1 = jnp.concatenate([pad(lo, 0.0), -sin, zh, pad(hi, 0.0)], axis=1)
    s2 = jnp.concatenate([pad(lo, 0.0), zh, sin, pad(hi, 0.0)], axis=1)
    return tuple(jnp.tile(t, (1, LANE // dh)) for t in (c, s1, s2))


def _rope_lanes(x, c, s1, s2, half):
    return x * c + pltpu.roll(x, LANE - half, 1) * s1 + pltpu.roll(x, half, 1) * s2


def _dsa_proj_kernel(x_ref, g_ref, w_ref, c_ref, s1_ref, s2_ref, ts_ref, qh_ref, kf_ref, kt_ref, vf_ref, vt_ref,
                     qi_ref, tail_ref, kib_ref, *, tm):
    xf = x_ref[0]
    h = xf * lax.rsqrt(jnp.mean(xf * xf, axis=-1, keepdims=True) + NORM_EPS) * g_ref[...]
    acc = jnp.dot(h.astype(BF16), w_ref[...], preferred_element_type=F32)
    c, s1, s2 = c_ref[...], s1_ref[...], s2_ref[...]
    half = A_ROT // 2
    hd = A_HEAD_DIM
    group = A_HEADS // A_KV_HEADS
    rope = lambda xs: _rope_lanes(xs, c, s1, s2, half)
    slab = lambda off: acc[:, off:off + LANE]
    for s in range(A_O_Q // LANE):
        qs = (rope(slab(s * LANE)) * (hd ** -0.5 * LOG2E)).astype(BF16)
        for e in range(LANE // hd):
            head = s * (LANE // hd) + e
            u = head % group
            qh_ref[0, head // group, 0, u * tm:(u + 1) * tm, :] = qs[:, e * hd:(e + 1) * hd]
    ks = rope(slab(A_O_Q))
    vs = slab(A_O_K)
    kf_ref[0] = ks
    vf_ref[0] = vs
    for e in range(A_KV_HEADS):
        kt_ref[0, e] = ks[:, e * hd:(e + 1) * hd].astype(BF16)
        vt_ref[0, e] = vs[:, e * hd:(e + 1) * hd].astype(BF16)
    for s in range(IDX_HEADS * IDX_DIM // LANE):
        qi_ref[0, :, s * LANE:(s + 1) * LANE] = rope(slab(A_O_V + s * LANE)).astype(BF16)
    is_key = lax.broadcasted_iota(jnp.int32, (tm, LANE), 1) < IDX_DIM
    tl = _rope_lanes(slab(A_O_QI), jnp.where(is_key, c, 1.0), jnp.where(is_key, s1, 0.0),
                     jnp.where(is_key, s2, 0.0), half) * ts_ref[...]
    tail_ref[0] = tl
    kib_ref[0] = tl[:, :IDX_DIM].astype(BF16)


def _dsa_proj(x, g, w_in, tabs, tm):
    B, T, D = x.shape
    assert (A_HEAD_DIM, A_ROT) == (IDX_DIM, IDX_ROT) and A_KV_HEADS * A_HEAD_DIM == LANE
    assert A_O_QI % LANE == 0 and IDX_DIM + IDX_HEADS <= LANE and T % tm == 0
    n_in = _round_up(A_IN, LANE)
    group = A_HEADS // A_KV_HEADS
    hd = A_HEAD_DIM
    nq = T // tm
    lanes = jnp.arange(LANE)
    tail_scale = jnp.where(lanes < IDX_DIM, 1.0, jnp.where(lanes < IDX_DIM + IDX_HEADS,
                                                            (IDX_HEADS * IDX_DIM) ** -0.5, 0.0)).astype(F32)
    row = lambda n: pl.BlockSpec((1, tm, n), lambda b, i: (b, i, 0))
    tab = pl.BlockSpec((tm, LANE), lambda b, i: (i, 0))
    const = lambda a: pl.BlockSpec(a.shape, lambda b, i: (0,) * a.ndim)
    kvh = pl.BlockSpec((1, A_KV_HEADS, tm, hd), lambda b, i: (b, 0, i, 0))
    w = _pad_cols(w_in, n_in).astype(BF16)
    g2 = g.reshape(1, D)
    ts = tail_scale.reshape(1, LANE)
    return pl.pallas_call(
        functools.partial(_dsa_proj_kernel, tm=tm),
        grid=(B, nq),
        in_specs=[row(D), const(g2), const(w), tab, tab, tab, const(ts)],
        out_specs=[pl.BlockSpec((1, A_KV_HEADS, 1, group * tm, hd), lambda b, i: (b, 0, i, 0, 0)),
                   row(LANE), kvh, row(LANE), kvh, row(IDX_HEADS * IDX_DIM), row(LANE), row(IDX_DIM)],
        out_shape=[jax.ShapeDtypeStruct((B, A_KV_HEADS, nq, group * tm, hd), BF16),
                   jax.ShapeDtypeStruct((B, T, LANE), F32), jax.ShapeDtypeStruct((B, A_KV_HEADS, T, hd), BF16),
                   jax.ShapeDtypeStruct((B, T, LANE), F32), jax.ShapeDtypeStruct((B, A_KV_HEADS, T, hd), BF16),
                   jax.ShapeDtypeStruct((B, T, IDX_HEADS * IDX_DIM), BF16),
                   jax.ShapeDtypeStruct((B, T, LANE), F32), jax.ShapeDtypeStruct((B, T, IDX_DIM), BF16)],
        compiler_params=_cparams(("parallel", "parallel")),
        name="dsa_proj",
    )(x, g2, w, *tabs, ts)


def _rms(xf, gain):
    return xf * lax.rsqrt(jnp.mean(xf * xf, axis=-1, keepdims=True) + NORM_EPS) * gain


def _mla_write_kv(lat, kpe_slab, wuk_ref, wuv_ref, kt_ref, vt_ref):
    lb = lat.astype(BF16)
    kn = jnp.dot(lb, wuk_ref[...], preferred_element_type=F32)
    vv = jnp.dot(lb, wuv_ref[...], preferred_element_type=F32)
    for h in range(C_HEADS):
        kt_ref[0, h] = (kn[:, h * LANE:(h + 1) * LANE] + kpe_slab).astype(BF16)
        vt_ref[0, h] = vv[:, h * C_V:(h + 1) * C_V].astype(BF16)


def _mla_proj_kernel(x_ref, g_ref, win_ref, gq_ref, gkv_ref, wuq_ref, wuk_ref, wuv_ref, c_ref, s1_ref, s2_ref,
                     qh_ref, kt_ref, vt_ref, lat_ref, kpe_ref):
    h = _rms(x_ref[0], g_ref[...]).astype(BF16)
    proj = jnp.dot(h, win_ref[...], preferred_element_type=F32)
    c, s1, s2 = c_ref[...], s1_ref[...], s2_ref[...]
    half = C_ROPE // 2
    q = jnp.dot(_rms(proj[:, :C_Q_RANK], gq_ref[...]).astype(BF16), wuq_ref[...], preferred_element_type=F32)
    scale = (C_NOPE + C_ROPE) ** -0.5 * LOG2E
    for hd in range(C_HEADS):
        qh_ref[0, hd, 0] = (_rope_lanes(q[:, hd * LANE:(hd + 1) * LANE], c, s1, s2, half) * scale).astype(BF16)
    lat = _rms(proj[:, C_Q_RANK:C_Q_RANK + C_KV_RANK], gkv_ref[...])
    lat_ref[0] = lat
    kpe_slab = _rope_lanes(proj[:, C_Q_RANK + C_KV_RANK:], c, s1, s2, half)
    kpe_ref[0] = kpe_slab[:, C_NOPE:C_NOPE + C_ROPE]
    _mla_write_kv(lat, kpe_slab, wuk_ref, wuv_ref, kt_ref, vt_ref)


def _mla_kv_kernel(lat_ref, kpe_ref, wuk_ref, wuv_ref, kt_ref, vt_ref):
    _mla_write_kv(lat_ref[0], kpe_ref[0], wuk_ref, wuv_ref, kt_ref, vt_ref)


def _mla_weights(w_in, w_uq, w_ukv):
    D = w_in.shape[0]
    zc = lambda rows, n: jnp.zeros((rows, n), w_in.dtype)
    w_in2 = jnp.concatenate([w_in[:, :C_Q_RANK + C_KV_RANK], zc(D, C_NOPE), w_in[:, C_Q_RANK + C_KV_RANK:],
                             zc(D, LANE - C_NOPE - C_ROPE)], axis=1)
    pad_heads = lambda w, d: jnp.pad(w.reshape(w.shape[0], C_HEADS, d), ((0, 0), (0, 0), (0, LANE - d))).reshape(
        w.shape[0], C_HEADS * LANE)
    w_uq2 = pad_heads(w_uq, C_NOPE + C_ROPE)
    ukv = w_ukv.reshape(C_KV_RANK, C_HEADS, C_NOPE + C_V)
    w_uk2 = pad_heads(ukv[..., :C_NOPE].reshape(C_KV_RANK, C_HEADS * C_NOPE), C_NOPE)
    w_uv2 = ukv[..., C_NOPE:].reshape(C_KV_RANK, C_HEADS * C_V)
    return tuple(a.astype(BF16) for a in (w_in2, w_uq2, w_uk2, w_uv2))


def _mla_proj(x, g, g_q, g_kv, weights, tabs, tm):
    B, T, D = x.shape
    w_in2, w_uq2, w_uk2, w_uv2 = weights
    nq = T // tm
    row = lambda n: pl.BlockSpec((1, tm, n), lambda b, i: (b, i, 0))
    tab = pl.BlockSpec((tm, LANE), lambda b, i: (i, 0))
    const = lambda a: pl.BlockSpec(a.shape, lambda b, i: (0,) * a.ndim)
    heads = lambda d: pl.BlockSpec((1, C_HEADS, tm, d), lambda b, i: (b, 0, i, 0))
    vec = lambda a: a.reshape(1, a.shape[0])
    return pl.pallas_call(
        _mla_proj_kernel,
        grid=(B, nq),
        in_specs=[row(D), const(vec(g)), const(w_in2), const(vec(g_q)), const(vec(g_kv)), const(w_uq2),
                  const(w_uk2), const(w_uv2), tab, tab, tab],
        out_specs=[pl.BlockSpec((1, C_HEADS, 1, tm, LANE), lambda b, i: (b, 0, i, 0, 0)), heads(LANE), heads(C_V),
                   row(C_KV_RANK), row(C_ROPE)],
        out_shape=[jax.ShapeDtypeStruct((B, C_HEADS, nq, tm, LANE), BF16),
                   jax.ShapeDtypeStruct((B, C_HEADS, T, LANE), BF16), jax.ShapeDtypeStruct((B, C_HEADS, T, C_V), BF16),
                   jax.ShapeDtypeStruct((B, T, C_KV_RANK), F32), jax.ShapeDtypeStruct((B, T, C_ROPE), F32)],
        compiler_params=_cparams(("parallel", "parallel")),
        name="mla_proj",
    )(x, vec(g), w_in2, vec(g_q), vec(g_kv), w_uq2, w_uk2, w_uv2, *tabs)


def _mla_kv(lat, kpe_slab, w_uk2, w_uv2, tm):
    B, P, _ = lat.shape
    row = lambda n: pl.BlockSpec((1, tm, n), lambda b, i: (b, i, 0))
    const = lambda a: pl.BlockSpec(a.shape, lambda b, i: (0,) * a.ndim)
    heads = lambda d: pl.BlockSpec((1, C_HEADS, tm, d), lambda b, i: (b, 0, i, 0))
    return pl.pallas_call(
        _mla_kv_kernel,
        grid=(B, P // tm),
        in_specs=[row(C_KV_RANK), row(LANE), const(w_uk2), const(w_uv2)],
        out_specs=[heads(LANE), heads(C_V)],
        out_shape=[jax.ShapeDtypeStruct((B, C_HEADS, P, LANE), BF16), jax.ShapeDtypeStruct((B, C_HEADS, P, C_V), BF16)],
        compiler_params=_cparams(("parallel", "parallel")),
        name="mla_kv",
    )(lat, kpe_slab, w_uk2, w_uv2)


def _head_sum(x, bo_ref):
    bw = bo_ref.shape[0]
    hi = x.astype(BF16)
    lo = (x - hi.astype(F32)).astype(BF16)
    bo = bo_ref[...]
    return jnp.concatenate(
        [jnp.dot(hi[:, j * bw:(j + 1) * bw], bo, preferred_element_type=F32)
         + jnp.dot(lo[:, j * bw:(j + 1) * bw], bo, preferred_element_type=F32) for j in range(x.shape[1] // bw)],
        axis=-1)


RWKV_HALO = 8


def _rwkv_pre_kernel(x_ref, xh_ref, sh_ref, g_ref, mu_ref, vec_ref, wr_ref, wk_ref, wv_ref, w1_ref, w2_ref,
                     a1_ref, a2_ref, g1_ref, g2_ref, bo_ref,
                     rp_ref, w_ref, k_ref, v_ref, a_ref, b_ref, yc_ref, bonus_ref, gate_ref, *, tm):
    i = pl.program_id(1)
    gain = g_ref[...]

    def norm(xf):
        return xf * lax.rsqrt(jnp.mean(xf * xf, axis=-1, keepdims=True) + NORM_EPS) * gain

    h = norm(x_ref[0])
    before = jnp.where(i > 0, norm(xh_ref[0])[RWKV_HALO - 1:RWKV_HALO, :], sh_ref[0])
    first = lax.broadcasted_iota(jnp.int32, (tm, 1), 0) == 0
    xx = jnp.where(first, before, pltpu.roll(h, 1, 0)) - h
    mu = mu_ref[...]
    vec = vec_ref[...]
    w0, a0, k_k, k_a, r_k = (vec[j:j + 1, :] for j in range(5))

    def mix(j):
        return (h + xx * mu[j:j + 1, :]).astype(BF16)

    dot = lambda a_, w_: jnp.dot(a_, w_[...], preferred_element_type=F32)
    r = dot(mix(0), wr_ref)
    wl = dot(jnp.tanh(dot(mix(1), w1_ref)).astype(BF16), w2_ref)
    k = dot(mix(2), wk_ref)
    v = dot(mix(3), wv_ref)
    al = dot(dot(mix(4), a1_ref).astype(BF16), a2_ref)
    gate_ref[0] = dot(jax.nn.sigmoid(dot(mix(5), g1_ref)).astype(BF16), g2_ref)
    z = -(w0 + wl)
    softplus = jnp.maximum(z, 0.0) + jnp.log(1.0 + jnp.exp(-jnp.abs(z)))
    decay = jnp.exp(-jnp.exp(-softplus - 0.5))
    a = jax.nn.sigmoid(a0 + al)
    kk = k * k_k
    kk = kk / jnp.maximum(jnp.sqrt(_head_sum(kk * kk, bo_ref)), 1e-12)
    k = k * (1.0 + (a - 1.0) * k_a)
    b = kk * a
    rp_ref[0] = decay * r - kk * _head_sum(b * r, bo_ref)
    w_ref[0] = decay
    k_ref[0] = k
    v_ref[0] = v
    a_ref[0] = -kk
    b_ref[0] = b
    yc_ref[0] = v * _head_sum(k * r, bo_ref)
    bonus_ref[0] = _head_sum(r * k * r_k, bo_ref) * v


def _rwkv_post_kernel(y_ref, yc_ref, bonus_ref, gate_ref, x_ref, ln_ref, wo_ref, bo_ref, o_ref):
    n = float(B_HEAD_DIM)
    y = y_ref[0] + yc_ref[0]
    d = y - _head_sum(y, bo_ref) / n
    var = _head_sum(d * d, bo_ref) / n
    ln = ln_ref[...]
    yn = d * lax.rsqrt(var + B_GN_EPS) * ln[0:1, :] + ln[1:2, :] + bonus_ref[0]
    o_ref[0] = x_ref[0] + jnp.dot((yn * gate_ref[0]).astype(BF16), wo_ref[...], preferred_element_type=F32)


def _block_ones():
    blk = jnp.arange(WKV_BW) // B_HEAD_DIM
    return (blk[:, None] == blk[None, :]).astype(BF16)


def _rwkv_pre(x, shift_prev, g, mu, vecs, ws, tm):
    B, T, D = x.shape
    H = RWKV_HALO
    assert T % tm == 0 and tm % H == 0
    nh = tm // H
    bo = _block_ones()
    row = pl.BlockSpec((1, tm, D), lambda b, i: (b, i, 0))
    const = lambda a: pl.BlockSpec(a.shape, lambda b, i: (0,) * a.ndim)
    pad8 = lambda a: jnp.pad(a, ((0, 8 - a.shape[0]), (0, 0)))
    mu8, vec8, g2 = pad8(mu), pad8(vecs), g.reshape(1, D)
    return pl.pallas_call(
        functools.partial(_rwkv_pre_kernel, tm=tm),
        grid=(B, T // tm),
        in_specs=[row, pl.BlockSpec((1, H, D), lambda b, i: (b, jnp.maximum(i * nh - 1, 0), 0)),
                  pl.BlockSpec((1, 1, D), lambda b, i: (b, 0, 0)), const(g2), const(mu8), const(vec8)]
                 + [const(a) for a in ws] + [const(bo)],
        out_specs=[row] * 9,
        out_shape=[jax.ShapeDtypeStruct((B, T, D), F32)] * 9,
        compiler_params=_cparams(("parallel", "parallel")),
        name="rwkv_pre",
    )(x, x, shift_prev.reshape(B, 1, D), g2, mu8, vec8, *ws, bo)


def _rwkv_post(y, yc, bonus, gate, x, ln, w_out, tm):
    B, T, D = x.shape
    bo = _block_ones()
    row = pl.BlockSpec((1, tm, D), lambda b, i: (b, i, 0))
    const = lambda a: pl.BlockSpec(a.shape, lambda b, i: (0,) * a.ndim)
    return pl.pallas_call(
        _rwkv_post_kernel,
        grid=(B, T // tm),
        in_specs=[row] * 5 + [const(ln), const(w_out), const(bo)],
        out_specs=row,
        out_shape=jax.ShapeDtypeStruct((B, T, D), F32),
        compiler_params=_cparams(("parallel", "parallel")),
        name="rwkv_post",
    )(y, yc, bonus, gate, x, ln, w_out, bo)


WKV_SUB = 8
WKV_NB = 4
WKV_BW = 2 * LANE


def _wkv_kernel(rp_ref, w_ref, k_ref, v_ref, a_ref, b_ref, s0_ref, bo_ref, dg_ref, y_ref, sT_ref, S, VB,
                *, tc, nb):
    c = pl.program_id(1)
    n = B_HEAD_DIM
    D = D_MODEL
    bw = WKV_BW
    nt = D // bw
    seg = nt * n

    @pl.when(c == 0)
    def _():
        S[...] = s0_ref[...]

    def tiles(x):
        return jnp.concatenate([x[:, j * bw:(j + 1) * bw] for j in range(nt)], axis=0)

    def untile(x):
        return jnp.concatenate([x[j * n:(j + 1) * n, :] for j in range(nt)], axis=-1)

    def block(sc, carry):
        base = pl.multiple_of(sc * WKV_SUB, WKV_SUB)
        rows = pl.ds(base, WKV_SUB)
        bo = bo_ref[...]
        dg = dg_ref[...]
        ins = []
        for bi in range(nb):
            r8, w8, k8, v8, a8, b8 = (x[bi, rows, :] for x in (rp_ref, w_ref, k_ref, v_ref, a_ref, b_ref))
            ins.append((r8, w8, k8, a8, b8))
            vd = jnp.concatenate([tiles((dg * v8[u:u + 1, :]).astype(BF16)) for u in range(WKV_SUB)], axis=0)
            VB[bi] = jnp.dot(vd, bo, preferred_element_type=F32)
        ys = [[] for _ in range(nb)]
        for u in range(WKV_SUB):
            for bi in range(nb):
                r8, w8, k8, a8, b8 = ins[bi]
                s = S[bi]
                pa = (s * a8[u:u + 1, :]).astype(BF16)
                pr = (s * r8[u:u + 1, :]).astype(BF16)
                rr = jnp.dot(jnp.concatenate([tiles(pa), tiles(pr)], axis=0), bo, preferred_element_type=F32)
                sa = untile(rr[:seg])
                yb = untile(rr[seg:])
                S[bi] = (s * w8[u:u + 1, :] + sa * b8[u:u + 1, :]
                         + untile(VB[bi, u * seg:(u + 1) * seg, :]) * k8[u:u + 1, :])
                ys[bi].append(jnp.sum(yb * dg, axis=0, keepdims=True))
        for bi in range(nb):
            y_ref[bi, rows, :] = jnp.concatenate(ys[bi], axis=0)
        return carry

    lax.fori_loop(0, tc // WKV_SUB, block, 0)

    @pl.when(c == pl.num_programs(1) - 1)
    def _():
        sT_ref[...] = S[...]


def _wkv(rp, w, k, v, a, b, s0):
    B, T, D = rp.shape
    n = B_HEAD_DIM
    H = B_HEADS
    nb = WKV_NB if B % WKV_NB == 0 else 1
    tc = _row_tile(T, 128)
    assert tc % WKV_SUB == 0
    bw = WKV_BW
    s0t = jnp.transpose(s0, (0, 2, 1, 3)).reshape(B, n, D)
    bo = _block_ones()
    dg =(jnp.arange(n)[:, None] == (jnp.arange(D) % n)[None, :]).astype(F32)
    seq = pl.BlockSpec((nb, tc, D), lambda bi, c: (bi, c, 0))
    st = pl.BlockSpec((nb, n, D), lambda bi, c: (bi, 0, 0))
    const = lambda a_: pl.BlockSpec(a_.shape, lambda bi, c: (0, 0))
    y, sT = pl.pallas_call(
        functools.partial(_wkv_kernel, tc=tc, nb=nb),
        grid=(B // nb, T // tc),
        in_specs=[seq] * 6 + [st, const(bo), const(dg)],
        out_specs=[seq, st],
        out_shape=[jax.ShapeDtypeStruct((B, T, D), F32), jax.ShapeDtypeStruct((B, n, D), F32)],
        scratch_shapes=[pltpu.VMEM((nb, n, D), F32), pltpu.VMEM((nb, WKV_SUB * (D // bw) * n, bw), F32)],
        compiler_params=_cparams(("parallel", "arbitrary")),
        name="wkv",
    )(rp, w, k, v, a, b, s0t, bo, dg)
    return y, jnp.transpose(sT.reshape(B, n, H, n), (0, 2, 1, 3))


FFN_HALO = SUBLANE_BF16


def _ffn_kernel(x_ref, xh_ref, g_ref, wug_ref, wuv_ref, cg_ref, cv_ref, pg_ref, pv_ref, wd_ref,
                o_ref, hn, ug, uv, acc, *, tm):
    i = pl.program_id(1)
    f = pl.program_id(2)
    H = FFN_HALO

    def norm(xf):
        return xf * lax.rsqrt(jnp.mean(xf * xf, axis=-1, keepdims=True) + NORM_EPS) * g_ref[...]

    @pl.when(f == 0)
    def _():
        halo = jnp.where(i > 0, norm(xh_ref[0]), 0.0)
        hn[0:H, :] = halo.astype(BF16)
        hn[H:H + tm, :] = norm(x_ref[0]).astype(BF16)
        acc[...] = jnp.zeros_like(acc)

    h = hn[...]
    ug[...] = jnp.dot(h, wug_ref[...], preferred_element_type=F32)
    uv[...] = jnp.dot(h, wuv_ref[...], preferred_element_type=F32)

    @pl.when(i == 0)
    def _():
        ug[H - 2:H, :] = pg_ref[0]
        uv[H - 2:H, :] = pv_ref[0]

    def conv(u, c_ref):
        cw = c_ref[...]
        return (cw[3:4, :] + u[H - 2:H - 2 + tm, :] * cw[0:1, :] + u[H - 1:H - 1 + tm, :] * cw[1:2, :]
                + u[H:H + tm, :] * cw[2:3, :])

    gate = conv(ug, cg_ref)
    val = conv(uv, cv_ref)
    act = (gate * jax.nn.sigmoid(gate) * val).astype(BF16)
    acc[...] += jnp.dot(act, wd_ref[...], preferred_element_type=F32)

    @pl.when(f == pl.num_programs(2) - 1)
    def _():
        o_ref[0] = x_ref[0] + acc[...]


def _ffn(x, g, wug, wuv, cg, cv, prev, wd, *, tm, tf):
    B, T, D = x.shape
    F = D_FF
    H = FFN_HALO
    tm = min(tm, T)
    assert T % tm == 0 and tm % H == 0 and F % tf == 0
    nh = tm // H
    pg, pv = prev[:, :, :F], prev[:, :, F:]
    return pl.pallas_call(
        functools.partial(_ffn_kernel, tm=tm),
        grid=(B, T // tm, F // tf),
        in_specs=[
            pl.BlockSpec((1, tm, D), lambda b, i, f: (b, i, 0)),
            pl.BlockSpec((1, H, D), lambda b, i, f: (b, jnp.maximum(i * nh - 1, 0), 0)),
            pl.BlockSpec((1, D), lambda b, i, f: (0, 0)),
            pl.BlockSpec((D, tf), lambda b, i, f: (0, f)),
            pl.BlockSpec((D, tf), lambda b, i, f: (0, f)),
            pl.BlockSpec((8, tf), lambda b, i, f: (0, f)),
            pl.BlockSpec((8, tf), lambda b, i, f: (0, f)),
            pl.BlockSpec((1, 2, tf), lambda b, i, f: (b, 0, f)),
            pl.BlockSpec((1, 2, tf), lambda b, i, f: (b, 0, f)),
            pl.BlockSpec((tf, D), lambda b, i, f: (f, 0)),
        ],
        out_specs=pl.BlockSpec((1, tm, D), lambda b, i, f: (b, i, 0)),
        out_shape=jax.ShapeDtypeStruct((B, T, D), F32),
        scratch_shapes=[pltpu.VMEM((tm + H, D), BF16), pltpu.VMEM((tm + H, tf), F32),
                        pltpu.VMEM((tm + H, tf), F32), pltpu.VMEM((tm, D), F32)],
        compiler_params=_cparams(("parallel", "arbitrary", "arbitrary")),
        name="conv_ffn",
    )(x, x, g.reshape(1, D), wug, wuv, cg, cv, pg, pv, wd)


def _rope(x, pos, rot):
    half = rot // 2
    inv = ROPE_THETA ** (-jnp.arange(half, dtype=F32) / half)
    ang = pos.astype(F32)[:, None] * inv[None, :]
    shape = (ang.shape[0],) + (1,) * (x.ndim - 3) + (half,)
    cos = jnp.cos(ang).reshape(shape)
    sin = jnp.sin(ang).reshape(shape)
    x1 = x[..., :half]
    x2 = x[..., half:rot]
    return jnp.concatenate([x1 * cos - x2 * sin, x1 * sin + x2 * cos, x[..., rot:]], axis=-1)


def _pad_cols(w, n):
    return jnp.pad(w, ((0, 0), (0, n - w.shape[1])))


def _pad_keys(a, L):
    return jnp.pad(a, ((0, 0), (0, L - a.shape[1])) + ((0, 0),) * (a.ndim - 2))


def _dsa_layer(x, pos, past_k, past_v, past_ki, g, w_in, w_out):
    B, T, D = x.shape
    tq = min(256, T)
    qh, kf, kt, vf, vt, qi, tail, kib = _dsa_proj(x, g, w_in, _rope_tables(pos, A_HEAD_DIM, A_ROT), tq)
    P = past_k.shape[1]
    L = P + T
    Lp = _round_up(L, ATTN_KB)
    if Lp != T:
        past = lambda a: jnp.transpose(a, (0, 2, 1, 3)).astype(BF16)
        kt = jnp.pad(jnp.concatenate([past(past_k), kt], axis=2), ((0, 0), (0, 0), (0, Lp - L), (0, 0)))
        vt = jnp.pad(jnp.concatenate([past(past_v), vt], axis=2), ((0, 0), (0, 0), (0, Lp - L), (0, 0)))
        kib = _pad_keys(jnp.concatenate([past_ki.astype(BF16), kib], axis=1), Lp)
    x = _attn2(qh, kt, vt, x, w_out.astype(BF16), pos0=P, tq=tq, group=A_HEADS // A_KV_HEADS,
               idx=(qi, tail, kib), top=min(TOPK_MAX, L // 4), wi_off=IDX_DIM)
    kv_rows = lambda a: a.reshape(B, T, A_KV_HEADS, A_HEAD_DIM)
    return x, kv_rows(kf), kv_rows(vf), tail[..., :IDX_DIM]


def _rwkv_layer(x, shift_prev, S0, g, mu, w_rkv, w0, w1, w2, a0, a1, a2, g1, g2, k_k, k_a, r_k, ln_w, ln_b,
                w_out):
    B, T, D = x.shape
    tm = min(256, T)
    bf = lambda a: a.astype(BF16)
    vecs = jnp.stack([w0, a0, k_k, k_a, r_k.reshape(D)], axis=0)
    ws = [bf(w_rkv[0]), bf(w_rkv[1]), bf(w_rkv[2]), bf(w1), bf(w2), bf(a1), bf(a2), bf(g1), bf(g2)]
    rp, decay, k, v, a_vec, b_vec, yc, bonus, gate = _rwkv_pre(x, shift_prev, g, mu, vecs, ws, tm)
    y, S = _wkv(rp, decay, k, v, a_vec, b_vec, S0)
    ln = jnp.pad(jnp.stack([ln_w, ln_b], axis=0), ((0, 6), (0, 0)))
    x_new = _rwkv_post(y, yc, bonus, gate, x, ln, bf(w_out), tm)
    assert T >= 8
    shift = _norm(x[:, T - 8:].reshape(B * 8, D), g).reshape(B, 8, D)[:, -1]
    return x_new, shift, S


def _mla_layer(x, pos, past_lat, past_rope, g, w_in, g_q, g_kv, w_uq, w_ukv, w_out):
    B, T, D = x.shape
    tq = min(256, T)
    weights = _mla_weights(w_in, w_uq, w_ukv)
    tabs = _rope_tables(pos, LANE, C_ROPE, offset=C_NOPE)
    qh, kt, vt, lat, kpe = _mla_proj(x, g, g_q, g_kv, weights, tabs, tq)
    P = past_lat.shape[1]
    L = P + T
    Lp = _round_up(L, ATTN_KB)
    if Lp != T:
        past_slab = jnp.pad(past_rope, ((0, 0), (0, 0), (C_NOPE, LANE - C_NOPE - C_ROPE)))
        kt_p, vt_p = _mla_kv(past_lat, past_slab, weights[2], weights[3], _row_tile(P, 256))
        kt = jnp.pad(jnp.concatenate([kt_p, kt], axis=2), ((0, 0), (0, 0), (0, Lp - L), (0, 0)))
        vt = jnp.pad(jnp.concatenate([vt_p, vt], axis=2), ((0, 0), (0, 0), (0, Lp - L), (0, 0)))
    x = _attn2(qh, kt, vt, x, w_out.astype(BF16), pos0=P, tq=tq, group=1)
    return x, lat, kpe


def _ffn_layer(x, prev, g, w_up, w_conv, b_conv, w_down):
    F = D_FF
    B, T, D = x.shape
    taps = jnp.concatenate([w_conv, b_conv[None, :], jnp.zeros((8 - CONV_W - 1, 2 * F), F32)], axis=0)
    w_up = w_up.astype(BF16)
    out = _ffn(x, g, w_up[:, :F], w_up[:, F:], taps[:, :F], taps[:, F:], prev, w_down.astype(BF16),
               tm=512, tf=F // 2)
    assert T >= 8
    u_last = _mm(x[:, T - 8:].reshape(B * 8, D), w_up, norm_g=g).reshape(B, 8, 2 * F)
    return out, u_last[:, 8 - (CONV_W - 1):]


def _trunk(x, pos0, st, w):
    B, T, D = x.shape
    pos = pos0 + jnp.arange(T, dtype=jnp.int32)
    new = {name: [] for name in ('a_k', 'a_v', 'a_idx', 'b_wkv', 'b_shift', 'c_lat', 'c_rope', 'ffn')}
    for i in range(DEPTH):
        j = i // N_MIXERS
        kind = i % N_MIXERS
        if kind == 0:
            x, k, v, ki = _dsa_layer(x, pos, st['a_k'][j], st['a_v'][j], st['a_idx'][j], w['n_mix'][i],
                                     w['a_w_in'][j], w['a_w_out'][j])
            new['a_k'].append(k)
            new['a_v'].append(v)
            new['a_idx'].append(ki)
        elif kind == 1:
            x, shift, S = _rwkv_layer(x, st['b_shift'][j], st['b_wkv'][j], w['n_mix'][i], w['b_mu'][j],
                                      w['b_w_rkv'][j], w['b_w0'][j], w['b_w1'][j], w['b_w2'][j], w['b_a0'][j],
                                      w['b_a1'][j], w['b_a2'][j], w['b_g1'][j], w['b_g2'][j], w['b_k_k'][j],
                                      w['b_k_a'][j], w['b_r_k'][j], w['b_ln_w'][j], w['b_ln_b'][j],
                                      w['b_w_out'][j])
            new['b_shift'].append(shift)
            new['b_wkv'].append(S)
        else:
            x, lat, kpe = _mla_layer(x, pos, st['c_lat'][j], st['c_rope'][j], w['n_mix'][i], w['c_w_in'][j],
                                     w['c_g_q'][j], w['c_g_kv'][j], w['c_w_uq'][j], w['c_w_ukv'][j],
                                     w['c_w_out'][j])
            new['c_lat'].append(lat)
            new['c_rope'].append(kpe)
        x, cbuf = _ffn_layer(x, st['ffn'][i], w['n_ffn'][i], w['f_w_up'][i], w['f_w_conv'][i],
                             w['f_b_conv'][i], w['f_w_down'][i])
        new['ffn'].append(cbuf)
    y = _norm(x.reshape(B * T, D), w['n_final']).reshape(B, T, D)
    return y, {name: jnp.stack(rows, axis=0) for name, rows in new.items()}


def kernel(x_prompt, x_sample, cache_a_k, cache_a_v, cache_a_idx, state_b_wkv, state_b_shift,
           cache_c_latent, cache_c_rope, state_ffn_conv, n_mix, n_ffn, n_final, a_w_in, a_w_out,
           b_mu, b_w_rkv, b_w0, b_w1, b_w2, b_a0, b_a1, b_a2, b_g1, b_g2, b_k_k, b_k_a, b_r_k,
           b_ln_w, b_ln_b, b_w_out, c_w_in, c_g_q, c_g_kv, c_w_uq, c_w_ukv, c_w_out,
           f_w_up, f_w_conv, f_b_conv, f_w_down):
    w = dict(n_mix=n_mix, n_ffn=n_ffn, n_final=n_final, a_w_in=a_w_in, a_w_out=a_w_out,
             b_mu=b_mu, b_w_rkv=b_w_rkv, b_w0=b_w0, b_w1=b_w1, b_w2=b_w2, b_a0=b_a0, b_a1=b_a1,
             b_a2=b_a2, b_g1=b_g1, b_g2=b_g2, b_k_k=b_k_k, b_k_a=b_k_a, b_r_k=b_r_k,
             b_ln_w=b_ln_w, b_ln_b=b_ln_b, b_w_out=b_w_out, c_w_in=c_w_in, c_g_q=c_g_q,
             c_g_kv=c_g_kv, c_w_uq=c_w_uq, c_w_ukv=c_w_ukv, c_w_out=c_w_out,
             f_w_up=f_w_up, f_w_conv=f_w_conv, f_b_conv=f_b_conv, f_w_down=f_w_down)
    Bp, Tp, D = x_prompt.shape
    n_a, n_b, n_c = cache_a_k.shape[0], state_b_wkv.shape[0], cache_c_latent.shape[0]
    st_prompt = dict(
        a_k=jnp.zeros((n_a, Bp, 0, A_KV_HEADS, A_HEAD_DIM), F32),
        a_v=jnp.zeros((n_a, Bp, 0, A_KV_HEADS, A_HEAD_DIM), F32),
        a_idx=jnp.zeros((n_a, Bp, 0, IDX_DIM), F32),
        b_wkv=jnp.zeros((n_b, Bp, B_HEADS, B_HEAD_DIM, B_HEAD_DIM), F32),
        b_shift=jnp.zeros((n_b, Bp, D), F32),
        c_lat=jnp.zeros((n_c, Bp, 0, C_KV_RANK), F32),
        c_rope=jnp.zeros((n_c, Bp, 0, C_ROPE), F32),
        ffn=jnp.zeros((DEPTH, Bp, CONV_W - 1, 2 * D_FF), F32))
    st_sample = dict(a_k=cache_a_k, a_v=cache_a_v, a_idx=cache_a_idx, b_wkv=state_b_wkv,
                     b_shift=state_b_shift, c_lat=cache_c_latent, c_rope=cache_c_rope,
                     ffn=state_ffn_conv)
    y_prompt, sp = _trunk(x_prompt, 0, st_prompt, w)
    y_sample, ss = _trunk(x_sample, cache_a_k.shape[2], st_sample, w)
    return (y_prompt, y_sample,
            sp['a_k'], ss['a_k'], sp['a_v'], ss['a_v'], sp['a_idx'], ss['a_idx'],
            sp['b_wkv'], ss['b_wkv'], sp['b_shift'], ss['b_shift'],
            sp['c_lat'], ss['c_lat'], sp['c_rope'], ss['c_rope'],
            sp['ffn'], ss['ffn'])
```

```python
import functools

import jax
import jax.numpy as jnp
from jax import lax
from jax.experimental import pallas as pl
from jax.experimental.pallas import tpu as pltpu

F32 = jnp.float32
BF16 = jnp.bfloat16

D_MODEL = 1024
DEPTH = 4
CHUNK = 64
N_MIXERS = 3
NORM_EPS = 1e-6
ROPE_THETA = 500000.0
A_HEADS, A_HEAD_DIM, A_KV_HEADS = 16, 64, 2
A_ROT = A_HEAD_DIM // 4
IDX_HEADS, IDX_DIM = 8, 64
IDX_ROT = IDX_DIM // 4
TOPK_MAX = 256
A_O_Q = A_HEADS * A_HEAD_DIM
A_O_K = A_O_Q + A_KV_HEADS * A_HEAD_DIM
A_O_V = A_O_K + A_KV_HEADS * A_HEAD_DIM
A_O_QI = A_O_V + IDX_HEADS * IDX_DIM
A_O_KI = A_O_QI + IDX_DIM
A_IN = A_O_KI + IDX_HEADS
B_HEAD_DIM = 64
B_HEADS = D_MODEL // B_HEAD_DIM
B_GN_EPS = 64e-5
C_HEADS, C_NOPE, C_ROPE, C_V = 16, 64, 32, 64
C_Q_RANK, C_KV_RANK = 512, 256
D_FF = 2816
CONV_W = 3

LANE = 128
SUBLANE_BF16 = 16
VMEM_LIMIT = 56 * 1024 * 1024
NEG_INF = float("-inf")
LOG2E = 1.4426950408889634


def _round_up(n, m):
    return (n + m - 1) // m * m


def _row_tile(M, pref):
    t = min(pref, M)
    while M % t:
        t //= 2
    return t


def _cparams(sem):
    return pltpu.CompilerParams(dimension_semantics=sem, vmem_limit_bytes=VMEM_LIMIT)


def _mm_kernel(*refs, has_norm, has_res):
    a_ref, w_ref = refs[0], refs[1]
    i = 2
    g_ref = r_ref = None
    if has_norm:
        g_ref = refs[i]
        i += 1
    if has_res:
        r_ref = refs[i]
        i += 1
    o_ref = refs[i]
    a = a_ref[...]
    if has_norm:
        af = a.astype(F32)
        a = af * lax.rsqrt(jnp.mean(af * af, axis=-1, keepdims=True) + NORM_EPS) * g_ref[...]
    acc = jnp.dot(a.astype(BF16), w_ref[...], preferred_element_type=F32)
    if has_res:
        acc = acc + r_ref[...]
    o_ref[...] = acc.astype(o_ref.dtype)


def _mm(a, w, *, norm_g=None, residual=None, out_dtype=F32, tm=512):
    M, K = a.shape
    N = w.shape[1]
    tm = _row_tile(M, tm)
    assert M % tm == 0 and N % LANE == 0
    ins = [a, w]
    specs = [pl.BlockSpec((tm, K), lambda i: (i, 0)), pl.BlockSpec((K, N), lambda i: (0, 0))]
    if norm_g is not None:
        ins.append(norm_g.reshape(1, K).astype(F32))
        specs.append(pl.BlockSpec((1, K), lambda i: (0, 0)))
    if residual is not None:
        ins.append(residual)
        specs.append(pl.BlockSpec((tm, N), lambda i: (i, 0)))
    return pl.pallas_call(
        functools.partial(_mm_kernel, has_norm=norm_g is not None, has_res=residual is not None),
        grid=(M // tm,),
        in_specs=specs,
        out_specs=pl.BlockSpec((tm, N), lambda i: (i, 0)),
        out_shape=jax.ShapeDtypeStruct((M, N), out_dtype),
        compiler_params=_cparams(("parallel",)),
        name="mm",
    )(*ins)


def _norm_kernel(x_ref, g_ref, o_ref):
    xf = x_ref[...]
    o_ref[...] = xf * lax.rsqrt(jnp.mean(xf * xf, axis=-1, keepdims=True) + NORM_EPS) * g_ref[...]


def _norm(x, g, tm=512):
    M, K = x.shape
    tm = _row_tile(M, tm)
    return pl.pallas_call(
        _norm_kernel,
        grid=(M // tm,),
        in_specs=[pl.BlockSpec((tm, K), lambda i: (i, 0)), pl.BlockSpec((1, K), lambda i: (0, 0))],
        out_specs=pl.BlockSpec((tm, K), lambda i: (i, 0)),
        out_shape=jax.ShapeDtypeStruct((M, K), F32),
        compiler_params=_cparams(("parallel",)),
        name="rmsnorm",
    )(x, g.reshape(1, K))


def _count(cond):
    return jnp.sum(jnp.where(cond, 1.0, 0.0), axis=-1, keepdims=True)


def _topk_select(score, top, key_idx):
    L = score.shape[1]
    kf = float(top)
    c0 = _count(score >= 0.0)
    neg = c0 < kf
    y = jnp.where(neg, -score, score)
    kp = jnp.where(neg, kf, float(L) - kf + 1.0)
    zero = jnp.zeros_like(c0)
    e_cur = zero
    t_cur = zero
    for b in range(7, -1, -1):
        step = 2 ** b
        cand = jnp.where(e_cur == 0.0, 2.0 ** (step - 127), t_cur * (2.0 ** step if step < 128 else 1.0))
        ok = _count(y < cand) < kp
        e_cur = jnp.where(ok, e_cur + float(step), e_cur)
        t_cur = jnp.where(ok, cand, t_cur)
    t_pow = t_cur
    for j in range(1, 24):
        cand = t_cur + t_pow * (2.0 ** -j)
        ok = _count(y < cand) < kp
        t_cur = jnp.where(ok, cand, t_cur)
    thr = jnp.where(neg, -t_cur, t_cur)
    gt = score > thr
    eq = score == thr
    need = kf - _count(gt)
    eqf = jnp.where(eq, 1.0, 0.0)
    n_eq = jnp.sum(eqf, axis=-1, keepdims=True)

    def index_cut():
        c_cur = zero
        nbits = max(1, (L - 1).bit_length())
        for b in range(nbits - 1, -1, -1):
            cand = c_cur + float(2 ** b)
            ok = jnp.sum(jnp.where(key_idx < cand, eqf, 0.0), axis=-1, keepdims=True) < need
            c_cur = jnp.where(ok, cand, c_cur)
        return c_cur

    any_split = jnp.max(jnp.where(n_eq > need, 1.0, 0.0)) > 0.0
    c_cut = lax.cond(any_split, index_cut, lambda: jnp.full_like(zero, float(L)))
    return gt | (eq & (key_idx <= c_cut))


def _attn_kernel(*refs, n_heads, group, dq, dv, tq, pos0, top, indexer, key_counts):
    if indexer:
        q_ref, k_ref, v_ref, qi_ref, wi_ref, ki_ref, o_ref = refs
    else:
        q_ref, k_ref, v_ref, o_ref = refs
    qb = pl.program_id(1)
    first = pos0 + qb * tq

    def body(L):
        row = lax.broadcasted_iota(jnp.int32, (tq, 1), 0) + first
        limit = (row & ~(CHUNK - 1)) + CHUNK
        key_i = lax.broadcasted_iota(jnp.int32, (tq, L), 1)
        valid = key_i < limit
        if indexer:
            ki = ki_ref[0, :L, :]
            qi = qi_ref[0]
            wi = wi_ref[0]
            score = jnp.zeros((tq, L), F32)
            for h in range(IDX_HEADS):
                d = lax.dot_general(qi[:, h * IDX_DIM:(h + 1) * IDX_DIM], ki, (((1,), (1,)), ((), ())),
                                    preferred_element_type=F32)
                score = score + wi[:, h:h + 1] * jnp.maximum(d, 0.0)
            score = jnp.where(valid, score, NEG_INF)
            sel = _topk_select(score, top, key_i.astype(F32))
            valid = sel & valid
        bias = jnp.where(valid, 0.0, NEG_INF)
        q = q_ref[0]
        kk = k_ref[0, :L, :]
        vv = v_ref[0, :L, :]
        outs = []
        for h in range(n_heads):
            g = h // group
            logits = lax.dot_general(q[:, h * dq:(h + 1) * dq], kk[:, g * dq:(g + 1) * dq],
                                     (((1,), (1,)), ((), ())), preferred_element_type=F32) + bias
            m = jnp.max(logits, axis=-1, keepdims=True)
            p = jnp.exp2(logits - m)
            s = jnp.sum(p, axis=-1, keepdims=True)
            o = jnp.dot(p.astype(BF16), vv[:, g * dv:(g + 1) * dv], preferred_element_type=F32)
            outs.append(o / s)
        o_ref[0] = jnp.concatenate(outs, axis=-1).astype(o_ref.dtype)

    if len(key_counts) == 1:
        body(key_counts[0])
    else:
        last_limit = ((first + tq - 1) & ~(CHUNK - 1)) + CHUNK
        lo = 0
        for L in key_counts:
            pl.when((last_limit > lo) & (last_limit <= L))(functools.partial(body, L))
            lo = L


ATTN_KEY_STEP = 512


def _attn(q, k, v, *, n_heads, group, dq, dv, pos0, tq, idx=None, top=0):
    B, T, _ = q.shape
    L = k.shape[1]
    assert T % tq == 0 and L % LANE == 0
    need = sorted({min(L, _round_up(_round_up(pos0 + (i + 1) * tq, CHUNK), ATTN_KEY_STEP)) for i in range(T // tq)})
    assert need[-1] == L or _round_up(pos0 + T, CHUNK) <= need[-1]
    assert idx is None or top <= need[0]
    ins = [q, k, v]
    specs = [pl.BlockSpec((1, tq, q.shape[2]), lambda b, i: (b, i, 0)),
             pl.BlockSpec((1, L, k.shape[2]), lambda b, i: (b, 0, 0)),
             pl.BlockSpec((1, L, v.shape[2]), lambda b, i: (b, 0, 0))]
    if idx is not None:
        qi, wi, ki = idx
        ins += [qi, wi, ki]
        specs += [pl.BlockSpec((1, tq, qi.shape[2]), lambda b, i: (b, i, 0)),
                  pl.BlockSpec((1, tq, wi.shape[2]), lambda b, i: (b, i, 0)),
                  pl.BlockSpec((1, L, ki.shape[2]), lambda b, i: (b, 0, 0))]
    return pl.pallas_call(
        functools.partial(_attn_kernel, n_heads=n_heads, group=group, dq=dq, dv=dv, tq=tq, pos0=pos0,
                          top=top, indexer=idx is not None, key_counts=tuple(need)),
        grid=(B, T // tq),
        in_specs=specs,
        out_specs=pl.BlockSpec((1, tq, n_heads * dv), lambda b, i: (b, i, 0)),
        out_shape=jax.ShapeDtypeStruct((B, T, n_heads * dv), BF16),
        compiler_params=_cparams(("parallel", "parallel")),
        name="dsa_attn" if idx is not None else "mla_attn",
    )(*ins)


ATTN_KB = 512
ATTN_KB_SHIFT = ATTN_KB.bit_length() - 1
SELECT_ALL = 1e9
MANTISSA_BITS = 23
ATTN_ROWS_PER_ITER = 1024


def _fold(x, op=jnp.add):
    acc = x[:, :LANE]
    for j in range(1, x.shape[1] // LANE):
        acc = op(acc, x[:, j * LANE:(j + 1) * LANE])
    return acc


def _topk_bias(tab_ref, qi_ref, wi_ref, ki_ref, SC, MS, *, nk, limit, lane_i, tq, top, wi_off):
    kb = ATTN_KB
    qi = qi_ref[0]
    wi = wi_ref[0][:, wi_off:wi_off + IDX_HEADS]
    zeros_l = jnp.zeros((tq, LANE), F32)
    zero = jnp.zeros((tq, 1), F32)
    kf = float(top)

    def lane_sum(body):
        acc = lax.fori_loop(0, nk, lambda j, a: a + _fold(body(j)), zeros_l)
        return jnp.sum(acc, axis=-1, keepdims=True)

    def ones_where(c):
        return jnp.where(c, 1.0, 0.0)

    def score_block(j):
        kij = ki_ref[0, pl.ds(pl.multiple_of(j * kb, kb), kb), :]
        sc = jnp.zeros((tq, kb), F32)
        for h in range(IDX_HEADS):
            d = lax.dot_general(qi[:, h * IDX_DIM:(h + 1) * IDX_DIM], kij, (((1,), (1,)), ((), ())),
                                preferred_element_type=F32)
            sc = sc + wi[:, h:h + 1] * jnp.maximum(d, 0.0)
        sc = jnp.where(lane_i + j * kb < limit, sc, NEG_INF)
        SC[j] = sc
        return ones_where(sc >= 0.0)

    c0 = lane_sum(score_block)
    neg = c0 < kf
    sgn = jnp.where(neg, -1.0, 1.0)
    kp = jnp.where(neg, kf, (nk * kb).astype(F32) - kf + 1.0)

    def flip(j, c):
        SC[j] = SC[j] * sgn
        return c

    lax.fori_loop(0, nk, flip, 0)

    def count_lt(cand):
        return lane_sum(lambda j: ones_where(SC[j] < cand))

    def exp_step(i, carry):
        e_cur, t_cur = carry
        cand = jnp.where(e_cur == 0.0, tab_ref[0, i], t_cur * tab_ref[1, i])
        ok = count_lt(cand) < kp
        return jnp.where(ok, e_cur + tab_ref[2, i], e_cur), jnp.where(ok, cand, t_cur)

    _, t_pow = lax.fori_loop(0, 8, exp_step, (zero, zero))

    def man_step(i, carry):
        t_cur, frac = carry
        cand = t_cur + frac
        ok = count_lt(cand) < kp
        return jnp.where(ok, cand, t_cur), frac * 0.5

    t_cur, _ = lax.fori_loop(0, MANTISSA_BITS, man_step, (t_pow, t_pow * 0.5))
    thr = t_cur * sgn

    def score(j):
        return SC[j] * sgn

    def key_idx(j):
        return (lane_i + j * kb).astype(F32)

    need = kf - lane_sum(lambda j: ones_where(score(j) > thr))
    n_eq = lane_sum(lambda j: ones_where(score(j) == thr))

    def index_cut():
        nbits = (SC.shape[0] * kb - 1).bit_length()

        def bit_step(i, carry):
            c_cur, bit = carry
            cand = c_cur + bit
            ok = lane_sum(lambda j: ones_where((score(j) == thr) & (key_idx(j) < cand))) < need
            return jnp.where(ok, cand, c_cur), bit * 0.5

        c_cur, _ = lax.fori_loop(0, nbits, bit_step, (zero, jnp.full((tq, 1), 2.0 ** (nbits - 1), F32)))
        return c_cur

    any_split = jnp.max(ones_where(n_eq > need)) > 0.0
    c_cut = lax.cond(any_split, index_cut, lambda: jnp.full((tq, 1), SELECT_ALL, F32))

    def write_bias(j, c):
        s = score(j)
        sel = (s > thr) | ((s == thr) & (key_idx(j) <= c_cut))
        MS[j] = jnp.where(sel & (lane_i + j * kb < limit), 0.0, NEG_INF)
        return c

    lax.fori_loop(0, nk, write_bias, 0)


def _topk_bias_t(tab_ref, qi_ref, wi_ref, ki_ref, SC, MS, *, nk, first, tq, top, wi_off):
    kb = ATTN_KB
    qi = qi_ref[0]
    w_t = jnp.transpose(wi_ref[0])[wi_off:wi_off + IDX_HEADS, :]
    part = 64
    zeros_p = jnp.zeros((part, tq), F32)
    zero = jnp.zeros((1, tq), F32)
    kf = float(top)
    pos_q = lax.broadcasted_iota(jnp.int32, (1, tq), 1) + first
    limit = (pos_q & ~(CHUNK - 1)) + CHUNK
    key_i = lax.broadcasted_iota(jnp.int32, (kb, tq), 0)

    def key_sum(body):
        def step(j, a):
            return a + jnp.sum(body(j).reshape(kb // part, part, tq), axis=0)

        return jnp.sum(lax.fori_loop(0, nk, step, zeros_p), axis=0, keepdims=True)

    def ones_where(c):
        return jnp.where(c, 1.0, 0.0)

    def score_block(j):
        kij = ki_ref[0, pl.ds(pl.multiple_of(j * kb, kb), kb), :]
        sc = jnp.zeros((kb, tq), F32)
        for h in range(IDX_HEADS):
            d = lax.dot_general(kij, qi[:, h * IDX_DIM:(h + 1) * IDX_DIM], (((1,), (1,)), ((), ())),
                                preferred_element_type=F32)
            sc = sc + w_t[h:h + 1, :] * jnp.maximum(d, 0.0)
        sc = jnp.where(key_i + j * kb < limit, sc, NEG_INF)
        SC[j] = sc
        return ones_where(sc >= 0.0)

    c0 = key_sum(score_block)
    neg = c0 < kf
    sgn = jnp.where(neg, -1.0, 1.0)
    kp = jnp.where(neg, kf, (nk * kb).astype(F32) - kf + 1.0)

    def flip(j, c):
        SC[j] = SC[j] * sgn
        return c

    lax.fori_loop(0, nk, flip, 0)

    def count_lt(cand):
        return key_sum(lambda j: ones_where(SC[j] < cand))

    def exp_step(i, carry):
        e_cur, t_cur = carry
        cand = jnp.where(e_cur == 0.0, tab_ref[0, i], t_cur * tab_ref[1, i])
        ok = count_lt(cand) < kp
        return jnp.where(ok, e_cur + tab_ref[2, i], e_cur), jnp.where(ok, cand, t_cur)

    _, t_pow = lax.fori_loop(0, 8, exp_step, (zero, zero))

    def man_step(i, carry):
        t_cur, frac = carry
        cand = t_cur + frac
        ok = count_lt(cand) < kp
        return jnp.where(ok, cand, t_cur), frac * 0.5

    t_cur, _ = lax.fori_loop(0, MANTISSA_BITS, man_step, (t_pow, t_pow * 0.5))
    thr = t_cur * sgn

    def score(j):
        return SC[j] * sgn

    def key_idx(j):
        return (key_i + j * kb).astype(F32)

    need = kf - key_sum(lambda j: ones_where(score(j) > thr))
    n_eq = key_sum(lambda j: ones_where(score(j) == thr))

    def index_cut():
        nbits = (SC.shape[0] * kb - 1).bit_length()

        def bit_step(i, carry):
            c_cur, bit = carry
            cand = c_cur + bit
            ok = key_sum(lambda j: ones_where((score(j) == thr) & (key_idx(j) < cand))) < need
            return jnp.where(ok, cand, c_cur), bit * 0.5

        c_cur, _ = lax.fori_loop(0, nbits, bit_step, (zero, jnp.full((1, tq), 2.0 ** (nbits - 1), F32)))
        return c_cur

    any_split = jnp.max(ones_where(n_eq > need)) > 0.0
    c_cut = lax.cond(any_split, index_cut, lambda: jnp.full((1, tq), SELECT_ALL, F32))

    def write_bias(j, c):
        s = score(j)
        sel = (s > thr) | ((s == thr) & (key_idx(j) <= c_cut))
        MS[j] = jnp.transpose(jnp.where(sel & (key_i + j * kb < limit), 0.0, NEG_INF))
        return c

    lax.fori_loop(0, nk, write_bias, 0)


def _attn2_kernel(*refs, n_kv, group, tq, pos0, top, indexer, wi_off):
    if indexer:
        (tab_ref, q_ref, k_ref, v_ref, x_ref, wo_ref, qi_ref, wi_ref, ki_ref, xo_ref,
         MS, LG, MACC, LACC, OACC, OH, OALL, SC) = refs
    else:
        q_ref, k_ref, v_ref, x_ref, wo_ref, xo_ref, MS, LG, MACC, LACC, OACC, OH, OALL = refs
    kb = ATTN_KB
    first = pos0 + pl.program_id(1) * tq
    last_limit = ((first + tq - 1) & ~(CHUNK - 1)) + CHUNK
    nk = (last_limit + (kb - 1)) >> ATTN_KB_SHIFT
    row = lax.broadcasted_iota(jnp.int32, (tq, 1), 0) + first
    limit = (row & ~(CHUNK - 1)) + CHUNK
    lane_i = lax.broadcasted_iota(jnp.int32, (tq, kb), 1)

    if indexer:
        _topk_bias_t(tab_ref, qi_ref, wi_ref, ki_ref, SC, MS, nk=nk, first=first, tq=tq, top=top, wi_off=wi_off)
    else:
        def causal_bias(j, c):
            MS[j] = jnp.where(lane_i + j * kb < limit, 0.0, NEG_INF)
            return c

        lax.fori_loop(0, nk, causal_bias, 0)

    hu = LG.shape[0]

    def per_kv_heads(gi, c):
        heads = [gi * hu + u for u in range(hu)]
        qs = [q_ref[0, g, 0] for g in heads]
        MACC[...] = jnp.full(MACC.shape, NEG_INF, F32)
        LACC[...] = jnp.zeros(LACC.shape, F32)
        OACC[...] = jnp.zeros(OACC.shape, F32)

        def logits_block(j, c_):
            keys = pl.ds(pl.multiple_of(j * kb, kb), kb)
            bias = MS[j][None]
            for u, g in enumerate(heads):
                lg = lax.dot_general(qs[u], k_ref[0, g, keys, :], (((1,), (1,)), ((), ())),
                                     preferred_element_type=F32)
                lg = (lg.reshape(group, tq, kb) + bias).reshape(group * tq, kb)
                LG[u, j] = lg
                MACC[u] = jnp.maximum(MACC[u], _fold(lg, jnp.maximum))
            return c_

        lax.fori_loop(0, nk, logits_block, 0)
        ms = [jnp.max(MACC[u], axis=-1, keepdims=True) for u in range(hu)]

        def value_block(j, c_):
            keys = pl.ds(pl.multiple_of(j * kb, kb), kb)
            for u, g in enumerate(heads):
                p = jnp.exp2(LG[u, j] - ms[u])
                LACC[u] += _fold(p)
                OACC[u] += jnp.dot(p.astype(BF16), v_ref[0, g, keys, :], preferred_element_type=F32)
            return c_

        lax.fori_loop(0, nk, value_block, 0)
        for u, g in enumerate(heads):
            OH[g] = (OACC[u] / jnp.sum(LACC[u], axis=-1, keepdims=True)).astype(OH.dtype)
        return c

    lax.fori_loop(0, n_kv // hu, per_kv_heads, 0)
    dv = OH.shape[2]
    for h in range(n_kv * group):
        OALL[:, h * dv:(h + 1) * dv] = OH[h // group, (h % group) * tq:(h % group + 1) * tq, :]
    xo_ref[0] = x_ref[0] + jnp.dot(OALL[...], wo_ref[...], preferred_element_type=F32)


def _head_major(a, tq, group):
    B, T, H, d = a.shape
    n_kv = H // group
    return jnp.transpose(a.reshape(B, T // tq, tq, n_kv, group, d), (0, 3, 1, 4, 2, 5)).reshape(
        B, n_kv, T // tq, group * tq, d)


def _attn2(qg, kt, vt, x, w_out, *, pos0, tq, group, idx=None, top=0, wi_off=0):
    B, n_kv, nq, rows, dq = qg.shape
    L, dv = kt.shape[2], vt.shape[3]
    T, D = x.shape[1], x.shape[2]
    kb = ATTN_KB
    assert nq * tq == T and rows == group * tq and L % kb == 0 and (idx is None or top <= kb)
    ins = [qg, kt, vt, x, w_out]
    specs = [pl.BlockSpec((1, n_kv, 1, rows, dq), lambda b, i: (b, 0, i, 0, 0)),
             pl.BlockSpec((1, n_kv, L, dq), lambda b, i: (b, 0, 0, 0)),
             pl.BlockSpec((1, n_kv, L, dv), lambda b, i: (b, 0, 0, 0)),
             pl.BlockSpec((1, tq, D), lambda b, i: (b, i, 0)),
             pl.BlockSpec(w_out.shape, lambda b, i: (0, 0))]
    hu = max(1, min(n_kv, ATTN_ROWS_PER_ITER // rows))
    assert n_kv % hu == 0
    scratch = [pltpu.VMEM((L // kb, tq, kb), F32), pltpu.VMEM((hu, L // kb, rows, kb), F32),
               pltpu.VMEM((hu, rows, LANE), F32), pltpu.VMEM((hu, rows, LANE), F32),
               pltpu.VMEM((hu, rows, dv), F32), pltpu.VMEM((n_kv, rows, dv), BF16),
               pltpu.VMEM((tq, n_kv * group * dv), BF16)]
    if idx is not None:
        qi, wi, ki = idx
        steps = [2 ** b for b in range(7, -1, -1)]
        tab = jnp.array([[2.0 ** (s - 127) for s in steps], [2.0 ** s if s < 128 else 1.0 for s in steps],
                         [float(s) for s in steps]], F32)
        ins = [tab] + ins + [qi, wi, ki]
        specs = ([pl.BlockSpec(memory_space=pltpu.SMEM)] + specs
                 + [pl.BlockSpec((1, tq, qi.shape[2]), lambda b, i: (b, i, 0)),
                    pl.BlockSpec((1, tq, wi.shape[2]), lambda b, i: (b, i, 0)),
                    pl.BlockSpec((1, L, ki.shape[2]), lambda b, i: (b, 0, 0))])
        scratch.append(pltpu.VMEM((L // kb, kb, tq), F32))
    return pl.pallas_call(
        functools.partial(_attn2_kernel, n_kv=n_kv, group=group, tq=tq, pos0=pos0, top=top,
                          indexer=idx is not None, wi_off=wi_off),
        grid=(B, nq),
        in_specs=specs,
        out_specs=pl.BlockSpec((1, tq, D), lambda b, i: (b, i, 0)),
        out_shape=jax.ShapeDtypeStruct((B, T, D), F32),
        scratch_shapes=scratch,
        compiler_params=_cparams(("parallel", "parallel")),
        name="dsa_attn" if idx is not None else "mla_attn",
    )(*ins)


def _rope_tables(pos, dh, rot, offset=0):
    half = rot // 2
    inv = ROPE_THETA ** (-jnp.arange(half, dtype=F32) / half)
    ang = pos.astype(F32)[:, None] * inv[None, :]
    cos, sin = jnp.cos(ang), jnp.sin(ang)
    T = pos.shape[0]
    pad = lambda n, v: jnp.full((T, n), v, F32)
    zh = pad(half, 0.0)
    lo, hi = offset, dh - offset - rot
    c = jnp.concatenate([pad(lo, 1.0), cos, cos, pad(hi, 1.0)], axis=1)
    s1 = jnp.concatenate([pad(lo, 0.0), -sin, zh, pad(hi, 0.0)], axis=1)
    s2 = jnp.concatenate([pad(lo, 0.0), zh, sin, pad(hi, 0.0)], axis=1)
    return tuple(jnp.tile(t, (1, LANE // dh)) for t in (c, s1, s2))


def _rope_lanes(x, c, s1, s2, half):
    return x * c + pltpu.roll(x, LANE - half, 1) * s1 + pltpu.roll(x, half, 1) * s2


def _dsa_proj_kernel(x_ref, g_ref, w_ref, c_ref, s1_ref, s2_ref, ts_ref, qh_ref, kf_ref, kt_ref, vf_ref, vt_ref,
                     qi_ref, tail_ref, kib_ref, *, tm):
    xf = x_ref[0]
    h = xf * lax.rsqrt(jnp.mean(xf * xf, axis=-1, keepdims=True) + NORM_EPS) * g_ref[...]
    acc = jnp.dot(h.astype(BF16), w_ref[...], preferred_element_type=F32)
    c, s1, s2 = c_ref[...], s1_ref[...], s2_ref[...]
    half = A_ROT // 2
    hd = A_HEAD_DIM
    group = A_HEADS // A_KV_HEADS
    rope = lambda xs: _rope_lanes(xs, c, s1, s2, half)
    slab = lambda off: acc[:, off:off + LANE]
    for s in range(A_O_Q // LANE):
        qs = (rope(slab(s * LANE)) * (hd ** -0.5 * LOG2E)).astype(BF16)
        for e in range(LANE // hd):
            head = s * (LANE // hd) + e
            u = head % group
            qh_ref[0, head // group, 0, u * tm:(u + 1) * tm, :] = qs[:, e * hd:(e + 1) * hd]
    ks = rope(slab(A_O_Q))
    vs = slab(A_O_K)
    kf_ref[0] = ks
    vf_ref[0] = vs
    for e in range(A_KV_HEADS):
        kt_ref[0, e] = ks[:, e * hd:(e + 1) * hd].astype(BF16)
        vt_ref[0, e] = vs[:, e * hd:(e + 1) * hd].astype(BF16)
    for s in range(IDX_HEADS * IDX_DIM // LANE):
        qi_ref[0, :, s * LANE:(s + 1) * LANE] = rope(slab(A_O_V + s * LANE)).astype(BF16)
    is_key = lax.broadcasted_iota(jnp.int32, (tm, LANE), 1) < IDX_DIM
    tl = _rope_lanes(slab(A_O_QI), jnp.where(is_key, c, 1.0), jnp.where(is_key, s1, 0.0),
                     jnp.where(is_key, s2, 0.0), half) * ts_ref[...]
    tail_ref[0] = tl
    kib_ref[0] = tl[:, :IDX_DIM].astype(BF16)


def _dsa_proj(x, g, w_in, tabs, tm):
    B, T, D = x.shape
    assert (A_HEAD_DIM, A_ROT) == (IDX_DIM, IDX_ROT) and A_KV_HEADS * A_HEAD_DIM == LANE
    assert A_O_QI % LANE == 0 and IDX_DIM + IDX_HEADS <= LANE and T % tm == 0
    n_in = _round_up(A_IN, LANE)
    group = A_HEADS // A_KV_HEADS
    hd = A_HEAD_DIM
    nq = T // tm
    lanes = jnp.arange(LANE)
    tail_scale = jnp.where(lanes < IDX_DIM, 1.0, jnp.where(lanes < IDX_DIM + IDX_HEADS,
                                                            (IDX_HEADS * IDX_DIM) ** -0.5, 0.0)).astype(F32)
    row = lambda n: pl.BlockSpec((1, tm, n), lambda b, i: (b, i, 0))
    tab = pl.BlockSpec((tm, LANE), lambda b, i: (i, 0))
    const = lambda a: pl.BlockSpec(a.shape, lambda b, i: (0,) * a.ndim)
    kvh = pl.BlockSpec((1, A_KV_HEADS, tm, hd), lambda b, i: (b, 0, i, 0))
    w = _pad_cols(w_in, n_in).astype(BF16)
    g2 = g.reshape(1, D)
    ts = tail_scale.reshape(1, LANE)
    return pl.pallas_call(
        functools.partial(_dsa_proj_kernel, tm=tm),
        grid=(B, nq),
        in_specs=[row(D), const(g2), const(w), tab, tab, tab, const(ts)],
        out_specs=[pl.BlockSpec((1, A_KV_HEADS, 1, group * tm, hd), lambda b, i: (b, 0, i, 0, 0)),
                   row(LANE), kvh, row(LANE), kvh, row(IDX_HEADS * IDX_DIM), row(LANE), row(IDX_DIM)],
        out_shape=[jax.ShapeDtypeStruct((B, A_KV_HEADS, nq, group * tm, hd), BF16),
                   jax.ShapeDtypeStruct((B, T, LANE), F32), jax.ShapeDtypeStruct((B, A_KV_HEADS, T, hd), BF16),
                   jax.ShapeDtypeStruct((B, T, LANE), F32), jax.ShapeDtypeStruct((B, A_KV_HEADS, T, hd), BF16),
                   jax.ShapeDtypeStruct((B, T, IDX_HEADS * IDX_DIM), BF16),
                   jax.ShapeDtypeStruct((B, T, LANE), F32), jax.ShapeDtypeStruct((B, T, IDX_DIM), BF16)],
        compiler_params=_cparams(("parallel", "parallel")),
        name="dsa_proj",
    )(x, g2, w, *tabs, ts)


def _rms(xf, gain):
    return xf * lax.rsqrt(jnp.mean(xf * xf, axis=-1, keepdims=True) + NORM_EPS) * gain


def _mla_write_kv(lat, kpe_slab, wuk_ref, wuv_ref, kt_ref, vt_ref):
    lb = lat.astype(BF16)
    kn = jnp.dot(lb, wuk_ref[...], preferred_element_type=F32)
    vv = jnp.dot(lb, wuv_ref[...], preferred_element_type=F32)
    for h in range(C_HEADS):
        kt_ref[0, h] = (kn[:, h * LANE:(h + 1) * LANE] + kpe_slab).astype(BF16)
        vt_ref[0, h] = vv[:, h * C_V:(h + 1) * C_V].astype(BF16)


def _mla_proj_kernel(x_ref, g_ref, win_ref, gq_ref, gkv_ref, wuq_ref, wuk_ref, wuv_ref, c_ref, s1_ref, s2_ref,
                     qh_ref, kt_ref, vt_ref, lat_ref, kpe_ref):
    h = _rms(x_ref[0], g_ref[...]).astype(BF16)
    proj = jnp.dot(h, win_ref[...], preferred_element_type=F32)
    c, s1, s2 = c_ref[...], s1_ref[...], s2_ref[...]
    half = C_ROPE // 2
    q = jnp.dot(_rms(proj[:, :C_Q_RANK], gq_ref[...]).astype(BF16), wuq_ref[...], preferred_element_type=F32)
    scale = (C_NOPE + C_ROPE) ** -0.5 * LOG2E
    for hd in range(C_HEADS):
        qh_ref[0, hd, 0] = (_rope_lanes(q[:, hd * LANE:(hd + 1) * LANE], c, s1, s2, half) * scale).astype(BF16)
    lat = _rms(proj[:, C_Q_RANK:C_Q_RANK + C_KV_RANK], gkv_ref[...])
    lat_ref[0] = lat
    kpe_slab = _rope_lanes(proj[:, C_Q_RANK + C_KV_RANK:], c, s1, s2, half)
    kpe_ref[0] = kpe_slab[:, C_NOPE:C_NOPE + C_ROPE]
    _mla_write_kv(lat, kpe_slab, wuk_ref, wuv_ref, kt_ref, vt_ref)


def _mla_kv_kernel(lat_ref, kpe_ref, wuk_ref, wuv_ref, kt_ref, vt_ref):
    _mla_write_kv(lat_ref[0], kpe_ref[0], wuk_ref, wuv_ref, kt_ref, vt_ref)


def _mla_weights(w_in, w_uq, w_ukv):
    D = w_in.shape[0]
    zc = lambda rows, n: jnp.zeros((rows, n), w_in.dtype)
    w_in2 = jnp.concatenate([w_in[:, :C_Q_RANK + C_KV_RANK], zc(D, C_NOPE), w_in[:, C_Q_RANK + C_KV_RANK:],
                             zc(D, LANE - C_NOPE - C_ROPE)], axis=1)
    pad_heads = lambda w, d: jnp.pad(w.reshape(w.shape[0], C_HEADS, d), ((0, 0), (0, 0), (0, LANE - d))).reshape(
        w.shape[0], C_HEADS * LANE)
    w_uq2 = pad_heads(w_uq, C_NOPE + C_ROPE)
    ukv = w_ukv.reshape(C_KV_RANK, C_HEADS, C_NOPE + C_V)
    w_uk2 = pad_heads(ukv[..., :C_NOPE].reshape(C_KV_RANK, C_HEADS * C_NOPE), C_NOPE)
    w_uv2 = ukv[..., C_NOPE:].reshape(C_KV_RANK, C_HEADS * C_V)
    return tuple(a.astype(BF16) for a in (w_in2, w_uq2, w_uk2, w_uv2))


def _mla_proj(x, g, g_q, g_kv, weights, tabs, tm):
    B, T, D = x.shape
    w_in2, w_uq2, w_uk2, w_uv2 = weights
    nq = T // tm
    row = lambda n: pl.BlockSpec((1, tm, n), lambda b, i: (b, i, 0))
    tab = pl.BlockSpec((tm, LANE), lambda b, i: (i, 0))
    const = lambda a: pl.BlockSpec(a.shape, lambda b, i: (0,) * a.ndim)
    heads = lambda d: pl.BlockSpec((1, C_HEADS, tm, d), lambda b, i: (b, 0, i, 0))
    vec = lambda a: a.reshape(1, a.shape[0])
    return pl.pallas_call(
        _mla_proj_kernel,
        grid=(B, nq),
        in_specs=[row(D), const(vec(g)), const(w_in2), const(vec(g_q)), const(vec(g_kv)), const(w_uq2),
                  const(w_uk2), const(w_uv2), tab, tab, tab],
        out_specs=[pl.BlockSpec((1, C_HEADS, 1, tm, LANE), lambda b, i: (b, 0, i, 0, 0)), heads(LANE), heads(C_V),
                   row(C_KV_RANK), row(C_ROPE)],
        out_shape=[jax.ShapeDtypeStruct((B, C_HEADS, nq, tm, LANE), BF16),
                   jax.ShapeDtypeStruct((B, C_HEADS, T, LANE), BF16), jax.ShapeDtypeStruct((B, C_HEADS, T, C_V), BF16),
                   jax.ShapeDtypeStruct((B, T, C_KV_RANK), F32), jax.ShapeDtypeStruct((B, T, C_ROPE), F32)],
        compiler_params=_cparams(("parallel", "parallel")),
        name="mla_proj",
    )(x, vec(g), w_in2, vec(g_q), vec(g_kv), w_uq2, w_uk2, w_uv2, *tabs)


def _mla_kv(lat, kpe_slab, w_uk2, w_uv2, tm):
    B, P, _ = lat.shape
    row = lambda n: pl.BlockSpec((1, tm, n), lambda b, i: (b, i, 0))
    const = lambda a: pl.BlockSpec(a.shape, lambda b, i: (0,) * a.ndim)
    heads = lambda d: pl.BlockSpec((1, C_HEADS, tm, d), lambda b, i: (b, 0, i, 0))
    return pl.pallas_call(
        _mla_kv_kernel,
        grid=(B, P // tm),
        in_specs=[row(C_KV_RANK), row(LANE), const(w_uk2), const(w_uv2)],
        out_specs=[heads(LANE), heads(C_V)],
        out_shape=[jax.ShapeDtypeStruct((B, C_HEADS, P, LANE), BF16), jax.ShapeDtypeStruct((B, C_HEADS, P, C_V), BF16)],
        compiler_params=_cparams(("parallel", "parallel")),
        name="mla_kv",
    )(lat, kpe_slab, w_uk2, w_uv2)


def _head_sum(x, bo_ref):
    bw = bo_ref.shape[0]
    hi = x.astype(BF16)
    lo = (x - hi.astype(F32)).astype(BF16)
    bo = bo_ref[...]
    return jnp.concatenate(
        [jnp.dot(hi[:, j * bw:(j + 1) * bw], bo, preferred_element_type=F32)
         + jnp.dot(lo[:, j * bw:(j + 1) * bw], bo, preferred_element_type=F32) for j in range(x.shape[1] // bw)],
        axis=-1)


RWKV_HALO = 8


def _rwkv_pre_kernel(x_ref, xh_ref, sh_ref, g_ref, mu_ref, vec_ref, wr_ref, wk_ref, wv_ref, w1_ref, w2_ref,
                     a1_ref, a2_ref, g1_ref, g2_ref, bo_ref,
                     rp_ref, w_ref, k_ref, v_ref, a_ref, b_ref, yc_ref, bonus_ref, gate_ref, *, tm):
    i = pl.program_id(1)
    gain = g_ref[...]

    def norm(xf):
        return xf * lax.rsqrt(jnp.mean(xf * xf, axis=-1, keepdims=True) + NORM_EPS) * gain

    h = norm(x_ref[0])
    before = jnp.where(i > 0, norm(xh_ref[0])[RWKV_HALO - 1:RWKV_HALO, :], sh_ref[0])
    first = lax.broadcasted_iota(jnp.int32, (tm, 1), 0) == 0
    xx = jnp.where(first, before, pltpu.roll(h, 1, 0)) - h
    mu = mu_ref[...]
    vec = vec_ref[...]
    w0, a0, k_k, k_a, r_k = (vec[j:j + 1, :] for j in range(5))

    def mix(j):
        return (h + xx * mu[j:j + 1, :]).astype(BF16)

    dot = lambda a_, w_: jnp.dot(a_, w_[...], preferred_element_type=F32)
    r = dot(mix(0), wr_ref)
    wl = dot(jnp.tanh(dot(mix(1), w1_ref)).astype(BF16), w2_ref)
    k = dot(mix(2), wk_ref)
    v = dot(mix(3), wv_ref)
    al = dot(dot(mix(4), a1_ref).astype(BF16), a2_ref)
    gate_ref[0] = dot(jax.nn.sigmoid(dot(mix(5), g1_ref)).astype(BF16), g2_ref)
    z = -(w0 + wl)
    softplus = jnp.maximum(z, 0.0) + jnp.log(1.0 + jnp.exp(-jnp.abs(z)))
    decay = jnp.exp(-jnp.exp(-softplus - 0.5))
    a = jax.nn.sigmoid(a0 + al)
    kk = k * k_k
    kk = kk / jnp.maximum(jnp.sqrt(_head_sum(kk * kk, bo_ref)), 1e-12)
    k = k * (1.0 + (a - 1.0) * k_a)
    b = kk * a
    rp_ref[0] = decay * r - kk * _head_sum(b * r, bo_ref)
    w_ref[0] = decay
    k_ref[0] = k
    v_ref[0] = v
    a_ref[0] = -kk
    b_ref[0] = b
    yc_ref[0] = v * _head_sum(k * r, bo_ref)
    bonus_ref[0] = _head_sum(r * k * r_k, bo_ref) * v


def _rwkv_post_kernel(y_ref, yc_ref, bonus_ref, gate_ref, x_ref, ln_ref, wo_ref, bo_ref, o_ref):
    n = float(B_HEAD_DIM)
    y = y_ref[0] + yc_ref[0]
    d = y - _head_sum(y, bo_ref) / n
    var = _head_sum(d * d, bo_ref) / n
    ln = ln_ref[...]
    yn = d * lax.rsqrt(var + B_GN_EPS) * ln[0:1, :] + ln[1:2, :] + bonus_ref[0]
    o_ref[0] = x_ref[0] + jnp.dot((yn * gate_ref[0]).astype(BF16), wo_ref[...], preferred_element_type=F32)


def _block_ones():
    blk = jnp.arange(WKV_BW) // B_HEAD_DIM
    return (blk[:, None] == blk[None, :]).astype(BF16)


def _rwkv_pre(x, shift_prev, g, mu, vecs, ws, tm):
    B, T, D = x.shape
    H = RWKV_HALO
    assert T % tm == 0 and tm % H == 0
    nh = tm // H
    bo = _block_ones()
    row = pl.BlockSpec((1, tm, D), lambda b, i: (b, i, 0))
    const = lambda a: pl.BlockSpec(a.shape, lambda b, i: (0,) * a.ndim)
    pad8 = lambda a: jnp.pad(a, ((0, 8 - a.shape[0]), (0, 0)))
    mu8, vec8, g2 = pad8(mu), pad8(vecs), g.reshape(1, D)
    return pl.pallas_call(
        functools.partial(_rwkv_pre_kernel, tm=tm),
        grid=(B, T // tm),
        in_specs=[row, pl.BlockSpec((1, H, D), lambda b, i: (b, jnp.maximum(i * nh - 1, 0), 0)),
                  pl.BlockSpec((1, 1, D), lambda b, i: (b, 0, 0)), const(g2), const(mu8), const(vec8)]
                 + [const(a) for a in ws] + [const(bo)],
        out_specs=[row] * 9,
        out_shape=[jax.ShapeDtypeStruct((B, T, D), F32)] * 9,
        compiler_params=_cparams(("parallel", "parallel")),
        name="rwkv_pre",
    )(x, x, shift_prev.reshape(B, 1, D), g2, mu8, vec8, *ws, bo)


def _rwkv_post(y, yc, bonus, gate, x, ln, w_out, tm):
    B, T, D = x.shape
    bo = _block_ones()
    row = pl.BlockSpec((1, tm, D), lambda b, i: (b, i, 0))
    const = lambda a: pl.BlockSpec(a.shape, lambda b, i: (0,) * a.ndim)
    return pl.pallas_call(
        _rwkv_post_kernel,
        grid=(B, T // tm),
        in_specs=[row] * 5 + [const(ln), const(w_out), const(bo)],
        out_specs=row,
        out_shape=jax.ShapeDtypeStruct((B, T, D), F32),
        compiler_params=_cparams(("parallel", "parallel")),
        name="rwkv_post",
    )(y, yc, bonus, gate, x, ln, w_out, bo)


WKV_SUB = 8
WKV_NB = 4
WKV_BW = 2 * LANE


def _wkv_kernel(rp_ref, w_ref, k_ref, v_ref, a_ref, b_ref, s0_ref, bo_ref, dg_ref, hs_ref, y_ref, sT_ref, S, VB,
                *, tc, nb):
    c = pl.program_id(1)
    n = B_HEAD_DIM
    D = D_MODEL
    bw = WKV_BW
    nt = D // bw
    seg = nt * n

    @pl.when(c == 0)
    def _():
        S[...] = s0_ref[...]

    def tiles(x):
        return jnp.concatenate([x[:, j * bw:(j + 1) * bw] for j in range(nt)], axis=0)

    def untile(x):
        return jnp.concatenate([x[j * n:(j + 1) * n, :] for j in range(nt)], axis=-1)

    def block(sc, carry):
        base = pl.multiple_of(sc * WKV_SUB, WKV_SUB)
        rows = pl.ds(base, WKV_SUB)
        bo = bo_ref[...]
        dg = dg_ref[...]
        ins = []
        for bi in range(nb):
            r8, w8, k8, v8, a8, b8 = (x[bi, rows, :] for x in (rp_ref, w_ref, k_ref, v_ref, a_ref, b_ref))
            ins.append((r8, w8, k8, a8, b8))
            vd = jnp.concatenate([tiles((dg * v8[u:u + 1, :]).astype(BF16)) for u in range(WKV_SUB)], axis=0)
            VB[bi] = jnp.dot(vd, bo, preferred_element_type=F32)
        hsel = hs_ref[...]
        for u in range(WKV_SUB):
            sas = []
            for bi in range(nb):
                r8, w8, k8, a8, b8 = ins[bi]
                s = S[bi]
                pa = (s * a8[u:u + 1, :]).astype(BF16)
                sas.append(jnp.dot(tiles(pa), bo, preferred_element_type=F32))
                pr = (s * r8[u:u + 1, :]).astype(BF16)
                y_ref[bi, base + u] = lax.dot_general(hsel, pr, (((1,), (1,)), ((), ())),
                                                      preferred_element_type=F32)
            for bi in range(nb):
                r8, w8, k8, a8, b8 = ins[bi]
                S[bi] = (S[bi] * w8[u:u + 1, :] + untile(sas[bi]) * b8[u:u + 1, :]
                         + untile(VB[bi, u * seg:(u + 1) * seg, :]) * k8[u:u + 1, :])
        return carry

    lax.fori_loop(0, tc // WKV_SUB, block, 0)

    @pl.when(c == pl.num_programs(1) - 1)
    def _():
        sT_ref[...] = S[...]


def _wkv(rp, w, k, v, a, b, s0):
    B, T, D = rp.shape
    n = B_HEAD_DIM
    H = B_HEADS
    nb = WKV_NB if B % WKV_NB == 0 else 1
    tc = _row_tile(T, 128)
    assert tc % WKV_SUB == 0
    bw = WKV_BW
    s0t = jnp.transpose(s0, (0, 2, 1, 3)).reshape(B, n, D)
    bo = _block_ones()
    dg =(jnp.arange(n)[:, None] == (jnp.arange(D) % n)[None, :]).astype(F32)
    hsel = (jnp.arange(H)[:, None] == (jnp.arange(D) // n)[None, :]).astype(BF16)
    seq = pl.BlockSpec((nb, tc, D), lambda bi, c: (bi, c, 0))
    st = pl.BlockSpec((nb, n, D), lambda bi, c: (bi, 0, 0))
    const = lambda a_: pl.BlockSpec(a_.shape, lambda bi, c: (0, 0))
    y, sT = pl.pallas_call(
        functools.partial(_wkv_kernel, tc=tc, nb=nb),
        grid=(B // nb, T // tc),
        in_specs=[seq] * 6 + [st, const(bo), const(dg), const(hsel)],
        out_specs=[pl.BlockSpec((nb, tc, H, n), lambda bi, c: (bi, c, 0, 0)), st],
        out_shape=[jax.ShapeDtypeStruct((B, T, H, n), F32), jax.ShapeDtypeStruct((B, n, D), F32)],
        scratch_shapes=[pltpu.VMEM((nb, n, D), F32), pltpu.VMEM((nb, WKV_SUB * (D // bw) * n, bw), F32)],
        compiler_params=_cparams(("parallel", "arbitrary")),
        name="wkv",
    )(rp, w, k, v, a, b, s0t, bo, dg, hsel)
    return y.reshape(B, T, D), jnp.transpose(sT.reshape(B, n, H, n), (0, 2, 1, 3))


FFN_HALO = SUBLANE_BF16


FFN_CW = 2 * LANE


def _ffn_kernel(x_ref, xh_ref, g_ref, wug_ref, wuv_ref, cg_ref, cv_ref, pg_ref, pv_ref, wd_ref,
                o_ref, hn, ug, uv, acc, *, tm):
    i = pl.program_id(1)
    H = FFN_HALO
    cw_ = FFN_CW

    def norm(xf):
        return xf * lax.rsqrt(jnp.mean(xf * xf, axis=-1, keepdims=True) + NORM_EPS) * g_ref[...]

    hn[0:H, :] = jnp.where(i > 0, norm(xh_ref[0]), 0.0).astype(BF16)
    hn[H:H + tm, :] = norm(x_ref[0]).astype(BF16)
    h = hn[...]
    first = jnp.where(i == 0, 1.0, 0.0)

    def conv(u, taps):
        return (taps[3:4, :] + u[H - 2:H - 2 + tm, :] * taps[0:1, :] + u[H - 1:H - 1 + tm, :] * taps[1:2, :]
                + u[H:H + tm, :] * taps[2:3, :])

    for c in range(D_FF // cw_):
        cols = slice(c * cw_, (c + 1) * cw_)
        for u_scr, w_ref, p_ref in ((ug, wug_ref, pg_ref), (uv, wuv_ref, pv_ref)):
            u = jnp.dot(h, w_ref[:, cols], preferred_element_type=F32)
            u_scr[c % 2, 0:H, :] = u[0:H, :] + p_ref[0, :, cols] * first
            u_scr[c % 2, H:, :] = u[H:, :]
        gate = conv(ug.at[c % 2], cg_ref[:, cols])
        val = conv(uv.at[c % 2], cv_ref[:, cols])
        act = (gate * jax.nn.sigmoid(gate) * val).astype(BF16)
        d = jnp.dot(act, wd_ref[cols, :], preferred_element_type=F32)
        if c == 0:
            acc[...] = d
        else:
            acc[...] += d
    o_ref[0] = x_ref[0] + acc[...]


def _ffn(x, g, wug, wuv, cg, cv, prev, wd, *, tm):
    B, T, D = x.shape
    F = D_FF
    H = FFN_HALO
    tm = min(tm, T)
    assert T % tm == 0 and tm % H == 0 and F % FFN_CW == 0
    nh = tm // H
    prev_h = jnp.pad(prev, ((0, 0), (H - prev.shape[1], 0), (0, 0)))
    pg, pv = prev_h[:, :, :F], prev_h[:, :, F:]
    const = lambda a: pl.BlockSpec(a.shape, lambda b, i: (0,) * a.ndim, pipeline_mode=pl.Buffered(1))
    g2 = g.reshape(1, D)
    return pl.pallas_call(
        functools.partial(_ffn_kernel, tm=tm),
        grid=(B, T // tm),
        in_specs=[
            pl.BlockSpec((1, tm, D), lambda b, i: (b, i, 0)),
            pl.BlockSpec((1, H, D), lambda b, i: (b, jnp.maximum(i * nh - 1, 0), 0)),
            const(g2), const(wug), const(wuv), const(cg), const(cv),
            pl.BlockSpec((1, H, F), lambda b, i: (b, 0, 0)),
            pl.BlockSpec((1, H, F), lambda b, i: (b, 0, 0)),
            const(wd),
        ],
        out_specs=pl.BlockSpec((1, tm, D), lambda b, i: (b, i, 0)),
        out_shape=jax.ShapeDtypeStruct((B, T, D), F32),
        scratch_shapes=[pltpu.VMEM((tm + H, D), BF16), pltpu.VMEM((2, tm + H, FFN_CW), F32),
                        pltpu.VMEM((2, tm + H, FFN_CW), F32), pltpu.VMEM((tm, D), F32)],
        compiler_params=_cparams(("parallel", "parallel")),
        name="conv_ffn",
    )(x, x, g2, wug, wuv, cg, cv, pg, pv, wd)


def _rope(x, pos, rot):
    half = rot // 2
    inv = ROPE_THETA ** (-jnp.arange(half, dtype=F32) / half)
    ang = pos.astype(F32)[:, None] * inv[None, :]
    shape = (ang.shape[0],) + (1,) * (x.ndim - 3) + (half,)
    cos = jnp.cos(ang).reshape(shape)
    sin = jnp.sin(ang).reshape(shape)
    x1 = x[..., :half]
    x2 = x[..., half:rot]
    return jnp.concatenate([x1 * cos - x2 * sin, x1 * sin + x2 * cos, x[..., rot:]], axis=-1)


def _pad_cols(w, n):
    return jnp.pad(w, ((0, 0), (0, n - w.shape[1])))


def _pad_keys(a, L):
    return jnp.pad(a, ((0, 0), (0, L - a.shape[1])) + ((0, 0),) * (a.ndim - 2))


def _dsa_layer(x, pos, past_k, past_v, past_ki, g, w_in, w_out):
    B, T, D = x.shape
    tq = min(256, T)
    qh, kf, kt, vf, vt, qi, tail, kib = _dsa_proj(x, g, w_in, _rope_tables(pos, A_HEAD_DIM, A_ROT), tq)
    P = past_k.shape[1]
    L = P + T
    Lp = _round_up(L, ATTN_KB)
    if Lp != T:
        past = lambda a: jnp.transpose(a, (0, 2, 1, 3)).astype(BF16)
        kt = jnp.pad(jnp.concatenate([past(past_k), kt], axis=2), ((0, 0), (0, 0), (0, Lp - L), (0, 0)))
        vt = jnp.pad(jnp.concatenate([past(past_v), vt], axis=2), ((0, 0), (0, 0), (0, Lp - L), (0, 0)))
        kib = _pad_keys(jnp.concatenate([past_ki.astype(BF16), kib], axis=1), Lp)
    x = _attn2(qh, kt, vt, x, w_out.astype(BF16), pos0=P, tq=tq, group=A_HEADS // A_KV_HEADS,
               idx=(qi, tail, kib), top=min(TOPK_MAX, L // 4), wi_off=IDX_DIM)
    kv_rows = lambda a: a.reshape(B, T, A_KV_HEADS, A_HEAD_DIM)
    return x, kv_rows(kf), kv_rows(vf), tail[..., :IDX_DIM]


def _rwkv_layer(x, shift_prev, S0, g, mu, w_rkv, w0, w1, w2, a0, a1, a2, g1, g2, k_k, k_a, r_k, ln_w, ln_b,
                w_out):
    B, T, D = x.shape
    tm = min(256, T)
    bf = lambda a: a.astype(BF16)
    vecs = jnp.stack([w0, a0, k_k, k_a, r_k.reshape(D)], axis=0)
    ws = [bf(w_rkv[0]), bf(w_rkv[1]), bf(w_rkv[2]), bf(w1), bf(w2), bf(a1), bf(a2), bf(g1), bf(g2)]
    rp, decay, k, v, a_vec, b_vec, yc, bonus, gate = _rwkv_pre(x, shift_prev, g, mu, vecs, ws, tm)
    y, S = _wkv(rp, decay, k, v, a_vec, b_vec, S0)
    ln = jnp.pad(jnp.stack([ln_w, ln_b], axis=0), ((0, 6), (0, 0)))
    x_new = _rwkv_post(y, yc, bonus, gate, x, ln, bf(w_out), tm)
    assert T >= 8
    shift = _norm(x[:, T - 8:].reshape(B * 8, D), g).reshape(B, 8, D)[:, -1]
    return x_new, shift, S


def _mla_layer(x, pos, past_lat, past_rope, g, w_in, g_q, g_kv, w_uq, w_ukv, w_out):
    B, T, D = x.shape
    tq = min(256, T)
    weights = _mla_weights(w_in, w_uq, w_ukv)
    tabs = _rope_tables(pos, LANE, C_ROPE, offset=C_NOPE)
    qh, kt, vt, lat, kpe = _mla_proj(x, g, g_q, g_kv, weights, tabs, tq)
    P = past_lat.shape[1]
    L = P + T
    Lp = _round_up(L, ATTN_KB)
    if Lp != T:
        past_slab = jnp.pad(past_rope, ((0, 0), (0, 0), (C_NOPE, LANE - C_NOPE - C_ROPE)))
        kt_p, vt_p = _mla_kv(past_lat, past_slab, weights[2], weights[3], _row_tile(P, 256))
        kt = jnp.pad(jnp.concatenate([kt_p, kt], axis=2), ((0, 0), (0, 0), (0, Lp - L), (0, 0)))
        vt = jnp.pad(jnp.concatenate([vt_p, vt], axis=2), ((0, 0), (0, 0), (0, Lp - L), (0, 0)))
    x = _attn2(qh, kt, vt, x, w_out.astype(BF16), pos0=P, tq=tq, group=1)
    return x, lat, kpe


def _ffn_layer(x, prev, g, w_up, w_conv, b_conv, w_down):
    F = D_FF
    B, T, D = x.shape
    taps = jnp.concatenate([w_conv, b_conv[None, :], jnp.zeros((8 - CONV_W - 1, 2 * F), F32)], axis=0)
    w_up = w_up.astype(BF16)
    out = _ffn(x, g, w_up[:, :F], w_up[:, F:], taps[:, :F], taps[:, F:], prev, w_down.astype(BF16), tm=512)
    assert T >= 8
    u_last = _mm(x[:, T - 8:].reshape(B * 8, D), w_up, norm_g=g).reshape(B, 8, 2 * F)
    return out, u_last[:, 8 - (CONV_W - 1):]


def _trunk(x, pos0, st, w):
    B, T, D = x.shape
    pos = pos0 + jnp.arange(T, dtype=jnp.int32)
    new = {name: [] for name in ('a_k', 'a_v', 'a_idx', 'b_wkv', 'b_shift', 'c_lat', 'c_rope', 'ffn')}
    for i in range(DEPTH):
        j = i // N_MIXERS
        kind = i % N_MIXERS
        if kind == 0:
            x, k, v, ki = _dsa_layer(x, pos, st['a_k'][j], st['a_v'][j], st['a_idx'][j], w['n_mix'][i],
                                     w['a_w_in'][j], w['a_w_out'][j])
            new['a_k'].append(k)
            new['a_v'].append(v)
            new['a_idx'].append(ki)
        elif kind == 1:
            x, shift, S = _rwkv_layer(x, st['b_shift'][j], st['b_wkv'][j], w['n_mix'][i], w['b_mu'][j],
                                      w['b_w_rkv'][j], w['b_w0'][j], w['b_w1'][j], w['b_w2'][j], w['b_a0'][j],
                                      w['b_a1'][j], w['b_a2'][j], w['b_g1'][j], w['b_g2'][j], w['b_k_k'][j],
                                      w['b_k_a'][j], w['b_r_k'][j], w['b_ln_w'][j], w['b_ln_b'][j],
                                      w['b_w_out'][j])
            new['b_shift'].append(shift)
            new['b_wkv'].append(S)
        else:
            x, lat, kpe = _mla_layer(x, pos, st['c_lat'][j], st['c_rope'][j], w['n_mix'][i], w['c_w_in'][j],
                                     w['c_g_q'][j], w['c_g_kv'][j], w['c_w_uq'][j], w['c_w_ukv'][j],
                                     w['c_w_out'][j])
            new['c_lat'].append(lat)
            new['c_rope'].append(kpe)
        x, cbuf = _ffn_layer(x, st['ffn'][i], w['n_ffn'][i], w['f_w_up'][i], w['f_w_conv'][i],
                             w['f_b_conv'][i], w['f_w_down'][i])
        new['ffn'].append(cbuf)
    y = _norm(x.reshape(B * T, D), w['n_final']).reshape(B, T, D)
    return y, {name: jnp.stack(rows, axis=0) for name, rows in new.items()}


def kernel(x_prompt, x_sample, cache_a_k, cache_a_v, cache_a_idx, state_b_wkv, state_b_shift,
           cache_c_latent, cache_c_rope, state_ffn_conv, n_mix, n_ffn, n_final, a_w_in, a_w_out,
           b_mu, b_w_rkv, b_w0, b_w1, b_w2, b_a0, b_a1, b_a2, b_g1, b_g2, b_k_k, b_k_a, b_r_k,
           b_ln_w, b_ln_b, b_w_out, c_w_in, c_g_q, c_g_kv, c_w_uq, c_w_ukv, c_w_out,
           f_w_up, f_w_conv, f_b_conv, f_w_down):
    w = dict(n_mix=n_mix, n_ffn=n_ffn, n_final=n_final, a_w_in=a_w_in, a_w_out=a_w_out,
             b_mu=b_mu, b_w_rkv=b_w_rkv, b_w0=b_w0, b_w1=b_w1, b_w2=b_w2, b_a0=b_a0, b_a1=b_a1,
             b_a2=b_a2, b_g1=b_g1, b_g2=b_g2, b_k_k=b_k_k, b_k_a=b_k_a, b_r_k=b_r_k,
             b_ln_w=b_ln_w, b_ln_b=b_ln_b, b_w_out=b_w_out, c_w_in=c_w_in, c_g_q=c_g_q,
             c_g_kv=c_g_kv, c_w_uq=c_w_uq, c_w_ukv=c_w_ukv, c_w_out=c_w_out,
             f_w_up=f_w_up, f_w_conv=f_w_conv, f_b_conv=f_b_conv, f_w_down=f_w_down)
    Bp, Tp, D = x_prompt.shape
    n_a, n_b, n_c = cache_a_k.shape[0], state_b_wkv.shape[0], cache_c_latent.shape[0]
    st_prompt = dict(
        a_k=jnp.zeros((n_a, Bp, 0, A_KV_HEADS, A_HEAD_DIM), F32),
        a_v=jnp.zeros((n_a, Bp, 0, A_KV_HEADS, A_HEAD_DIM), F32),
        a_idx=jnp.zeros((n_a, Bp, 0, IDX_DIM), F32),
        b_wkv=jnp.zeros((n_b, Bp, B_HEADS, B_HEAD_DIM, B_HEAD_DIM), F32),
        b_shift=jnp.zeros((n_b, Bp, D), F32),
        c_lat=jnp.zeros((n_c, Bp, 0, C_KV_RANK), F32),
        c_rope=jnp.zeros((n_c, Bp, 0, C_ROPE), F32),
        ffn=jnp.zeros((DEPTH, Bp, CONV_W - 1, 2 * D_FF), F32))
    st_sample = dict(a_k=cache_a_k, a_v=cache_a_v, a_idx=cache_a_idx, b_wkv=state_b_wkv,
                     b_shift=state_b_shift, c_lat=cache_c_latent, c_rope=cache_c_rope,
                     ffn=state_ffn_conv)
    y_prompt, sp = _trunk(x_prompt, 0, st_prompt, w)
    y_sample, ss = _trunk(x_sample, cache_a_k.shape[2], st_sample, w)
    return (y_prompt, y_sample,
            sp['a_k'], ss['a_k'], sp['a_v'], ss['a_v'], sp['a_idx'], ss['a_idx'],
            sp['b_wkv'], ss['b_wkv'], sp['b_shift'], ss['b_shift'],
            sp['c_lat'], ss['c_lat'], sp['c_rope'], ss['c_rope'],
            sp['ffn'], ss['ffn'])
```

```python
import functools

import jax
import jax.numpy as jnp
from jax import lax
from jax.experimental import pallas as pl
from jax.experimental.pallas import tpu as pltpu

F32 = jnp.float32
BF16 = jnp.bfloat16

D_MODEL = 1024
DEPTH = 4
CHUNK = 64
N_MIXERS = 3
NORM_EPS = 1e-6
ROPE_THETA = 500000.0
A_HEADS, A_HEAD_DIM, A_KV_HEADS = 16, 64, 2
A_ROT = A_HEAD_DIM // 4
IDX_HEADS, IDX_DIM = 8, 64
IDX_ROT = IDX_DIM // 4
TOPK_MAX = 256
A_O_Q = A_HEADS * A_HEAD_DIM
A_O_K = A_O_Q + A_KV_HEADS * A_HEAD_DIM
A_O_V = A_O_K + A_KV_HEADS * A_HEAD_DIM
A_O_QI = A_O_V + IDX_HEADS * IDX_DIM
A_O_KI = A_O_QI + IDX_DIM
A_IN = A_O_KI + IDX_HEADS
B_HEAD_DIM = 64
B_HEADS = D_MODEL // B_HEAD_DIM
B_GN_EPS = 64e-5
C_HEADS, C_NOPE, C_ROPE, C_V = 16, 64, 32, 64
C_Q_RANK, C_KV_RANK = 512, 256
D_FF = 2816
CONV_W = 3

LANE = 128
SUBLANE_BF16 = 16
VMEM_LIMIT = 56 * 1024 * 1024
NEG_INF = float("-inf")
LOG2E = 1.4426950408889634


def _round_up(n, m):
    return (n + m - 1) // m * m


def _row_tile(M, pref):
    t = min(pref, M)
    while M % t:
        t //= 2
    return t


def _cparams(sem):
    return pltpu.CompilerParams(dimension_semantics=sem, vmem_limit_bytes=VMEM_LIMIT)


def _mm_kernel(*refs, has_norm, has_res):
    a_ref, w_ref = refs[0], refs[1]
    i = 2
    g_ref = r_ref = None
    if has_norm:
        g_ref = refs[i]
        i += 1
    if has_res:
        r_ref = refs[i]
        i += 1
    o_ref = refs[i]
    a = a_ref[...]
    if has_norm:
        af = a.astype(F32)
        a = af * lax.rsqrt(jnp.mean(af * af, axis=-1, keepdims=True) + NORM_EPS) * g_ref[...]
    acc = jnp.dot(a.astype(BF16), w_ref[...], preferred_element_type=F32)
    if has_res:
        acc = acc + r_ref[...]
    o_ref[...] = acc.astype(o_ref.dtype)


def _mm(a, w, *, norm_g=None, residual=None, out_dtype=F32, tm=512):
    M, K = a.shape
    N = w.shape[1]
    tm = _row_tile(M, tm)
    assert M % tm == 0 and N % LANE == 0
    ins = [a, w]
    specs = [pl.BlockSpec((tm, K), lambda i: (i, 0)), pl.BlockSpec((K, N), lambda i: (0, 0))]
    if norm_g is not None:
        ins.append(norm_g.reshape(1, K).astype(F32))
        specs.append(pl.BlockSpec((1, K), lambda i: (0, 0)))
    if residual is not None:
        ins.append(residual)
        specs.append(pl.BlockSpec((tm, N), lambda i: (i, 0)))
    return pl.pallas_call(
        functools.partial(_mm_kernel, has_norm=norm_g is not None, has_res=residual is not None),
        grid=(M // tm,),
        in_specs=specs,
        out_specs=pl.BlockSpec((tm, N), lambda i: (i, 0)),
        out_shape=jax.ShapeDtypeStruct((M, N), out_dtype),
        compiler_params=_cparams(("parallel",)),
        name="mm",
    )(*ins)


def _norm_kernel(x_ref, g_ref, o_ref):
    xf = x_ref[...]
    o_ref[...] = xf * lax.rsqrt(jnp.mean(xf * xf, axis=-1, keepdims=True) + NORM_EPS) * g_ref[...]


def _norm(x, g, tm=512):
    M, K = x.shape
    tm = _row_tile(M, tm)
    return pl.pallas_call(
        _norm_kernel,
        grid=(M // tm,),
        in_specs=[pl.BlockSpec((tm, K), lambda i: (i, 0)), pl.BlockSpec((1, K), lambda i: (0, 0))],
        out_specs=pl.BlockSpec((tm, K), lambda i: (i, 0)),
        out_shape=jax.ShapeDtypeStruct((M, K), F32),
        compiler_params=_cparams(("parallel",)),
        name="rmsnorm",
    )(x, g.reshape(1, K))


def _count(cond):
    return jnp.sum(jnp.where(cond, 1.0, 0.0), axis=-1, keepdims=True)


def _topk_select(score, top, key_idx):
    L = score.shape[1]
    kf = float(top)
    c0 = _count(score >= 0.0)
    neg = c0 < kf
    y = jnp.where(neg, -score, score)
    kp = jnp.where(neg, kf, float(L) - kf + 1.0)
    zero = jnp.zeros_like(c0)
    e_cur = zero
    t_cur = zero
    for b in range(7, -1, -1):
        step = 2 ** b
        cand = jnp.where(e_cur == 0.0, 2.0 ** (step - 127), t_cur * (2.0 ** step if step < 128 else 1.0))
        ok = _count(y < cand) < kp
        e_cur = jnp.where(ok, e_cur + float(step), e_cur)
        t_cur = jnp.where(ok, cand, t_cur)
    t_pow = t_cur
    for j in range(1, 24):
        cand = t_cur + t_pow * (2.0 ** -j)
        ok = _count(y < cand) < kp
        t_cur = jnp.where(ok, cand, t_cur)
    thr = jnp.where(neg, -t_cur, t_cur)
    gt = score > thr
    eq = score == thr
    need = kf - _count(gt)
    eqf = jnp.where(eq, 1.0, 0.0)
    n_eq = jnp.sum(eqf, axis=-1, keepdims=True)

    def index_cut():
        c_cur = zero
        nbits = max(1, (L - 1).bit_length())
        for b in range(nbits - 1, -1, -1):
            cand = c_cur + float(2 ** b)
            ok = jnp.sum(jnp.where(key_idx < cand, eqf, 0.0), axis=-1, keepdims=True) < need
            c_cur = jnp.where(ok, cand, c_cur)
        return c_cur

    any_split = jnp.max(jnp.where(n_eq > need, 1.0, 0.0)) > 0.0
    c_cut = lax.cond(any_split, index_cut, lambda: jnp.full_like(zero, float(L)))
    return gt | (eq & (key_idx <= c_cut))


def _attn_kernel(*refs, n_heads, group, dq, dv, tq, pos0, top, indexer, key_counts):
    if indexer:
        q_ref, k_ref, v_ref, qi_ref, wi_ref, ki_ref, o_ref = refs
    else:
        q_ref, k_ref, v_ref, o_ref = refs
    qb = pl.program_id(1)
    first = pos0 + qb * tq

    def body(L):
        row = lax.broadcasted_iota(jnp.int32, (tq, 1), 0) + first
        limit = (row & ~(CHUNK - 1)) + CHUNK
        key_i = lax.broadcasted_iota(jnp.int32, (tq, L), 1)
        valid = key_i < limit
        if indexer:
            ki = ki_ref[0, :L, :]
            qi = qi_ref[0]
            wi = wi_ref[0]
            score = jnp.zeros((tq, L), F32)
            for h in range(IDX_HEADS):
                d = lax.dot_general(qi[:, h * IDX_DIM:(h + 1) * IDX_DIM], ki, (((1,), (1,)), ((), ())),
                                    preferred_element_type=F32)
                score = score + wi[:, h:h + 1] * jnp.maximum(d, 0.0)
            score = jnp.where(valid, score, NEG_INF)
            sel = _topk_select(score, top, key_i.astype(F32))
            valid = sel & valid
        bias = jnp.where(valid, 0.0, NEG_INF)
        q = q_ref[0]
        kk = k_ref[0, :L, :]
        vv = v_ref[0, :L, :]
        outs = []
        for h in range(n_heads):
            g = h // group
            logits = lax.dot_general(q[:, h * dq:(h + 1) * dq], kk[:, g * dq:(g + 1) * dq],
                                     (((1,), (1,)), ((), ())), preferred_element_type=F32) + bias
            m = jnp.max(logits, axis=-1, keepdims=True)
            p = jnp.exp2(logits - m)
            s = jnp.sum(p, axis=-1, keepdims=True)
            o = jnp.dot(p.astype(BF16), vv[:, g * dv:(g + 1) * dv], preferred_element_type=F32)
            outs.append(o / s)
        o_ref[0] = jnp.concatenate(outs, axis=-1).astype(o_ref.dtype)

    if len(key_counts) == 1:
        body(key_counts[0])
    else:
        last_limit = ((first + tq - 1) & ~(CHUNK - 1)) + CHUNK
        lo = 0
        for L in key_counts:
            pl.when((last_limit > lo) & (last_limit <= L))(functools.partial(body, L))
            lo = L


ATTN_KEY_STEP = 512


def _attn(q, k, v, *, n_heads, group, dq, dv, pos0, tq, idx=None, top=0):
    B, T, _ = q.shape
    L = k.shape[1]
    assert T % tq == 0 and L % LANE == 0
    need = sorted({min(L, _round_up(_round_up(pos0 + (i + 1) * tq, CHUNK), ATTN_KEY_STEP)) for i in range(T // tq)})
    assert need[-1] == L or _round_up(pos0 + T, CHUNK) <= need[-1]
    assert idx is None or top <= need[0]
    ins = [q, k, v]
    specs = [pl.BlockSpec((1, tq, q.shape[2]), lambda b, i: (b, i, 0)),
             pl.BlockSpec((1, L, k.shape[2]), lambda b, i: (b, 0, 0)),
             pl.BlockSpec((1, L, v.shape[2]), lambda b, i: (b, 0, 0))]
    if idx is not None:
        qi, wi, ki = idx
        ins += [qi, wi, ki]
        specs += [pl.BlockSpec((1, tq, qi.shape[2]), lambda b, i: (b, i, 0)),
                  pl.BlockSpec((1, tq, wi.shape[2]), lambda b, i: (b, i, 0)),
                  pl.BlockSpec((1, L, ki.shape[2]), lambda b, i: (b, 0, 0))]
    return pl.pallas_call(
        functools.partial(_attn_kernel, n_heads=n_heads, group=group, dq=dq, dv=dv, tq=tq, pos0=pos0,
                          top=top, indexer=idx is not None, key_counts=tuple(need)),
        grid=(B, T // tq),
        in_specs=specs,
        out_specs=pl.BlockSpec((1, tq, n_heads * dv), lambda b, i: (b, i, 0)),
        out_shape=jax.ShapeDtypeStruct((B, T, n_heads * dv), BF16),
        compiler_params=_cparams(("parallel", "parallel")),
        name="dsa_attn" if idx is not None else "mla_attn",
    )(*ins)


ATTN_KB = 512
ATTN_KB_SHIFT = ATTN_KB.bit_length() - 1
SELECT_ALL = 1e9
MANTISSA_BITS = 23
ATTN_ROWS_PER_ITER = 1024


def _fold(x, op=jnp.add):
    acc = x[:, :LANE]
    for j in range(1, x.shape[1] // LANE):
        acc = op(acc, x[:, j * LANE:(j + 1) * LANE])
    return acc


def _topk_bias(tab_ref, qi_ref, wi_ref, ki_ref, SC, MS, *, nk, limit, lane_i, tq, top, wi_off):
    kb = ATTN_KB
    qi = qi_ref[0]
    wi = wi_ref[0][:, wi_off:wi_off + IDX_HEADS]
    zeros_l = jnp.zeros((tq, LANE), F32)
    zero = jnp.zeros((tq, 1), F32)
    kf = float(top)

    def lane_sum(body):
        acc = lax.fori_loop(0, nk, lambda j, a: a + _fold(body(j)), zeros_l)
        return jnp.sum(acc, axis=-1, keepdims=True)

    def ones_where(c):
        return jnp.where(c, 1.0, 0.0)

    def score_block(j):
        kij = ki_ref[0, pl.ds(pl.multiple_of(j * kb, kb), kb), :]
        sc = jnp.zeros((tq, kb), F32)
        for h in range(IDX_HEADS):
            d = lax.dot_general(qi[:, h * IDX_DIM:(h + 1) * IDX_DIM], kij, (((1,), (1,)), ((), ())),
                                preferred_element_type=F32)
            sc = sc + wi[:, h:h + 1] * jnp.maximum(d, 0.0)
        sc = jnp.where(lane_i + j * kb < limit, sc, NEG_INF)
        SC[j] = sc
        return ones_where(sc >= 0.0)

    c0 = lane_sum(score_block)
    neg = c0 < kf
    sgn = jnp.where(neg, -1.0, 1.0)
    kp = jnp.where(neg, kf, (nk * kb).astype(F32) - kf + 1.0)

    def flip(j, c):
        SC[j] = SC[j] * sgn
        return c

    lax.fori_loop(0, nk, flip, 0)

    def count_lt(cand):
        return lane_sum(lambda j: ones_where(SC[j] < cand))

    def exp_step(i, carry):
        e_cur, t_cur = carry
        cand = jnp.where(e_cur == 0.0, tab_ref[0, i], t_cur * tab_ref[1, i])
        ok = count_lt(cand) < kp
        return jnp.where(ok, e_cur + tab_ref[2, i], e_cur), jnp.where(ok, cand, t_cur)

    _, t_pow = lax.fori_loop(0, 8, exp_step, (zero, zero))

    def man_step(i, carry):
        t_cur, frac = carry
        cand = t_cur + frac
        ok = count_lt(cand) < kp
        return jnp.where(ok, cand, t_cur), frac * 0.5

    t_cur, _ = lax.fori_loop(0, MANTISSA_BITS, man_step, (t_pow, t_pow * 0.5))
    thr = t_cur * sgn

    def score(j):
        return SC[j] * sgn

    def key_idx(j):
        return (lane_i + j * kb).astype(F32)

    need = kf - lane_sum(lambda j: ones_where(score(j) > thr))
    n_eq = lane_sum(lambda j: ones_where(score(j) == thr))

    def index_cut():
        nbits = (SC.shape[0] * kb - 1).bit_length()

        def bit_step(i, carry):
            c_cur, bit = carry
            cand = c_cur + bit
            ok = lane_sum(lambda j: ones_where((score(j) == thr) & (key_idx(j) < cand))) < need
            return jnp.where(ok, cand, c_cur), bit * 0.5

        c_cur, _ = lax.fori_loop(0, nbits, bit_step, (zero, jnp.full((tq, 1), 2.0 ** (nbits - 1), F32)))
        return c_cur

    any_split = jnp.max(ones_where(n_eq > need)) > 0.0
    c_cut = lax.cond(any_split, index_cut, lambda: jnp.full((tq, 1), SELECT_ALL, F32))

    def write_bias(j, c):
        s = score(j)
        sel = (s > thr) | ((s == thr) & (key_idx(j) <= c_cut))
        MS[j] = jnp.where(sel & (lane_i + j * kb < limit), 0.0, NEG_INF)
        return c

    lax.fori_loop(0, nk, write_bias, 0)


def _topk_bias_t(tab_ref, qi_ref, wi_ref, ki_ref, SC, MS, *, nk, first, tq, top, wi_off):
    kb = ATTN_KB
    qi = qi_ref[0]
    w_t = jnp.transpose(wi_ref[0])[wi_off:wi_off + IDX_HEADS, :]
    part = 64
    zeros_p = jnp.zeros((part, tq), F32)
    zero = jnp.zeros((1, tq), F32)
    kf = float(top)
    pos_q = lax.broadcasted_iota(jnp.int32, (1, tq), 1) + first
    limit = (pos_q & ~(CHUNK - 1)) + CHUNK
    key_i = lax.broadcasted_iota(jnp.int32, (kb, tq), 0)

    def key_sum(body):
        def step(j, a):
            return a + jnp.sum(body(j).reshape(kb // part, part, tq), axis=0)

        return jnp.sum(lax.fori_loop(0, nk, step, zeros_p), axis=0, keepdims=True)

    def ones_where(c):
        return jnp.where(c, 1.0, 0.0)

    def score_block(j):
        kij = ki_ref[0, pl.ds(pl.multiple_of(j * kb, kb), kb), :]
        sc = jnp.zeros((kb, tq), F32)
        for h in range(IDX_HEADS):
            d = lax.dot_general(kij, qi[:, h * IDX_DIM:(h + 1) * IDX_DIM], (((1,), (1,)), ((), ())),
                                preferred_element_type=F32)
            sc = sc + w_t[h:h + 1, :] * jnp.maximum(d, 0.0)
        sc = jnp.where(key_i + j * kb < limit, sc, NEG_INF)
        SC[j] = sc
        return ones_where(sc >= 0.0)

    c0 = key_sum(score_block)
    neg = c0 < kf
    sgn = jnp.where(neg, -1.0, 1.0)
    kp = jnp.where(neg, kf, (nk * kb).astype(F32) - kf + 1.0)

    def flip(j, c):
        SC[j] = SC[j] * sgn
        return c

    lax.fori_loop(0, nk, flip, 0)

    def count_lt(cand):
        return key_sum(lambda j: ones_where(SC[j] < cand))

    def exp_step(i, carry):
        e_cur, t_cur = carry
        cand = jnp.where(e_cur == 0.0, tab_ref[0, i], t_cur * tab_ref[1, i])
        ok = count_lt(cand) < kp
        return jnp.where(ok, e_cur + tab_ref[2, i], e_cur), jnp.where(ok, cand, t_cur)

    _, t_pow = lax.fori_loop(0, 8, exp_step, (zero, zero))

    def man_step(i, carry):
        t_cur, frac = carry
        cand = t_cur + frac
        ok = count_lt(cand) < kp
        return jnp.where(ok, cand, t_cur), frac * 0.5

    t_cur, _ = lax.fori_loop(0, MANTISSA_BITS, man_step, (t_pow, t_pow * 0.5))
    thr = t_cur * sgn

    def score(j):
        return SC[j] * sgn

    def key_idx(j):
        return (key_i + j * kb).astype(F32)

    need = kf - key_sum(lambda j: ones_where(score(j) > thr))
    n_eq = key_sum(lambda j: ones_where(score(j) == thr))

    def index_cut():
        nbits = (SC.shape[0] * kb - 1).bit_length()

        def bit_step(i, carry):
            c_cur, bit = carry
            cand = c_cur + bit
            ok = key_sum(lambda j: ones_where((score(j) == thr) & (key_idx(j) < cand))) < need
            return jnp.where(ok, cand, c_cur), bit * 0.5

        c_cur, _ = lax.fori_loop(0, nbits, bit_step, (zero, jnp.full((1, tq), 2.0 ** (nbits - 1), F32)))
        return c_cur

    any_split = jnp.max(ones_where(n_eq > need)) > 0.0
    c_cut = lax.cond(any_split, index_cut, lambda: jnp.full((1, tq), SELECT_ALL, F32))

    def write_bias(j, c):
        s = score(j)
        sel = (s > thr) | ((s == thr) & (key_idx(j) <= c_cut))
        MS[j] = jnp.transpose(jnp.where(sel & (key_i + j * kb < limit), 0.0, NEG_INF))
        return c

    lax.fori_loop(0, nk, write_bias, 0)


def _attn2_kernel(*refs, n_kv, group, tq, pos0, top, indexer, wi_off):
    if indexer:
        (tab_ref, q_ref, k_ref, v_ref, x_ref, wo_ref, qi_ref, wi_ref, ki_ref, xo_ref,
         MS, LG, MACC, LACC, OACC, OH, OALL, SC) = refs
    else:
        q_ref, k_ref, v_ref, x_ref, wo_ref, xo_ref, MS, LG, MACC, LACC, OACC, OH, OALL = refs
    kb = ATTN_KB
    first = pos0 + pl.program_id(1) * tq
    last_limit = ((first + tq - 1) & ~(CHUNK - 1)) + CHUNK
    nk = (last_limit + (kb - 1)) >> ATTN_KB_SHIFT
    row = lax.broadcasted_iota(jnp.int32, (tq, 1), 0) + first
    limit = (row & ~(CHUNK - 1)) + CHUNK
    lane_i = lax.broadcasted_iota(jnp.int32, (tq, kb), 1)

    if indexer:
        _topk_bias_t(tab_ref, qi_ref, wi_ref, ki_ref, SC, MS, nk=nk, first=first, tq=tq, top=top, wi_off=wi_off)
    else:
        def causal_bias(j, c):
            MS[j] = jnp.where(lane_i + j * kb < limit, 0.0, NEG_INF)
            return c

        lax.fori_loop(0, nk, causal_bias, 0)

    hu = LG.shape[0]

    def per_kv_heads(gi, c):
        heads = [gi * hu + u for u in range(hu)]
        qs = [q_ref[0, g, 0] for g in heads]
        MACC[...] = jnp.full(MACC.shape, NEG_INF, F32)
        LACC[...] = jnp.zeros(LACC.shape, F32)
        OACC[...] = jnp.zeros(OACC.shape, F32)

        def logits_block(j, c_):
            keys = pl.ds(pl.multiple_of(j * kb, kb), kb)
            bias = MS[j][None]
            for u, g in enumerate(heads):
                lg = lax.dot_general(qs[u], k_ref[0, g, keys, :], (((1,), (1,)), ((), ())),
                                     preferred_element_type=F32)
                lg = (lg.reshape(group, tq, kb) + bias).reshape(group * tq, kb)
                LG[u, j] = lg
                MACC[u] = jnp.maximum(MACC[u], _fold(lg, jnp.maximum))
            return c_

        lax.fori_loop(0, nk, logits_block, 0)
        ms = [jnp.max(MACC[u], axis=-1, keepdims=True) for u in range(hu)]

        def value_block(j, c_):
            keys = pl.ds(pl.multiple_of(j * kb, kb), kb)
            for u, g in enumerate(heads):
                p = jnp.exp2(LG[u, j] - ms[u])
                LACC[u] += _fold(p)
                OACC[u] += jnp.dot(p.astype(BF16), v_ref[0, g, keys, :], preferred_element_type=F32)
            return c_

        lax.fori_loop(0, nk, value_block, 0)
        for u, g in enumerate(heads):
            OH[g] = (OACC[u] / jnp.sum(LACC[u], axis=-1, keepdims=True)).astype(OH.dtype)
        return c

    lax.fori_loop(0, n_kv // hu, per_kv_heads, 0)
    dv = OH.shape[2]
    for h in range(n_kv * group):
        OALL[:, h * dv:(h + 1) * dv] = OH[h // group, (h % group) * tq:(h % group + 1) * tq, :]
    xo_ref[0] = x_ref[0] + jnp.dot(OALL[...], wo_ref[...], preferred_element_type=F32)


def _head_major(a, tq, group):
    B, T, H, d = a.shape
    n_kv = H // group
    return jnp.transpose(a.reshape(B, T // tq, tq, n_kv, group, d), (0, 3, 1, 4, 2, 5)).reshape(
        B, n_kv, T // tq, group * tq, d)


def _attn2(qg, kt, vt, x, w_out, *, pos0, tq, group, idx=None, top=0, wi_off=0):
    B, n_kv, nq, rows, dq = qg.shape
    L, dv = kt.shape[2], vt.shape[3]
    T, D = x.shape[1], x.shape[2]
    kb = ATTN_KB
    assert nq * tq == T and rows == group * tq and L % kb == 0 and (idx is None or top <= kb)
    ins = [qg, kt, vt, x, w_out]
    specs = [pl.BlockSpec((1, n_kv, 1, rows, dq), lambda b, i: (b, 0, i, 0, 0)),
             pl.BlockSpec((1, n_kv, L, dq), lambda b, i: (b, 0, 0, 0)),
             pl.BlockSpec((1, n_kv, L, dv), lambda b, i: (b, 0, 0, 0)),
             pl.BlockSpec((1, tq, D), lambda b, i: (b, i, 0)),
             pl.BlockSpec(w_out.shape, lambda b, i: (0, 0))]
    hu = max(1, min(n_kv, ATTN_ROWS_PER_ITER // rows))
    assert n_kv % hu == 0
    scratch = [pltpu.VMEM((L // kb, tq, kb), F32), pltpu.VMEM((hu, L // kb, rows, kb), F32),
               pltpu.VMEM((hu, rows, LANE), F32), pltpu.VMEM((hu, rows, LANE), F32),
               pltpu.VMEM((hu, rows, dv), F32), pltpu.VMEM((n_kv, rows, dv), BF16),
               pltpu.VMEM((tq, n_kv * group * dv), BF16)]
    if idx is not None:
        qi, wi, ki = idx
        steps = [2 ** b for b in range(7, -1, -1)]
        tab = jnp.array([[2.0 ** (s - 127) for s in steps], [2.0 ** s if s < 128 else 1.0 for s in steps],
                         [float(s) for s in steps]], F32)
        ins = [tab] + ins + [qi, wi, ki]
        specs = ([pl.BlockSpec(memory_space=pltpu.SMEM)] + specs
                 + [pl.BlockSpec((1, tq, qi.shape[2]), lambda b, i: (b, i, 0)),
                    pl.BlockSpec((1, tq, wi.shape[2]), lambda b, i: (b, i, 0)),
                    pl.BlockSpec((1, L, ki.shape[2]), lambda b, i: (b, 0, 0))])
        scratch.append(pltpu.VMEM((L // kb, kb, tq), F32))
    return pl.pallas_call(
        functools.partial(_attn2_kernel, n_kv=n_kv, group=group, tq=tq, pos0=pos0, top=top,
                          indexer=idx is not None, wi_off=wi_off),
        grid=(B, nq),
        in_specs=specs,
        out_specs=pl.BlockSpec((1, tq, D), lambda b, i: (b, i, 0)),
        out_shape=jax.ShapeDtypeStruct((B, T, D), F32),
        scratch_shapes=scratch,
        compiler_params=_cparams(("parallel", "parallel")),
        name="dsa_attn" if idx is not None else "mla_attn",
    )(*ins)


def _rope_tables(pos, dh, rot, offset=0):
    half = rot // 2
    inv = ROPE_THETA ** (-jnp.arange(half, dtype=F32) / half)
    ang = pos.astype(F32)[:, None] * inv[None, :]
    cos, sin = jnp.cos(ang), jnp.sin(ang)
    T = pos.shape[0]
    pad = lambda n, v: jnp.full((T, n), v, F32)
    zh = pad(half, 0.0)
    lo, hi = offset, dh - offset - rot
    c = jnp.concatenate([pad(lo, 1.0), cos, cos, pad(hi, 1.0)], axis=1)
    s1 = jnp.concatenate([pad(lo, 0.0), -sin, zh, pad(hi, 0.0)], axis=1)
    s2 = jnp.concatenate([pad(lo, 0.0), zh, sin, pad(hi, 0.0)], axis=1)
    return tuple(jnp.tile(t, (1, LANE // dh)) for t in (c, s1, s2))


def _rope_lanes(x, c, s1, s2, half):
    return x * c + pltpu.roll(x, LANE - half, 1) * s1 + pltpu.roll(x, half, 1) * s2


def _dsa_proj_kernel(x_ref, g_ref, w_ref, c_ref, s1_ref, s2_ref, ts_ref, qh_ref, kf_ref, kt_ref, vf_ref, vt_ref,
                     qi_ref, tail_ref, kib_ref, *, tm):
    xf = x_ref[0]
    h = xf * lax.rsqrt(jnp.mean(xf * xf, axis=-1, keepdims=True) + NORM_EPS) * g_ref[...]
    acc = jnp.dot(h.astype(BF16), w_ref[...], preferred_element_type=F32)
    c, s1, s2 = c_ref[...], s1_ref[...], s2_ref[...]
    half = A_ROT // 2
    hd = A_HEAD_DIM
    group = A_HEADS // A_KV_HEADS
    rope = lambda xs: _rope_lanes(xs, c, s1, s2, half)
    slab = lambda off: acc[:, off:off + LANE]
    for s in range(A_O_Q // LANE):
        qs = (rope(slab(s * LANE)) * (hd ** -0.5 * LOG2E)).astype(BF16)
        for e in range(LANE // hd):
            head = s * (LANE // hd) + e
            u = head % group
            qh_ref[0, head // group, 0, u * tm:(u + 1) * tm, :] = qs[:, e * hd:(e + 1) * hd]
    ks = rope(slab(A_O_Q))
    vs = slab(A_O_K)
    kf_ref[0] = ks
    vf_ref[0] = vs
    for e in range(A_KV_HEADS):
        kt_ref[0, e] = ks[:, e * hd:(e + 1) * hd].astype(BF16)
        vt_ref[0, e] = vs[:, e * hd:(e + 1) * hd].astype(BF16)
    for s in range(IDX_HEADS * IDX_DIM // LANE):
        qi_ref[0, :, s * LANE:(s + 1) * LANE] = rope(slab(A_O_V + s * LANE)).astype(BF16)
    is_key = lax.broadcasted_iota(jnp.int32, (tm, LANE), 1) < IDX_DIM
    tl = _rope_lanes(slab(A_O_QI), jnp.where(is_key, c, 1.0), jnp.where(is_key, s1, 0.0),
                     jnp.where(is_key, s2, 0.0), half) * ts_ref[...]
    tail_ref[0] = tl
    kib_ref[0] = tl[:, :IDX_DIM].astype(BF16)


def _dsa_proj(x, g, w_in, tabs, tm):
    B, T, D = x.shape
    assert (A_HEAD_DIM, A_ROT) == (IDX_DIM, IDX_ROT) and A_KV_HEADS * A_HEAD_DIM == LANE
    assert A_O_QI % LANE == 0 and IDX_DIM + IDX_HEADS <= LANE and T % tm == 0
    n_in = _round_up(A_IN, LANE)
    group = A_HEADS // A_KV_HEADS
    hd = A_HEAD_DIM
    nq = T // tm
    lanes = jnp.arange(LANE)
    tail_scale = jnp.where(lanes < IDX_DIM, 1.0, jnp.where(lanes < IDX_DIM + IDX_HEADS,
                                                            (IDX_HEADS * IDX_DIM) ** -0.5, 0.0)).astype(F32)
    row = lambda n: pl.BlockSpec((1, tm, n), lambda b, i: (b, i, 0))
    tab = pl.BlockSpec((tm, LANE), lambda b, i: (i, 0))
    const = lambda a: pl.BlockSpec(a.shape, lambda b, i: (0,) * a.ndim)
    kvh = pl.BlockSpec((1, A_KV_HEADS, tm, hd), lambda b, i: (b, 0, i, 0))
    w = _pad_cols(w_in, n_in).astype(BF16)
    g2 = g.reshape(1, D)
    ts = tail_scale.reshape(1, LANE)
    return pl.pallas_call(
        functools.partial(_dsa_proj_kernel, tm=tm),
        grid=(B, nq),
        in_specs=[row(D), const(g2), const(w), tab, tab, tab, const(ts)],
        out_specs=[pl.BlockSpec((1, A_KV_HEADS, 1, group * tm, hd), lambda b, i: (b, 0, i, 0, 0)),
                   row(LANE), kvh, row(LANE), kvh, row(IDX_HEADS * IDX_DIM), row(LANE), row(IDX_DIM)],
        out_shape=[jax.ShapeDtypeStruct((B, A_KV_HEADS, nq, group * tm, hd), BF16),
                   jax.ShapeDtypeStruct((B, T, LANE), F32), jax.ShapeDtypeStruct((B, A_KV_HEADS, T, hd), BF16),
                   jax.ShapeDtypeStruct((B, T, LANE), F32), jax.ShapeDtypeStruct((B, A_KV_HEADS, T, hd), BF16),
                   jax.ShapeDtypeStruct((B, T, IDX_HEADS * IDX_DIM), BF16),
                   jax.ShapeDtypeStruct((B, T, LANE), F32), jax.ShapeDtypeStruct((B, T, IDX_DIM), BF16)],
        compiler_params=_cparams(("parallel", "parallel")),
        name="dsa_proj",
    )(x, g2, w, *tabs, ts)


def _rms(xf, gain):
    return xf * lax.rsqrt(jnp.mean(xf * xf, axis=-1, keepdims=True) + NORM_EPS) * gain


def _mla_write_kv(lat, kpe_slab, wuk_ref, wuv_ref, kt_ref, vt_ref):
    lb = lat.astype(BF16)
    kn = jnp.dot(lb, wuk_ref[...], preferred_element_type=F32)
    vv = jnp.dot(lb, wuv_ref[...], preferred_element_type=F32)
    for h in range(C_HEADS):
        kt_ref[0, h] = (kn[:, h * LANE:(h + 1) * LANE] + kpe_slab).astype(BF16)
        vt_ref[0, h] = vv[:, h * C_V:(h + 1) * C_V].astype(BF16)


def _mla_proj_kernel(x_ref, g_ref, win_ref, gq_ref, gkv_ref, wuq_ref, wuk_ref, wuv_ref, c_ref, s1_ref, s2_ref,
                     qh_ref, kt_ref, vt_ref, lat_ref, kpe_ref):
    h = _rms(x_ref[0], g_ref[...]).astype(BF16)
    proj = jnp.dot(h, win_ref[...], preferred_element_type=F32)
    c, s1, s2 = c_ref[...], s1_ref[...], s2_ref[...]
    half = C_ROPE // 2
    q = jnp.dot(_rms(proj[:, :C_Q_RANK], gq_ref[...]).astype(BF16), wuq_ref[...], preferred_element_type=F32)
    scale = (C_NOPE + C_ROPE) ** -0.5 * LOG2E
    for hd in range(C_HEADS):
        qh_ref[0, hd, 0] = (_rope_lanes(q[:, hd * LANE:(hd + 1) * LANE], c, s1, s2, half) * scale).astype(BF16)
    lat = _rms(proj[:, C_Q_RANK:C_Q_RANK + C_KV_RANK], gkv_ref[...])
    lat_ref[0] = lat
    kpe_slab = _rope_lanes(proj[:, C_Q_RANK + C_KV_RANK:], c, s1, s2, half)
    kpe_ref[0] = kpe_slab[:, C_NOPE:C_NOPE + C_ROPE]
    _mla_write_kv(lat, kpe_slab, wuk_ref, wuv_ref, kt_ref, vt_ref)


def _mla_kv_kernel(lat_ref, kpe_ref, wuk_ref, wuv_ref, kt_ref, vt_ref):
    _mla_write_kv(lat_ref[0], kpe_ref[0], wuk_ref, wuv_ref, kt_ref, vt_ref)


def _mla_weights(w_in, w_uq, w_ukv):
    D = w_in.shape[0]
    zc = lambda rows, n: jnp.zeros((rows, n), w_in.dtype)
    w_in2 = jnp.concatenate([w_in[:, :C_Q_RANK + C_KV_RANK], zc(D, C_NOPE), w_in[:, C_Q_RANK + C_KV_RANK:],
                             zc(D, LANE - C_NOPE - C_ROPE)], axis=1)
    pad_heads = lambda w, d: jnp.pad(w.reshape(w.shape[0], C_HEADS, d), ((0, 0), (0, 0), (0, LANE - d))).reshape(
        w.shape[0], C_HEADS * LANE)
    w_uq2 = pad_heads(w_uq, C_NOPE + C_ROPE)
    ukv = w_ukv.reshape(C_KV_RANK, C_HEADS, C_NOPE + C_V)
    w_uk2 = pad_heads(ukv[..., :C_NOPE].reshape(C_KV_RANK, C_HEADS * C_NOPE), C_NOPE)
    w_uv2 = ukv[..., C_NOPE:].reshape(C_KV_RANK, C_HEADS * C_V)
    return tuple(a.astype(BF16) for a in (w_in2, w_uq2, w_uk2, w_uv2))


def _mla_proj(x, g, g_q, g_kv, weights, tabs, tm):
    B, T, D = x.shape
    w_in2, w_uq2, w_uk2, w_uv2 = weights
    nq = T // tm
    row = lambda n: pl.BlockSpec((1, tm, n), lambda b, i: (b, i, 0))
    tab = pl.BlockSpec((tm, LANE), lambda b, i: (i, 0))
    const = lambda a: pl.BlockSpec(a.shape, lambda b, i: (0,) * a.ndim)
    heads = lambda d: pl.BlockSpec((1, C_HEADS, tm, d), lambda b, i: (b, 0, i, 0))
    vec = lambda a: a.reshape(1, a.shape[0])
    return pl.pallas_call(
        _mla_proj_kernel,
        grid=(B, nq),
        in_specs=[row(D), const(vec(g)), const(w_in2), const(vec(g_q)), const(vec(g_kv)), const(w_uq2),
                  const(w_uk2), const(w_uv2), tab, tab, tab],
        out_specs=[pl.BlockSpec((1, C_HEADS, 1, tm, LANE), lambda b, i: (b, 0, i, 0, 0)), heads(LANE), heads(C_V),
                   row(C_KV_RANK), row(C_ROPE)],
        out_shape=[jax.ShapeDtypeStruct((B, C_HEADS, nq, tm, LANE), BF16),
                   jax.ShapeDtypeStruct((B, C_HEADS, T, LANE), BF16), jax.ShapeDtypeStruct((B, C_HEADS, T, C_V), BF16),
                   jax.ShapeDtypeStruct((B, T, C_KV_RANK), F32), jax.ShapeDtypeStruct((B, T, C_ROPE), F32)],
        compiler_params=_cparams(("parallel", "parallel")),
        name="mla_proj",
    )(x, vec(g), w_in2, vec(g_q), vec(g_kv), w_uq2, w_uk2, w_uv2, *tabs)


def _mla_kv(lat, kpe_slab, w_uk2, w_uv2, tm):
    B, P, _ = lat.shape
    row = lambda n: pl.BlockSpec((1, tm, n), lambda b, i: (b, i, 0))
    const = lambda a: pl.BlockSpec(a.shape, lambda b, i: (0,) * a.ndim)
    heads = lambda d: pl.BlockSpec((1, C_HEADS, tm, d), lambda b, i: (b, 0, i, 0))
    return pl.pallas_call(
        _mla_kv_kernel,
        grid=(B, P // tm),
        in_specs=[row(C_KV_RANK), row(LANE), const(w_uk2), const(w_uv2)],
        out_specs=[heads(LANE), heads(C_V)],
        out_shape=[jax.ShapeDtypeStruct((B, C_HEADS, P, LANE), BF16), jax.ShapeDtypeStruct((B, C_HEADS, P, C_V), BF16)],
        compiler_params=_cparams(("parallel", "parallel")),
        name="mla_kv",
    )(lat, kpe_slab, w_uk2, w_uv2)


def _head_sum(x, bo_ref):
    bw = bo_ref.shape[0]
    hi = x.astype(BF16)
    lo = (x - hi.astype(F32)).astype(BF16)
    bo = bo_ref[...]
    return jnp.concatenate(
        [jnp.dot(hi[:, j * bw:(j + 1) * bw], bo, preferred_element_type=F32)
         + jnp.dot(lo[:, j * bw:(j + 1) * bw], bo, preferred_element_type=F32) for j in range(x.shape[1] // bw)],
        axis=-1)


RWKV_HALO = 8


def _rwkv_pre_kernel(x_ref, xh_ref, sh_ref, g_ref, mu_ref, vec_ref, wr_ref, wk_ref, wv_ref, w1_ref, w2_ref,
                     a1_ref, a2_ref, g1_ref, g2_ref, bo_ref,
                     rp_ref, w_ref, k_ref, v_ref, a_ref, b_ref, yc_ref, bonus_ref, gate_ref, *, tm):
    i = pl.program_id(1)
    gain = g_ref[...]

    def norm(xf):
        return xf * lax.rsqrt(jnp.mean(xf * xf, axis=-1, keepdims=True) + NORM_EPS) * gain

    h = norm(x_ref[0])
    before = jnp.where(i > 0, norm(xh_ref[0])[RWKV_HALO - 1:RWKV_HALO, :], sh_ref[0])
    first = lax.broadcasted_iota(jnp.int32, (tm, 1), 0) == 0
    xx = jnp.where(first, before, pltpu.roll(h, 1, 0)) - h
    mu = mu_ref[...]
    vec = vec_ref[...]
    w0, a0, k_k, k_a, r_k = (vec[j:j + 1, :] for j in range(5))

    def mix(j):
        return (h + xx * mu[j:j + 1, :]).astype(BF16)

    dot = lambda a_, w_: jnp.dot(a_, w_[...], preferred_element_type=F32)
    r = dot(mix(0), wr_ref)
    wl = dot(jnp.tanh(dot(mix(1), w1_ref)).astype(BF16), w2_ref)
    k = dot(mix(2), wk_ref)
    v = dot(mix(3), wv_ref)
    al = dot(dot(mix(4), a1_ref).astype(BF16), a2_ref)
    gate_ref[0] = dot(jax.nn.sigmoid(dot(mix(5), g1_ref)).astype(BF16), g2_ref)
    z = -(w0 + wl)
    softplus = jnp.maximum(z, 0.0) + jnp.log(1.0 + jnp.exp(-jnp.abs(z)))
    decay = jnp.exp(-jnp.exp(-softplus - 0.5))
    a = jax.nn.sigmoid(a0 + al)
    kk = k * k_k
    kk = kk / jnp.maximum(jnp.sqrt(_head_sum(kk * kk, bo_ref)), 1e-12)
    k = k * (1.0 + (a - 1.0) * k_a)
    b = kk * a
    rp_ref[0] = decay * r - kk * _head_sum(b * r, bo_ref)
    w_ref[0] = decay
    k_ref[0] = k
    v_ref[0] = v
    a_ref[0] = -kk
    b_ref[0] = b
    yc_ref[0] = v * _head_sum(k * r, bo_ref)
    bonus_ref[0] = _head_sum(r * k * r_k, bo_ref) * v


def _rwkv_post_kernel(y_ref, yc_ref, bonus_ref, gate_ref, x_ref, ln_ref, wo_ref, bo_ref, o_ref):
    n = float(B_HEAD_DIM)
    y = y_ref[0] + yc_ref[0]
    d = y - _head_sum(y, bo_ref) / n
    var = _head_sum(d * d, bo_ref) / n
    ln = ln_ref[...]
    yn = d * lax.rsqrt(var + B_GN_EPS) * ln[0:1, :] + ln[1:2, :] + bonus_ref[0]
    o_ref[0] = x_ref[0] + jnp.dot((yn * gate_ref[0]).astype(BF16), wo_ref[...], preferred_element_type=F32)


def _block_ones():
    blk = jnp.arange(WKV_BW) // B_HEAD_DIM
    return (blk[:, None] == blk[None, :]).astype(BF16)


def _rwkv_pre(x, shift_prev, g, mu, vecs, ws, tm):
    B, T, D = x.shape
    H = RWKV_HALO
    assert T % tm == 0 and tm % H == 0
    nh = tm // H
    bo = _block_ones()
    row = pl.BlockSpec((1, tm, D), lambda b, i: (b, i, 0))
    const = lambda a: pl.BlockSpec(a.shape, lambda b, i: (0,) * a.ndim)
    pad8 = lambda a: jnp.pad(a, ((0, 8 - a.shape[0]), (0, 0)))
    mu8, vec8, g2 = pad8(mu), pad8(vecs), g.reshape(1, D)
    return pl.pallas_call(
        functools.partial(_rwkv_pre_kernel, tm=tm),
        grid=(B, T // tm),
        in_specs=[row, pl.BlockSpec((1, H, D), lambda b, i: (b, jnp.maximum(i * nh - 1, 0), 0)),
                  pl.BlockSpec((1, 1, D), lambda b, i: (b, 0, 0)), const(g2), const(mu8), const(vec8)]
                 + [const(a) for a in ws] + [const(bo)],
        out_specs=[row] * 9,
        out_shape=[jax.ShapeDtypeStruct((B, T, D), F32)] * 9,
        compiler_params=_cparams(("parallel", "parallel")),
        name="rwkv_pre",
    )(x, x, shift_prev.reshape(B, 1, D), g2, mu8, vec8, *ws, bo)


def _rwkv_post(y, yc, bonus, gate, x, ln, w_out, tm):
    B, T, D = x.shape
    bo = _block_ones()
    row = pl.BlockSpec((1, tm, D), lambda b, i: (b, i, 0))
    const = lambda a: pl.BlockSpec(a.shape, lambda b, i: (0,) * a.ndim)
    return pl.pallas_call(
        _rwkv_post_kernel,
        grid=(B, T // tm),
        in_specs=[row] * 5 + [const(ln), const(w_out), const(bo)],
        out_specs=row,
        out_shape=jax.ShapeDtypeStruct((B, T, D), F32),
        compiler_params=_cparams(("parallel", "parallel")),
        name="rwkv_post",
    )(y, yc, bonus, gate, x, ln, w_out, bo)


WKV_SUB = 8
WKV_NB = 4
WKV_BW = 2 * LANE


def _wkv_kernel(rp_ref, w_ref, k_ref, v_ref, a_ref, b_ref, s0_ref, bo_ref, dg_ref, hs_ref, y_ref, sT_ref, S, VB,
                *, tc, nb):
    c = pl.program_id(1)
    n = B_HEAD_DIM
    D = D_MODEL
    bw = WKV_BW
    nt = D // bw
    seg = nt * n

    @pl.when(c == 0)
    def _():
        S[...] = s0_ref[...]

    def tiles(x):
        return jnp.concatenate([x[:, j * bw:(j + 1) * bw] for j in range(nt)], axis=0)

    def untile(x):
        return jnp.concatenate([x[j * n:(j + 1) * n, :] for j in range(nt)], axis=-1)

    def block(sc, carry):
        base = pl.multiple_of(sc * WKV_SUB, WKV_SUB)
        rows = pl.ds(base, WKV_SUB)
        bo = bo_ref[...]
        dg = dg_ref[...]
        ins = []
        for bi in range(nb):
            r8, w8, k8, v8, a8, b8 = (x[bi, rows, :] for x in (rp_ref, w_ref, k_ref, v_ref, a_ref, b_ref))
            ins.append((r8, w8, k8, a8, b8))
            vd = jnp.concatenate([tiles((dg * v8[u:u + 1, :]).astype(BF16)) for u in range(WKV_SUB)], axis=0)
            VB[bi] = jnp.dot(vd, bo, preferred_element_type=F32)
        hsel = hs_ref[...]
        for u in range(WKV_SUB):
            sas = []
            for bi in range(nb):
                r8, w8, k8, a8, b8 = ins[bi]
                s = S[bi]
                pa = (s * a8[u:u + 1, :]).astype(BF16)
                sas.append(jnp.dot(tiles(pa), bo, preferred_element_type=F32))
                pr = (s * r8[u:u + 1, :]).astype(BF16)
                y_ref[bi, base + u] = lax.dot_general(hsel, pr, (((1,), (1,)), ((), ())),
                                                      preferred_element_type=F32)
            for bi in range(nb):
                r8, w8, k8, a8, b8 = ins[bi]
                S[bi] = (S[bi] * w8[u:u + 1, :] + untile(sas[bi]) * b8[u:u + 1, :]
                         + untile(VB[bi, u * seg:(u + 1) * seg, :]) * k8[u:u + 1, :])
        return carry

    lax.fori_loop(0, tc // WKV_SUB, block, 0)

    @pl.when(c == pl.num_programs(1) - 1)
    def _():
        sT_ref[...] = S[...]


def _wkv(rp, w, k, v, a, b, s0):
    B, T, D = rp.shape
    n = B_HEAD_DIM
    H = B_HEADS
    nb = WKV_NB if B % WKV_NB == 0 else 1
    tc = _row_tile(T, 128)
    assert tc % WKV_SUB == 0
    bw = WKV_BW
    s0t = jnp.transpose(s0, (0, 2, 1, 3)).reshape(B, n, D)
    bo = _block_ones()
    dg =(jnp.arange(n)[:, None] == (jnp.arange(D) % n)[None, :]).astype(F32)
    hsel = (jnp.arange(H)[:, None] == (jnp.arange(D) // n)[None, :]).astype(BF16)
    seq = pl.BlockSpec((nb, tc, D), lambda bi, c: (bi, c, 0))
    st = pl.BlockSpec((nb, n, D), lambda bi, c: (bi, 0, 0))
    const = lambda a_: pl.BlockSpec(a_.shape, lambda bi, c: (0, 0))
    y, sT = pl.pallas_call(
        functools.partial(_wkv_kernel, tc=tc, nb=nb),
        grid=(B // nb, T // tc),
        in_specs=[seq] * 6 + [st, const(bo), const(dg), const(hsel)],
        out_specs=[pl.BlockSpec((nb, tc, H, n), lambda bi, c: (bi, c, 0, 0)), st],
        out_shape=[jax.ShapeDtypeStruct((B, T, H, n), F32), jax.ShapeDtypeStruct((B, n, D), F32)],
        scratch_shapes=[pltpu.VMEM((nb, n, D), F32), pltpu.VMEM((nb, WKV_SUB * (D // bw) * n, bw), F32)],
        compiler_params=_cparams(("parallel", "arbitrary")),
        name="wkv",
    )(rp, w, k, v, a, b, s0t, bo, dg, hsel)
    return y.reshape(B, T, D), jnp.transpose(sT.reshape(B, n, H, n), (0, 2, 1, 3))


FFN_HALO = SUBLANE_BF16


FFN_CW = 2 * LANE


def _ffn_body(x_ref, xh_ref, g_ref, wug_ref, wuv_ref, cg_ref, cv_ref, pg_ref, pv_ref, wd_ref, gf_ref,
              o_ref, st_ref, y_ref, hn, ug, uv, act, acc, *, tm, nb):
    i = pl.program_id(1)
    H = FFN_HALO
    cw_ = FFN_CW
    seg = tm + H

    def norm(xf, gain):
        return xf * lax.rsqrt(jnp.mean(xf * xf, axis=-1, keepdims=True) + NORM_EPS) * gain

    for s in range(nb):
        hn[s * seg:s * seg + H, :] = jnp.where(i > 0, norm(xh_ref[s], g_ref[...]), 0.0).astype(BF16)
        hn[s * seg + H:(s + 1) * seg, :] = norm(x_ref[s], g_ref[...]).astype(BF16)
    h = hn[...]
    first = jnp.where(i == 0, 1.0, 0.0)

    def conv(u, r0, taps):
        return (taps[3:4, :] + u[r0 - 2:r0 - 2 + tm, :] * taps[0:1, :] + u[r0 - 1:r0 - 1 + tm, :] * taps[1:2, :]
                + u[r0:r0 + tm, :] * taps[2:3, :])

    for c in range(D_FF // cw_):
        cols = slice(c * cw_, (c + 1) * cw_)
        for half, (u_scr, w_ref, p_ref) in enumerate(((ug, wug_ref, pg_ref), (uv, wuv_ref, pv_ref))):
            u = jnp.dot(h, w_ref[:, cols], preferred_element_type=F32)
            u_scr[c % 2] = u
            for s in range(nb):
                u_scr[c % 2, s * seg:s * seg + H, :] = u[s * seg:s * seg + H, :] + p_ref[s, :, cols] * first
                st_ref[s, 0, :, c * cw_ + half * D_FF:(c + 1) * cw_ + half * D_FF] = u[(s + 1) * seg - 8:(s + 1) * seg, :]
        for s in range(nb):
            gate = conv(ug.at[c % 2], s * seg + H, cg_ref[:, cols])
            val = conv(uv.at[c % 2], s * seg + H, cv_ref[:, cols])
            act[s * tm:(s + 1) * tm, :] = (gate * jax.nn.sigmoid(gate) * val).astype(BF16)
        d = jnp.dot(act[...], wd_ref[cols, :], preferred_element_type=F32)
        if c == 0:
            acc[...] = d
        else:
            acc[...] += d
    for s in range(nb):
        out = x_ref[s] + acc[s * tm:(s + 1) * tm, :]
        o_ref[s] = out
        if gf_ref is not None:
            y_ref[s] = norm(out, gf_ref[...])


def _ffn_kernel(*refs, tm, nb, final):
    if final:
        (x_ref, xh_ref, g_ref, wug_ref, wuv_ref, cg_ref, cv_ref, pg_ref, pv_ref, wd_ref, gf_ref,
         o_ref, st_ref, y_ref, hn, ug, uv, act, acc) = refs
    else:
        (x_ref, xh_ref, g_ref, wug_ref, wuv_ref, cg_ref, cv_ref, pg_ref, pv_ref, wd_ref,
         o_ref, st_ref, hn, ug, uv, act, acc) = refs
        gf_ref = y_ref = None
    _ffn_body(x_ref, xh_ref, g_ref, wug_ref, wuv_ref, cg_ref, cv_ref, pg_ref, pv_ref, wd_ref, gf_ref,
              o_ref, st_ref, y_ref, hn, ug, uv, act, acc, tm=tm, nb=nb)


FFN_ROWS = 512


def _ffn(x, g, wug, wuv, cg, cv, prev, wd, final_g=None):
    B, T, D = x.shape
    F = D_FF
    H = FFN_HALO
    tm = min(FFN_ROWS, T)
    nb = _row_tile(B, max(1, FFN_ROWS // tm))
    assert T % tm == 0 and tm % H == 0 and F % FFN_CW == 0 and tm >= 8
    nh = tm // H
    nt = T // tm
    prev_h = jnp.pad(prev, ((0, 0), (H - prev.shape[1], 0), (0, 0)))
    pg, pv = prev_h[:, :, :F], prev_h[:, :, F:]
    const = lambda a: pl.BlockSpec(a.shape, lambda b, i: (0,) * a.ndim, pipeline_mode=pl.Buffered(1))
    row = pl.BlockSpec((nb, tm, D), lambda b, i: (b, i, 0))
    g2 = g.reshape(1, D)
    ins = [x, x, g2, wug, wuv, cg, cv, pg, pv, wd]
    specs = [row, pl.BlockSpec((nb, H, D), lambda b, i: (b, jnp.maximum(i * nh - 1, 0), 0)),
             const(g2), const(wug), const(wuv), const(cg), const(cv),
             pl.BlockSpec((nb, H, F), lambda b, i: (b, 0, 0)), pl.BlockSpec((nb, H, F), lambda b, i: (b, 0, 0)),
             const(wd)]
    outs = [row, pl.BlockSpec((nb, 1, 8, 2 * F), lambda b, i: (b, i, 0, 0))]
    shapes = [jax.ShapeDtypeStruct((B, T, D), F32), jax.ShapeDtypeStruct((B, nt, 8, 2 * F), F32)]
    if final_g is not None:
        gf = final_g.reshape(1, D)
        ins.append(gf)
        specs.append(const(gf))
        outs.append(row)
        shapes.append(jax.ShapeDtypeStruct((B, T, D), F32))
    res = pl.pallas_call(
        functools.partial(_ffn_kernel, tm=tm, nb=nb, final=final_g is not None),
        grid=(B // nb, nt),
        in_specs=specs,
        out_specs=outs,
        out_shape=shapes,
        scratch_shapes=[pltpu.VMEM((nb * (tm + H), D), BF16), pltpu.VMEM((2, nb * (tm + H), FFN_CW), F32),
                        pltpu.VMEM((2, nb * (tm + H), FFN_CW), F32), pltpu.VMEM((nb * tm, FFN_CW), BF16),
                        pltpu.VMEM((nb * tm, D), F32)],
        compiler_params=_cparams(("parallel", "parallel")),
        name="conv_ffn",
    )(*ins)
    state = res[1][:, nt - 1, 8 - (CONV_W - 1):, :]
    return (res[0], state) + tuple(res[2:])


def _rope(x, pos, rot):
    half = rot // 2
    inv = ROPE_THETA ** (-jnp.arange(half, dtype=F32) / half)
    ang = pos.astype(F32)[:, None] * inv[None, :]
    shape = (ang.shape[0],) + (1,) * (x.ndim - 3) + (half,)
    cos = jnp.cos(ang).reshape(shape)
    sin = jnp.sin(ang).reshape(shape)
    x1 = x[..., :half]
    x2 = x[..., half:rot]
    return jnp.concatenate([x1 * cos - x2 * sin, x1 * sin + x2 * cos, x[..., rot:]], axis=-1)


def _pad_cols(w, n):
    return jnp.pad(w, ((0, 0), (0, n - w.shape[1])))


def _pad_keys(a, L):
    return jnp.pad(a, ((0, 0), (0, L - a.shape[1])) + ((0, 0),) * (a.ndim - 2))


def _dsa_layer(x, pos, past_k, past_v, past_ki, g, w_in, w_out):
    B, T, D = x.shape
    tq = min(256, T)
    qh, kf, kt, vf, vt, qi, tail, kib = _dsa_proj(x, g, w_in, _rope_tables(pos, A_HEAD_DIM, A_ROT), tq)
    P = past_k.shape[1]
    L = P + T
    Lp = _round_up(L, ATTN_KB)
    if Lp != T:
        past = lambda a: jnp.transpose(a, (0, 2, 1, 3)).astype(BF16)
        kt = jnp.pad(jnp.concatenate([past(past_k), kt], axis=2), ((0, 0), (0, 0), (0, Lp - L), (0, 0)))
        vt = jnp.pad(jnp.concatenate([past(past_v), vt], axis=2), ((0, 0), (0, 0), (0, Lp - L), (0, 0)))
        kib = _pad_keys(jnp.concatenate([past_ki.astype(BF16), kib], axis=1), Lp)
    x = _attn2(qh, kt, vt, x, w_out.astype(BF16), pos0=P, tq=tq, group=A_HEADS // A_KV_HEADS,
               idx=(qi, tail, kib), top=min(TOPK_MAX, L // 4), wi_off=IDX_DIM)
    kv_rows = lambda a: a.reshape(B, T, A_KV_HEADS, A_HEAD_DIM)
    return x, kv_rows(kf), kv_rows(vf), tail[..., :IDX_DIM]


def _rwkv_layer(x, shift_prev, S0, g, mu, w_rkv, w0, w1, w2, a0, a1, a2, g1, g2, k_k, k_a, r_k, ln_w, ln_b,
                w_out):
    B, T, D = x.shape
    tm = min(256, T)
    bf = lambda a: a.astype(BF16)
    vecs = jnp.stack([w0, a0, k_k, k_a, r_k.reshape(D)], axis=0)
    ws = [bf(w_rkv[0]), bf(w_rkv[1]), bf(w_rkv[2]), bf(w1), bf(w2), bf(a1), bf(a2), bf(g1), bf(g2)]
    rp, decay, k, v, a_vec, b_vec, yc, bonus, gate = _rwkv_pre(x, shift_prev, g, mu, vecs, ws, tm)
    y, S = _wkv(rp, decay, k, v, a_vec, b_vec, S0)
    ln = jnp.pad(jnp.stack([ln_w, ln_b], axis=0), ((0, 6), (0, 0)))
    x_new = _rwkv_post(y, yc, bonus, gate, x, ln, bf(w_out), tm)
    assert T >= 8
    shift = _norm(x[:, T - 8:].reshape(B * 8, D), g).reshape(B, 8, D)[:, -1]
    return x_new, shift, S


def _mla_layer(x, pos, past_lat, past_rope, g, w_in, g_q, g_kv, w_uq, w_ukv, w_out):
    B, T, D = x.shape
    tq = min(256, T)
    weights = _mla_weights(w_in, w_uq, w_ukv)
    tabs = _rope_tables(pos, LANE, C_ROPE, offset=C_NOPE)
    qh, kt, vt, lat, kpe = _mla_proj(x, g, g_q, g_kv, weights, tabs, tq)
    P = past_lat.shape[1]
    L = P + T
    Lp = _round_up(L, ATTN_KB)
    if Lp != T:
        past_slab = jnp.pad(past_rope, ((0, 0), (0, 0), (C_NOPE, LANE - C_NOPE - C_ROPE)))
        kt_p, vt_p = _mla_kv(past_lat, past_slab, weights[2], weights[3], _row_tile(P, 256))
        kt = jnp.pad(jnp.concatenate([kt_p, kt], axis=2), ((0, 0), (0, 0), (0, Lp - L), (0, 0)))
        vt = jnp.pad(jnp.concatenate([vt_p, vt], axis=2), ((0, 0), (0, 0), (0, Lp - L), (0, 0)))
    x = _attn2(qh, kt, vt, x, w_out.astype(BF16), pos0=P, tq=tq, group=1)
    return x, lat, kpe


def _ffn_layer(x, prev, g, w_up, w_conv, b_conv, w_down, final_g=None):
    F = D_FF
    taps = jnp.concatenate([w_conv, b_conv[None, :], jnp.zeros((8 - CONV_W - 1, 2 * F), F32)], axis=0)
    w_up = w_up.astype(BF16)
    return _ffn(x, g, w_up[:, :F], w_up[:, F:], taps[:, :F], taps[:, F:], prev, w_down.astype(BF16), final_g)


def _trunk(x, pos0, st, w):
    B, T, D = x.shape
    pos = pos0 + jnp.arange(T, dtype=jnp.int32)
    new = {name: [] for name in ('a_k', 'a_v', 'a_idx', 'b_wkv', 'b_shift', 'c_lat', 'c_rope', 'ffn')}
    for i in range(DEPTH):
        j = i // N_MIXERS
        kind = i % N_MIXERS
        if kind == 0:
            x, k, v, ki = _dsa_layer(x, pos, st['a_k'][j], st['a_v'][j], st['a_idx'][j], w['n_mix'][i],
                                     w['a_w_in'][j], w['a_w_out'][j])
            new['a_k'].append(k)
            new['a_v'].append(v)
            new['a_idx'].append(ki)
        elif kind == 1:
            x, shift, S = _rwkv_layer(x, st['b_shift'][j], st['b_wkv'][j], w['n_mix'][i], w['b_mu'][j],
                                      w['b_w_rkv'][j], w['b_w0'][j], w['b_w1'][j], w['b_w2'][j], w['b_a0'][j],
                                      w['b_a1'][j], w['b_a2'][j], w['b_g1'][j], w['b_g2'][j], w['b_k_k'][j],
                                      w['b_k_a'][j], w['b_r_k'][j], w['b_ln_w'][j], w['b_ln_b'][j],
                                      w['b_w_out'][j])
            new['b_shift'].append(shift)
            new['b_wkv'].append(S)
        else:
            x, lat, kpe = _mla_layer(x, pos, st['c_lat'][j], st['c_rope'][j], w['n_mix'][i], w['c_w_in'][j],
                                     w['c_g_q'][j], w['c_g_kv'][j], w['c_w_uq'][j], w['c_w_ukv'][j],
                                     w['c_w_out'][j])
            new['c_lat'].append(lat)
            new['c_rope'].append(kpe)
        last = i == DEPTH - 1
        res = _ffn_layer(x, st['ffn'][i], w['n_ffn'][i], w['f_w_up'][i], w['f_w_conv'][i],
                         w['f_b_conv'][i], w['f_w_down'][i], w['n_final'] if last else None)
        x = res[0]
        new['ffn'].append(res[1])
    return res[2], {name: jnp.stack(rows, axis=0) for name, rows in new.items()}


def kernel(x_prompt, x_sample, cache_a_k, cache_a_v, cache_a_idx, state_b_wkv, state_b_shift,
           cache_c_latent, cache_c_rope, state_ffn_conv, n_mix, n_ffn, n_final, a_w_in, a_w_out,
           b_mu, b_w_rkv, b_w0, b_w1, b_w2, b_a0, b_a1, b_a2, b_g1, b_g2, b_k_k, b_k_a, b_r_k,
           b_ln_w, b_ln_b, b_w_out, c_w_in, c_g_q, c_g_kv, c_w_uq, c_w_ukv, c_w_out,
           f_w_up, f_w_conv, f_b_conv, f_w_down):
    w = dict(n_mix=n_mix, n_ffn=n_ffn, n_final=n_final, a_w_in=a_w_in, a_w_out=a_w_out,
             b_mu=b_mu, b_w_rkv=b_w_rkv, b_w0=b_w0, b_w1=b_w1, b_w2=b_w2, b_a0=b_a0, b_a1=b_a1,
             b_a2=b_a2, b_g1=b_g1, b_g2=b_g2, b_k_k=b_k_k, b_k_a=b_k_a, b_r_k=b_r_k,
             b_ln_w=b_ln_w, b_ln_b=b_ln_b, b_w_out=b_w_out, c_w_in=c_w_in, c_g_q=c_g_q,
             c_g_kv=c_g_kv, c_w_uq=c_w_uq, c_w_ukv=c_w_ukv, c_w_out=c_w_out,
             f_w_up=f_w_up, f_w_conv=f_w_conv, f_b_conv=f_b_conv, f_w_down=f_w_down)
    Bp, Tp, D = x_prompt.shape
    n_a, n_b, n_c = cache_a_k.shape[0], state_b_wkv.shape[0], cache_c_latent.shape[0]
    st_prompt = dict(
        a_k=jnp.zeros((n_a, Bp, 0, A_KV_HEADS, A_HEAD_DIM), F32),
        a_v=jnp.zeros((n_a, Bp, 0, A_KV_HEADS, A_HEAD_DIM), F32),
        a_idx=jnp.zeros((n_a, Bp, 0, IDX_DIM), F32),
        b_wkv=jnp.zeros((n_b, Bp, B_HEADS, B_HEAD_DIM, B_HEAD_DIM), F32),
        b_shift=jnp.zeros((n_b, Bp, D), F32),
        c_lat=jnp.zeros((n_c, Bp, 0, C_KV_RANK), F32),
        c_rope=jnp.zeros((n_c, Bp, 0, C_ROPE), F32),
        ffn=jnp.zeros((DEPTH, Bp, CONV_W - 1, 2 * D_FF), F32))
    st_sample = dict(a_k=cache_a_k, a_v=cache_a_v, a_idx=cache_a_idx, b_wkv=state_b_wkv,
                     b_shift=state_b_shift, c_lat=cache_c_latent, c_rope=cache_c_rope,
                     ffn=state_ffn_conv)
    y_prompt, sp = _trunk(x_prompt, 0, st_prompt, w)
    y_sample, ss = _trunk(x_sample, cache_a_k.shape[2], st_sample, w)
    return (y_prompt, y_sample,
            sp['a_k'], ss['a_k'], sp['a_v'], ss['a_v'], sp['a_idx'], ss['a_idx'],
            sp['b_wkv'], ss['b_wkv'], sp['b_shift'], ss['b_shift'],
            sp['c_lat'], ss['c_lat'], sp['c_rope'], ss['c_rope'],
            sp['ffn'], ss['ffn'])
```

```python
import functools

import jax
import jax.numpy as jnp
from jax import lax
from jax.experimental import pallas as pl
from jax.experimental.pallas import tpu as pltpu

F32 = jnp.float32
BF16 = jnp.bfloat16

D_MODEL = 1024
DEPTH = 4
CHUNK = 64
N_MIXERS = 3
NORM_EPS = 1e-6
ROPE_THETA = 500000.0
A_HEADS, A_HEAD_DIM, A_KV_HEADS = 16, 64, 2
A_ROT = A_HEAD_DIM // 4
IDX_HEADS, IDX_DIM = 8, 64
IDX_ROT = IDX_DIM // 4
TOPK_MAX = 256
A_O_Q = A_HEADS * A_HEAD_DIM
A_O_K = A_O_Q + A_KV_HEADS * A_HEAD_DIM
A_O_V = A_O_K + A_KV_HEADS * A_HEAD_DIM
A_O_QI = A_O_V + IDX_HEADS * IDX_DIM
A_O_KI = A_O_QI + IDX_DIM
A_IN = A_O_KI + IDX_HEADS
B_HEAD_DIM = 64
B_HEADS = D_MODEL // B_HEAD_DIM
B_GN_EPS = 64e-5
C_HEADS, C_NOPE, C_ROPE, C_V = 16, 64, 32, 64
C_Q_RANK, C_KV_RANK = 512, 256
D_FF = 2816
CONV_W = 3

LANE = 128
SUBLANE_BF16 = 16
VMEM_LIMIT = 56 * 1024 * 1024
NEG_INF = float("-inf")
LOG2E = 1.4426950408889634


def _round_up(n, m):
    return (n + m - 1) // m * m


def _row_tile(M, pref):
    t = min(pref, M)
    while M % t:
        t //= 2
    return t


def _cparams(sem):
    return pltpu.CompilerParams(dimension_semantics=sem, vmem_limit_bytes=VMEM_LIMIT)


def _mm_kernel(*refs, has_norm, has_res):
    a_ref, w_ref = refs[0], refs[1]
    i = 2
    g_ref = r_ref = None
    if has_norm:
        g_ref = refs[i]
        i += 1
    if has_res:
        r_ref = refs[i]
        i += 1
    o_ref = refs[i]
    a = a_ref[...]
    if has_norm:
        af = a.astype(F32)
        a = af * lax.rsqrt(jnp.mean(af * af, axis=-1, keepdims=True) + NORM_EPS) * g_ref[...]
    acc = jnp.dot(a.astype(BF16), w_ref[...], preferred_element_type=F32)
    if has_res:
        acc = acc + r_ref[...]
    o_ref[...] = acc.astype(o_ref.dtype)


def _mm(a, w, *, norm_g=None, residual=None, out_dtype=F32, tm=512):
    M, K = a.shape
    N = w.shape[1]
    tm = _row_tile(M, tm)
    assert M % tm == 0 and N % LANE == 0
    ins = [a, w]
    specs = [pl.BlockSpec((tm, K), lambda i: (i, 0)), pl.BlockSpec((K, N), lambda i: (0, 0))]
    if norm_g is not None:
        ins.append(norm_g.reshape(1, K).astype(F32))
        specs.append(pl.BlockSpec((1, K), lambda i: (0, 0)))
    if residual is not None:
        ins.append(residual)
        specs.append(pl.BlockSpec((tm, N), lambda i: (i, 0)))
    return pl.pallas_call(
        functools.partial(_mm_kernel, has_norm=norm_g is not None, has_res=residual is not None),
        grid=(M // tm,),
        in_specs=specs,
        out_specs=pl.BlockSpec((tm, N), lambda i: (i, 0)),
        out_shape=jax.ShapeDtypeStruct((M, N), out_dtype),
        compiler_params=_cparams(("parallel",)),
        name="mm",
    )(*ins)


def _norm_kernel(x_ref, g_ref, o_ref):
    xf = x_ref[...]
    o_ref[...] = xf * lax.rsqrt(jnp.mean(xf * xf, axis=-1, keepdims=True) + NORM_EPS) * g_ref[...]


def _norm(x, g, tm=512):
    M, K = x.shape
    tm = _row_tile(M, tm)
    return pl.pallas_call(
        _norm_kernel,
        grid=(M // tm,),
        in_specs=[pl.BlockSpec((tm, K), lambda i: (i, 0)), pl.BlockSpec((1, K), lambda i: (0, 0))],
        out_specs=pl.BlockSpec((tm, K), lambda i: (i, 0)),
        out_shape=jax.ShapeDtypeStruct((M, K), F32),
        compiler_params=_cparams(("parallel",)),
        name="rmsnorm",
    )(x, g.reshape(1, K))


def _count(cond):
    return jnp.sum(jnp.where(cond, 1.0, 0.0), axis=-1, keepdims=True)


def _topk_select(score, top, key_idx):
    L = score.shape[1]
    kf = float(top)
    c0 = _count(score >= 0.0)
    neg = c0 < kf
    y = jnp.where(neg, -score, score)
    kp = jnp.where(neg, kf, float(L) - kf + 1.0)
    zero = jnp.zeros_like(c0)
    e_cur = zero
    t_cur = zero
    for b in range(7, -1, -1):
        step = 2 ** b
        cand = jnp.where(e_cur == 0.0, 2.0 ** (step - 127), t_cur * (2.0 ** step if step < 128 else 1.0))
        ok = _count(y < cand) < kp
        e_cur = jnp.where(ok, e_cur + float(step), e_cur)
        t_cur = jnp.where(ok, cand, t_cur)
    t_pow = t_cur
    for j in range(1, 24):
        cand = t_cur + t_pow * (2.0 ** -j)
        ok = _count(y < cand) < kp
        t_cur = jnp.where(ok, cand, t_cur)
    thr = jnp.where(neg, -t_cur, t_cur)
    gt = score > thr
    eq = score == thr
    need = kf - _count(gt)
    eqf = jnp.where(eq, 1.0, 0.0)
    n_eq = jnp.sum(eqf, axis=-1, keepdims=True)

    def index_cut():
        c_cur = zero
        nbits = max(1, (L - 1).bit_length())
        for b in range(nbits - 1, -1, -1):
            cand = c_cur + float(2 ** b)
            ok = jnp.sum(jnp.where(key_idx < cand, eqf, 0.0), axis=-1, keepdims=True) < need
            c_cur = jnp.where(ok, cand, c_cur)
        return c_cur

    any_split = jnp.max(jnp.where(n_eq > need, 1.0, 0.0)) > 0.0
    c_cut = lax.cond(any_split, index_cut, lambda: jnp.full_like(zero, float(L)))
    return gt | (eq & (key_idx <= c_cut))


def _attn_kernel(*refs, n_heads, group, dq, dv, tq, pos0, top, indexer, key_counts):
    if indexer:
        q_ref, k_ref, v_ref, qi_ref, wi_ref, ki_ref, o_ref = refs
    else:
        q_ref, k_ref, v_ref, o_ref = refs
    qb = pl.program_id(1)
    first = pos0 + qb * tq

    def body(L):
        row = lax.broadcasted_iota(jnp.int32, (tq, 1), 0) + first
        limit = (row & ~(CHUNK - 1)) + CHUNK
        key_i = lax.broadcasted_iota(jnp.int32, (tq, L), 1)
        valid = key_i < limit
        if indexer:
            ki = ki_ref[0, :L, :]
            qi = qi_ref[0]
            wi = wi_ref[0]
            score = jnp.zeros((tq, L), F32)
            for h in range(IDX_HEADS):
                d = lax.dot_general(qi[:, h * IDX_DIM:(h + 1) * IDX_DIM], ki, (((1,), (1,)), ((), ())),
                                    preferred_element_type=F32)
                score = score + wi[:, h:h + 1] * jnp.maximum(d, 0.0)
            score = jnp.where(valid, score, NEG_INF)
            sel = _topk_select(score, top, key_i.astype(F32))
            valid = sel & valid
        bias = jnp.where(valid, 0.0, NEG_INF)
        q = q_ref[0]
        kk = k_ref[0, :L, :]
        vv = v_ref[0, :L, :]
        outs = []
        for h in range(n_heads):
            g = h // group
            logits = lax.dot_general(q[:, h * dq:(h + 1) * dq], kk[:, g * dq:(g + 1) * dq],
                                     (((1,), (1,)), ((), ())), preferred_element_type=F32) + bias
            m = jnp.max(logits, axis=-1, keepdims=True)
            p = jnp.exp2(logits - m)
            s = jnp.sum(p, axis=-1, keepdims=True)
            o = jnp.dot(p.astype(BF16), vv[:, g * dv:(g + 1) * dv], preferred_element_type=F32)
            outs.append(o / s)
        o_ref[0] = jnp.concatenate(outs, axis=-1).astype(o_ref.dtype)

    if len(key_counts) == 1:
        body(key_counts[0])
    else:
        last_limit = ((first + tq - 1) & ~(CHUNK - 1)) + CHUNK
        lo = 0
        for L in key_counts:
            pl.when((last_limit > lo) & (last_limit <= L))(functools.partial(body, L))
            lo = L


ATTN_KEY_STEP = 512


def _attn(q, k, v, *, n_heads, group, dq, dv, pos0, tq, idx=None, top=0):
    B, T, _ = q.shape
    L = k.shape[1]
    assert T % tq == 0 and L % LANE == 0
    need = sorted({min(L, _round_up(_round_up(pos0 + (i + 1) * tq, CHUNK), ATTN_KEY_STEP)) for i in range(T // tq)})
    assert need[-1] == L or _round_up(pos0 + T, CHUNK) <= need[-1]
    assert idx is None or top <= need[0]
    ins = [q, k, v]
    specs = [pl.BlockSpec((1, tq, q.shape[2]), lambda b, i: (b, i, 0)),
             pl.BlockSpec((1, L, k.shape[2]), lambda b, i: (b, 0, 0)),
             pl.BlockSpec((1, L, v.shape[2]), lambda b, i: (b, 0, 0))]
    if idx is not None:
        qi, wi, ki = idx
        ins += [qi, wi, ki]
        specs += [pl.BlockSpec((1, tq, qi.shape[2]), lambda b, i: (b, i, 0)),
                  pl.BlockSpec((1, tq, wi.shape[2]), lambda b, i: (b, i, 0)),
                  pl.BlockSpec((1, L, ki.shape[2]), lambda b, i: (b, 0, 0))]
    return pl.pallas_call(
        functools.partial(_attn_kernel, n_heads=n_heads, group=group, dq=dq, dv=dv, tq=tq, pos0=pos0,
                          top=top, indexer=idx is not None, key_counts=tuple(need)),
        grid=(B, T // tq),
        in_specs=specs,
        out_specs=pl.BlockSpec((1, tq, n_heads * dv), lambda b, i: (b, i, 0)),
        out_shape=jax.ShapeDtypeStruct((B, T, n_heads * dv), BF16),
        compiler_params=_cparams(("parallel", "parallel")),
        name="dsa_attn" if idx is not None else "mla_attn",
    )(*ins)


ATTN_KB = 512
ATTN_KB_SHIFT = ATTN_KB.bit_length() - 1
SELECT_ALL = 1e9
MANTISSA_BITS = 23
ATTN_ROWS_PER_ITER = 1024


def _fold(x, op=jnp.add):
    acc = x[:, :LANE]
    for j in range(1, x.shape[1] // LANE):
        acc = op(acc, x[:, j * LANE:(j + 1) * LANE])
    return acc


def _topk_bias(tab_ref, qi_ref, wi_ref, ki_ref, SC, MS, *, nk, limit, lane_i, tq, top, wi_off):
    kb = ATTN_KB
    qi = qi_ref[0]
    wi = wi_ref[0][:, wi_off:wi_off + IDX_HEADS]
    zeros_l = jnp.zeros((tq, LANE), F32)
    zero = jnp.zeros((tq, 1), F32)
    kf = float(top)

    def lane_sum(body):
        acc = lax.fori_loop(0, nk, lambda j, a: a + _fold(body(j)), zeros_l)
        return jnp.sum(acc, axis=-1, keepdims=True)

    def ones_where(c):
        return jnp.where(c, 1.0, 0.0)

    def score_block(j):
        kij = ki_ref[0, pl.ds(pl.multiple_of(j * kb, kb), kb), :]
        sc = jnp.zeros((tq, kb), F32)
        for h in range(IDX_HEADS):
            d = lax.dot_general(qi[:, h * IDX_DIM:(h + 1) * IDX_DIM], kij, (((1,), (1,)), ((), ())),
                                preferred_element_type=F32)
            sc = sc + wi[:, h:h + 1] * jnp.maximum(d, 0.0)
        sc = jnp.where(lane_i + j * kb < limit, sc, NEG_INF)
        SC[j] = sc
        return ones_where(sc >= 0.0)

    c0 = lane_sum(score_block)
    neg = c0 < kf
    sgn = jnp.where(neg, -1.0, 1.0)
    kp = jnp.where(neg, kf, (nk * kb).astype(F32) - kf + 1.0)

    def flip(j, c):
        SC[j] = SC[j] * sgn
        return c

    lax.fori_loop(0, nk, flip, 0)

    def count_lt(cand):
        return lane_sum(lambda j: ones_where(SC[j] < cand))

    def exp_step(i, carry):
        e_cur, t_cur = carry
        cand = jnp.where(e_cur == 0.0, tab_ref[0, i], t_cur * tab_ref[1, i])
        ok = count_lt(cand) < kp
        return jnp.where(ok, e_cur + tab_ref[2, i], e_cur), jnp.where(ok, cand, t_cur)

    _, t_pow = lax.fori_loop(0, 8, exp_step, (zero, zero))

    def man_step(i, carry):
        t_cur, frac = carry
        cand = t_cur + frac
        ok = count_lt(cand) < kp
        return jnp.where(ok, cand, t_cur), frac * 0.5

    t_cur, _ = lax.fori_loop(0, MANTISSA_BITS, man_step, (t_pow, t_pow * 0.5))
    thr = t_cur * sgn

    def score(j):
        return SC[j] * sgn

    def key_idx(j):
        return (lane_i + j * kb).astype(F32)

    need = kf - lane_sum(lambda j: ones_where(score(j) > thr))
    n_eq = lane_sum(lambda j: ones_where(score(j) == thr))

    def index_cut():
        nbits = (SC.shape[0] * kb - 1).bit_length()

        def bit_step(i, carry):
            c_cur, bit = carry
            cand = c_cur + bit
            ok = lane_sum(lambda j: ones_where((score(j) == thr) & (key_idx(j) < cand))) < need
            return jnp.where(ok, cand, c_cur), bit * 0.5

        c_cur, _ = lax.fori_loop(0, nbits, bit_step, (zero, jnp.full((tq, 1), 2.0 ** (nbits - 1), F32)))
        return c_cur

    any_split = jnp.max(ones_where(n_eq > need)) > 0.0
    c_cut = lax.cond(any_split, index_cut, lambda: jnp.full((tq, 1), SELECT_ALL, F32))

    def write_bias(j, c):
        s = score(j)
        sel = (s > thr) | ((s == thr) & (key_idx(j) <= c_cut))
        MS[j] = jnp.where(sel & (lane_i + j * kb < limit), 0.0, NEG_INF)
        return c

    lax.fori_loop(0, nk, write_bias, 0)


def _topk_bias_t(tab_ref, qi_ref, wi_ref, ki_ref, SC, MS, *, nk, first, tq, top, wi_off):
    kb = ATTN_KB
    qi = qi_ref[0]
    w_t = jnp.transpose(wi_ref[0])[wi_off:wi_off + IDX_HEADS, :]
    part = 64
    zeros_p = jnp.zeros((part, tq), F32)
    zero = jnp.zeros((1, tq), F32)
    kf = float(top)
    pos_q = lax.broadcasted_iota(jnp.int32, (1, tq), 1) + first
    limit = (pos_q & ~(CHUNK - 1)) + CHUNK
    key_i = lax.broadcasted_iota(jnp.int32, (kb, tq), 0)

    def key_sum(body):
        def step(j, a):
            return a + jnp.sum(body(j).reshape(kb // part, part, tq), axis=0)

        return jnp.sum(lax.fori_loop(0, nk, step, zeros_p), axis=0, keepdims=True)

    def ones_where(c):
        return jnp.where(c, 1.0, 0.0)

    def score_block(j):
        kij = ki_ref[0, pl.ds(pl.multiple_of(j * kb, kb), kb), :]
        sc = jnp.zeros((kb, tq), F32)
        for h in range(IDX_HEADS):
            d = lax.dot_general(kij, qi[:, h * IDX_DIM:(h + 1) * IDX_DIM], (((1,), (1,)), ((), ())),
                                preferred_element_type=F32)
            sc = sc + w_t[h:h + 1, :] * jnp.maximum(d, 0.0)
        sc = jnp.where(key_i + j * kb < limit, sc, NEG_INF)
        SC[j] = sc
        return ones_where(sc >= 0.0)

    c0 = key_sum(score_block)
    neg = c0 < kf
    sgn = jnp.where(neg, -1.0, 1.0)
    kp = jnp.where(neg, kf, (nk * kb).astype(F32) - kf + 1.0)

    def flip(j, c):
        SC[j] = SC[j] * sgn
        return c

    lax.fori_loop(0, nk, flip, 0)

    def count_lt(cand):
        return key_sum(lambda j: ones_where(SC[j] < cand))

    def exp_step(i, carry):
        e_cur, t_cur = carry
        cand = jnp.where(e_cur == 0.0, tab_ref[0, i], t_cur * tab_ref[1, i])
        ok = count_lt(cand) < kp
        return jnp.where(ok, e_cur + tab_ref[2, i], e_cur), jnp.where(ok, cand, t_cur)

    _, t_pow = lax.fori_loop(0, 8, exp_step, (zero, zero))

    def man_step(i, carry):
        t_cur, frac = carry
        cand = t_cur + frac
        ok = count_lt(cand) < kp
        return jnp.where(ok, cand, t_cur), frac * 0.5

    t_cur, _ = lax.fori_loop(0, MANTISSA_BITS, man_step, (t_pow, t_pow * 0.5))
    thr = t_cur * sgn

    def score(j):
        return SC[j] * sgn

    def key_idx(j):
        return (key_i + j * kb).astype(F32)

    need = kf - key_sum(lambda j: ones_where(score(j) > thr))
    n_eq = key_sum(lambda j: ones_where(score(j) == thr))

    def index_cut():
        nbits = (SC.shape[0] * kb - 1).bit_length()

        def bit_step(i, carry):
            c_cur, bit = carry
            cand = c_cur + bit
            ok = key_sum(lambda j: ones_where((score(j) == thr) & (key_idx(j) < cand))) < need
            return jnp.where(ok, cand, c_cur), bit * 0.5

        c_cur, _ = lax.fori_loop(0, nbits, bit_step, (zero, jnp.full((1, tq), 2.0 ** (nbits - 1), F32)))
        return c_cur

    any_split = jnp.max(ones_where(n_eq > need)) > 0.0
    c_cut = lax.cond(any_split, index_cut, lambda: jnp.full((1, tq), SELECT_ALL, F32))

    def write_bias(j, c):
        s = score(j)
        sel = (s > thr) | ((s == thr) & (key_idx(j) <= c_cut))
        MS[j] = jnp.transpose(jnp.where(sel & (key_i + j * kb < limit), 0.0, NEG_INF))
        return c

    lax.fori_loop(0, nk, write_bias, 0)


def _attn2_kernel(*refs, n_kv, group, tq, pos0, top, indexer, wi_off):
    if indexer:
        (tab_ref, q_ref, k_ref, v_ref, x_ref, wo_ref, qi_ref, wi_ref, ki_ref, xo_ref,
         MS, LG, MACC, LACC, OACC, OH, OALL, SC) = refs
    else:
        q_ref, k_ref, v_ref, x_ref, wo_ref, xo_ref, MS, LG, MACC, LACC, OACC, OH, OALL = refs
    kb = ATTN_KB
    first = pos0 + pl.program_id(1) * tq
    last_limit = ((first + tq - 1) & ~(CHUNK - 1)) + CHUNK
    nk = (last_limit + (kb - 1)) >> ATTN_KB_SHIFT
    row = lax.broadcasted_iota(jnp.int32, (tq, 1), 0) + first
    limit = (row & ~(CHUNK - 1)) + CHUNK
    lane_i = lax.broadcasted_iota(jnp.int32, (tq, kb), 1)

    if indexer:
        _topk_bias_t(tab_ref, qi_ref, wi_ref, ki_ref, SC, MS, nk=nk, first=first, tq=tq, top=top, wi_off=wi_off)
    else:
        def causal_bias(j, c):
            MS[j] = jnp.where(lane_i + j * kb < limit, 0.0, NEG_INF)
            return c

        lax.fori_loop(0, nk, causal_bias, 0)

    hu = LG.shape[0]

    def per_kv_heads(gi, c):
        heads = [gi * hu + u for u in range(hu)]
        qs = [q_ref[0, g, 0] for g in heads]
        MACC[...] = jnp.full(MACC.shape, NEG_INF, F32)
        LACC[...] = jnp.zeros(LACC.shape, F32)
        OACC[...] = jnp.zeros(OACC.shape, F32)

        def logits_block(j, c_):
            keys = pl.ds(pl.multiple_of(j * kb, kb), kb)
            bias = MS[j][None]
            for u, g in enumerate(heads):
                lg = lax.dot_general(qs[u], k_ref[0, g, keys, :], (((1,), (1,)), ((), ())),
                                     preferred_element_type=F32)
                lg = (lg.reshape(group, tq, kb) + bias).reshape(group * tq, kb)
                LG[u, j] = lg
                MACC[u] = jnp.maximum(MACC[u], _fold(lg, jnp.maximum))
            return c_

        lax.fori_loop(0, nk, logits_block, 0)
        ms = [jnp.max(MACC[u], axis=-1, keepdims=True) for u in range(hu)]

        def value_block(j, c_):
            keys = pl.ds(pl.multiple_of(j * kb, kb), kb)
            for u, g in enumerate(heads):
                p = jnp.exp2(LG[u, j] - ms[u])
                LACC[u] += _fold(p)
                OACC[u] += jnp.dot(p.astype(BF16), v_ref[0, g, keys, :], preferred_element_type=F32)
            return c_

        lax.fori_loop(0, nk, value_block, 0)
        for u, g in enumerate(heads):
            OH[g] = (OACC[u] / jnp.sum(LACC[u], axis=-1, keepdims=True)).astype(OH.dtype)
        return c

    lax.fori_loop(0, n_kv // hu, per_kv_heads, 0)
    dv = OH.shape[2]
    for h in range(n_kv * group):
        OALL[:, h * dv:(h + 1) * dv] = OH[h // group, (h % group) * tq:(h % group + 1) * tq, :]
    xo_ref[0] = x_ref[0] + jnp.dot(OALL[...], wo_ref[...], preferred_element_type=F32)


def _head_major(a, tq, group):
    B, T, H, d = a.shape
    n_kv = H // group
    return jnp.transpose(a.reshape(B, T // tq, tq, n_kv, group, d), (0, 3, 1, 4, 2, 5)).reshape(
        B, n_kv, T // tq, group * tq, d)


def _attn2(qg, kt, vt, x, w_out, *, pos0, tq, group, idx=None, top=0, wi_off=0):
    B, n_kv, nq, rows, dq = qg.shape
    L, dv = kt.shape[2], vt.shape[3]
    T, D = x.shape[1], x.shape[2]
    kb = ATTN_KB
    assert nq * tq == T and rows == group * tq and L % kb == 0 and (idx is None or top <= kb)
    ins = [qg, kt, vt, x, w_out]
    specs = [pl.BlockSpec((1, n_kv, 1, rows, dq), lambda b, i: (b, 0, i, 0, 0)),
             pl.BlockSpec((1, n_kv, L, dq), lambda b, i: (b, 0, 0, 0)),
             pl.BlockSpec((1, n_kv, L, dv), lambda b, i: (b, 0, 0, 0)),
             pl.BlockSpec((1, tq, D), lambda b, i: (b, i, 0)),
             pl.BlockSpec(w_out.shape, lambda b, i: (0, 0))]
    hu = max(1, min(n_kv, ATTN_ROWS_PER_ITER // rows))
    assert n_kv % hu == 0
    scratch = [pltpu.VMEM((L // kb, tq, kb), F32), pltpu.VMEM((hu, L // kb, rows, kb), F32),
               pltpu.VMEM((hu, rows, LANE), F32), pltpu.VMEM((hu, rows, LANE), F32),
               pltpu.VMEM((hu, rows, dv), F32), pltpu.VMEM((n_kv, rows, dv), BF16),
               pltpu.VMEM((tq, n_kv * group * dv), BF16)]
    if idx is not None:
        qi, wi, ki = idx
        steps = [2 ** b for b in range(7, -1, -1)]
        tab = jnp.array([[2.0 ** (s - 127) for s in steps], [2.0 ** s if s < 128 else 1.0 for s in steps],
                         [float(s) for s in steps]], F32)
        ins = [tab] + ins + [qi, wi, ki]
        specs = ([pl.BlockSpec(memory_space=pltpu.SMEM)] + specs
                 + [pl.BlockSpec((1, tq, qi.shape[2]), lambda b, i: (b, i, 0)),
                    pl.BlockSpec((1, tq, wi.shape[2]), lambda b, i: (b, i, 0)),
                    pl.BlockSpec((1, L, ki.shape[2]), lambda b, i: (b, 0, 0))])
        scratch.append(pltpu.VMEM((L // kb, kb, tq), F32))
    return pl.pallas_call(
        functools.partial(_attn2_kernel, n_kv=n_kv, group=group, tq=tq, pos0=pos0, top=top,
                          indexer=idx is not None, wi_off=wi_off),
        grid=(B, nq),
        in_specs=specs,
        out_specs=pl.BlockSpec((1, tq, D), lambda b, i: (b, i, 0)),
        out_shape=jax.ShapeDtypeStruct((B, T, D), F32),
        scratch_shapes=scratch,
        compiler_params=_cparams(("parallel", "parallel")),
        name="dsa_attn" if idx is not None else "mla_attn",
    )(*ins)


def _rope_tables(pos, dh, rot, offset=0):
    half = rot // 2
    inv = ROPE_THETA ** (-jnp.arange(half, dtype=F32) / half)
    ang = pos.astype(F32)[:, None] * inv[None, :]
    cos, sin = jnp.cos(ang), jnp.sin(ang)
    T = pos.shape[0]
    pad = lambda n, v: jnp.full((T, n), v, F32)
    zh = pad(half, 0.0)
    lo, hi = offset, dh - offset - rot
    c = jnp.concatenate([pad(lo, 1.0), cos, cos, pad(hi, 1.0)], axis=1)
    s1 = jnp.concatenate([pad(lo, 0.0), -sin, zh, pad(hi, 0.0)], axis=1)
    s2 = jnp.concatenate([pad(lo, 0.0), zh, sin, pad(hi, 0.0)], axis=1)
    return tuple(jnp.tile(t, (1, LANE // dh)) for t in (c, s1, s2))


def _rope_lanes(x, c, s1, s2, half):
    return x * c + pltpu.roll(x, LANE - half, 1) * s1 + pltpu.roll(x, half, 1) * s2


def _dsa_proj_kernel(x_ref, g_ref, w_ref, c_ref, s1_ref, s2_ref, ts_ref, qh_ref, kf_ref, kt_ref, vf_ref, vt_ref,
                     qi_ref, tail_ref, kib_ref, *, tm):
    xf = x_ref[0]
    h = xf * lax.rsqrt(jnp.mean(xf * xf, axis=-1, keepdims=True) + NORM_EPS) * g_ref[...]
    acc = jnp.dot(h.astype(BF16), w_ref[...], preferred_element_type=F32)
    c, s1, s2 = c_ref[...], s1_ref[...], s2_ref[...]
    half = A_ROT // 2
    hd = A_HEAD_DIM
    group = A_HEADS // A_KV_HEADS
    rope = lambda xs: _rope_lanes(xs, c, s1, s2, half)
    slab = lambda off: acc[:, off:off + LANE]
    for s in range(A_O_Q // LANE):
        qs = (rope(slab(s * LANE)) * (hd ** -0.5 * LOG2E)).astype(BF16)
        for e in range(LANE // hd):
            head = s * (LANE // hd) + e
            u = head % group
            qh_ref[0, head // group, 0, u * tm:(u + 1) * tm, :] = qs[:, e * hd:(e + 1) * hd]
    ks = rope(slab(A_O_Q))
    vs = slab(A_O_K)
    kf_ref[0] = ks
    vf_ref[0] = vs
    for e in range(A_KV_HEADS):
        kt_ref[0, e] = ks[:, e * hd:(e + 1) * hd].astype(BF16)
        vt_ref[0, e] = vs[:, e * hd:(e + 1) * hd].astype(BF16)
    for s in range(IDX_HEADS * IDX_DIM // LANE):
        qi_ref[0, :, s * LANE:(s + 1) * LANE] = rope(slab(A_O_V + s * LANE)).astype(BF16)
    is_key = lax.broadcasted_iota(jnp.int32, (tm, LANE), 1) < IDX_DIM
    tl = _rope_lanes(slab(A_O_QI), jnp.where(is_key, c, 1.0), jnp.where(is_key, s1, 0.0),
                     jnp.where(is_key, s2, 0.0), half) * ts_ref[...]
    tail_ref[0] = tl
    kib_ref[0] = tl[:, :IDX_DIM].astype(BF16)


def _dsa_proj(x, g, w_in, tabs, tm):
    B, T, D = x.shape
    assert (A_HEAD_DIM, A_ROT) == (IDX_DIM, IDX_ROT) and A_KV_HEADS * A_HEAD_DIM == LANE
    assert A_O_QI % LANE == 0 and IDX_DIM + IDX_HEADS <= LANE and T % tm == 0
    n_in = _round_up(A_IN, LANE)
    group = A_HEADS // A_KV_HEADS
    hd = A_HEAD_DIM
    nq = T // tm
    lanes = jnp.arange(LANE)
    tail_scale = jnp.where(lanes < IDX_DIM, 1.0, jnp.where(lanes < IDX_DIM + IDX_HEADS,
                                                            (IDX_HEADS * IDX_DIM) ** -0.5, 0.0)).astype(F32)
    row = lambda n: pl.BlockSpec((1, tm, n), lambda b, i: (b, i, 0))
    tab = pl.BlockSpec((tm, LANE), lambda b, i: (i, 0))
    const = lambda a: pl.BlockSpec(a.shape, lambda b, i: (0,) * a.ndim)
    kvh = pl.BlockSpec((1, A_KV_HEADS, tm, hd), lambda b, i: (b, 0, i, 0))
    w = _pad_cols(w_in, n_in).astype(BF16)
    g2 = g.reshape(1, D)
    ts = tail_scale.reshape(1, LANE)
    return pl.pallas_call(
        functools.partial(_dsa_proj_kernel, tm=tm),
        grid=(B, nq),
        in_specs=[row(D), const(g2), const(w), tab, tab, tab, const(ts)],
        out_specs=[pl.BlockSpec((1, A_KV_HEADS, 1, group * tm, hd), lambda b, i: (b, 0, i, 0, 0)),
                   row(LANE), kvh, row(LANE), kvh, row(IDX_HEADS * IDX_DIM), row(LANE), row(IDX_DIM)],
        out_shape=[jax.ShapeDtypeStruct((B, A_KV_HEADS, nq, group * tm, hd), BF16),
                   jax.ShapeDtypeStruct((B, T, LANE), F32), jax.ShapeDtypeStruct((B, A_KV_HEADS, T, hd), BF16),
                   jax.ShapeDtypeStruct((B, T, LANE), F32), jax.ShapeDtypeStruct((B, A_KV_HEADS, T, hd), BF16),
                   jax.ShapeDtypeStruct((B, T, IDX_HEADS * IDX_DIM), BF16),
                   jax.ShapeDtypeStruct((B, T, LANE), F32), jax.ShapeDtypeStruct((B, T, IDX_DIM), BF16)],
        compiler_params=_cparams(("parallel", "parallel")),
        name="dsa_proj",
    )(x, g2, w, *tabs, ts)


def _rms(xf, gain):
    return xf * lax.rsqrt(jnp.mean(xf * xf, axis=-1, keepdims=True) + NORM_EPS) * gain


def _mla_write_kv(lat, kpe_slab, wuk_ref, wuv_ref, kt_ref, vt_ref):
    lb = lat.astype(BF16)
    kn = jnp.dot(lb, wuk_ref[...], preferred_element_type=F32)
    vv = jnp.dot(lb, wuv_ref[...], preferred_element_type=F32)
    for h in range(C_HEADS):
        kt_ref[0, h] = (kn[:, h * LANE:(h + 1) * LANE] + kpe_slab).astype(BF16)
        vt_ref[0, h] = vv[:, h * C_V:(h + 1) * C_V].astype(BF16)


def _mla_proj_kernel(x_ref, g_ref, win_ref, gq_ref, gkv_ref, wuq_ref, wuk_ref, wuv_ref, c_ref, s1_ref, s2_ref,
                     qh_ref, kt_ref, vt_ref, lat_ref, kpe_ref):
    h = _rms(x_ref[0], g_ref[...]).astype(BF16)
    proj = jnp.dot(h, win_ref[...], preferred_element_type=F32)
    c, s1, s2 = c_ref[...], s1_ref[...], s2_ref[...]
    half = C_ROPE // 2
    q = jnp.dot(_rms(proj[:, :C_Q_RANK], gq_ref[...]).astype(BF16), wuq_ref[...], preferred_element_type=F32)
    scale = (C_NOPE + C_ROPE) ** -0.5 * LOG2E
    for hd in range(C_HEADS):
        qh_ref[0, hd, 0] = (_rope_lanes(q[:, hd * LANE:(hd + 1) * LANE], c, s1, s2, half) * scale).astype(BF16)
    lat = _rms(proj[:, C_Q_RANK:C_Q_RANK + C_KV_RANK], gkv_ref[...])
    lat_ref[0] = lat
    kpe_slab = _rope_lanes(proj[:, C_Q_RANK + C_KV_RANK:], c, s1, s2, half)
    kpe_ref[0] = kpe_slab[:, C_NOPE:C_NOPE + C_ROPE]
    _mla_write_kv(lat, kpe_slab, wuk_ref, wuv_ref, kt_ref, vt_ref)


def _mla_kv_kernel(lat_ref, kpe_ref, wuk_ref, wuv_ref, kt_ref, vt_ref):
    _mla_write_kv(lat_ref[0], kpe_ref[0], wuk_ref, wuv_ref, kt_ref, vt_ref)


def _mla_weights(w_in, w_uq, w_ukv):
    D = w_in.shape[0]
    zc = lambda rows, n: jnp.zeros((rows, n), w_in.dtype)
    w_in2 = jnp.concatenate([w_in[:, :C_Q_RANK + C_KV_RANK], zc(D, C_NOPE), w_in[:, C_Q_RANK + C_KV_RANK:],
                             zc(D, LANE - C_NOPE - C_ROPE)], axis=1)
    pad_heads = lambda w, d: jnp.pad(w.reshape(w.shape[0], C_HEADS, d), ((0, 0), (0, 0), (0, LANE - d))).reshape(
        w.shape[0], C_HEADS * LANE)
    w_uq2 = pad_heads(w_uq, C_NOPE + C_ROPE)
    ukv = w_ukv.reshape(C_KV_RANK, C_HEADS, C_NOPE + C_V)
    w_uk2 = pad_heads(ukv[..., :C_NOPE].reshape(C_KV_RANK, C_HEADS * C_NOPE), C_NOPE)
    w_uv2 = ukv[..., C_NOPE:].reshape(C_KV_RANK, C_HEADS * C_V)
    return tuple(a.astype(BF16) for a in (w_in2, w_uq2, w_uk2, w_uv2))


def _mla_proj(x, g, g_q, g_kv, weights, tabs, tm):
    B, T, D = x.shape
    w_in2, w_uq2, w_uk2, w_uv2 = weights
    nq = T // tm
    row = lambda n: pl.BlockSpec((1, tm, n), lambda b, i: (b, i, 0))
    tab = pl.BlockSpec((tm, LANE), lambda b, i: (i, 0))
    const = lambda a: pl.BlockSpec(a.shape, lambda b, i: (0,) * a.ndim)
    heads = lambda d: pl.BlockSpec((1, C_HEADS, tm, d), lambda b, i: (b, 0, i, 0))
    vec = lambda a: a.reshape(1, a.shape[0])
    return pl.pallas_call(
        _mla_proj_kernel,
        grid=(B, nq),
        in_specs=[row(D), const(vec(g)), const(w_in2), const(vec(g_q)), const(vec(g_kv)), const(w_uq2),
                  const(w_uk2), const(w_uv2), tab, tab, tab],
        out_specs=[pl.BlockSpec((1, C_HEADS, 1, tm, LANE), lambda b, i: (b, 0, i, 0, 0)), heads(LANE), heads(C_V),
                   row(C_KV_RANK), row(C_ROPE)],
        out_shape=[jax.ShapeDtypeStruct((B, C_HEADS, nq, tm, LANE), BF16),
                   jax.ShapeDtypeStruct((B, C_HEADS, T, LANE), BF16), jax.ShapeDtypeStruct((B, C_HEADS, T, C_V), BF16),
                   jax.ShapeDtypeStruct((B, T, C_KV_RANK), F32), jax.ShapeDtypeStruct((B, T, C_ROPE), F32)],
        compiler_params=_cparams(("parallel", "parallel")),
        name="mla_proj",
    )(x, vec(g), w_in2, vec(g_q), vec(g_kv), w_uq2, w_uk2, w_uv2, *tabs)


def _mla_kv(lat, kpe_slab, w_uk2, w_uv2, tm):
    B, P, _ = lat.shape
    row = lambda n: pl.BlockSpec((1, tm, n), lambda b, i: (b, i, 0))
    const = lambda a: pl.BlockSpec(a.shape, lambda b, i: (0,) * a.ndim)
    heads = lambda d: pl.BlockSpec((1, C_HEADS, tm, d), lambda b, i: (b, 0, i, 0))
    return pl.pallas_call(
        _mla_kv_kernel,
        grid=(B, P // tm),
        in_specs=[row(C_KV_RANK), row(LANE), const(w_uk2), const(w_uv2)],
        out_specs=[heads(LANE), heads(C_V)],
        out_shape=[jax.ShapeDtypeStruct((B, C_HEADS, P, LANE), BF16), jax.ShapeDtypeStruct((B, C_HEADS, P, C_V), BF16)],
        compiler_params=_cparams(("parallel", "parallel")),
        name="mla_kv",
    )(lat, kpe_slab, w_uk2, w_uv2)


def _head_sum(x, bo_ref):
    bw = bo_ref.shape[0]
    hi = x.astype(BF16)
    lo = (x - hi.astype(F32)).astype(BF16)
    bo = bo_ref[...]
    return jnp.concatenate(
        [jnp.dot(hi[:, j * bw:(j + 1) * bw], bo, preferred_element_type=F32)
         + jnp.dot(lo[:, j * bw:(j + 1) * bw], bo, preferred_element_type=F32) for j in range(x.shape[1] // bw)],
        axis=-1)


RWKV_HALO = 8


def _rwkv_pre_kernel(x_ref, xh_ref, sh_ref, g_ref, mu_ref, vec_ref, wr_ref, wk_ref, wv_ref, w1_ref, w2_ref,
                     a1_ref, a2_ref, g1_ref, g2_ref, bo_ref,
                     rp_ref, w_ref, k_ref, v_ref, a_ref, b_ref, yc_ref, bonus_ref, gate_ref, *, tm):
    i = pl.program_id(1)
    gain = g_ref[...]

    def norm(xf):
        return xf * lax.rsqrt(jnp.mean(xf * xf, axis=-1, keepdims=True) + NORM_EPS) * gain

    h = norm(x_ref[0])
    before = jnp.where(i > 0, norm(xh_ref[0])[RWKV_HALO - 1:RWKV_HALO, :], sh_ref[0])
    first = lax.broadcasted_iota(jnp.int32, (tm, 1), 0) == 0
    xx = jnp.where(first, before, pltpu.roll(h, 1, 0)) - h
    mu = mu_ref[...]
    vec = vec_ref[...]
    w0, a0, k_k, k_a, r_k = (vec[j:j + 1, :] for j in range(5))

    def mix(j):
        return (h + xx * mu[j:j + 1, :]).astype(BF16)

    dot = lambda a_, w_: jnp.dot(a_, w_[...], preferred_element_type=F32)
    r = dot(mix(0), wr_ref)
    wl = dot(jnp.tanh(dot(mix(1), w1_ref)).astype(BF16), w2_ref)
    k = dot(mix(2), wk_ref)
    v = dot(mix(3), wv_ref)
    al = dot(dot(mix(4), a1_ref).astype(BF16), a2_ref)
    gate_ref[0] = dot(jax.nn.sigmoid(dot(mix(5), g1_ref)).astype(BF16), g2_ref)
    z = -(w0 + wl)
    softplus = jnp.maximum(z, 0.0) + jnp.log(1.0 + jnp.exp(-jnp.abs(z)))
    decay = jnp.exp(-jnp.exp(-softplus - 0.5))
    a = jax.nn.sigmoid(a0 + al)
    kk = k * k_k
    kk = kk / jnp.maximum(jnp.sqrt(_head_sum(kk * kk, bo_ref)), 1e-12)
    k = k * (1.0 + (a - 1.0) * k_a)
    b = kk * a
    rp_ref[0] = decay * r - kk * _head_sum(b * r, bo_ref)
    w_ref[0] = decay
    k_ref[0] = k
    v_ref[0] = v
    a_ref[0] = -kk
    b_ref[0] = b
    yc_ref[0] = v * _head_sum(k * r, bo_ref)
    bonus_ref[0] = _head_sum(r * k * r_k, bo_ref) * v


def _rwkv_post_kernel(y_ref, yc_ref, bonus_ref, gate_ref, x_ref, ln_ref, wo_ref, bo_ref, o_ref):
    n = float(B_HEAD_DIM)
    y = y_ref[0] + yc_ref[0]
    d = y - _head_sum(y, bo_ref) / n
    var = _head_sum(d * d, bo_ref) / n
    ln = ln_ref[...]
    yn = d * lax.rsqrt(var + B_GN_EPS) * ln[0:1, :] + ln[1:2, :] + bonus_ref[0]
    o_ref[0] = x_ref[0] + jnp.dot((yn * gate_ref[0]).astype(BF16), wo_ref[...], preferred_element_type=F32)


def _block_ones():
    blk = jnp.arange(WKV_BW) // B_HEAD_DIM
    return (blk[:, None] == blk[None, :]).astype(BF16)


def _rwkv_pre(x, shift_prev, g, mu, vecs, ws, tm):
    B, T, D = x.shape
    H = RWKV_HALO
    assert T % tm == 0 and tm % H == 0
    nh = tm // H
    bo = _block_ones()
    row = pl.BlockSpec((1, tm, D), lambda b, i: (b, i, 0))
    const = lambda a: pl.BlockSpec(a.shape, lambda b, i: (0,) * a.ndim)
    pad8 = lambda a: jnp.pad(a, ((0, 8 - a.shape[0]), (0, 0)))
    mu8, vec8, g2 = pad8(mu), pad8(vecs), g.reshape(1, D)
    return pl.pallas_call(
        functools.partial(_rwkv_pre_kernel, tm=tm),
        grid=(B, T // tm),
        in_specs=[row, pl.BlockSpec((1, H, D), lambda b, i: (b, jnp.maximum(i * nh - 1, 0), 0)),
                  pl.BlockSpec((1, 1, D), lambda b, i: (b, 0, 0)), const(g2), const(mu8), const(vec8)]
                 + [const(a) for a in ws] + [const(bo)],
        out_specs=[row] * 9,
        out_shape=[jax.ShapeDtypeStruct((B, T, D), F32)] * 9,
        compiler_params=_cparams(("parallel", "parallel")),
        name="rwkv_pre",
    )(x, x, shift_prev.reshape(B, 1, D), g2, mu8, vec8, *ws, bo)


def _rwkv_post(y, yc, bonus, gate, x, ln, w_out, tm):
    B, T, D = x.shape
    bo = _block_ones()
    row = pl.BlockSpec((1, tm, D), lambda b, i: (b, i, 0))
    const = lambda a: pl.BlockSpec(a.shape, lambda b, i: (0,) * a.ndim)
    return pl.pallas_call(
        _rwkv_post_kernel,
        grid=(B, T // tm),
        in_specs=[row] * 5 + [const(ln), const(w_out), const(bo)],
        out_specs=row,
        out_shape=jax.ShapeDtypeStruct((B, T, D), F32),
        compiler_params=_cparams(("parallel", "parallel")),
        name="rwkv_post",
    )(y, yc, bonus, gate, x, ln, w_out, bo)


WKV_SUB = 8
WKV_NB = 8
WKV_TC = 64
WKV_BW = 2 * LANE


def _wkv_kernel(rp_ref, w_ref, k_ref, vh_ref, a_ref, b_ref, s0_ref, gs_ref, hs_ref, y_ref, sT_ref, S, *, tc, nb):
    c = pl.program_id(1)
    H = B_HEADS

    @pl.when(c == 0)
    def _():
        S[...] = s0_ref[...]

    def block(sc, carry):
        base = pl.multiple_of(sc * WKV_SUB, WKV_SUB)
        rows = pl.ds(base, WKV_SUB)
        gs = gs_ref[...]
        hsel = hs_ref[...]
        ins = [tuple(x[bi, rows, :] for x in (rp_ref, w_ref, k_ref, a_ref, b_ref)) for bi in range(nb)]
        for u in range(WKV_SUB):
            sas = []
            for bi in range(nb):
                r8, w8, k8, a8, b8 = ins[bi]
                s = S[bi]
                pa = (s * a8[u:u + 1, :]).astype(BF16)
                sas.append(jnp.dot(pa, gs, preferred_element_type=F32))
                pr = (s * r8[u:u + 1, :]).astype(BF16)
                y_ref[bi, base + u] = lax.dot_general(hsel, pr, (((1,), (1,)), ((), ())),
                                                      preferred_element_type=F32)
            deltas = []
            for bi in range(nb):
                r8, w8, k8, a8, b8 = ins[bi]
                v_t = jnp.transpose(vh_ref[bi, base + u])
                lhs = jnp.concatenate([sas[bi][:, :H], v_t], axis=1).astype(BF16)
                rhs = jnp.concatenate([hsel * b8[u:u + 1, :].astype(BF16), hsel * k8[u:u + 1, :].astype(BF16)],
                                      axis=0)
                deltas.append(jnp.dot(lhs, rhs, preferred_element_type=F32))
            for bi in range(nb):
                r8, w8, k8, a8, b8 = ins[bi]
                S[bi] = S[bi] * w8[u:u + 1, :] + deltas[bi]
        return carry

    lax.fori_loop(0, tc // WKV_SUB, block, 0)

    @pl.when(c == pl.num_programs(1) - 1)
    def _():
        sT_ref[...] = S[...]


def _wkv(rp, w, k, v, a, b, s0):
    B, T, D = rp.shape
    n = B_HEAD_DIM
    H = B_HEADS
    nb = _row_tile(B, WKV_NB)
    tc = _row_tile(T, WKV_TC)
    assert tc % WKV_SUB == 0 and H <= LANE
    s0t = jnp.transpose(s0, (0, 2, 1, 3)).reshape(B, n, D)
    hsel = (jnp.arange(H)[:, None] == (jnp.arange(D) // n)[None, :]).astype(BF16)
    gs = (jnp.arange(D)[:, None] // n == jnp.arange(LANE)[None, :]).astype(BF16)
    seq = pl.BlockSpec((nb, tc, D), lambda bi, c: (bi, c, 0))
    seqh = pl.BlockSpec((nb, tc, H, n), lambda bi, c: (bi, c, 0, 0))
    st = pl.BlockSpec((nb, n, D), lambda bi, c: (bi, 0, 0))
    const = lambda a_: pl.BlockSpec(a_.shape, lambda bi, c: (0, 0))
    y, sT = pl.pallas_call(
        functools.partial(_wkv_kernel, tc=tc, nb=nb),
        grid=(B // nb, T // tc),
        in_specs=[seq, seq, seq, seqh, seq, seq, st, const(gs), const(hsel)],
        out_specs=[seqh, st],
        out_shape=[jax.ShapeDtypeStruct((B, T, H, n), F32), jax.ShapeDtypeStruct((B, n, D), F32)],
        scratch_shapes=[pltpu.VMEM((nb, n, D), F32)],
        compiler_params=_cparams(("parallel", "arbitrary")),
        name="wkv",
    )(rp, w, k, v.reshape(B, T, H, n), a, b, s0t, gs, hsel)
    return y.reshape(B, T, D), jnp.transpose(sT.reshape(B, n, H, n), (0, 2, 1, 3))


FFN_HALO = SUBLANE_BF16


FFN_CW = 2 * LANE


def _ffn_body(x_ref, xh_ref, g_ref, wug_ref, wuv_ref, cg_ref, cv_ref, pg_ref, pv_ref, wd_ref, gf_ref,
              o_ref, st_ref, y_ref, hn, ug, uv, act, acc, *, tm, nb):
    i = pl.program_id(1)
    H = FFN_HALO
    cw_ = FFN_CW
    seg = tm + H

    def norm(xf, gain):
        return xf * lax.rsqrt(jnp.mean(xf * xf, axis=-1, keepdims=True) + NORM_EPS) * gain

    for s in range(nb):
        hn[s * seg:s * seg + H, :] = jnp.where(i > 0, norm(xh_ref[s], g_ref[...]), 0.0).astype(BF16)
        hn[s * seg + H:(s + 1) * seg, :] = norm(x_ref[s], g_ref[...]).astype(BF16)
    h = hn[...]
    first = jnp.where(i == 0, 1.0, 0.0)

    def conv(u, r0, taps):
        return (taps[3:4, :] + u[r0 - 2:r0 - 2 + tm, :] * taps[0:1, :] + u[r0 - 1:r0 - 1 + tm, :] * taps[1:2, :]
                + u[r0:r0 + tm, :] * taps[2:3, :])

    for c in range(D_FF // cw_):
        cols = slice(c * cw_, (c + 1) * cw_)
        for half, (u_scr, w_ref, p_ref) in enumerate(((ug, wug_ref, pg_ref), (uv, wuv_ref, pv_ref))):
            u = jnp.dot(h, w_ref[:, cols], preferred_element_type=F32)
            u_scr[c % 2] = u
            for s in range(nb):
                u_scr[c % 2, s * seg:s * seg + H, :] = u[s * seg:s * seg + H, :] + p_ref[s, :, cols] * first
                st_ref[s, 0, :, c * cw_ + half * D_FF:(c + 1) * cw_ + half * D_FF] = u[(s + 1) * seg - 8:(s + 1) * seg, :]
        for s in range(nb):
            gate = conv(ug.at[c % 2], s * seg + H, cg_ref[:, cols])
            val = conv(uv.at[c % 2], s * seg + H, cv_ref[:, cols])
            act[s * tm:(s + 1) * tm, :] = (gate * jax.nn.sigmoid(gate) * val).astype(BF16)
        d = jnp.dot(act[...], wd_ref[cols, :], preferred_element_type=F32)
        if c == 0:
            acc[...] = d
        else:
            acc[...] += d
    for s in range(nb):
        out = x_ref[s] + acc[s * tm:(s + 1) * tm, :]
        o_ref[s] = out
        if gf_ref is not None:
            y_ref[s] = norm(out, gf_ref[...])


def _ffn_kernel(*refs, tm, nb, final):
    if final:
        (x_ref, xh_ref, g_ref, wug_ref, wuv_ref, cg_ref, cv_ref, pg_ref, pv_ref, wd_ref, gf_ref,
         o_ref, st_ref, y_ref, hn, ug, uv, act, acc) = refs
    else:
        (x_ref, xh_ref, g_ref, wug_ref, wuv_ref, cg_ref, cv_ref, pg_ref, pv_ref, wd_ref,
         o_ref, st_ref, hn, ug, uv, act, acc) = refs
        gf_ref = y_ref = None
    _ffn_body(x_ref, xh_ref, g_ref, wug_ref, wuv_ref, cg_ref, cv_ref, pg_ref, pv_ref, wd_ref, gf_ref,
              o_ref, st_ref, y_ref, hn, ug, uv, act, acc, tm=tm, nb=nb)


FFN_ROWS = 512


def _ffn(x, g, wug, wuv, cg, cv, prev, wd, final_g=None):
    B, T, D = x.shape
    F = D_FF
    H = FFN_HALO
    tm = min(FFN_ROWS, T)
    nb = _row_tile(B, max(1, FFN_ROWS // tm))
    assert T % tm == 0 and tm % H == 0 and F % FFN_CW == 0 and tm >= 8
    nh = tm // H
    nt = T // tm
    prev_h = jnp.pad(prev, ((0, 0), (H - prev.shape[1], 0), (0, 0)))
    pg, pv = prev_h[:, :, :F], prev_h[:, :, F:]
    const = lambda a: pl.BlockSpec(a.shape, lambda b, i: (0,) * a.ndim, pipeline_mode=pl.Buffered(1))
    row = pl.BlockSpec((nb, tm, D), lambda b, i: (b, i, 0))
    g2 = g.reshape(1, D)
    ins = [x, x, g2, wug, wuv, cg, cv, pg, pv, wd]
    specs = [row, pl.BlockSpec((nb, H, D), lambda b, i: (b, jnp.maximum(i * nh - 1, 0), 0)),
             const(g2), const(wug), const(wuv), const(cg), const(cv),
             pl.BlockSpec((nb, H, F), lambda b, i: (b, 0, 0)), pl.BlockSpec((nb, H, F), lambda b, i: (b, 0, 0)),
             const(wd)]
    outs = [row, pl.BlockSpec((nb, 1, 8, 2 * F), lambda b, i: (b, i, 0, 0))]
    shapes = [jax.ShapeDtypeStruct((B, T, D), F32), jax.ShapeDtypeStruct((B, nt, 8, 2 * F), F32)]
    if final_g is not None:
        gf = final_g.reshape(1, D)
        ins.append(gf)
        specs.append(const(gf))
        outs.append(row)
        shapes.append(jax.ShapeDtypeStruct((B, T, D), F32))
    res = pl.pallas_call(
        functools.partial(_ffn_kernel, tm=tm, nb=nb, final=final_g is not None),
        grid=(B // nb, nt),
        in_specs=specs,
        out_specs=outs,
        out_shape=shapes,
        scratch_shapes=[pltpu.VMEM((nb * (tm + H), D), BF16), pltpu.VMEM((2, nb * (tm + H), FFN_CW), F32),
                        pltpu.VMEM((2, nb * (tm + H), FFN_CW), F32), pltpu.VMEM((nb * tm, FFN_CW), BF16),
                        pltpu.VMEM((nb * tm, D), F32)],
        compiler_params=_cparams(("parallel", "parallel")),
        name="conv_ffn",
    )(*ins)
    state = res[1][:, nt - 1, 8 - (CONV_W - 1):, :]
    return (res[0], state) + tuple(res[2:])


def _rope(x, pos, rot):
    half = rot // 2
    inv = ROPE_THETA ** (-jnp.arange(half, dtype=F32) / half)
    ang = pos.astype(F32)[:, None] * inv[None, :]
    shape = (ang.shape[0],) + (1,) * (x.ndim - 3) + (half,)
    cos = jnp.cos(ang).reshape(shape)
    sin = jnp.sin(ang).reshape(shape)
    x1 = x[..., :half]
    x2 = x[..., half:rot]
    return jnp.concatenate([x1 * cos - x2 * sin, x1 * sin + x2 * cos, x[..., rot:]], axis=-1)


def _pad_cols(w, n):
    return jnp.pad(w, ((0, 0), (0, n - w.shape[1])))


def _pad_keys(a, L):
    return jnp.pad(a, ((0, 0), (0, L - a.shape[1])) + ((0, 0),) * (a.ndim - 2))


def _dsa_layer(x, pos, past_k, past_v, past_ki, g, w_in, w_out):
    B, T, D = x.shape
    tq = min(256, T)
    qh, kf, kt, vf, vt, qi, tail, kib = _dsa_proj(x, g, w_in, _rope_tables(pos, A_HEAD_DIM, A_ROT), tq)
    P = past_k.shape[1]
    L = P + T
    Lp = _round_up(L, ATTN_KB)
    if Lp != T:
        past = lambda a: jnp.transpose(a, (0, 2, 1, 3)).astype(BF16)
        kt = jnp.pad(jnp.concatenate([past(past_k), kt], axis=2), ((0, 0), (0, 0), (0, Lp - L), (0, 0)))
        vt = jnp.pad(jnp.concatenate([past(past_v), vt], axis=2), ((0, 0), (0, 0), (0, Lp - L), (0, 0)))
        kib = _pad_keys(jnp.concatenate([past_ki.astype(BF16), kib], axis=1), Lp)
    x = _attn2(qh, kt, vt, x, w_out.astype(BF16), pos0=P, tq=tq, group=A_HEADS // A_KV_HEADS,
               idx=(qi, tail, kib), top=min(TOPK_MAX, L // 4), wi_off=IDX_DIM)
    kv_rows = lambda a: a.reshape(B, T, A_KV_HEADS, A_HEAD_DIM)
    return x, kv_rows(kf), kv_rows(vf), tail[..., :IDX_DIM]


def _rwkv_layer(x, shift_prev, S0, g, mu, w_rkv, w0, w1, w2, a0, a1, a2, g1, g2, k_k, k_a, r_k, ln_w, ln_b,
                w_out):
    B, T, D = x.shape
    tm = min(256, T)
    bf = lambda a: a.astype(BF16)
    vecs = jnp.stack([w0, a0, k_k, k_a, r_k.reshape(D)], axis=0)
    ws = [bf(w_rkv[0]), bf(w_rkv[1]), bf(w_rkv[2]), bf(w1), bf(w2), bf(a1), bf(a2), bf(g1), bf(g2)]
    rp, decay, k, v, a_vec, b_vec, yc, bonus, gate = _rwkv_pre(x, shift_prev, g, mu, vecs, ws, tm)
    y, S = _wkv(rp, decay, k, v, a_vec, b_vec, S0)
    ln = jnp.pad(jnp.stack([ln_w, ln_b], axis=0), ((0, 6), (0, 0)))
    x_new = _rwkv_post(y, yc, bonus, gate, x, ln, bf(w_out), tm)
    assert T >= 8
    shift = _norm(x[:, T - 8:].reshape(B * 8, D), g).reshape(B, 8, D)[:, -1]
    return x_new, shift, S


def _mla_layer(x, pos, past_lat, past_rope, g, w_in, g_q, g_kv, w_uq, w_ukv, w_out):
    B, T, D = x.shape
    tq = min(256, T)
    weights = _mla_weights(w_in, w_uq, w_ukv)
    tabs = _rope_tables(pos, LANE, C_ROPE, offset=C_NOPE)
    qh, kt, vt, lat, kpe = _mla_proj(x, g, g_q, g_kv, weights, tabs, tq)
    P = past_lat.shape[1]
    L = P + T
    Lp = _round_up(L, ATTN_KB)
    if Lp != T:
        past_slab = jnp.pad(past_rope, ((0, 0), (0, 0), (C_NOPE, LANE - C_NOPE - C_ROPE)))
        kt_p, vt_p = _mla_kv(past_lat, past_slab, weights[2], weights[3], _row_tile(P, 256))
        kt = jnp.pad(jnp.concatenate([kt_p, kt], axis=2), ((0, 0), (0, 0), (0, Lp - L), (0, 0)))
        vt = jnp.pad(jnp.concatenate([vt_p, vt], axis=2), ((0, 0), (0, 0), (0, Lp - L), (0, 0)))
    x = _attn2(qh, kt, vt, x, w_out.astype(BF16), pos0=P, tq=tq, group=1)
    return x, lat, kpe


def _ffn_layer(x, prev, g, w_up, w_conv, b_conv, w_down, final_g=None):
    F = D_FF
    taps = jnp.concatenate([w_conv, b_conv[None, :], jnp.zeros((8 - CONV_W - 1, 2 * F), F32)], axis=0)
    w_up = w_up.astype(BF16)
    return _ffn(x, g, w_up[:, :F], w_up[:, F:], taps[:, :F], taps[:, F:], prev, w_down.astype(BF16), final_g)


def _trunk(x, pos0, st, w):
    B, T, D = x.shape
    pos = pos0 + jnp.arange(T, dtype=jnp.int32)
    new = {name: [] for name in ('a_k', 'a_v', 'a_idx', 'b_wkv', 'b_shift', 'c_lat', 'c_rope', 'ffn')}
    for i in range(DEPTH):
        j = i // N_MIXERS
        kind = i % N_MIXERS
        if kind == 0:
            x, k, v, ki = _dsa_layer(x, pos, st['a_k'][j], st['a_v'][j], st['a_idx'][j], w['n_mix'][i],
                                     w['a_w_in'][j], w['a_w_out'][j])
            new['a_k'].append(k)
            new['a_v'].append(v)
            new['a_idx'].append(ki)
        elif kind == 1:
            x, shift, S = _rwkv_layer(x, st['b_shift'][j], st['b_wkv'][j], w['n_mix'][i], w['b_mu'][j],
                                      w['b_w_rkv'][j], w['b_w0'][j], w['b_w1'][j], w['b_w2'][j], w['b_a0'][j],
                                      w['b_a1'][j], w['b_a2'][j], w['b_g1'][j], w['b_g2'][j], w['b_k_k'][j],
                                      w['b_k_a'][j], w['b_r_k'][j], w['b_ln_w'][j], w['b_ln_b'][j],
                                      w['b_w_out'][j])
            new['b_shift'].append(shift)
            new['b_wkv'].append(S)
        else:
            x, lat, kpe = _mla_layer(x, pos, st['c_lat'][j], st['c_rope'][j], w['n_mix'][i], w['c_w_in'][j],
                                     w['c_g_q'][j], w['c_g_kv'][j], w['c_w_uq'][j], w['c_w_ukv'][j],
                                     w['c_w_out'][j])
            new['c_lat'].append(lat)
            new['c_rope'].append(kpe)
        last = i == DEPTH - 1
        res = _ffn_layer(x, st['ffn'][i], w['n_ffn'][i], w['f_w_up'][i], w['f_w_conv'][i],
                         w['f_b_conv'][i], w['f_w_down'][i], w['n_final'] if last else None)
        x = res[0]
        new['ffn'].append(res[1])
    return res[2], {name: jnp.stack(rows, axis=0) for name, rows in new.items()}


def kernel(x_prompt, x_sample, cache_a_k, cache_a_v, cache_a_idx, state_b_wkv, state_b_shift,
           cache_c_latent, cache_c_rope, state_ffn_conv, n_mix, n_ffn, n_final, a_w_in, a_w_out,
           b_mu, b_w_rkv, b_w0, b_w1, b_w2, b_a0, b_a1, b_a2, b_g1, b_g2, b_k_k, b_k_a, b_r_k,
           b_ln_w, b_ln_b, b_w_out, c_w_in, c_g_q, c_g_kv, c_w_uq, c_w_ukv, c_w_out,
           f_w_up, f_w_conv, f_b_conv, f_w_down):
    w = dict(n_mix=n_mix, n_ffn=n_ffn, n_final=n_final, a_w_in=a_w_in, a_w_out=a_w_out,
             b_mu=b_mu, b_w_rkv=b_w_rkv, b_w0=b_w0, b_w1=b_w1, b_w2=b_w2, b_a0=b_a0, b_a1=b_a1,
             b_a2=b_a2, b_g1=b_g1, b_g2=b_g2, b_k_k=b_k_k, b_k_a=b_k_a, b_r_k=b_r_k,
             b_ln_w=b_ln_w, b_ln_b=b_ln_b, b_w_out=b_w_out, c_w_in=c_w_in, c_g_q=c_g_q,
             c_g_kv=c_g_kv, c_w_uq=c_w_uq, c_w_ukv=c_w_ukv, c_w_out=c_w_out,
             f_w_up=f_w_up, f_w_conv=f_w_conv, f_b_conv=f_b_conv, f_w_down=f_w_down)
    Bp, Tp, D = x_prompt.shape
    n_a, n_b, n_c = cache_a_k.shape[0], state_b_wkv.shape[0], cache_c_latent.shape[0]
    st_prompt = dict(
        a_k=jnp.zeros((n_a, Bp, 0, A_KV_HEADS, A_HEAD_DIM), F32),
        a_v=jnp.zeros((n_a, Bp, 0, A_KV_HEADS, A_HEAD_DIM), F32),
        a_idx=jnp.zeros((n_a, Bp, 0, IDX_DIM), F32),
        b_wkv=jnp.zeros((n_b, Bp, B_HEADS, B_HEAD_DIM, B_HEAD_DIM), F32),
        b_shift=jnp.zeros((n_b, Bp, D), F32),
        c_lat=jnp.zeros((n_c, Bp, 0, C_KV_RANK), F32),
        c_rope=jnp.zeros((n_c, Bp, 0, C_ROPE), F32),
        ffn=jnp.zeros((DEPTH, Bp, CONV_W - 1, 2 * D_FF), F32))
    st_sample = dict(a_k=cache_a_k, a_v=cache_a_v, a_idx=cache_a_idx, b_wkv=state_b_wkv,
                     b_shift=state_b_shift, c_lat=cache_c_latent, c_rope=cache_c_rope,
                     ffn=state_ffn_conv)
    y_prompt, sp = _trunk(x_prompt, 0, st_prompt, w)
    y_sample, ss = _trunk(x_sample, cache_a_k.shape[2], st_sample, w)
    return (y_prompt, y_sample,
            sp['a_k'], ss['a_k'], sp['a_v'], ss['a_v'], sp['a_idx'], ss['a_idx'],
            sp['b_wkv'], ss['b_wkv'], sp['b_shift'], ss['b_shift'],
            sp['c_lat'], ss['c_lat'], sp['c_rope'], ss['c_rope'],
            sp['ffn'], ss['ffn'])
```

```python
import functools

import jax
import jax.numpy as jnp
from jax import lax
from jax.experimental import pallas as pl
from jax.experimental.pallas import tpu as pltpu

F32 = jnp.float32
BF16 = jnp.bfloat16

D_MODEL = 1024
DEPTH = 4
CHUNK = 64
N_MIXERS = 3
NORM_EPS = 1e-6
ROPE_THETA = 500000.0
A_HEADS, A_HEAD_DIM, A_KV_HEADS = 16, 64, 2
A_ROT = A_HEAD_DIM // 4
IDX_HEADS, IDX_DIM = 8, 64
IDX_ROT = IDX_DIM // 4
TOPK_MAX = 256
A_O_Q = A_HEADS * A_HEAD_DIM
A_O_K = A_O_Q + A_KV_HEADS * A_HEAD_DIM
A_O_V = A_O_K + A_KV_HEADS * A_HEAD_DIM
A_O_QI = A_O_V + IDX_HEADS * IDX_DIM
A_O_KI = A_O_QI + IDX_DIM
A_IN = A_O_KI + IDX_HEADS
B_HEAD_DIM = 64
B_HEADS = D_MODEL // B_HEAD_DIM
B_GN_EPS = 64e-5
C_HEADS, C_NOPE, C_ROPE, C_V = 16, 64, 32, 64
C_Q_RANK, C_KV_RANK = 512, 256
D_FF = 2816
CONV_W = 3

LANE = 128
SUBLANE_BF16 = 16
VMEM_LIMIT = 56 * 1024 * 1024
NEG_INF = float("-inf")
LOG2E = 1.4426950408889634


def _round_up(n, m):
    return (n + m - 1) // m * m


def _row_tile(M, pref):
    t = min(pref, M)
    while M % t:
        t //= 2
    return t


def _cparams(sem):
    return pltpu.CompilerParams(dimension_semantics=sem, vmem_limit_bytes=VMEM_LIMIT)


def _norm_kernel(x_ref, g_ref, o_ref):
    xf = x_ref[...]
    o_ref[...] = xf * lax.rsqrt(jnp.mean(xf * xf, axis=-1, keepdims=True) + NORM_EPS) * g_ref[...]


def _norm(x, g, tm=512):
    M, K = x.shape
    tm = _row_tile(M, tm)
    return pl.pallas_call(
        _norm_kernel,
        grid=(M // tm,),
        in_specs=[pl.BlockSpec((tm, K), lambda i: (i, 0)), pl.BlockSpec((1, K), lambda i: (0, 0))],
        out_specs=pl.BlockSpec((tm, K), lambda i: (i, 0)),
        out_shape=jax.ShapeDtypeStruct((M, K), F32),
        compiler_params=_cparams(("parallel",)),
        name="rmsnorm",
    )(x, g.reshape(1, K))


ATTN_KB = 512
ATTN_KB_SHIFT = ATTN_KB.bit_length() - 1
SELECT_ALL = 1e9
MANTISSA_BITS = 23
ATTN_ROWS_PER_ITER = 1024


def _fold(x, op=jnp.add):
    acc = x[:, :LANE]
    for j in range(1, x.shape[1] // LANE):
        acc = op(acc, x[:, j * LANE:(j + 1) * LANE])
    return acc


def _topk_bias_t(tab_ref, qi_ref, wi_ref, ki_ref, SC, MS, *, nk, first, tq, top, wi_off):
    kb = ATTN_KB
    qi = qi_ref[0]
    w_t = jnp.transpose(wi_ref[0])[wi_off:wi_off + IDX_HEADS, :]
    part = 64
    zeros_p = jnp.zeros((part, tq), F32)
    zero = jnp.zeros((1, tq), F32)
    kf = float(top)
    pos_q = lax.broadcasted_iota(jnp.int32, (1, tq), 1) + first
    limit = (pos_q & ~(CHUNK - 1)) + CHUNK
    key_i = lax.broadcasted_iota(jnp.int32, (kb, tq), 0)

    def key_sum(body):
        def step(j, a):
            return a + jnp.sum(body(j).reshape(kb // part, part, tq), axis=0)

        return jnp.sum(lax.fori_loop(0, nk, step, zeros_p), axis=0, keepdims=True)

    def ones_where(c):
        return jnp.where(c, 1.0, 0.0)

    def score_block(j):
        kij = ki_ref[0, pl.ds(pl.multiple_of(j * kb, kb), kb), :]
        sc = jnp.zeros((kb, tq), F32)
        for h in range(IDX_HEADS):
            d = lax.dot_general(kij, qi[:, h * IDX_DIM:(h + 1) * IDX_DIM], (((1,), (1,)), ((), ())),
                                preferred_element_type=F32)
            sc = sc + w_t[h:h + 1, :] * jnp.maximum(d, 0.0)
        sc = jnp.where(key_i + j * kb < limit, sc, NEG_INF)
        SC[j] = sc
        return ones_where(sc >= 0.0)

    c0 = key_sum(score_block)
    neg = c0 < kf
    sgn = jnp.where(neg, -1.0, 1.0)
    kp = jnp.where(neg, kf, (nk * kb).astype(F32) - kf + 1.0)

    def flip(j, c):
        SC[j] = SC[j] * sgn
        return c

    lax.fori_loop(0, nk, flip, 0)

    def count_lt(cand):
        return key_sum(lambda j: ones_where(SC[j] < cand))

    def exp_step(i, carry):
        e_cur, t_cur = carry
        cand = jnp.where(e_cur == 0.0, tab_ref[0, i], t_cur * tab_ref[1, i])
        ok = count_lt(cand) < kp
        return jnp.where(ok, e_cur + tab_ref[2, i], e_cur), jnp.where(ok, cand, t_cur)

    _, t_pow = lax.fori_loop(0, 8, exp_step, (zero, zero))

    def man_step(i, carry):
        t_cur, frac = carry
        cand = t_cur + frac
        ok = count_lt(cand) < kp
        return jnp.where(ok, cand, t_cur), frac * 0.5

    t_cur, _ = lax.fori_loop(0, MANTISSA_BITS, man_step, (t_pow, t_pow * 0.5))
    thr = t_cur * sgn

    def score(j):
        return SC[j] * sgn

    def key_idx(j):
        return (key_i + j * kb).astype(F32)

    need = kf - key_sum(lambda j: ones_where(score(j) > thr))
    n_eq = key_sum(lambda j: ones_where(score(j) == thr))

    def index_cut():
        nbits = (SC.shape[0] * kb - 1).bit_length()

        def bit_step(i, carry):
            c_cur, bit = carry
            cand = c_cur + bit
            ok = key_sum(lambda j: ones_where((score(j) == thr) & (key_idx(j) < cand))) < need
            return jnp.where(ok, cand, c_cur), bit * 0.5

        c_cur, _ = lax.fori_loop(0, nbits, bit_step, (zero, jnp.full((1, tq), 2.0 ** (nbits - 1), F32)))
        return c_cur

    any_split = jnp.max(ones_where(n_eq > need)) > 0.0
    c_cut = lax.cond(any_split, index_cut, lambda: jnp.full((1, tq), SELECT_ALL, F32))

    def write_bias(j, c):
        s = score(j)
        sel = (s > thr) | ((s == thr) & (key_idx(j) <= c_cut))
        MS[j] = jnp.transpose(jnp.where(sel & (key_i + j * kb < limit), 0.0, NEG_INF))
        return c

    lax.fori_loop(0, nk, write_bias, 0)


def _attn2_kernel(*refs, n_kv, group, tq, pos0, top, indexer, wi_off):
    if indexer:
        (tab_ref, q_ref, k_ref, v_ref, x_ref, wo_ref, qi_ref, wi_ref, ki_ref, xo_ref,
         MS, LG, MACC, LACC, OACC, OH, OALL, SC) = refs
    else:
        q_ref, k_ref, v_ref, x_ref, wo_ref, xo_ref, MS, LG, MACC, LACC, OACC, OH, OALL = refs
    kb = ATTN_KB
    first = pos0 + pl.program_id(1) * tq
    last_limit = ((first + tq - 1) & ~(CHUNK - 1)) + CHUNK
    nk = (last_limit + (kb - 1)) >> ATTN_KB_SHIFT
    row = lax.broadcasted_iota(jnp.int32, (tq, 1), 0) + first
    limit = (row & ~(CHUNK - 1)) + CHUNK
    lane_i = lax.broadcasted_iota(jnp.int32, (tq, kb), 1)

    if indexer:
        _topk_bias_t(tab_ref, qi_ref, wi_ref, ki_ref, SC, MS, nk=nk, first=first, tq=tq, top=top, wi_off=wi_off)
    else:
        def causal_bias(j, c):
            MS[j] = jnp.where(lane_i + j * kb < limit, 0.0, NEG_INF)
            return c

        lax.fori_loop(0, nk, causal_bias, 0)

    hu = LG.shape[0]

    def per_kv_heads(gi, c):
        heads = [gi * hu + u for u in range(hu)]
        qs = [q_ref[0, g, 0] for g in heads]
        MACC[...] = jnp.full(MACC.shape, NEG_INF, F32)
        LACC[...] = jnp.zeros(LACC.shape, F32)
        OACC[...] = jnp.zeros(OACC.shape, F32)

        def logits_block(j, c_):
            keys = pl.ds(pl.multiple_of(j * kb, kb), kb)
            bias = MS[j][None]
            for u, g in enumerate(heads):
                lg = lax.dot_general(qs[u], k_ref[0, g, keys, :], (((1,), (1,)), ((), ())),
                                     preferred_element_type=F32)
                lg = (lg.reshape(group, tq, kb) + bias).reshape(group * tq, kb)
                LG[u, j] = lg
                MACC[u] = jnp.maximum(MACC[u], _fold(lg, jnp.maximum))
            return c_

        lax.fori_loop(0, nk, logits_block, 0)
        ms = [jnp.max(MACC[u], axis=-1, keepdims=True) for u in range(hu)]

        def value_block(j, c_):
            keys = pl.ds(pl.multiple_of(j * kb, kb), kb)
            for u, g in enumerate(heads):
                p = jnp.exp2(LG[u, j] - ms[u])
                LACC[u] += _fold(p)
                OACC[u] += jnp.dot(p.astype(BF16), v_ref[0, g, keys, :], preferred_element_type=F32)
            return c_

        lax.fori_loop(0, nk, value_block, 0)
        for u, g in enumerate(heads):
            OH[g] = (OACC[u] / jnp.sum(LACC[u], axis=-1, keepdims=True)).astype(OH.dtype)
        return c

    lax.fori_loop(0, n_kv // hu, per_kv_heads, 0)
    dv = OH.shape[2]
    for h in range(n_kv * group):
        OALL[:, h * dv:(h + 1) * dv] = OH[h // group, (h % group) * tq:(h % group + 1) * tq, :]
    xo_ref[0] = x_ref[0] + jnp.dot(OALL[...], wo_ref[...], preferred_element_type=F32)


def _attn2(qg, kt, vt, x, w_out, *, pos0, tq, group, idx=None, top=0, wi_off=0):
    B, n_kv, nq, rows, dq = qg.shape
    L, dv = kt.shape[2], vt.shape[3]
    T, D = x.shape[1], x.shape[2]
    kb = ATTN_KB
    assert nq * tq == T and rows == group * tq and L % kb == 0 and (idx is None or top <= kb)
    ins = [qg, kt, vt, x, w_out]
    specs = [pl.BlockSpec((1, n_kv, 1, rows, dq), lambda b, i: (b, 0, i, 0, 0)),
             pl.BlockSpec((1, n_kv, L, dq), lambda b, i: (b, 0, 0, 0)),
             pl.BlockSpec((1, n_kv, L, dv), lambda b, i: (b, 0, 0, 0)),
             pl.BlockSpec((1, tq, D), lambda b, i: (b, i, 0)),
             pl.BlockSpec(w_out.shape, lambda b, i: (0, 0))]
    hu = max(1, min(n_kv, ATTN_ROWS_PER_ITER // rows))
    assert n_kv % hu == 0
    scratch = [pltpu.VMEM((L // kb, tq, kb), F32), pltpu.VMEM((hu, L // kb, rows, kb), F32),
               pltpu.VMEM((hu, rows, LANE), F32), pltpu.VMEM((hu, rows, LANE), F32),
               pltpu.VMEM((hu, rows, dv), F32), pltpu.VMEM((n_kv, rows, dv), BF16),
               pltpu.VMEM((tq, n_kv * group * dv), BF16)]
    if idx is not None:
        qi, wi, ki = idx
        steps = [2 ** b for b in range(7, -1, -1)]
        tab = jnp.array([[2.0 ** (s - 127) for s in steps], [2.0 ** s if s < 128 else 1.0 for s in steps],
                         [float(s) for s in steps]], F32)
        ins = [tab] + ins + [qi, wi, ki]
        specs = ([pl.BlockSpec(memory_space=pltpu.SMEM)] + specs
                 + [pl.BlockSpec((1, tq, qi.shape[2]), lambda b, i: (b, i, 0)),
                    pl.BlockSpec((1, tq, wi.shape[2]), lambda b, i: (b, i, 0)),
                    pl.BlockSpec((1, L, ki.shape[2]), lambda b, i: (b, 0, 0))])
        scratch.append(pltpu.VMEM((L // kb, kb, tq), F32))
    return pl.pallas_call(
        functools.partial(_attn2_kernel, n_kv=n_kv, group=group, tq=tq, pos0=pos0, top=top,
                          indexer=idx is not None, wi_off=wi_off),
        grid=(B, nq),
        in_specs=specs,
        out_specs=pl.BlockSpec((1, tq, D), lambda b, i: (b, i, 0)),
        out_shape=jax.ShapeDtypeStruct((B, T, D), F32),
        scratch_shapes=scratch,
        compiler_params=_cparams(("parallel", "parallel")),
        name="dsa_attn" if idx is not None else "mla_attn",
    )(*ins)


def _rope_tables(pos, dh, rot, offset=0):
    half = rot // 2
    inv = ROPE_THETA ** (-jnp.arange(half, dtype=F32) / half)
    ang = pos.astype(F32)[:, None] * inv[None, :]
    cos, sin = jnp.cos(ang), jnp.sin(ang)
    T = pos.shape[0]
    pad = lambda n, v: jnp.full((T, n), v, F32)
    zh = pad(half, 0.0)
    lo, hi = offset, dh - offset - rot
    c = jnp.concatenate([pad(lo, 1.0), cos, cos, pad(hi, 1.0)], axis=1)
    s1 = jnp.concatenate([pad(lo, 0.0), -sin, zh, pad(hi, 0.0)], axis=1)
    s2 = jnp.concatenate([pad(lo, 0.0), zh, sin, pad(hi, 0.0)], axis=1)
    return tuple(jnp.tile(t, (1, LANE // dh)) for t in (c, s1, s2))


def _rope_lanes(x, c, s1, s2, half):
    return x * c + pltpu.roll(x, LANE - half, 1) * s1 + pltpu.roll(x, half, 1) * s2


def _dsa_proj_kernel(x_ref, g_ref, w_ref, c_ref, s1_ref, s2_ref, ts_ref, qh_ref, kf_ref, kt_ref, vf_ref, vt_ref,
                     qi_ref, tail_ref, kib_ref, *, tm):
    xf = x_ref[0]
    h = xf * lax.rsqrt(jnp.mean(xf * xf, axis=-1, keepdims=True) + NORM_EPS) * g_ref[...]
    acc = jnp.dot(h.astype(BF16), w_ref[...], preferred_element_type=F32)
    c, s1, s2 = c_ref[...], s1_ref[...], s2_ref[...]
    half = A_ROT // 2
    hd = A_HEAD_DIM
    group = A_HEADS // A_KV_HEADS
    rope = lambda xs: _rope_lanes(xs, c, s1, s2, half)
    slab = lambda off: acc[:, off:off + LANE]
    for s in range(A_O_Q // LANE):
        qs = (rope(slab(s * LANE)) * (hd ** -0.5 * LOG2E)).astype(BF16)
        for e in range(LANE // hd):
            head = s * (LANE // hd) + e
            u = head % group
            qh_ref[0, head // group, 0, u * tm:(u + 1) * tm, :] = qs[:, e * hd:(e + 1) * hd]
    ks = rope(slab(A_O_Q))
    vs = slab(A_O_K)
    kf_ref[0] = ks
    vf_ref[0] = vs
    for e in range(A_KV_HEADS):
        kt_ref[0, e] = ks[:, e * hd:(e + 1) * hd].astype(BF16)
        vt_ref[0, e] = vs[:, e * hd:(e + 1) * hd].astype(BF16)
    for s in range(IDX_HEADS * IDX_DIM // LANE):
        qi_ref[0, :, s * LANE:(s + 1) * LANE] = rope(slab(A_O_V + s * LANE)).astype(BF16)
    is_key = lax.broadcasted_iota(jnp.int32, (tm, LANE), 1) < IDX_DIM
    tl = _rope_lanes(slab(A_O_QI), jnp.where(is_key, c, 1.0), jnp.where(is_key, s1, 0.0),
                     jnp.where(is_key, s2, 0.0), half) * ts_ref[...]
    tail_ref[0] = tl
    kib_ref[0] = tl[:, :IDX_DIM].astype(BF16)


def _dsa_proj(x, g, w_in, tabs, tm):
    B, T, D = x.shape
    assert (A_HEAD_DIM, A_ROT) == (IDX_DIM, IDX_ROT) and A_KV_HEADS * A_HEAD_DIM == LANE
    assert A_O_QI % LANE == 0 and IDX_DIM + IDX_HEADS <= LANE and T % tm == 0
    n_in = _round_up(A_IN, LANE)
    group = A_HEADS // A_KV_HEADS
    hd = A_HEAD_DIM
    nq = T // tm
    lanes = jnp.arange(LANE)
    tail_scale = jnp.where(lanes < IDX_DIM, 1.0, jnp.where(lanes < IDX_DIM + IDX_HEADS,
                                                            (IDX_HEADS * IDX_DIM) ** -0.5, 0.0)).astype(F32)
    row = lambda n: pl.BlockSpec((1, tm, n), lambda b, i: (b, i, 0))
    tab = pl.BlockSpec((tm, LANE), lambda b, i: (i, 0))
    const = lambda a: pl.BlockSpec(a.shape, lambda b, i: (0,) * a.ndim)
    kvh = pl.BlockSpec((1, A_KV_HEADS, tm, hd), lambda b, i: (b, 0, i, 0))
    w = _pad_cols(w_in, n_in).astype(BF16)
    g2 = g.reshape(1, D)
    ts = tail_scale.reshape(1, LANE)
    return pl.pallas_call(
        functools.partial(_dsa_proj_kernel, tm=tm),
        grid=(B, nq),
        in_specs=[row(D), const(g2), const(w), tab, tab, tab, const(ts)],
        out_specs=[pl.BlockSpec((1, A_KV_HEADS, 1, group * tm, hd), lambda b, i: (b, 0, i, 0, 0)),
                   row(LANE), kvh, row(LANE), kvh, row(IDX_HEADS * IDX_DIM), row(LANE), row(IDX_DIM)],
        out_shape=[jax.ShapeDtypeStruct((B, A_KV_HEADS, nq, group * tm, hd), BF16),
                   jax.ShapeDtypeStruct((B, T, LANE), F32), jax.ShapeDtypeStruct((B, A_KV_HEADS, T, hd), BF16),
                   jax.ShapeDtypeStruct((B, T, LANE), F32), jax.ShapeDtypeStruct((B, A_KV_HEADS, T, hd), BF16),
                   jax.ShapeDtypeStruct((B, T, IDX_HEADS * IDX_DIM), BF16),
                   jax.ShapeDtypeStruct((B, T, LANE), F32), jax.ShapeDtypeStruct((B, T, IDX_DIM), BF16)],
        compiler_params=_cparams(("parallel", "parallel")),
        name="dsa_proj",
    )(x, g2, w, *tabs, ts)


def _rms(xf, gain):
    return xf * lax.rsqrt(jnp.mean(xf * xf, axis=-1, keepdims=True) + NORM_EPS) * gain


def _mla_write_kv(lat, kpe_slab, wuk_ref, wuv_ref, kt_ref, vt_ref):
    lb = lat.astype(BF16)
    kn = jnp.dot(lb, wuk_ref[...], preferred_element_type=F32)
    vv = jnp.dot(lb, wuv_ref[...], preferred_element_type=F32)
    for h in range(C_HEADS):
        kt_ref[0, h] = (kn[:, h * LANE:(h + 1) * LANE] + kpe_slab).astype(BF16)
        vt_ref[0, h] = vv[:, h * C_V:(h + 1) * C_V].astype(BF16)


def _mla_proj_kernel(x_ref, g_ref, win_ref, gq_ref, gkv_ref, wuq_ref, wuk_ref, wuv_ref, c_ref, s1_ref, s2_ref,
                     qh_ref, kt_ref, vt_ref, lat_ref, kpe_ref):
    h = _rms(x_ref[0], g_ref[...]).astype(BF16)
    proj = jnp.dot(h, win_ref[...], preferred_element_type=F32)
    c, s1, s2 = c_ref[...], s1_ref[...], s2_ref[...]
    half = C_ROPE // 2
    q = jnp.dot(_rms(proj[:, :C_Q_RANK], gq_ref[...]).astype(BF16), wuq_ref[...], preferred_element_type=F32)
    scale = (C_NOPE + C_ROPE) ** -0.5 * LOG2E
    for hd in range(C_HEADS):
        qh_ref[0, hd, 0] = (_rope_lanes(q[:, hd * LANE:(hd + 1) * LANE], c, s1, s2, half) * scale).astype(BF16)
    lat = _rms(proj[:, C_Q_RANK:C_Q_RANK + C_KV_RANK], gkv_ref[...])
    lat_ref[0] = lat
    kpe_slab = _rope_lanes(proj[:, C_Q_RANK + C_KV_RANK:], c, s1, s2, half)
    kpe_ref[0] = kpe_slab[:, C_NOPE:C_NOPE + C_ROPE]
    _mla_write_kv(lat, kpe_slab, wuk_ref, wuv_ref, kt_ref, vt_ref)


def _mla_kv_kernel(lat_ref, kpe_ref, wuk_ref, wuv_ref, kt_ref, vt_ref):
    _mla_write_kv(lat_ref[0], kpe_ref[0], wuk_ref, wuv_ref, kt_ref, vt_ref)


def _mla_weights(w_in, w_uq, w_ukv):
    D = w_in.shape[0]
    zc = lambda rows, n: jnp.zeros((rows, n), w_in.dtype)
    w_in2 = jnp.concatenate([w_in[:, :C_Q_RANK + C_KV_RANK], zc(D, C_NOPE), w_in[:, C_Q_RANK + C_KV_RANK:],
                             zc(D, LANE - C_NOPE - C_ROPE)], axis=1)
    pad_heads = lambda w, d: jnp.pad(w.reshape(w.shape[0], C_HEADS, d), ((0, 0), (0, 0), (0, LANE - d))).reshape(
        w.shape[0], C_HEADS * LANE)
    w_uq2 = pad_heads(w_uq, C_NOPE + C_ROPE)
    ukv = w_ukv.reshape(C_KV_RANK, C_HEADS, C_NOPE + C_V)
    w_uk2 = pad_heads(ukv[..., :C_NOPE].reshape(C_KV_RANK, C_HEADS * C_NOPE), C_NOPE)
    w_uv2 = ukv[..., C_NOPE:].reshape(C_KV_RANK, C_HEADS * C_V)
    return tuple(a.astype(BF16) for a in (w_in2, w_uq2, w_uk2, w_uv2))


def _mla_proj(x, g, g_q, g_kv, weights, tabs, tm):
    B, T, D = x.shape
    w_in2, w_uq2, w_uk2, w_uv2 = weights
    nq = T // tm
    row = lambda n: pl.BlockSpec((1, tm, n), lambda b, i: (b, i, 0))
    tab = pl.BlockSpec((tm, LANE), lambda b, i: (i, 0))
    const = lambda a: pl.BlockSpec(a.shape, lambda b, i: (0,) * a.ndim)
    heads = lambda d: pl.BlockSpec((1, C_HEADS, tm, d), lambda b, i: (b, 0, i, 0))
    vec = lambda a: a.reshape(1, a.shape[0])
    return pl.pallas_call(
        _mla_proj_kernel,
        grid=(B, nq),
        in_specs=[row(D), const(vec(g)), const(w_in2), const(vec(g_q)), const(vec(g_kv)), const(w_uq2),
                  const(w_uk2), const(w_uv2), tab, tab, tab],
        out_specs=[pl.BlockSpec((1, C_HEADS, 1, tm, LANE), lambda b, i: (b, 0, i, 0, 0)), heads(LANE), heads(C_V),
                   row(C_KV_RANK), row(C_ROPE)],
        out_shape=[jax.ShapeDtypeStruct((B, C_HEADS, nq, tm, LANE), BF16),
                   jax.ShapeDtypeStruct((B, C_HEADS, T, LANE), BF16), jax.ShapeDtypeStruct((B, C_HEADS, T, C_V), BF16),
                   jax.ShapeDtypeStruct((B, T, C_KV_RANK), F32), jax.ShapeDtypeStruct((B, T, C_ROPE), F32)],
        compiler_params=_cparams(("parallel", "parallel")),
        name="mla_proj",
    )(x, vec(g), w_in2, vec(g_q), vec(g_kv), w_uq2, w_uk2, w_uv2, *tabs)


def _mla_kv(lat, kpe_slab, w_uk2, w_uv2, tm):
    B, P, _ = lat.shape
    row = lambda n: pl.BlockSpec((1, tm, n), lambda b, i: (b, i, 0))
    const = lambda a: pl.BlockSpec(a.shape, lambda b, i: (0,) * a.ndim)
    heads = lambda d: pl.BlockSpec((1, C_HEADS, tm, d), lambda b, i: (b, 0, i, 0))
    return pl.pallas_call(
        _mla_kv_kernel,
        grid=(B, P // tm),
        in_specs=[row(C_KV_RANK), row(LANE), const(w_uk2), const(w_uv2)],
        out_specs=[heads(LANE), heads(C_V)],
        out_shape=[jax.ShapeDtypeStruct((B, C_HEADS, P, LANE), BF16), jax.ShapeDtypeStruct((B, C_HEADS, P, C_V), BF16)],
        compiler_params=_cparams(("parallel", "parallel")),
        name="mla_kv",
    )(lat, kpe_slab, w_uk2, w_uv2)


def _head_sum(x, bo_ref):
    bw = bo_ref.shape[0]
    hi = x.astype(BF16)
    lo = (x - hi.astype(F32)).astype(BF16)
    bo = bo_ref[...]
    return jnp.concatenate(
        [jnp.dot(hi[:, j * bw:(j + 1) * bw], bo, preferred_element_type=F32)
         + jnp.dot(lo[:, j * bw:(j + 1) * bw], bo, preferred_element_type=F32) for j in range(x.shape[1] // bw)],
        axis=-1)


RWKV_HALO = 8


def _rwkv_pre_kernel(x_ref, xh_ref, sh_ref, g_ref, mu_ref, vec_ref, wr_ref, wk_ref, wv_ref, w1_ref, w2_ref,
                     a1_ref, a2_ref, g1_ref, g2_ref, bo_ref,
                     rp_ref, w_ref, k_ref, v_ref, a_ref, b_ref, yc_ref, bonus_ref, gate_ref, *, tm):
    i = pl.program_id(1)
    gain = g_ref[...]

    def norm(xf):
        return xf * lax.rsqrt(jnp.mean(xf * xf, axis=-1, keepdims=True) + NORM_EPS) * gain

    h = norm(x_ref[0])
    before = jnp.where(i > 0, norm(xh_ref[0])[RWKV_HALO - 1:RWKV_HALO, :], sh_ref[0])
    first = lax.broadcasted_iota(jnp.int32, (tm, 1), 0) == 0
    xx = jnp.where(first, before, pltpu.roll(h, 1, 0)) - h
    mu = mu_ref[...]
    vec = vec_ref[...]
    w0, a0, k_k, k_a, r_k = (vec[j:j + 1, :] for j in range(5))

    def mix(j):
        return (h + xx * mu[j:j + 1, :]).astype(BF16)

    dot = lambda a_, w_: jnp.dot(a_, w_[...], preferred_element_type=F32)
    r = dot(mix(0), wr_ref)
    wl = dot(jnp.tanh(dot(mix(1), w1_ref)).astype(BF16), w2_ref)
    k = dot(mix(2), wk_ref)
    v = dot(mix(3), wv_ref)
    al = dot(dot(mix(4), a1_ref).astype(BF16), a2_ref)
    gate_ref[0] = dot(jax.nn.sigmoid(dot(mix(5), g1_ref)).astype(BF16), g2_ref)
    z = -(w0 + wl)
    softplus = jnp.maximum(z, 0.0) + jnp.log(1.0 + jnp.exp(-jnp.abs(z)))
    decay = jnp.exp(-jnp.exp(-softplus - 0.5))
    a = jax.nn.sigmoid(a0 + al)
    kk = k * k_k
    kk = kk / jnp.maximum(jnp.sqrt(_head_sum(kk * kk, bo_ref)), 1e-12)
    k = k * (1.0 + (a - 1.0) * k_a)
    b = kk * a
    rp_ref[0] = decay * r - kk * _head_sum(b * r, bo_ref)
    w_ref[0] = decay
    k_ref[0] = k
    v_ref[0] = v
    a_ref[0] = -kk
    b_ref[0] = b
    yc_ref[0] = v * _head_sum(k * r, bo_ref)
    bonus_ref[0] = _head_sum(r * k * r_k, bo_ref) * v


def _rwkv_post_kernel(y_ref, yc_ref, bonus_ref, gate_ref, x_ref, ln_ref, wo_ref, bo_ref, o_ref):
    n = float(B_HEAD_DIM)
    y = y_ref[0] + yc_ref[0]
    d = y - _head_sum(y, bo_ref) / n
    var = _head_sum(d * d, bo_ref) / n
    ln = ln_ref[...]
    yn = d * lax.rsqrt(var + B_GN_EPS) * ln[0:1, :] + ln[1:2, :] + bonus_ref[0]
    o_ref[0] = x_ref[0] + jnp.dot((yn * gate_ref[0]).astype(BF16), wo_ref[...], preferred_element_type=F32)


def _block_ones():
    blk = jnp.arange(WKV_BW) // B_HEAD_DIM
    return (blk[:, None] == blk[None, :]).astype(BF16)


def _rwkv_pre(x, shift_prev, g, mu, vecs, ws, tm):
    B, T, D = x.shape
    H = RWKV_HALO
    assert T % tm == 0 and tm % H == 0
    nh = tm // H
    bo = _block_ones()
    row = pl.BlockSpec((1, tm, D), lambda b, i: (b, i, 0))
    const = lambda a: pl.BlockSpec(a.shape, lambda b, i: (0,) * a.ndim)
    pad8 = lambda a: jnp.pad(a, ((0, 8 - a.shape[0]), (0, 0)))
    mu8, vec8, g2 = pad8(mu), pad8(vecs), g.reshape(1, D)
    return pl.pallas_call(
        functools.partial(_rwkv_pre_kernel, tm=tm),
        grid=(B, T // tm),
        in_specs=[row, pl.BlockSpec((1, H, D), lambda b, i: (b, jnp.maximum(i * nh - 1, 0), 0)),
                  pl.BlockSpec((1, 1, D), lambda b, i: (b, 0, 0)), const(g2), const(mu8), const(vec8)]
                 + [const(a) for a in ws] + [const(bo)],
        out_specs=[row] * 9,
        out_shape=[jax.ShapeDtypeStruct((B, T, D), F32)] * 9,
        compiler_params=_cparams(("parallel", "parallel")),
        name="rwkv_pre",
    )(x, x, shift_prev.reshape(B, 1, D), g2, mu8, vec8, *ws, bo)


def _rwkv_post(y, yc, bonus, gate, x, ln, w_out, tm):
    B, T, D = x.shape
    bo = _block_ones()
    row = pl.BlockSpec((1, tm, D), lambda b, i: (b, i, 0))
    const = lambda a: pl.BlockSpec(a.shape, lambda b, i: (0,) * a.ndim)
    return pl.pallas_call(
        _rwkv_post_kernel,
        grid=(B, T // tm),
        in_specs=[row] * 5 + [const(ln), const(w_out), const(bo)],
        out_specs=row,
        out_shape=jax.ShapeDtypeStruct((B, T, D), F32),
        compiler_params=_cparams(("parallel", "parallel")),
        name="rwkv_post",
    )(y, yc, bonus, gate, x, ln, w_out, bo)


WKV_SUB = 8
WKV_NB = 8
WKV_TC = 64
WKV_BW = 2 * LANE


def _wkv_kernel(rp_ref, w_ref, k_ref, vh_ref, a_ref, b_ref, s0_ref, gs_ref, hs_ref, y_ref, sT_ref, S, *, tc, nb):
    c = pl.program_id(1)
    H = B_HEADS

    @pl.when(c == 0)
    def _():
        S[...] = s0_ref[...]

    def block(sc, carry):
        base = pl.multiple_of(sc * WKV_SUB, WKV_SUB)
        rows = pl.ds(base, WKV_SUB)
        gs = gs_ref[...]
        hsel = hs_ref[...]
        ins = [tuple(x[bi, rows, :] for x in (rp_ref, w_ref, k_ref, a_ref, b_ref)) for bi in range(nb)]
        for u in range(WKV_SUB):
            sas = []
            for bi in range(nb):
                r8, w8, k8, a8, b8 = ins[bi]
                s = S[bi]
                pa = (s * a8[u:u + 1, :]).astype(BF16)
                sas.append(jnp.dot(pa, gs, preferred_element_type=F32))
                pr = (s * r8[u:u + 1, :]).astype(BF16)
                y_ref[bi, base + u] = lax.dot_general(hsel, pr, (((1,), (1,)), ((), ())),
                                                      preferred_element_type=F32)
            deltas = []
            for bi in range(nb):
                r8, w8, k8, a8, b8 = ins[bi]
                v_t = jnp.transpose(vh_ref[bi, base + u])
                lhs = jnp.concatenate([sas[bi][:, :H], v_t], axis=1).astype(BF16)
                rhs = jnp.concatenate([hsel * b8[u:u + 1, :].astype(BF16), hsel * k8[u:u + 1, :].astype(BF16)],
                                      axis=0)
                deltas.append(jnp.dot(lhs, rhs, preferred_element_type=F32))
            for bi in range(nb):
                r8, w8, k8, a8, b8 = ins[bi]
                S[bi] = S[bi] * w8[u:u + 1, :] + deltas[bi]
        return carry

    lax.fori_loop(0, tc // WKV_SUB, block, 0)

    @pl.when(c == pl.num_programs(1) - 1)
    def _():
        sT_ref[...] = S[...]


def _wkv(rp, w, k, v, a, b, s0):
    B, T, D = rp.shape
    n = B_HEAD_DIM
    H = B_HEADS
    nb = _row_tile(B, WKV_NB)
    tc = _row_tile(T, WKV_TC)
    assert tc % WKV_SUB == 0 and H <= LANE
    s0t = jnp.transpose(s0, (0, 2, 1, 3)).reshape(B, n, D)
    hsel = (jnp.arange(H)[:, None] == (jnp.arange(D) // n)[None, :]).astype(BF16)
    gs = (jnp.arange(D)[:, None] // n == jnp.arange(LANE)[None, :]).astype(BF16)
    seq = pl.BlockSpec((nb, tc, D), lambda bi, c: (bi, c, 0))
    seqh = pl.BlockSpec((nb, tc, H, n), lambda bi, c: (bi, c, 0, 0))
    st = pl.BlockSpec((nb, n, D), lambda bi, c: (bi, 0, 0))
    const = lambda a_: pl.BlockSpec(a_.shape, lambda bi, c: (0, 0))
    y, sT = pl.pallas_call(
        functools.partial(_wkv_kernel, tc=tc, nb=nb),
        grid=(B // nb, T // tc),
        in_specs=[seq, seq, seq, seqh, seq, seq, st, const(gs), const(hsel)],
        out_specs=[seqh, st],
        out_shape=[jax.ShapeDtypeStruct((B, T, H, n), F32), jax.ShapeDtypeStruct((B, n, D), F32)],
        scratch_shapes=[pltpu.VMEM((nb, n, D), F32)],
        compiler_params=_cparams(("parallel", "arbitrary")),
        name="wkv",
    )(rp, w, k, v.reshape(B, T, H, n), a, b, s0t, gs, hsel)
    return y.reshape(B, T, D), jnp.transpose(sT.reshape(B, n, H, n), (0, 2, 1, 3))


FFN_HALO = SUBLANE_BF16


FFN_CW = 2 * LANE


def _ffn_body(x_ref, xh_ref, g_ref, wug_ref, wuv_ref, cg_ref, cv_ref, pg_ref, pv_ref, wd_ref, gf_ref,
              o_ref, st_ref, y_ref, hn, ug, uv, act, acc, *, tm, nb):
    i = pl.program_id(1)
    H = FFN_HALO
    cw_ = FFN_CW
    seg = tm + H

    def norm(xf, gain):
        return xf * lax.rsqrt(jnp.mean(xf * xf, axis=-1, keepdims=True) + NORM_EPS) * gain

    for s in range(nb):
        hn[s * seg:s * seg + H, :] = jnp.where(i > 0, norm(xh_ref[s], g_ref[...]), 0.0).astype(BF16)
        hn[s * seg + H:(s + 1) * seg, :] = norm(x_ref[s], g_ref[...]).astype(BF16)
    h = hn[...]
    first = jnp.where(i == 0, 1.0, 0.0)

    def conv(u, r0, taps):
        return (taps[3:4, :] + u[r0 - 2:r0 - 2 + tm, :] * taps[0:1, :] + u[r0 - 1:r0 - 1 + tm, :] * taps[1:2, :]
                + u[r0:r0 + tm, :] * taps[2:3, :])

    for c in range(D_FF // cw_):
        cols = slice(c * cw_, (c + 1) * cw_)
        for half, (u_scr, w_ref, p_ref) in enumerate(((ug, wug_ref, pg_ref), (uv, wuv_ref, pv_ref))):
            u = jnp.dot(h, w_ref[:, cols], preferred_element_type=F32)
            u_scr[c % 2] = u.astype(BF16)
            for s in range(nb):
                u_scr[c % 2, s * seg:s * seg + H, :] = (u[s * seg:s * seg + H, :]
                                                        + p_ref[s, :, cols] * first).astype(BF16)
                st_ref[s, 0, :, c * cw_ + half * D_FF:(c + 1) * cw_ + half * D_FF] = u[(s + 1) * seg - 8:(s + 1) * seg, :]
        for s in range(nb):
            gate = conv(ug.at[c % 2], s * seg + H, cg_ref[:, cols].astype(BF16))
            val = conv(uv.at[c % 2], s * seg + H, cv_ref[:, cols].astype(BF16))
            act[s * tm:(s + 1) * tm, :] = gate * jax.nn.sigmoid(gate) * val
        d = jnp.dot(act[...], wd_ref[cols, :], preferred_element_type=F32)
        if c == 0:
            acc[...] = d
        else:
            acc[...] += d
    for s in range(nb):
        out = x_ref[s] + acc[s * tm:(s + 1) * tm, :]
        o_ref[s] = out
        if gf_ref is not None:
            y_ref[s] = norm(out, gf_ref[...])


def _ffn_kernel(*refs, tm, nb, final):
    if final:
        (x_ref, xh_ref, g_ref, wug_ref, wuv_ref, cg_ref, cv_ref, pg_ref, pv_ref, wd_ref, gf_ref,
         o_ref, st_ref, y_ref, hn, ug, uv, act, acc) = refs
    else:
        (x_ref, xh_ref, g_ref, wug_ref, wuv_ref, cg_ref, cv_ref, pg_ref, pv_ref, wd_ref,
         o_ref, st_ref, hn, ug, uv, act, acc) = refs
        gf_ref = y_ref = None
    _ffn_body(x_ref, xh_ref, g_ref, wug_ref, wuv_ref, cg_ref, cv_ref, pg_ref, pv_ref, wd_ref, gf_ref,
              o_ref, st_ref, y_ref, hn, ug, uv, act, acc, tm=tm, nb=nb)


FFN_ROWS = 512


def _ffn(x, g, wug, wuv, cg, cv, prev, wd, final_g=None):
    B, T, D = x.shape
    F = D_FF
    H = FFN_HALO
    tm = min(FFN_ROWS, T)
    nb = _row_tile(B, max(1, FFN_ROWS // tm))
    assert T % tm == 0 and tm % H == 0 and F % FFN_CW == 0 and tm >= 8
    nh = tm // H
    nt = T // tm
    prev_h = jnp.pad(prev, ((0, 0), (H - prev.shape[1], 0), (0, 0)))
    pg, pv = prev_h[:, :, :F], prev_h[:, :, F:]
    const = lambda a: pl.BlockSpec(a.shape, lambda b, i: (0,) * a.ndim, pipeline_mode=pl.Buffered(1))
    row = pl.BlockSpec((nb, tm, D), lambda b, i: (b, i, 0))
    g2 = g.reshape(1, D)
    ins = [x, x, g2, wug, wuv, cg, cv, pg, pv, wd]
    specs = [row, pl.BlockSpec((nb, H, D), lambda b, i: (b, jnp.maximum(i * nh - 1, 0), 0)),
             const(g2), const(wug), const(wuv), const(cg), const(cv),
             pl.BlockSpec((nb, H, F), lambda b, i: (b, 0, 0)), pl.BlockSpec((nb, H, F), lambda b, i: (b, 0, 0)),
             const(wd)]
    outs = [row, pl.BlockSpec((nb, 1, 8, 2 * F), lambda b, i: (b, i, 0, 0))]
    shapes = [jax.ShapeDtypeStruct((B, T, D), F32), jax.ShapeDtypeStruct((B, nt, 8, 2 * F), F32)]
    if final_g is not None:
        gf = final_g.reshape(1, D)
        ins.append(gf)
        specs.append(const(gf))
        outs.append(row)
        shapes.append(jax.ShapeDtypeStruct((B, T, D), F32))
    res = pl.pallas_call(
        functools.partial(_ffn_kernel, tm=tm, nb=nb, final=final_g is not None),
        grid=(B // nb, nt),
        in_specs=specs,
        out_specs=outs,
        out_shape=shapes,
        scratch_shapes=[pltpu.VMEM((nb * (tm + H), D), BF16), pltpu.VMEM((2, nb * (tm + H), FFN_CW), BF16),
                        pltpu.VMEM((2, nb * (tm + H), FFN_CW), BF16), pltpu.VMEM((nb * tm, FFN_CW), BF16),
                        pltpu.VMEM((nb * tm, D), F32)],
        compiler_params=_cparams(("parallel", "parallel")),
        name="conv_ffn",
    )(*ins)
    state = res[1][:, nt - 1, 8 - (CONV_W - 1):, :]
    return (res[0], state) + tuple(res[2:])


def _pad_cols(w, n):
    return jnp.pad(w, ((0, 0), (0, n - w.shape[1])))


def _pad_keys(a, L):
    return jnp.pad(a, ((0, 0), (0, L - a.shape[1])) + ((0, 0),) * (a.ndim - 2))


def _dsa_layer(x, pos, past_k, past_v, past_ki, g, w_in, w_out):
    B, T, D = x.shape
    tq = min(256, T)
    qh, kf, kt, vf, vt, qi, tail, kib = _dsa_proj(x, g, w_in, _rope_tables(pos, A_HEAD_DIM, A_ROT), tq)
    P = past_k.shape[1]
    L = P + T
    Lp = _round_up(L, ATTN_KB)
    if Lp != T:
        past = lambda a: jnp.transpose(a, (0, 2, 1, 3)).astype(BF16)
        kt = jnp.pad(jnp.concatenate([past(past_k), kt], axis=2), ((0, 0), (0, 0), (0, Lp - L), (0, 0)))
        vt = jnp.pad(jnp.concatenate([past(past_v), vt], axis=2), ((0, 0), (0, 0), (0, Lp - L), (0, 0)))
        kib = _pad_keys(jnp.concatenate([past_ki.astype(BF16), kib], axis=1), Lp)
    x = _attn2(qh, kt, vt, x, w_out.astype(BF16), pos0=P, tq=tq, group=A_HEADS // A_KV_HEADS,
               idx=(qi, tail, kib), top=min(TOPK_MAX, L // 4), wi_off=IDX_DIM)
    kv_rows = lambda a: a.reshape(B, T, A_KV_HEADS, A_HEAD_DIM)
    return x, kv_rows(kf), kv_rows(vf), tail[..., :IDX_DIM]


def _rwkv_layer(x, shift_prev, S0, g, mu, w_rkv, w0, w1, w2, a0, a1, a2, g1, g2, k_k, k_a, r_k, ln_w, ln_b,
                w_out):
    B, T, D = x.shape
    tm = min(256, T)
    bf = lambda a: a.astype(BF16)
    vecs = jnp.stack([w0, a0, k_k, k_a, r_k.reshape(D)], axis=0)
    ws = [bf(w_rkv[0]), bf(w_rkv[1]), bf(w_rkv[2]), bf(w1), bf(w2), bf(a1), bf(a2), bf(g1), bf(g2)]
    rp, decay, k, v, a_vec, b_vec, yc, bonus, gate = _rwkv_pre(x, shift_prev, g, mu, vecs, ws, tm)
    y, S = _wkv(rp, decay, k, v, a_vec, b_vec, S0)
    ln = jnp.pad(jnp.stack([ln_w, ln_b], axis=0), ((0, 6), (0, 0)))
    x_new = _rwkv_post(y, yc, bonus, gate, x, ln, bf(w_out), tm)
    assert T >= 8
    shift = _norm(x[:, T - 8:].reshape(B * 8, D), g).reshape(B, 8, D)[:, -1]
    return x_new, shift, S


def _mla_layer(x, pos, past_lat, past_rope, g, w_in, g_q, g_kv, w_uq, w_ukv, w_out):
    B, T, D = x.shape
    tq = min(256, T)
    weights = _mla_weights(w_in, w_uq, w_ukv)
    tabs = _rope_tables(pos, LANE, C_ROPE, offset=C_NOPE)
    qh, kt, vt, lat, kpe = _mla_proj(x, g, g_q, g_kv, weights, tabs, tq)
    P = past_lat.shape[1]
    L = P + T
    Lp = _round_up(L, ATTN_KB)
    if Lp != T:
        past_slab = jnp.pad(past_rope, ((0, 0), (0, 0), (C_NOPE, LANE - C_NOPE - C_ROPE)))
        kt_p, vt_p = _mla_kv(past_lat, past_slab, weights[2], weights[3], _row_tile(P, 256))
        kt = jnp.pad(jnp.concatenate([kt_p, kt], axis=2), ((0, 0), (0, 0), (0, Lp - L), (0, 0)))
        vt = jnp.pad(jnp.concatenate([vt_p, vt], axis=2), ((0, 0), (0, 0), (0, Lp - L), (0, 0)))
    x = _attn2(qh, kt, vt, x, w_out.astype(BF16), pos0=P, tq=tq, group=1)
    return x, lat, kpe


def _ffn_layer(x, prev, g, w_up, w_conv, b_conv, w_down, final_g=None):
    F = D_FF
    taps = jnp.concatenate([w_conv, b_conv[None, :], jnp.zeros((8 - CONV_W - 1, 2 * F), F32)], axis=0)
    w_up = w_up.astype(BF16)
    return _ffn(x, g, w_up[:, :F], w_up[:, F:], taps[:, :F], taps[:, F:], prev, w_down.astype(BF16), final_g)


def _trunk(x, pos0, st, w):
    B, T, D = x.shape
    pos = pos0 + jnp.arange(T, dtype=jnp.int32)
    new = {name: [] for name in ('a_k', 'a_v', 'a_idx', 'b_wkv', 'b_shift', 'c_lat', 'c_rope', 'ffn')}
    for i in range(DEPTH):
        j = i // N_MIXERS
        kind = i % N_MIXERS
        if kind == 0:
            x, k, v, ki = _dsa_layer(x, pos, st['a_k'][j], st['a_v'][j], st['a_idx'][j], w['n_mix'][i],
                                     w['a_w_in'][j], w['a_w_out'][j])
            new['a_k'].append(k)
            new['a_v'].append(v)
            new['a_idx'].append(ki)
        elif kind == 1:
            x, shift, S = _rwkv_layer(x, st['b_shift'][j], st['b_wkv'][j], w['n_mix'][i], w['b_mu'][j],
                                      w['b_w_rkv'][j], w['b_w0'][j], w['b_w1'][j], w['b_w2'][j], w['b_a0'][j],
                                      w['b_a1'][j], w['b_a2'][j], w['b_g1'][j], w['b_g2'][j], w['b_k_k'][j],
                                      w['b_k_a'][j], w['b_r_k'][j], w['b_ln_w'][j], w['b_ln_b'][j],
                                      w['b_w_out'][j])
            new['b_shift'].append(shift)
            new['b_wkv'].append(S)
        else:
            x, lat, kpe = _mla_layer(x, pos, st['c_lat'][j], st['c_rope'][j], w['n_mix'][i], w['c_w_in'][j],
                                     w['c_g_q'][j], w['c_g_kv'][j], w['c_w_uq'][j], w['c_w_ukv'][j],
                                     w['c_w_out'][j])
            new['c_lat'].append(lat)
            new['c_rope'].append(kpe)
        last = i == DEPTH - 1
        res = _ffn_layer(x, st['ffn'][i], w['n_ffn'][i], w['f_w_up'][i], w['f_w_conv'][i],
                         w['f_b_conv'][i], w['f_w_down'][i], w['n_final'] if last else None)
        x = res[0]
        new['ffn'].append(res[1])
    return res[2], {name: jnp.stack(rows, axis=0) for name, rows in new.items()}


def kernel(x_prompt, x_sample, cache_a_k, cache_a_v, cache_a_idx, state_b_wkv, state_b_shift,
           cache_c_latent, cache_c_rope, state_ffn_conv, n_mix, n_ffn, n_final, a_w_in, a_w_out,
           b_mu, b_w_rkv, b_w0, b_w1, b_w2, b_a0, b_a1, b_a2, b_g1, b_g2, b_k_k, b_k_a, b_r_k,
           b_ln_w, b_ln_b, b_w_out, c_w_in, c_g_q, c_g_kv, c_w_uq, c_w_ukv, c_w_out,
           f_w_up, f_w_conv, f_b_conv, f_w_down):
    w = dict(n_mix=n_mix, n_ffn=n_ffn, n_final=n_final, a_w_in=a_w_in, a_w_out=a_w_out,
             b_mu=b_mu, b_w_rkv=b_w_rkv, b_w0=b_w0, b_w1=b_w1, b_w2=b_w2, b_a0=b_a0, b_a1=b_a1,
             b_a2=b_a2, b_g1=b_g1, b_g2=b_g2, b_k_k=b_k_k, b_k_a=b_k_a, b_r_k=b_r_k,
             b_ln_w=b_ln_w, b_ln_b=b_ln_b, b_w_out=b_w_out, c_w_in=c_w_in, c_g_q=c_g_q,
             c_g_kv=c_g_kv, c_w_uq=c_w_uq, c_w_ukv=c_w_ukv, c_w_out=c_w_out,
             f_w_up=f_w_up, f_w_conv=f_w_conv, f_b_conv=f_b_conv, f_w_down=f_w_down)
    Bp, Tp, D = x_prompt.shape
    n_a, n_b, n_c = cache_a_k.shape[0], state_b_wkv.shape[0], cache_c_latent.shape[0]
    st_prompt = dict(
        a_k=jnp.zeros((n_a, Bp, 0, A_KV_HEADS, A_HEAD_DIM), F32),
        a_v=jnp.zeros((n_a, Bp, 0, A_KV_HEADS, A_HEAD_DIM), F32),
        a_idx=jnp.zeros((n_a, Bp, 0, IDX_DIM), F32),
        b_wkv=jnp.zeros((n_b, Bp, B_HEADS, B_HEAD_DIM, B_HEAD_DIM), F32),
        b_shift=jnp.zeros((n_b, Bp, D), F32),
        c_lat=jnp.zeros((n_c, Bp, 0, C_KV_RANK), F32),
        c_rope=jnp.zeros((n_c, Bp, 0, C_ROPE), F32),
        ffn=jnp.zeros((DEPTH, Bp, CONV_W - 1, 2 * D_FF), F32))
    st_sample = dict(a_k=cache_a_k, a_v=cache_a_v, a_idx=cache_a_idx, b_wkv=state_b_wkv,
                     b_shift=state_b_shift, c_lat=cache_c_latent, c_rope=cache_c_rope,
                     ffn=state_ffn_conv)
    y_prompt, sp = _trunk(x_prompt, 0, st_prompt, w)
    y_sample, ss = _trunk(x_sample, cache_a_k.shape[2], st_sample, w)
    return (y_prompt, y_sample,
            sp['a_k'], ss['a_k'], sp['a_v'], ss['a_v'], sp['a_idx'], ss['a_idx'],
            sp['b_wkv'], ss['b_wkv'], sp['b_shift'], ss['b_shift'],
            sp['c_lat'], ss['c_lat'], sp['c_rope'], ss['c_rope'],
            sp['ffn'], ss['ffn'])
```

```python
import functools

import jax
import jax.numpy as jnp
from jax import lax
from jax.experimental import pallas as pl
from jax.experimental.pallas import tpu as pltpu

F32 = jnp.float32
BF16 = jnp.bfloat16

D_MODEL = 1024
DEPTH = 4
CHUNK = 64
N_MIXERS = 3
NORM_EPS = 1e-6
ROPE_THETA = 500000.0
A_HEADS, A_HEAD_DIM, A_KV_HEADS = 16, 64, 2
A_ROT = A_HEAD_DIM // 4
IDX_HEADS, IDX_DIM = 8, 64
IDX_ROT = IDX_DIM // 4
TOPK_MAX = 256
A_O_Q = A_HEADS * A_HEAD_DIM
A_O_K = A_O_Q + A_KV_HEADS * A_HEAD_DIM
A_O_V = A_O_K + A_KV_HEADS * A_HEAD_DIM
A_O_QI = A_O_V + IDX_HEADS * IDX_DIM
A_O_KI = A_O_QI + IDX_DIM
A_IN = A_O_KI + IDX_HEADS
B_HEAD_DIM = 64
B_HEADS = D_MODEL // B_HEAD_DIM
B_GN_EPS = 64e-5
C_HEADS, C_NOPE, C_ROPE, C_V = 16, 64, 32, 64
C_Q_RANK, C_KV_RANK = 512, 256
D_FF = 2816
CONV_W = 3

LANE = 128
SUBLANE_BF16 = 16
VMEM_LIMIT = 56 * 1024 * 1024
NEG_INF = float("-inf")
LOG2E = 1.4426950408889634


def _round_up(n, m):
    return (n + m - 1) // m * m


def _row_tile(M, pref):
    t = min(pref, M)
    while M % t:
        t //= 2
    return t


def _cparams(sem):
    return pltpu.CompilerParams(dimension_semantics=sem, vmem_limit_bytes=VMEM_LIMIT)


def _norm_kernel(x_ref, g_ref, o_ref):
    xf = x_ref[...]
    o_ref[...] = xf * lax.rsqrt(jnp.mean(xf * xf, axis=-1, keepdims=True) + NORM_EPS) * g_ref[...]


def _norm(x, g, tm=512):
    M, K = x.shape
    tm = _row_tile(M, tm)
    return pl.pallas_call(
        _norm_kernel,
        grid=(M // tm,),
        in_specs=[pl.BlockSpec((tm, K), lambda i: (i, 0)), pl.BlockSpec((1, K), lambda i: (0, 0))],
        out_specs=pl.BlockSpec((tm, K), lambda i: (i, 0)),
        out_shape=jax.ShapeDtypeStruct((M, K), F32),
        compiler_params=_cparams(("parallel",)),
        name="rmsnorm",
    )(x, g.reshape(1, K))


ATTN_KB = 512
ATTN_KB_SHIFT = ATTN_KB.bit_length() - 1
SELECT_ALL = 1e9
MANTISSA_BITS = 23
ATTN_ROWS_PER_ITER = 1024


def _fold(x, op=jnp.add):
    acc = x[:, :LANE]
    for j in range(1, x.shape[1] // LANE):
        acc = op(acc, x[:, j * LANE:(j + 1) * LANE])
    return acc


def _topk_bias_t(tab_ref, qi_ref, wi_ref, ki_ref, SC, MS, *, nk, first, tq, top, wi_off):
    kb = ATTN_KB
    qi = qi_ref[0]
    w_t = jnp.transpose(wi_ref[0])[wi_off:wi_off + IDX_HEADS, :]
    part = 64
    zeros_p = jnp.zeros((part, tq), F32)
    zero = jnp.zeros((1, tq), F32)
    kf = float(top)
    pos_q = lax.broadcasted_iota(jnp.int32, (1, tq), 1) + first
    limit = (pos_q & ~(CHUNK - 1)) + CHUNK
    key_i = lax.broadcasted_iota(jnp.int32, (kb, tq), 0)

    def key_sum(body):
        def step(j, a):
            return a + jnp.sum(body(j).reshape(kb // part, part, tq), axis=0)

        return jnp.sum(lax.fori_loop(0, nk, step, zeros_p), axis=0, keepdims=True)

    def ones_where(c):
        return jnp.where(c, 1.0, 0.0)

    def score_block(j):
        kij = ki_ref[0, pl.ds(pl.multiple_of(j * kb, kb), kb), :]
        sc = jnp.zeros((kb, tq), F32)
        for h in range(IDX_HEADS):
            d = lax.dot_general(kij, qi[:, h * IDX_DIM:(h + 1) * IDX_DIM], (((1,), (1,)), ((), ())),
                                preferred_element_type=F32)
            sc = sc + w_t[h:h + 1, :] * jnp.maximum(d, 0.0)
        sc = jnp.where(key_i + j * kb < limit, sc, NEG_INF)
        SC[j] = sc
        return ones_where(sc >= 0.0)

    c0 = key_sum(score_block)
    neg = c0 < kf
    sgn = jnp.where(neg, -1.0, 1.0)
    kp = jnp.where(neg, kf, (nk * kb).astype(F32) - kf + 1.0)

    def flip(j, c):
        SC[j] = SC[j] * sgn
        return c

    lax.fori_loop(0, nk, flip, 0)

    def count_lt(cand):
        return key_sum(lambda j: ones_where(SC[j] < cand))

    def exp_step(i, carry):
        e_cur, t_cur = carry
        cand = jnp.where(e_cur == 0.0, tab_ref[0, i], t_cur * tab_ref[1, i])
        ok = count_lt(cand) < kp
        return jnp.where(ok, e_cur + tab_ref[2, i], e_cur), jnp.where(ok, cand, t_cur)

    _, t_pow = lax.fori_loop(0, 8, exp_step, (zero, zero))

    def man_step(i, carry):
        t_cur, frac = carry
        cand = t_cur + frac
        ok = count_lt(cand) < kp
        return jnp.where(ok, cand, t_cur), frac * 0.5

    t_cur, _ = lax.fori_loop(0, MANTISSA_BITS, man_step, (t_pow, t_pow * 0.5))
    thr = t_cur * sgn

    def score(j):
        return SC[j] * sgn

    def key_idx(j):
        return (key_i + j * kb).astype(F32)

    need = kf - key_sum(lambda j: ones_where(score(j) > thr))
    n_eq = key_sum(lambda j: ones_where(score(j) == thr))

    def index_cut():
        nbits = (SC.shape[0] * kb - 1).bit_length()

        def bit_step(i, carry):
            c_cur, bit = carry
            cand = c_cur + bit
            ok = key_sum(lambda j: ones_where((score(j) == thr) & (key_idx(j) < cand))) < need
            return jnp.where(ok, cand, c_cur), bit * 0.5

        c_cur, _ = lax.fori_loop(0, nbits, bit_step, (zero, jnp.full((1, tq), 2.0 ** (nbits - 1), F32)))
        return c_cur

    any_split = jnp.max(ones_where(n_eq > need)) > 0.0
    c_cut = lax.cond(any_split, index_cut, lambda: jnp.full((1, tq), SELECT_ALL, F32))

    def write_bias(j, c):
        s = score(j)
        sel = (s > thr) | ((s == thr) & (key_idx(j) <= c_cut))
        MS[j] = jnp.transpose(jnp.where(sel & (key_i + j * kb < limit), 0.0, NEG_INF))
        return c

    lax.fori_loop(0, nk, write_bias, 0)


def _attn2_kernel(*refs, n_kv, group, tq, pos0, top, indexer, wi_off):
    if indexer:
        (tab_ref, q_ref, k_ref, v_ref, x_ref, wo_ref, qi_ref, wi_ref, ki_ref, xo_ref,
         MS, LG, MACC, LACC, OACC, OH, OALL, SC) = refs
    else:
        q_ref, k_ref, v_ref, x_ref, wo_ref, xo_ref, MS, LG, MACC, LACC, OACC, OH, OALL = refs
    kb = ATTN_KB
    first = pos0 + pl.program_id(1) * tq
    last_limit = ((first + tq - 1) & ~(CHUNK - 1)) + CHUNK
    nk = (last_limit + (kb - 1)) >> ATTN_KB_SHIFT
    row = lax.broadcasted_iota(jnp.int32, (tq, 1), 0) + first
    limit = (row & ~(CHUNK - 1)) + CHUNK
    lane_i = lax.broadcasted_iota(jnp.int32, (tq, kb), 1)

    if indexer:
        _topk_bias_t(tab_ref, qi_ref, wi_ref, ki_ref, SC, MS, nk=nk, first=first, tq=tq, top=top, wi_off=wi_off)
    else:
        def causal_bias(j, c):
            MS[j] = jnp.where(lane_i + j * kb < limit, 0.0, NEG_INF)
            return c

        lax.fori_loop(0, nk, causal_bias, 0)

    hu = LG.shape[0]

    def per_kv_heads(gi, c):
        heads = [gi * hu + u for u in range(hu)]
        qs = [q_ref[0, g, 0] for g in heads]
        MACC[...] = jnp.full(MACC.shape, NEG_INF, F32)
        LACC[...] = jnp.zeros(LACC.shape, F32)
        OACC[...] = jnp.zeros(OACC.shape, F32)

        def logits_block(j, c_):
            keys = pl.ds(pl.multiple_of(j * kb, kb), kb)
            bias = MS[j][None]
            for u, g in enumerate(heads):
                lg = lax.dot_general(qs[u], k_ref[0, g, keys, :], (((1,), (1,)), ((), ())),
                                     preferred_element_type=F32)
                lg = (lg.reshape(group, tq, kb) + bias).reshape(group * tq, kb)
                LG[u, j] = lg
                MACC[u] = jnp.maximum(MACC[u], _fold(lg, jnp.maximum))
            return c_

        lax.fori_loop(0, nk, logits_block, 0)
        ms = [jnp.max(MACC[u], axis=-1, keepdims=True) for u in range(hu)]

        def value_block(j, c_):
            keys = pl.ds(pl.multiple_of(j * kb, kb), kb)
            for u, g in enumerate(heads):
                p = jnp.exp2(LG[u, j] - ms[u])
                LACC[u] += _fold(p)
                OACC[u] += jnp.dot(p.astype(BF16), v_ref[0, g, keys, :], preferred_element_type=F32)
            return c_

        lax.fori_loop(0, nk, value_block, 0)
        for u, g in enumerate(heads):
            OH[g] = (OACC[u] / jnp.sum(LACC[u], axis=-1, keepdims=True)).astype(OH.dtype)
        return c

    lax.fori_loop(0, n_kv // hu, per_kv_heads, 0)
    dv = OH.shape[2]
    for h in range(n_kv * group):
        OALL[:, h * dv:(h + 1) * dv] = OH[h // group, (h % group) * tq:(h % group + 1) * tq, :]
    xo_ref[0] = x_ref[0] + jnp.dot(OALL[...], wo_ref[...], preferred_element_type=F32)


def _attn2(qg, kt, vt, x, w_out, *, pos0, tq, group, idx=None, top=0, wi_off=0):
    B, n_kv, nq, rows, dq = qg.shape
    L, dv = kt.shape[2], vt.shape[3]
    T, D = x.shape[1], x.shape[2]
    kb = ATTN_KB
    assert nq * tq == T and rows == group * tq and L % kb == 0 and (idx is None or top <= kb)
    ins = [qg, kt, vt, x, w_out]
    specs = [pl.BlockSpec((1, n_kv, 1, rows, dq), lambda b, i: (b, 0, i, 0, 0)),
             pl.BlockSpec((1, n_kv, L, dq), lambda b, i: (b, 0, 0, 0)),
             pl.BlockSpec((1, n_kv, L, dv), lambda b, i: (b, 0, 0, 0)),
             pl.BlockSpec((1, tq, D), lambda b, i: (b, i, 0)),
             pl.BlockSpec(w_out.shape, lambda b, i: (0, 0))]
    hu = max(1, min(n_kv, ATTN_ROWS_PER_ITER // rows))
    assert n_kv % hu == 0
    scratch = [pltpu.VMEM((L // kb, tq, kb), F32), pltpu.VMEM((hu, L // kb, rows, kb), F32),
               pltpu.VMEM((hu, rows, LANE), F32), pltpu.VMEM((hu, rows, LANE), F32),
               pltpu.VMEM((hu, rows, dv), F32), pltpu.VMEM((n_kv, rows, dv), BF16),
               pltpu.VMEM((tq, n_kv * group * dv), BF16)]
    if idx is not None:
        qi, wi, ki = idx
        steps = [2 ** b for b in range(7, -1, -1)]
        tab = jnp.array([[2.0 ** (s - 127) for s in steps], [2.0 ** s if s < 128 else 1.0 for s in steps],
                         [float(s) for s in steps]], F32)
        ins = [tab] + ins + [qi, wi, ki]
        specs = ([pl.BlockSpec(memory_space=pltpu.SMEM)] + specs
                 + [pl.BlockSpec((1, tq, qi.shape[2]), lambda b, i: (b, i, 0)),
                    pl.BlockSpec((1, tq, wi.shape[2]), lambda b, i: (b, i, 0)),
                    pl.BlockSpec((1, L, ki.shape[2]), lambda b, i: (b, 0, 0))])
        scratch.append(pltpu.VMEM((L // kb, kb, tq), F32))
    return pl.pallas_call(
        functools.partial(_attn2_kernel, n_kv=n_kv, group=group, tq=tq, pos0=pos0, top=top,
                          indexer=idx is not None, wi_off=wi_off),
        grid=(B, nq),
        in_specs=specs,
        out_specs=pl.BlockSpec((1, tq, D), lambda b, i: (b, i, 0)),
        out_shape=jax.ShapeDtypeStruct((B, T, D), F32),
        scratch_shapes=scratch,
        compiler_params=_cparams(("parallel", "parallel")),
        name="dsa_attn" if idx is not None else "mla_attn",
    )(*ins)


def _rope_tables(pos, dh, rot, offset=0):
    half = rot // 2
    inv = ROPE_THETA ** (-jnp.arange(half, dtype=F32) / half)
    ang = pos.astype(F32)[:, None] * inv[None, :]
    cos, sin = jnp.cos(ang), jnp.sin(ang)
    T = pos.shape[0]
    pad = lambda n, v: jnp.full((T, n), v, F32)
    zh = pad(half, 0.0)
    lo, hi = offset, dh - offset - rot
    c = jnp.concatenate([pad(lo, 1.0), cos, cos, pad(hi, 1.0)], axis=1)
    s1 = jnp.concatenate([pad(lo, 0.0), -sin, zh, pad(hi, 0.0)], axis=1)
    s2 = jnp.concatenate([pad(lo, 0.0), zh, sin, pad(hi, 0.0)], axis=1)
    return tuple(jnp.tile(t, (1, LANE // dh)) for t in (c, s1, s2))


def _rope_lanes(x, c, s1, s2, half):
    return x * c + pltpu.roll(x, LANE - half, 1) * s1 + pltpu.roll(x, half, 1) * s2


def _dsa_proj_kernel(x_ref, g_ref, w_ref, c_ref, s1_ref, s2_ref, ts_ref, qh_ref, kf_ref, kt_ref, vf_ref, vt_ref,
                     qi_ref, tail_ref, kib_ref, *, tm):
    xf = x_ref[0]
    h = xf * lax.rsqrt(jnp.mean(xf * xf, axis=-1, keepdims=True) + NORM_EPS) * g_ref[...]
    acc = jnp.dot(h.astype(BF16), w_ref[...], preferred_element_type=F32)
    c, s1, s2 = c_ref[...], s1_ref[...], s2_ref[...]
    half = A_ROT // 2
    hd = A_HEAD_DIM
    group = A_HEADS // A_KV_HEADS
    rope = lambda xs: _rope_lanes(xs, c, s1, s2, half)
    slab = lambda off: acc[:, off:off + LANE]
    for s in range(A_O_Q // LANE):
        qs = (rope(slab(s * LANE)) * (hd ** -0.5 * LOG2E)).astype(BF16)
        for e in range(LANE // hd):
            head = s * (LANE // hd) + e
            u = head % group
            qh_ref[0, head // group, 0, u * tm:(u + 1) * tm, :] = qs[:, e * hd:(e + 1) * hd]
    ks = rope(slab(A_O_Q))
    vs = slab(A_O_K)
    kf_ref[0] = ks
    vf_ref[0] = vs
    for e in range(A_KV_HEADS):
        kt_ref[0, e] = ks[:, e * hd:(e + 1) * hd].astype(BF16)
        vt_ref[0, e] = vs[:, e * hd:(e + 1) * hd].astype(BF16)
    for s in range(IDX_HEADS * IDX_DIM // LANE):
        qi_ref[0, :, s * LANE:(s + 1) * LANE] = rope(slab(A_O_V + s * LANE)).astype(BF16)
    is_key = lax.broadcasted_iota(jnp.int32, (tm, LANE), 1) < IDX_DIM
    tl = _rope_lanes(slab(A_O_QI), jnp.where(is_key, c, 1.0), jnp.where(is_key, s1, 0.0),
                     jnp.where(is_key, s2, 0.0), half) * ts_ref[...]
    tail_ref[0] = tl
    kib_ref[0] = tl[:, :IDX_DIM].astype(BF16)


def _dsa_proj(x, g, w_in, tabs, tm):
    B, T, D = x.shape
    assert (A_HEAD_DIM, A_ROT) == (IDX_DIM, IDX_ROT) and A_KV_HEADS * A_HEAD_DIM == LANE
    assert A_O_QI % LANE == 0 and IDX_DIM + IDX_HEADS <= LANE and T % tm == 0
    n_in = _round_up(A_IN, LANE)
    group = A_HEADS // A_KV_HEADS
    hd = A_HEAD_DIM
    nq = T // tm
    lanes = jnp.arange(LANE)
    tail_scale = jnp.where(lanes < IDX_DIM, 1.0, jnp.where(lanes < IDX_DIM + IDX_HEADS,
                                                            (IDX_HEADS * IDX_DIM) ** -0.5, 0.0)).astype(F32)
    row = lambda n: pl.BlockSpec((1, tm, n), lambda b, i: (b, i, 0))
    tab = pl.BlockSpec((tm, LANE), lambda b, i: (i, 0))
    const = lambda a: pl.BlockSpec(a.shape, lambda b, i: (0,) * a.ndim)
    kvh = pl.BlockSpec((1, A_KV_HEADS, tm, hd), lambda b, i: (b, 0, i, 0))
    w = _pad_cols(w_in, n_in).astype(BF16)
    g2 = g.reshape(1, D)
    ts = tail_scale.reshape(1, LANE)
    return pl.pallas_call(
        functools.partial(_dsa_proj_kernel, tm=tm),
        grid=(B, nq),
        in_specs=[row(D), const(g2), const(w), tab, tab, tab, const(ts)],
        out_specs=[pl.BlockSpec((1, A_KV_HEADS, 1, group * tm, hd), lambda b, i: (b, 0, i, 0, 0)),
                   row(LANE), kvh, row(LANE), kvh, row(IDX_HEADS * IDX_DIM), row(LANE), row(IDX_DIM)],
        out_shape=[jax.ShapeDtypeStruct((B, A_KV_HEADS, nq, group * tm, hd), BF16),
                   jax.ShapeDtypeStruct((B, T, LANE), F32), jax.ShapeDtypeStruct((B, A_KV_HEADS, T, hd), BF16),
                   jax.ShapeDtypeStruct((B, T, LANE), F32), jax.ShapeDtypeStruct((B, A_KV_HEADS, T, hd), BF16),
                   jax.ShapeDtypeStruct((B, T, IDX_HEADS * IDX_DIM), BF16),
                   jax.ShapeDtypeStruct((B, T, LANE), F32), jax.ShapeDtypeStruct((B, T, IDX_DIM), BF16)],
        compiler_params=_cparams(("parallel", "parallel")),
        name="dsa_proj",
    )(x, g2, w, *tabs, ts)


def _rms(xf, gain):
    return xf * lax.rsqrt(jnp.mean(xf * xf, axis=-1, keepdims=True) + NORM_EPS) * gain


def _mla_write_kv(lat, kpe_slab, wuk_ref, wuv_ref, kt_ref, vt_ref):
    lb = lat.astype(BF16)
    kn = jnp.dot(lb, wuk_ref[...], preferred_element_type=F32)
    vv = jnp.dot(lb, wuv_ref[...], preferred_element_type=F32)
    for h in range(C_HEADS):
        kt_ref[0, h] = (kn[:, h * LANE:(h + 1) * LANE] + kpe_slab).astype(BF16)
        vt_ref[0, h] = vv[:, h * C_V:(h + 1) * C_V].astype(BF16)


def _mla_proj_kernel(x_ref, g_ref, win_ref, gq_ref, gkv_ref, wuq_ref, wuk_ref, wuv_ref, c_ref, s1_ref, s2_ref,
                     qh_ref, kt_ref, vt_ref, lat_ref, kpe_ref):
    h = _rms(x_ref[0], g_ref[...]).astype(BF16)
    proj = jnp.dot(h, win_ref[...], preferred_element_type=F32)
    c, s1, s2 = c_ref[...], s1_ref[...], s2_ref[...]
    half = C_ROPE // 2
    q = jnp.dot(_rms(proj[:, :C_Q_RANK], gq_ref[...]).astype(BF16), wuq_ref[...], preferred_element_type=F32)
    scale = (C_NOPE + C_ROPE) ** -0.5 * LOG2E
    for hd in range(C_HEADS):
        qh_ref[0, hd, 0] = (_rope_lanes(q[:, hd * LANE:(hd + 1) * LANE], c, s1, s2, half) * scale).astype(BF16)
    lat = _rms(proj[:, C_Q_RANK:C_Q_RANK + C_KV_RANK], gkv_ref[...])
    lat_ref[0] = lat
    kpe_slab = _rope_lanes(proj[:, C_Q_RANK + C_KV_RANK:], c, s1, s2, half)
    kpe_ref[0] = kpe_slab[:, C_NOPE:C_NOPE + C_ROPE]
    _mla_write_kv(lat, kpe_slab, wuk_ref, wuv_ref, kt_ref, vt_ref)


def _mla_kv_kernel(lat_ref, kpe_ref, wuk_ref, wuv_ref, kt_ref, vt_ref):
    _mla_write_kv(lat_ref[0], kpe_ref[0], wuk_ref, wuv_ref, kt_ref, vt_ref)


def _mla_weights(w_in, w_uq, w_ukv):
    D = w_in.shape[0]
    zc = lambda rows, n: jnp.zeros((rows, n), w_in.dtype)
    w_in2 = jnp.concatenate([w_in[:, :C_Q_RANK + C_KV_RANK], zc(D, C_NOPE), w_in[:, C_Q_RANK + C_KV_RANK:],
                             zc(D, LANE - C_NOPE - C_ROPE)], axis=1)
    pad_heads = lambda w, d: jnp.pad(w.reshape(w.shape[0], C_HEADS, d), ((0, 0), (0, 0), (0, LANE - d))).reshape(
        w.shape[0], C_HEADS * LANE)
    w_uq2 = pad_heads(w_uq, C_NOPE + C_ROPE)
    ukv = w_ukv.reshape(C_KV_RANK, C_HEADS, C_NOPE + C_V)
    w_uk2 = pad_heads(ukv[..., :C_NOPE].reshape(C_KV_RANK, C_HEADS * C_NOPE), C_NOPE)
    w_uv2 = ukv[..., C_NOPE:].reshape(C_KV_RANK, C_HEADS * C_V)
    return tuple(a.astype(BF16) for a in (w_in2, w_uq2, w_uk2, w_uv2))


def _mla_proj(x, g, g_q, g_kv, weights, tabs, tm):
    B, T, D = x.shape
    w_in2, w_uq2, w_uk2, w_uv2 = weights
    nq = T // tm
    row = lambda n: pl.BlockSpec((1, tm, n), lambda b, i: (b, i, 0))
    tab = pl.BlockSpec((tm, LANE), lambda b, i: (i, 0))
    const = lambda a: pl.BlockSpec(a.shape, lambda b, i: (0,) * a.ndim)
    heads = lambda d: pl.BlockSpec((1, C_HEADS, tm, d), lambda b, i: (b, 0, i, 0))
    vec = lambda a: a.reshape(1, a.shape[0])
    return pl.pallas_call(
        _mla_proj_kernel,
        grid=(B, nq),
        in_specs=[row(D), const(vec(g)), const(w_in2), const(vec(g_q)), const(vec(g_kv)), const(w_uq2),
                  const(w_uk2), const(w_uv2), tab, tab, tab],
        out_specs=[pl.BlockSpec((1, C_HEADS, 1, tm, LANE), lambda b, i: (b, 0, i, 0, 0)), heads(LANE), heads(C_V),
                   row(C_KV_RANK), row(C_ROPE)],
        out_shape=[jax.ShapeDtypeStruct((B, C_HEADS, nq, tm, LANE), BF16),
                   jax.ShapeDtypeStruct((B, C_HEADS, T, LANE), BF16), jax.ShapeDtypeStruct((B, C_HEADS, T, C_V), BF16),
                   jax.ShapeDtypeStruct((B, T, C_KV_RANK), F32), jax.ShapeDtypeStruct((B, T, C_ROPE), F32)],
        compiler_params=_cparams(("parallel", "parallel")),
        name="mla_proj",
    )(x, vec(g), w_in2, vec(g_q), vec(g_kv), w_uq2, w_uk2, w_uv2, *tabs)


def _mla_kv(lat, kpe_slab, w_uk2, w_uv2, tm):
    B, P, _ = lat.shape
    row = lambda n: pl.BlockSpec((1, tm, n), lambda b, i: (b, i, 0))
    const = lambda a: pl.BlockSpec(a.shape, lambda b, i: (0,) * a.ndim)
    heads = lambda d: pl.BlockSpec((1, C_HEADS, tm, d), lambda b, i: (b, 0, i, 0))
    return pl.pallas_call(
        _mla_kv_kernel,
        grid=(B, P // tm),
        in_specs=[row(C_KV_RANK), row(LANE), const(w_uk2), const(w_uv2)],
        out_specs=[heads(LANE), heads(C_V)],
        out_shape=[jax.ShapeDtypeStruct((B, C_HEADS, P, LANE), BF16), jax.ShapeDtypeStruct((B, C_HEADS, P, C_V), BF16)],
        compiler_params=_cparams(("parallel", "parallel")),
        name="mla_kv",
    )(lat, kpe_slab, w_uk2, w_uv2)


def _head_sum(x, bo_ref, split=True):
    bw = bo_ref.shape[0]
    bo = bo_ref[...]
    hi = x.astype(BF16)
    terms = [hi, (x - hi.astype(F32)).astype(BF16)] if split else [hi]
    return jnp.concatenate(
        [sum(jnp.dot(t[:, j * bw:(j + 1) * bw], bo, preferred_element_type=F32) for t in terms)
         for j in range(x.shape[1] // bw)], axis=-1)


RWKV_HALO = 8


def _rwkv_pre_kernel(x_ref, xh_ref, sh_ref, g_ref, mu_ref, vec_ref, wr_ref, wk_ref, wv_ref, w1_ref, w2_ref,
                     a1_ref, a2_ref, g1_ref, g2_ref, bo_ref,
                     rp_ref, w_ref, k_ref, v_ref, a_ref, b_ref, yc_ref, bonus_ref, gate_ref, *, tm):
    i = pl.program_id(1)
    gain = g_ref[...]

    def norm(xf):
        return xf * lax.rsqrt(jnp.mean(xf * xf, axis=-1, keepdims=True) + NORM_EPS) * gain

    h = norm(x_ref[0])
    before = jnp.where(i > 0, norm(xh_ref[0])[RWKV_HALO - 1:RWKV_HALO, :], sh_ref[0])
    first = lax.broadcasted_iota(jnp.int32, (tm, 1), 0) == 0
    xx = jnp.where(first, before, pltpu.roll(h, 1, 0)) - h
    mu = mu_ref[...]
    vec = vec_ref[...]
    w0, a0, k_k, k_a, r_k = (vec[j:j + 1, :] for j in range(5))

    def mix(j):
        return (h + xx * mu[j:j + 1, :]).astype(BF16)

    dot = lambda a_, w_: jnp.dot(a_, w_[...], preferred_element_type=F32)
    r = dot(mix(0), wr_ref)
    wl = dot(jnp.tanh(dot(mix(1), w1_ref)).astype(BF16), w2_ref)
    k = dot(mix(2), wk_ref)
    v = dot(mix(3), wv_ref)
    al = dot(dot(mix(4), a1_ref).astype(BF16), a2_ref)
    gate_ref[0] = dot(jax.nn.sigmoid(dot(mix(5), g1_ref)).astype(BF16), g2_ref)
    z = -(w0 + wl)
    softplus = jnp.maximum(z, 0.0) + jnp.log(1.0 + jnp.exp(-jnp.abs(z)))
    decay = jnp.exp(-jnp.exp(-softplus - 0.5))
    a = jax.nn.sigmoid(a0 + al)
    kk = k * k_k
    kk = kk / jnp.maximum(jnp.sqrt(_head_sum(kk * kk, bo_ref)), 1e-12)
    k = k * (1.0 + (a - 1.0) * k_a)
    b = kk * a
    rp_ref[0] = decay * r - kk * _head_sum(b * r, bo_ref, split=False)
    w_ref[0] = decay
    k_ref[0] = k
    v_ref[0] = v
    a_ref[0] = -kk
    b_ref[0] = b
    yc_ref[0] = v * _head_sum(k * r, bo_ref, split=False)
    bonus_ref[0] = _head_sum(r * k * r_k, bo_ref, split=False) * v


def _rwkv_post_kernel(y_ref, yc_ref, bonus_ref, gate_ref, x_ref, ln_ref, wo_ref, bo_ref, o_ref):
    n = float(B_HEAD_DIM)
    y = y_ref[0] + yc_ref[0]
    d = y - _head_sum(y, bo_ref) / n
    var = _head_sum(d * d, bo_ref) / n
    ln = ln_ref[...]
    yn = d * lax.rsqrt(var + B_GN_EPS) * ln[0:1, :] + ln[1:2, :] + bonus_ref[0]
    o_ref[0] = x_ref[0] + jnp.dot((yn * gate_ref[0]).astype(BF16), wo_ref[...], preferred_element_type=F32)


def _block_ones():
    blk = jnp.arange(WKV_BW) // B_HEAD_DIM
    return (blk[:, None] == blk[None, :]).astype(BF16)


def _rwkv_pre(x, shift_prev, g, mu, vecs, ws, tm):
    B, T, D = x.shape
    H = RWKV_HALO
    assert T % tm == 0 and tm % H == 0
    nh = tm // H
    bo = _block_ones()
    row = pl.BlockSpec((1, tm, D), lambda b, i: (b, i, 0))
    const = lambda a: pl.BlockSpec(a.shape, lambda b, i: (0,) * a.ndim)
    pad8 = lambda a: jnp.pad(a, ((0, 8 - a.shape[0]), (0, 0)))
    mu8, vec8, g2 = pad8(mu), pad8(vecs), g.reshape(1, D)
    return pl.pallas_call(
        functools.partial(_rwkv_pre_kernel, tm=tm),
        grid=(B, T // tm),
        in_specs=[row, pl.BlockSpec((1, H, D), lambda b, i: (b, jnp.maximum(i * nh - 1, 0), 0)),
                  pl.BlockSpec((1, 1, D), lambda b, i: (b, 0, 0)), const(g2), const(mu8), const(vec8)]
                 + [const(a) for a in ws] + [const(bo)],
        out_specs=[row] * 9,
        out_shape=[jax.ShapeDtypeStruct((B, T, D), F32)] * 9,
        compiler_params=_cparams(("parallel", "parallel")),
        name="rwkv_pre",
    )(x, x, shift_prev.reshape(B, 1, D), g2, mu8, vec8, *ws, bo)


def _rwkv_post(y, yc, bonus, gate, x, ln, w_out, tm):
    B, T, D = x.shape
    bo = _block_ones()
    row = pl.BlockSpec((1, tm, D), lambda b, i: (b, i, 0))
    const = lambda a: pl.BlockSpec(a.shape, lambda b, i: (0,) * a.ndim)
    return pl.pallas_call(
        _rwkv_post_kernel,
        grid=(B, T // tm),
        in_specs=[row] * 5 + [const(ln), const(w_out), const(bo)],
        out_specs=row,
        out_shape=jax.ShapeDtypeStruct((B, T, D), F32),
        compiler_params=_cparams(("parallel", "parallel")),
        name="rwkv_post",
    )(y, yc, bonus, gate, x, ln, w_out, bo)


WKV_SUB = 16
WKV_NB = 8
WKV_TC = 64
WKV_BW = 2 * LANE


def _wkv_kernel(rp_ref, w_ref, k_ref, vh_ref, a_ref, b_ref, s0_ref, gs_ref, hs_ref, y_ref, sT_ref, S, *, tc, nb):
    c = pl.program_id(1)
    H = B_HEADS

    @pl.when(c == 0)
    def _():
        S[...] = s0_ref[...]

    def block(sc, carry):
        base = pl.multiple_of(sc * WKV_SUB, WKV_SUB)
        rows = pl.ds(base, WKV_SUB)
        gs = gs_ref[...]
        hsel = hs_ref[...]
        ins = [tuple(x[bi, rows, :] for x in (rp_ref, w_ref, k_ref, a_ref, b_ref)) for bi in range(nb)]
        for u in range(WKV_SUB):
            sas = []
            for bi in range(nb):
                r8, w8, k8, a8, b8 = ins[bi]
                s = S[bi]
                pa = (s * a8[u:u + 1, :]).astype(BF16)
                sas.append(jnp.dot(pa, gs, preferred_element_type=F32))
                pr = (s * r8[u:u + 1, :]).astype(BF16)
                y_ref[bi, base + u] = lax.dot_general(hsel, pr, (((1,), (1,)), ((), ())),
                                                      preferred_element_type=F32)
            deltas = []
            for bi in range(nb):
                r8, w8, k8, a8, b8 = ins[bi]
                v_t = jnp.transpose(vh_ref[bi, base + u])
                lhs = jnp.concatenate([sas[bi][:, :H], v_t], axis=1).astype(BF16)
                rhs = jnp.concatenate([hsel * b8[u:u + 1, :].astype(BF16), hsel * k8[u:u + 1, :].astype(BF16)],
                                      axis=0)
                deltas.append(jnp.dot(lhs, rhs, preferred_element_type=F32))
            for bi in range(nb):
                r8, w8, k8, a8, b8 = ins[bi]
                S[bi] = S[bi] * w8[u:u + 1, :] + deltas[bi]
        return carry

    lax.fori_loop(0, tc // WKV_SUB, block, 0)

    @pl.when(c == pl.num_programs(1) - 1)
    def _():
        sT_ref[...] = S[...]


def _wkv(rp, w, k, v, a, b, s0):
    B, T, D = rp.shape
    n = B_HEAD_DIM
    H = B_HEADS
    nb = _row_tile(B, WKV_NB)
    tc = _row_tile(T, WKV_TC)
    assert tc % WKV_SUB == 0 and H <= LANE
    s0t = jnp.transpose(s0, (0, 2, 1, 3)).reshape(B, n, D)
    hsel = (jnp.arange(H)[:, None] == (jnp.arange(D) // n)[None, :]).astype(BF16)
    gs = (jnp.arange(D)[:, None] // n == jnp.arange(LANE)[None, :]).astype(BF16)
    seq = pl.BlockSpec((nb, tc, D), lambda bi, c: (bi, c, 0))
    seqh = pl.BlockSpec((nb, tc, H, n), lambda bi, c: (bi, c, 0, 0))
    st = pl.BlockSpec((nb, n, D), lambda bi, c: (bi, 0, 0))
    const = lambda a_: pl.BlockSpec(a_.shape, lambda bi, c: (0, 0))
    y, sT = pl.pallas_call(
        functools.partial(_wkv_kernel, tc=tc, nb=nb),
        grid=(B // nb, T // tc),
        in_specs=[seq, seq, seq, seqh, seq, seq, st, const(gs), const(hsel)],
        out_specs=[seqh, st],
        out_shape=[jax.ShapeDtypeStruct((B, T, H, n), F32), jax.ShapeDtypeStruct((B, n, D), F32)],
        scratch_shapes=[pltpu.VMEM((nb, n, D), F32)],
        compiler_params=_cparams(("parallel", "arbitrary")),
        name="wkv",
    )(rp, w, k, v.reshape(B, T, H, n), a, b, s0t, gs, hsel)
    return y.reshape(B, T, D), jnp.transpose(sT.reshape(B, n, H, n), (0, 2, 1, 3))


FFN_HALO = SUBLANE_BF16


FFN_CW = 2 * LANE


def _ffn_body(x_ref, xh_ref, g_ref, wug_ref, wuv_ref, cg_ref, cv_ref, pg_ref, pv_ref, wd_ref, gf_ref,
              o_ref, st_ref, y_ref, hn, ug, uv, act, acc, *, tm, nb):
    i = pl.program_id(1)
    H = FFN_HALO
    cw_ = FFN_CW
    seg = tm + H

    def norm(xf, gain):
        return xf * lax.rsqrt(jnp.mean(xf * xf, axis=-1, keepdims=True) + NORM_EPS) * gain

    for s in range(nb):
        hn[s * seg:s * seg + H, :] = jnp.where(i > 0, norm(xh_ref[s], g_ref[...]), 0.0).astype(BF16)
        hn[s * seg + H:(s + 1) * seg, :] = norm(x_ref[s], g_ref[...]).astype(BF16)
    h = hn[...]
    first = jnp.where(i == 0, 1.0, 0.0)

    def conv(u, r0, taps):
        return (taps[3:4, :] + u[r0 - 2:r0 - 2 + tm, :] * taps[0:1, :] + u[r0 - 1:r0 - 1 + tm, :] * taps[1:2, :]
                + u[r0:r0 + tm, :] * taps[2:3, :])

    for c in range(D_FF // cw_):
        cols = slice(c * cw_, (c + 1) * cw_)
        for half, (u_scr, w_ref, p_ref) in enumerate(((ug, wug_ref, pg_ref), (uv, wuv_ref, pv_ref))):
            u = jnp.dot(h, w_ref[:, cols], preferred_element_type=F32)
            u_scr[c % 2] = u.astype(BF16)
            for s in range(nb):
                u_scr[c % 2, s * seg:s * seg + H, :] = (u[s * seg:s * seg + H, :]
                                                        + p_ref[s, :, cols] * first).astype(BF16)
                st_ref[s, 0, :, c * cw_ + half * D_FF:(c + 1) * cw_ + half * D_FF] = u[(s + 1) * seg - 8:(s + 1) * seg, :]
        for s in range(nb):
            gate = conv(ug.at[c % 2], s * seg + H, cg_ref[:, cols].astype(BF16))
            val = conv(uv.at[c % 2], s * seg + H, cv_ref[:, cols].astype(BF16))
            act[s * tm:(s + 1) * tm, :] = gate * jax.nn.sigmoid(gate) * val
        d = jnp.dot(act[...], wd_ref[cols, :], preferred_element_type=F32)
        if c == 0:
            acc[...] = d
        else:
            acc[...] += d
    for s in range(nb):
        out = x_ref[s] + acc[s * tm:(s + 1) * tm, :]
        o_ref[s] = out
        if gf_ref is not None:
            y_ref[s] = norm(out, gf_ref[...])


def _ffn_kernel(*refs, tm, nb, final):
    if final:
        (x_ref, xh_ref, g_ref, wug_ref, wuv_ref, cg_ref, cv_ref, pg_ref, pv_ref, wd_ref, gf_ref,
         o_ref, st_ref, y_ref, hn, ug, uv, act, acc) = refs
    else:
        (x_ref, xh_ref, g_ref, wug_ref, wuv_ref, cg_ref, cv_ref, pg_ref, pv_ref, wd_ref,
         o_ref, st_ref, hn, ug, uv, act, acc) = refs
        gf_ref = y_ref = None
    _ffn_body(x_ref, xh_ref, g_ref, wug_ref, wuv_ref, cg_ref, cv_ref, pg_ref, pv_ref, wd_ref, gf_ref,
              o_ref, st_ref, y_ref, hn, ug, uv, act, acc, tm=tm, nb=nb)


FFN_ROWS = 512


def _ffn(x, g, wug, wuv, cg, cv, prev, wd, final_g=None):
    B, T, D = x.shape
    F = D_FF
    H = FFN_HALO
    tm = min(FFN_ROWS, T)
    nb = _row_tile(B, max(1, FFN_ROWS // tm))
    assert T % tm == 0 and tm % H == 0 and F % FFN_CW == 0 and tm >= 8
    nh = tm // H
    nt = T // tm
    prev_h = jnp.pad(prev, ((0, 0), (H - prev.shape[1], 0), (0, 0)))
    pg, pv = prev_h[:, :, :F], prev_h[:, :, F:]
    const = lambda a: pl.BlockSpec(a.shape, lambda b, i: (0,) * a.ndim, pipeline_mode=pl.Buffered(1))
    row = pl.BlockSpec((nb, tm, D), lambda b, i: (b, i, 0))
    g2 = g.reshape(1, D)
    ins = [x, x, g2, wug, wuv, cg, cv, pg, pv, wd]
    specs = [row, pl.BlockSpec((nb, H, D), lambda b, i: (b, jnp.maximum(i * nh - 1, 0), 0)),
             const(g2), const(wug), const(wuv), const(cg), const(cv),
             pl.BlockSpec((nb, H, F), lambda b, i: (b, 0, 0)), pl.BlockSpec((nb, H, F), lambda b, i: (b, 0, 0)),
             const(wd)]
    outs = [row, pl.BlockSpec((nb, 1, 8, 2 * F), lambda b, i: (b, i, 0, 0))]
    shapes = [jax.ShapeDtypeStruct((B, T, D), F32), jax.ShapeDtypeStruct((B, nt, 8, 2 * F), F32)]
    if final_g is not None:
        gf = final_g.reshape(1, D)
        ins.append(gf)
        specs.append(const(gf))
        outs.append(row)
        shapes.append(jax.ShapeDtypeStruct((B, T, D), F32))
    res = pl.pallas_call(
        functools.partial(_ffn_kernel, tm=tm, nb=nb, final=final_g is not None),
        grid=(B // nb, nt),
        in_specs=specs,
        out_specs=outs,
        out_shape=shapes,
        scratch_shapes=[pltpu.VMEM((nb * (tm + H), D), BF16), pltpu.VMEM((2, nb * (tm + H), FFN_CW), BF16),
                        pltpu.VMEM((2, nb * (tm + H), FFN_CW), BF16), pltpu.VMEM((nb * tm, FFN_CW), BF16),
                        pltpu.VMEM((nb * tm, D), F32)],
        compiler_params=_cparams(("parallel", "parallel")),
        name="conv_ffn",
    )(*ins)
    state = res[1][:, nt - 1, 8 - (CONV_W - 1):, :]
    return (res[0], state) + tuple(res[2:])


def _pad_cols(w, n):
    return jnp.pad(w, ((0, 0), (0, n - w.shape[1])))


def _pad_keys(a, L):
    return jnp.pad(a, ((0, 0), (0, L - a.shape[1])) + ((0, 0),) * (a.ndim - 2))


def _dsa_layer(x, pos, past_k, past_v, past_ki, g, w_in, w_out):
    B, T, D = x.shape
    tq = min(256, T)
    qh, kf, kt, vf, vt, qi, tail, kib = _dsa_proj(x, g, w_in, _rope_tables(pos, A_HEAD_DIM, A_ROT), tq)
    P = past_k.shape[1]
    L = P + T
    Lp = _round_up(L, ATTN_KB)
    if Lp != T:
        past = lambda a: jnp.transpose(a, (0, 2, 1, 3)).astype(BF16)
        kt = jnp.pad(jnp.concatenate([past(past_k), kt], axis=2), ((0, 0), (0, 0), (0, Lp - L), (0, 0)))
        vt = jnp.pad(jnp.concatenate([past(past_v), vt], axis=2), ((0, 0), (0, 0), (0, Lp - L), (0, 0)))
        kib = _pad_keys(jnp.concatenate([past_ki.astype(BF16), kib], axis=1), Lp)
    x = _attn2(qh, kt, vt, x, w_out.astype(BF16), pos0=P, tq=tq, group=A_HEADS // A_KV_HEADS,
               idx=(qi, tail, kib), top=min(TOPK_MAX, L // 4), wi_off=IDX_DIM)
    kv_rows = lambda a: a.reshape(B, T, A_KV_HEADS, A_HEAD_DIM)
    return x, kv_rows(kf), kv_rows(vf), tail[..., :IDX_DIM]


def _rwkv_layer(x, shift_prev, S0, g, mu, w_rkv, w0, w1, w2, a0, a1, a2, g1, g2, k_k, k_a, r_k, ln_w, ln_b,
                w_out):
    B, T, D = x.shape
    tm = min(256, T)
    bf = lambda a: a.astype(BF16)
    vecs = jnp.stack([w0, a0, k_k, k_a, r_k.reshape(D)], axis=0)
    ws = [bf(w_rkv[0]), bf(w_rkv[1]), bf(w_rkv[2]), bf(w1), bf(w2), bf(a1), bf(a2), bf(g1), bf(g2)]
    rp, decay, k, v, a_vec, b_vec, yc, bonus, gate = _rwkv_pre(x, shift_prev, g, mu, vecs, ws, tm)
    y, S = _wkv(rp, decay, k, v, a_vec, b_vec, S0)
    ln = jnp.pad(jnp.stack([ln_w, ln_b], axis=0), ((0, 6), (0, 0)))
    x_new = _rwkv_post(y, yc, bonus, gate, x, ln, bf(w_out), tm)
    assert T >= 8
    shift = _norm(x[:, T - 8:].reshape(B * 8, D), g).reshape(B, 8, D)[:, -1]
    return x_new, shift, S


def _mla_layer(x, pos, past_lat, past_rope, g, w_in, g_q, g_kv, w_uq, w_ukv, w_out):
    B, T, D = x.shape
    tq = min(256, T)
    weights = _mla_weights(w_in, w_uq, w_ukv)
    tabs = _rope_tables(pos, LANE, C_ROPE, offset=C_NOPE)
    qh, kt, vt, lat, kpe = _mla_proj(x, g, g_q, g_kv, weights, tabs, tq)
    P = past_lat.shape[1]
    L = P + T
    Lp = _round_up(L, ATTN_KB)
    if Lp != T:
        past_slab = jnp.pad(past_rope, ((0, 0), (0, 0), (C_NOPE, LANE - C_NOPE - C_ROPE)))
        kt_p, vt_p = _mla_kv(past_lat, past_slab, weights[2], weights[3], _row_tile(P, 256))
        kt = jnp.pad(jnp.concatenate([kt_p, kt], axis=2), ((0, 0), (0, 0), (0, Lp - L), (0, 0)))
        vt = jnp.pad(jnp.concatenate([vt_p, vt], axis=2), ((0, 0), (0, 0), (0, Lp - L), (0, 0)))
    x = _attn2(qh, kt, vt, x, w_out.astype(BF16), pos0=P, tq=tq, group=1)
    return x, lat, kpe


def _ffn_layer(x, prev, g, w_up, w_conv, b_conv, w_down, final_g=None):
    F = D_FF
    taps = jnp.concatenate([w_conv, b_conv[None, :], jnp.zeros((8 - CONV_W - 1, 2 * F), F32)], axis=0)
    w_up = w_up.astype(BF16)
    return _ffn(x, g, w_up[:, :F], w_up[:, F:], taps[:, :F], taps[:, F:], prev, w_down.astype(BF16), final_g)


def _trunk(x, pos0, st, w):
    B, T, D = x.shape
    pos = pos0 + jnp.arange(T, dtype=jnp.int32)
    new = {name: [] for name in ('a_k', 'a_v', 'a_idx', 'b_wkv', 'b_shift', 'c_lat', 'c_rope', 'ffn')}
    for i in range(DEPTH):
        j = i // N_MIXERS
        kind = i % N_MIXERS
        if kind == 0:
            x, k, v, ki = _dsa_layer(x, pos, st['a_k'][j], st['a_v'][j], st['a_idx'][j], w['n_mix'][i],
                                     w['a_w_in'][j], w['a_w_out'][j])
            new['a_k'].append(k)
            new['a_v'].append(v)
            new['a_idx'].append(ki)
        elif kind == 1:
            x, shift, S = _rwkv_layer(x, st['b_shift'][j], st['b_wkv'][j], w['n_mix'][i], w['b_mu'][j],
                                      w['b_w_rkv'][j], w['b_w0'][j], w['b_w1'][j], w['b_w2'][j], w['b_a0'][j],
                                      w['b_a1'][j], w['b_a2'][j], w['b_g1'][j], w['b_g2'][j], w['b_k_k'][j],
                                      w['b_k_a'][j], w['b_r_k'][j], w['b_ln_w'][j], w['b_ln_b'][j],
                                      w['b_w_out'][j])
            new['b_shift'].append(shift)
            new['b_wkv'].append(S)
        else:
            x, lat, kpe = _mla_layer(x, pos, st['c_lat'][j], st['c_rope'][j], w['n_mix'][i], w['c_w_in'][j],
                                     w['c_g_q'][j], w['c_g_kv'][j], w['c_w_uq'][j], w['c_w_ukv'][j],
                                     w['c_w_out'][j])
            new['c_lat'].append(lat)
            new['c_rope'].append(kpe)
        last = i == DEPTH - 1
        res = _ffn_layer(x, st['ffn'][i], w['n_ffn'][i], w['f_w_up'][i], w['f_w_conv'][i],
                         w['f_b_conv'][i], w['f_w_down'][i], w['n_final'] if last else None)
        x = res[0]
        new['ffn'].append(res[1])
    return res[2], {name: jnp.stack(rows, axis=0) for name, rows in new.items()}


def kernel(x_prompt, x_sample, cache_a_k, cache_a_v, cache_a_idx, state_b_wkv, state_b_shift,
           cache_c_latent, cache_c_rope, state_ffn_conv, n_mix, n_ffn, n_final, a_w_in, a_w_out,
           b_mu, b_w_rkv, b_w0, b_w1, b_w2, b_a0, b_a1, b_a2, b_g1, b_g2, b_k_k, b_k_a, b_r_k,
           b_ln_w, b_ln_b, b_w_out, c_w_in, c_g_q, c_g_kv, c_w_uq, c_w_ukv, c_w_out,
           f_w_up, f_w_conv, f_b_conv, f_w_down):
    w = dict(n_mix=n_mix, n_ffn=n_ffn, n_final=n_final, a_w_in=a_w_in, a_w_out=a_w_out,
             b_mu=b_mu, b_w_rkv=b_w_rkv, b_w0=b_w0, b_w1=b_w1, b_w2=b_w2, b_a0=b_a0, b_a1=b_a1,
             b_a2=b_a2, b_g1=b_g1, b_g2=b_g2, b_k_k=b_k_k, b_k_a=b_k_a, b_r_k=b_r_k,
             b_ln_w=b_ln_w, b_ln_b=b_ln_b, b_w_out=b_w_out, c_w_in=c_w_in, c_g_q=c_g_q,
             c_g_kv=c_g_kv, c_w_uq=c_w_uq, c_w_ukv=c_w_ukv, c_w_out=c_w_out,
             f_w_up=f_w_up, f_w_conv=f_w_conv, f_b_conv=f_b_conv, f_w_down=f_w_down)
    Bp, Tp, D = x_prompt.shape
    n_a, n_b, n_c = cache_a_k.shape[0], state_b_wkv.shape[0], cache_c_latent.shape[0]
    st_prompt = dict(
        a_k=jnp.zeros((n_a, Bp, 0, A_KV_HEADS, A_HEAD_DIM), F32),
        a_v=jnp.zeros((n_a, Bp, 0, A_KV_HEADS, A_HEAD_DIM), F32),
        a_idx=jnp.zeros((n_a, Bp, 0, IDX_DIM), F32),
        b_wkv=jnp.zeros((n_b, Bp, B_HEADS, B_HEAD_DIM, B_HEAD_DIM), F32),
        b_shift=jnp.zeros((n_b, Bp, D), F32),
        c_lat=jnp.zeros((n_c, Bp, 0, C_KV_RANK), F32),
        c_rope=jnp.zeros((n_c, Bp, 0, C_ROPE), F32),
        ffn=jnp.zeros((DEPTH, Bp, CONV_W - 1, 2 * D_FF), F32))
    st_sample = dict(a_k=cache_a_k, a_v=cache_a_v, a_idx=cache_a_idx, b_wkv=state_b_wkv,
                     b_shift=state_b_shift, c_lat=cache_c_latent, c_rope=cache_c_rope,
                     ffn=state_ffn_conv)
    y_prompt, sp = _trunk(x_prompt, 0, st_prompt, w)
    y_sample, ss = _trunk(x_sample, cache_a_k.shape[2], st_sample, w)
    return (y_prompt, y_sample,
            sp['a_k'], ss['a_k'], sp['a_v'], ss['a_v'], sp['a_idx'], ss['a_idx'],
            sp['b_wkv'], ss['b_wkv'], sp['b_shift'], ss['b_shift'],
            sp['c_lat'], ss['c_lat'], sp['c_rope'], ss['c_rope'],
            sp['ffn'], ss['ffn'])
```

```python
import functools

import jax
import jax.numpy as jnp
from jax import lax
from jax.experimental import pallas as pl
from jax.experimental.pallas import tpu as pltpu

F32 = jnp.float32
BF16 = jnp.bfloat16

D_MODEL = 1024
DEPTH = 4
CHUNK = 64
N_MIXERS = 3
NORM_EPS = 1e-6
ROPE_THETA = 500000.0
A_HEADS, A_HEAD_DIM, A_KV_HEADS = 16, 64, 2
A_ROT = A_HEAD_DIM // 4
IDX_HEADS, IDX_DIM = 8, 64
IDX_ROT = IDX_DIM // 4
TOPK_MAX = 256
A_O_Q = A_HEADS * A_HEAD_DIM
A_O_K = A_O_Q + A_KV_HEADS * A_HEAD_DIM
A_O_V = A_O_K + A_KV_HEADS * A_HEAD_DIM
A_O_QI = A_O_V + IDX_HEADS * IDX_DIM
A_O_KI = A_O_QI + IDX_DIM
A_IN = A_O_KI + IDX_HEADS
B_HEAD_DIM = 64
B_HEADS = D_MODEL // B_HEAD_DIM
B_GN_EPS = 64e-5
C_HEADS, C_NOPE, C_ROPE, C_V = 16, 64, 32, 64
C_Q_RANK, C_KV_RANK = 512, 256
D_FF = 2816
CONV_W = 3

LANE = 128
SUBLANE_BF16 = 16
VMEM_LIMIT = 56 * 1024 * 1024
NEG_INF = float("-inf")
LOG2E = 1.4426950408889634


def _round_up(n, m):
    return (n + m - 1) // m * m


def _row_tile(M, pref):
    t = min(pref, M)
    while M % t:
        t //= 2
    return t


def _cparams(sem):
    return pltpu.CompilerParams(dimension_semantics=sem, vmem_limit_bytes=VMEM_LIMIT)


def _norm_kernel(x_ref, g_ref, o_ref):
    xf = x_ref[...]
    o_ref[...] = xf * lax.rsqrt(jnp.mean(xf * xf, axis=-1, keepdims=True) + NORM_EPS) * g_ref[...]


def _norm(x, g, tm=512):
    M, K = x.shape
    tm = _row_tile(M, tm)
    return pl.pallas_call(
        _norm_kernel,
        grid=(M // tm,),
        in_specs=[pl.BlockSpec((tm, K), lambda i: (i, 0)), pl.BlockSpec((1, K), lambda i: (0, 0))],
        out_specs=pl.BlockSpec((tm, K), lambda i: (i, 0)),
        out_shape=jax.ShapeDtypeStruct((M, K), F32),
        compiler_params=_cparams(("parallel",)),
        name="rmsnorm",
    )(x, g.reshape(1, K))


ATTN_KB = 512
ATTN_KB_SHIFT = ATTN_KB.bit_length() - 1
SELECT_ALL = 1e9
MANTISSA_BITS = 23
ATTN_ROWS_PER_ITER = 1024
TOPK_PART_ROWS = 64


def _for_key_blocks(nk, body, init):
    carry = lax.fori_loop(0, nk // 2, lambda i, c: body(2 * i + 1, body(2 * i, c)), init)
    return lax.cond(nk % 2 == 1, lambda c: body(nk - 1, c), lambda c: c, carry)


def _fold(x, op=jnp.add):
    acc = x[:, :LANE]
    for j in range(1, x.shape[1] // LANE):
        acc = op(acc, x[:, j * LANE:(j + 1) * LANE])
    return acc


def _topk_bias_t(tab_ref, qi_ref, wi_ref, ki_ref, SC, MS, *, nk, first, tq, top, wi_off):
    kb = ATTN_KB
    qi = qi_ref[0]
    w_t = jnp.transpose(wi_ref[0])[wi_off:wi_off + IDX_HEADS, :]
    part = TOPK_PART_ROWS
    zeros_p = jnp.zeros((part, tq), F32)
    zero = jnp.zeros((1, tq), F32)
    kf = float(top)
    pos_q = lax.broadcasted_iota(jnp.int32, (1, tq), 1) + first
    limit = (pos_q & ~(CHUNK - 1)) + CHUNK
    key_i = lax.broadcasted_iota(jnp.int32, (kb, tq), 0)

    def key_sum(body):
        def step(j, a):
            return a + jnp.sum(body(j).reshape(kb // part, part, tq), axis=0)

        return jnp.sum(_for_key_blocks(nk, step, zeros_p), axis=0, keepdims=True)

    def ones_where(c):
        return jnp.where(c, 1.0, 0.0)

    def score_block(j):
        kij = ki_ref[0, pl.ds(pl.multiple_of(j * kb, kb), kb), :]
        sc = jnp.zeros((kb, tq), F32)
        for h in range(IDX_HEADS):
            d = lax.dot_general(kij, qi[:, h * IDX_DIM:(h + 1) * IDX_DIM], (((1,), (1,)), ((), ())),
                                preferred_element_type=F32)
            sc = sc + w_t[h:h + 1, :] * jnp.maximum(d, 0.0)
        sc = jnp.where(key_i + j * kb < limit, sc, NEG_INF)
        SC[j] = sc
        return ones_where(sc >= 0.0)

    c0 = key_sum(score_block)
    neg = c0 < kf
    sgn = jnp.where(neg, -1.0, 1.0)
    kp = jnp.where(neg, kf, (nk * kb).astype(F32) - kf + 1.0)

    def flip(j, c):
        SC[j] = SC[j] * sgn
        return c

    _for_key_blocks(nk, flip, 0)

    def count_lt(cand):
        return key_sum(lambda j: ones_where(SC[j] < cand))

    def exp_step(i, carry):
        e_cur, t_cur = carry
        cand = jnp.where(e_cur == 0.0, tab_ref[0, i], t_cur * tab_ref[1, i])
        ok = count_lt(cand) < kp
        return jnp.where(ok, e_cur + tab_ref[2, i], e_cur), jnp.where(ok, cand, t_cur)

    _, t_pow = lax.fori_loop(0, 8, exp_step, (zero, zero))

    def man_step(i, carry):
        t_cur, frac = carry
        cand = t_cur + frac
        ok = count_lt(cand) < kp
        return jnp.where(ok, cand, t_cur), frac * 0.5

    t_cur, _ = lax.fori_loop(0, MANTISSA_BITS, man_step, (t_pow, t_pow * 0.5))
    thr = t_cur * sgn

    def score(j):
        return SC[j] * sgn

    def key_idx(j):
        return (key_i + j * kb).astype(F32)

    need = kf - key_sum(lambda j: ones_where(score(j) > thr))
    n_eq = key_sum(lambda j: ones_where(score(j) == thr))

    def index_cut():
        nbits = (SC.shape[0] * kb - 1).bit_length()

        def bit_step(i, carry):
            c_cur, bit = carry
            cand = c_cur + bit
            ok = key_sum(lambda j: ones_where((score(j) == thr) & (key_idx(j) < cand))) < need
            return jnp.where(ok, cand, c_cur), bit * 0.5

        c_cur, _ = lax.fori_loop(0, nbits, bit_step, (zero, jnp.full((1, tq), 2.0 ** (nbits - 1), F32)))
        return c_cur

    any_split = jnp.max(ones_where(n_eq > need)) > 0.0
    c_cut = lax.cond(any_split, index_cut, lambda: jnp.full((1, tq), SELECT_ALL, F32))

    def write_bias(j, c):
        s = score(j)
        sel = (s > thr) | ((s == thr) & (key_idx(j) <= c_cut))
        MS[j] = jnp.transpose(jnp.where(sel & (key_i + j * kb < limit), 0.0, NEG_INF))
        return c

    _for_key_blocks(nk, write_bias, 0)


def _attn2_kernel(*refs, n_kv, group, tq, pos0, top, indexer, wi_off):
    if indexer:
        (tab_ref, q_ref, k_ref, v_ref, x_ref, wo_ref, qi_ref, wi_ref, ki_ref, xo_ref,
         MS, LG, MACC, LACC, OACC, OH, OALL, SC) = refs
    else:
        q_ref, k_ref, v_ref, x_ref, wo_ref, xo_ref, MS, LG, MACC, LACC, OACC, OH, OALL = refs
    kb = ATTN_KB
    first = pos0 + pl.program_id(1) * tq
    last_limit = ((first + tq - 1) & ~(CHUNK - 1)) + CHUNK
    nk = (last_limit + (kb - 1)) >> ATTN_KB_SHIFT
    row = lax.broadcasted_iota(jnp.int32, (tq, 1), 0) + first
    limit = (row & ~(CHUNK - 1)) + CHUNK
    lane_i = lax.broadcasted_iota(jnp.int32, (tq, kb), 1)

    if indexer:
        _topk_bias_t(tab_ref, qi_ref, wi_ref, ki_ref, SC, MS, nk=nk, first=first, tq=tq, top=top, wi_off=wi_off)
    else:
        def causal_bias(j, c):
            MS[j] = jnp.where(lane_i + j * kb < limit, 0.0, NEG_INF)
            return c

        _for_key_blocks(nk, causal_bias, 0)

    hu = LG.shape[0]

    def per_kv_heads(gi, c):
        heads = [gi * hu + u for u in range(hu)]
        qs = [q_ref[0, g, 0] for g in heads]
        MACC[...] = jnp.full(MACC.shape, NEG_INF, F32)
        LACC[...] = jnp.zeros(LACC.shape, F32)
        OACC[...] = jnp.zeros(OACC.shape, F32)

        def logits_block(j, c_):
            keys = pl.ds(pl.multiple_of(j * kb, kb), kb)
            bias = MS[j][None]
            for u, g in enumerate(heads):
                lg = lax.dot_general(qs[u], k_ref[0, g, keys, :], (((1,), (1,)), ((), ())),
                                     preferred_element_type=F32)
                lg = (lg.reshape(group, tq, kb) + bias).reshape(group * tq, kb)
                LG[u, j] = lg
                MACC[u] = jnp.maximum(MACC[u], _fold(lg, jnp.maximum))
            return c_

        _for_key_blocks(nk, logits_block, 0)
        ms = [jnp.max(MACC[u], axis=-1, keepdims=True) for u in range(hu)]

        def value_block(j, c_):
            keys = pl.ds(pl.multiple_of(j * kb, kb), kb)
            for u, g in enumerate(heads):
                p = jnp.exp2(LG[u, j] - ms[u])
                LACC[u] += _fold(p)
                OACC[u] += jnp.dot(p.astype(BF16), v_ref[0, g, keys, :], preferred_element_type=F32)
            return c_

        _for_key_blocks(nk, value_block, 0)
        for u, g in enumerate(heads):
            OH[g] = (OACC[u] / jnp.sum(LACC[u], axis=-1, keepdims=True)).astype(OH.dtype)
        return c

    lax.fori_loop(0, n_kv // hu, per_kv_heads, 0)
    dv = OH.shape[2]
    for h in range(n_kv * group):
        OALL[:, h * dv:(h + 1) * dv] = OH[h // group, (h % group) * tq:(h % group + 1) * tq, :]
    xo_ref[0] = x_ref[0] + jnp.dot(OALL[...], wo_ref[...], preferred_element_type=F32)


def _attn2(qg, kt, vt, x, w_out, *, pos0, tq, group, idx=None, top=0, wi_off=0):
    B, n_kv, nq, rows, dq = qg.shape
    L, dv = kt.shape[2], vt.shape[3]
    T, D = x.shape[1], x.shape[2]
    kb = ATTN_KB
    assert nq * tq == T and rows == group * tq and L % kb == 0 and (idx is None or top <= kb)
    ins = [qg, kt, vt, x, w_out]
    specs = [pl.BlockSpec((1, n_kv, 1, rows, dq), lambda b, i: (b, 0, i, 0, 0)),
             pl.BlockSpec((1, n_kv, L, dq), lambda b, i: (b, 0, 0, 0)),
             pl.BlockSpec((1, n_kv, L, dv), lambda b, i: (b, 0, 0, 0)),
             pl.BlockSpec((1, tq, D), lambda b, i: (b, i, 0)),
             pl.BlockSpec(w_out.shape, lambda b, i: (0, 0))]
    hu = max(1, min(n_kv, ATTN_ROWS_PER_ITER // rows))
    assert n_kv % hu == 0
    scratch = [pltpu.VMEM((L // kb, tq, kb), F32), pltpu.VMEM((hu, L // kb, rows, kb), F32),
               pltpu.VMEM((hu, rows, LANE), F32), pltpu.VMEM((hu, rows, LANE), F32),
               pltpu.VMEM((hu, rows, dv), F32), pltpu.VMEM((n_kv, rows, dv), BF16),
               pltpu.VMEM((tq, n_kv * group * dv), BF16)]
    if idx is not None:
        qi, wi, ki = idx
        steps = [2 ** b for b in range(7, -1, -1)]
        tab = jnp.array([[2.0 ** (s - 127) for s in steps], [2.0 ** s if s < 128 else 1.0 for s in steps],
                         [float(s) for s in steps]], F32)
        ins = [tab] + ins + [qi, wi, ki]
        specs = ([pl.BlockSpec(memory_space=pltpu.SMEM)] + specs
                 + [pl.BlockSpec((1, tq, qi.shape[2]), lambda b, i: (b, i, 0)),
                    pl.BlockSpec((1, tq, wi.shape[2]), lambda b, i: (b, i, 0)),
                    pl.BlockSpec((1, L, ki.shape[2]), lambda b, i: (b, 0, 0))])
        scratch.append(pltpu.VMEM((L // kb, kb, tq), F32))
    return pl.pallas_call(
        functools.partial(_attn2_kernel, n_kv=n_kv, group=group, tq=tq, pos0=pos0, top=top,
                          indexer=idx is not None, wi_off=wi_off),
        grid=(B, nq),
        in_specs=specs,
        out_specs=pl.BlockSpec((1, tq, D), lambda b, i: (b, i, 0)),
        out_shape=jax.ShapeDtypeStruct((B, T, D), F32),
        scratch_shapes=scratch,
        compiler_params=_cparams(("parallel", "parallel")),
        name="dsa_attn" if idx is not None else "mla_attn",
    )(*ins)


def _rope_tables(pos, dh, rot, offset=0):
    half = rot // 2
    inv = ROPE_THETA ** (-jnp.arange(half, dtype=F32) / half)
    ang = pos.astype(F32)[:, None] * inv[None, :]
    cos, sin = jnp.cos(ang), jnp.sin(ang)
    T = pos.shape[0]
    pad = lambda n, v: jnp.full((T, n), v, F32)
    zh = pad(half, 0.0)
    lo, hi = offset, dh - offset - rot
    c = jnp.concatenate([pad(lo, 1.0), cos, cos, pad(hi, 1.0)], axis=1)
    s1 = jnp.concatenate([pad(lo, 0.0), -sin, zh, pad(hi, 0.0)], axis=1)
    s2 = jnp.concatenate([pad(lo, 0.0), zh, sin, pad(hi, 0.0)], axis=1)
    return tuple(jnp.tile(t, (1, LANE // dh)) for t in (c, s1, s2))


def _rope_lanes(x, c, s1, s2, half):
    return x * c + pltpu.roll(x, LANE - half, 1) * s1 + pltpu.roll(x, half, 1) * s2


def _dsa_proj_kernel(x_ref, g_ref, w_ref, c_ref, s1_ref, s2_ref, ts_ref, qh_ref, kf_ref, kt_ref, vf_ref, vt_ref,
                     qi_ref, tail_ref, kib_ref, *, tm):
    xf = x_ref[0]
    h = xf * lax.rsqrt(jnp.mean(xf * xf, axis=-1, keepdims=True) + NORM_EPS) * g_ref[...]
    acc = jnp.dot(h.astype(BF16), w_ref[...], preferred_element_type=F32)
    c, s1, s2 = c_ref[...], s1_ref[...], s2_ref[...]
    half = A_ROT // 2
    hd = A_HEAD_DIM
    group = A_HEADS // A_KV_HEADS
    rope = lambda xs: _rope_lanes(xs, c, s1, s2, half)
    slab = lambda off: acc[:, off:off + LANE]
    for s in range(A_O_Q // LANE):
        qs = (rope(slab(s * LANE)) * (hd ** -0.5 * LOG2E)).astype(BF16)
        for e in range(LANE // hd):
            head = s * (LANE // hd) + e
            u = head % group
            qh_ref[0, head // group, 0, u * tm:(u + 1) * tm, :] = qs[:, e * hd:(e + 1) * hd]
    ks = rope(slab(A_O_Q))
    vs = slab(A_O_K)
    kf_ref[0] = ks
    vf_ref[0] = vs
    for e in range(A_KV_HEADS):
        kt_ref[0, e] = ks[:, e * hd:(e + 1) * hd].astype(BF16)
        vt_ref[0, e] = vs[:, e * hd:(e + 1) * hd].astype(BF16)
    for s in range(IDX_HEADS * IDX_DIM // LANE):
        qi_ref[0, :, s * LANE:(s + 1) * LANE] = rope(slab(A_O_V + s * LANE)).astype(BF16)
    is_key = lax.broadcasted_iota(jnp.int32, (tm, LANE), 1) < IDX_DIM
    tl = _rope_lanes(slab(A_O_QI), jnp.where(is_key, c, 1.0), jnp.where(is_key, s1, 0.0),
                     jnp.where(is_key, s2, 0.0), half) * ts_ref[...]
    tail_ref[0] = tl
    kib_ref[0] = tl[:, :IDX_DIM].astype(BF16)


def _dsa_proj(x, g, w_in, tabs, tm):
    B, T, D = x.shape
    assert (A_HEAD_DIM, A_ROT) == (IDX_DIM, IDX_ROT) and A_KV_HEADS * A_HEAD_DIM == LANE
    assert A_O_QI % LANE == 0 and IDX_DIM + IDX_HEADS <= LANE and T % tm == 0
    n_in = _round_up(A_IN, LANE)
    group = A_HEADS // A_KV_HEADS
    hd = A_HEAD_DIM
    nq = T // tm
    lanes = jnp.arange(LANE)
    tail_scale = jnp.where(lanes < IDX_DIM, 1.0, jnp.where(lanes < IDX_DIM + IDX_HEADS,
                                                            (IDX_HEADS * IDX_DIM) ** -0.5, 0.0)).astype(F32)
    row = lambda n: pl.BlockSpec((1, tm, n), lambda b, i: (b, i, 0))
    tab = pl.BlockSpec((tm, LANE), lambda b, i: (i, 0))
    const = lambda a: pl.BlockSpec(a.shape, lambda b, i: (0,) * a.ndim)
    kvh = pl.BlockSpec((1, A_KV_HEADS, tm, hd), lambda b, i: (b, 0, i, 0))
    w = _pad_cols(w_in, n_in).astype(BF16)
    g2 = g.reshape(1, D)
    ts = tail_scale.reshape(1, LANE)
    return pl.pallas_call(
        functools.partial(_dsa_proj_kernel, tm=tm),
        grid=(B, nq),
        in_specs=[row(D), const(g2), const(w), tab, tab, tab, const(ts)],
        out_specs=[pl.BlockSpec((1, A_KV_HEADS, 1, group * tm, hd), lambda b, i: (b, 0, i, 0, 0)),
                   row(LANE), kvh, row(LANE), kvh, row(IDX_HEADS * IDX_DIM), row(LANE), row(IDX_DIM)],
        out_shape=[jax.ShapeDtypeStruct((B, A_KV_HEADS, nq, group * tm, hd), BF16),
                   jax.ShapeDtypeStruct((B, T, LANE), F32), jax.ShapeDtypeStruct((B, A_KV_HEADS, T, hd), BF16),
                   jax.ShapeDtypeStruct((B, T, LANE), F32), jax.ShapeDtypeStruct((B, A_KV_HEADS, T, hd), BF16),
                   jax.ShapeDtypeStruct((B, T, IDX_HEADS * IDX_DIM), BF16),
                   jax.ShapeDtypeStruct((B, T, LANE), F32), jax.ShapeDtypeStruct((B, T, IDX_DIM), BF16)],
        compiler_params=_cparams(("parallel", "parallel")),
        name="dsa_proj",
    )(x, g2, w, *tabs, ts)


def _rms(xf, gain):
    return xf * lax.rsqrt(jnp.mean(xf * xf, axis=-1, keepdims=True) + NORM_EPS) * gain


def _mla_write_kv(lat, kpe_slab, wuk_ref, wuv_ref, kt_ref, vt_ref):
    lb = lat.astype(BF16)
    kn = jnp.dot(lb, wuk_ref[...], preferred_element_type=F32)
    vv = jnp.dot(lb, wuv_ref[...], preferred_element_type=F32)
    for h in range(C_HEADS):
        kt_ref[0, h] = (kn[:, h * LANE:(h + 1) * LANE] + kpe_slab).astype(BF16)
        vt_ref[0, h] = vv[:, h * C_V:(h + 1) * C_V].astype(BF16)


def _mla_proj_kernel(x_ref, g_ref, win_ref, gq_ref, gkv_ref, wuq_ref, wuk_ref, wuv_ref, c_ref, s1_ref, s2_ref,
                     qh_ref, kt_ref, vt_ref, lat_ref, kpe_ref):
    h = _rms(x_ref[0], g_ref[...]).astype(BF16)
    proj = jnp.dot(h, win_ref[...], preferred_element_type=F32)
    c, s1, s2 = c_ref[...], s1_ref[...], s2_ref[...]
    half = C_ROPE // 2
    q = jnp.dot(_rms(proj[:, :C_Q_RANK], gq_ref[...]).astype(BF16), wuq_ref[...], preferred_element_type=F32)
    scale = (C_NOPE + C_ROPE) ** -0.5 * LOG2E
    for hd in range(C_HEADS):
        qh_ref[0, hd, 0] = (_rope_lanes(q[:, hd * LANE:(hd + 1) * LANE], c, s1, s2, half) * scale).astype(BF16)
    lat = _rms(proj[:, C_Q_RANK:C_Q_RANK + C_KV_RANK], gkv_ref[...])
    lat_ref[0] = lat
    kpe_slab = _rope_lanes(proj[:, C_Q_RANK + C_KV_RANK:], c, s1, s2, half)
    kpe_ref[0] = kpe_slab[:, C_NOPE:C_NOPE + C_ROPE]
    _mla_write_kv(lat, kpe_slab, wuk_ref, wuv_ref, kt_ref, vt_ref)


def _mla_kv_kernel(lat_ref, kpe_ref, wuk_ref, wuv_ref, kt_ref, vt_ref):
    _mla_write_kv(lat_ref[0], kpe_ref[0], wuk_ref, wuv_ref, kt_ref, vt_ref)


def _mla_weights(w_in, w_uq, w_ukv):
    D = w_in.shape[0]
    zc = lambda rows, n: jnp.zeros((rows, n), w_in.dtype)
    w_in2 = jnp.concatenate([w_in[:, :C_Q_RANK + C_KV_RANK], zc(D, C_NOPE), w_in[:, C_Q_RANK + C_KV_RANK:],
                             zc(D, LANE - C_NOPE - C_ROPE)], axis=1)
    pad_heads = lambda w, d: jnp.pad(w.reshape(w.shape[0], C_HEADS, d), ((0, 0), (0, 0), (0, LANE - d))).reshape(
        w.shape[0], C_HEADS * LANE)
    w_uq2 = pad_heads(w_uq, C_NOPE + C_ROPE)
    ukv = w_ukv.reshape(C_KV_RANK, C_HEADS, C_NOPE + C_V)
    w_uk2 = pad_heads(ukv[..., :C_NOPE].reshape(C_KV_RANK, C_HEADS * C_NOPE), C_NOPE)
    w_uv2 = ukv[..., C_NOPE:].reshape(C_KV_RANK, C_HEADS * C_V)
    return tuple(a.astype(BF16) for a in (w_in2, w_uq2, w_uk2, w_uv2))


def _mla_proj(x, g, g_q, g_kv, weights, tabs, tm):
    B, T, D = x.shape
    w_in2, w_uq2, w_uk2, w_uv2 = weights
    nq = T // tm
    row = lambda n: pl.BlockSpec((1, tm, n), lambda b, i: (b, i, 0))
    tab = pl.BlockSpec((tm, LANE), lambda b, i: (i, 0))
    const = lambda a: pl.BlockSpec(a.shape, lambda b, i: (0,) * a.ndim)
    heads = lambda d: pl.BlockSpec((1, C_HEADS, tm, d), lambda b, i: (b, 0, i, 0))
    vec = lambda a: a.reshape(1, a.shape[0])
    return pl.pallas_call(
        _mla_proj_kernel,
        grid=(B, nq),
        in_specs=[row(D), const(vec(g)), const(w_in2), const(vec(g_q)), const(vec(g_kv)), const(w_uq2),
                  const(w_uk2), const(w_uv2), tab, tab, tab],
        out_specs=[pl.BlockSpec((1, C_HEADS, 1, tm, LANE), lambda b, i: (b, 0, i, 0, 0)), heads(LANE), heads(C_V),
                   row(C_KV_RANK), row(C_ROPE)],
        out_shape=[jax.ShapeDtypeStruct((B, C_HEADS, nq, tm, LANE), BF16),
                   jax.ShapeDtypeStruct((B, C_HEADS, T, LANE), BF16), jax.ShapeDtypeStruct((B, C_HEADS, T, C_V), BF16),
                   jax.ShapeDtypeStruct((B, T, C_KV_RANK), F32), jax.ShapeDtypeStruct((B, T, C_ROPE), F32)],
        compiler_params=_cparams(("parallel", "parallel")),
        name="mla_proj",
    )(x, vec(g), w_in2, vec(g_q), vec(g_kv), w_uq2, w_uk2, w_uv2, *tabs)


def _mla_kv(lat, kpe_slab, w_uk2, w_uv2, tm):
    B, P, _ = lat.shape
    row = lambda n: pl.BlockSpec((1, tm, n), lambda b, i: (b, i, 0))
    const = lambda a: pl.BlockSpec(a.shape, lambda b, i: (0,) * a.ndim)
    heads = lambda d: pl.BlockSpec((1, C_HEADS, tm, d), lambda b, i: (b, 0, i, 0))
    return pl.pallas_call(
        _mla_kv_kernel,
        grid=(B, P // tm),
        in_specs=[row(C_KV_RANK), row(LANE), const(w_uk2), const(w_uv2)],
        out_specs=[heads(LANE), heads(C_V)],
        out_shape=[jax.ShapeDtypeStruct((B, C_HEADS, P, LANE), BF16), jax.ShapeDtypeStruct((B, C_HEADS, P, C_V), BF16)],
        compiler_params=_cparams(("parallel", "parallel")),
        name="mla_kv",
    )(lat, kpe_slab, w_uk2, w_uv2)


def _head_sum(x, bo_ref, split=True):
    bw = bo_ref.shape[0]
    bo = bo_ref[...]
    hi = x.astype(BF16)
    terms = [hi, (x - hi.astype(F32)).astype(BF16)] if split else [hi]
    return jnp.concatenate(
        [sum(jnp.dot(t[:, j * bw:(j + 1) * bw], bo, preferred_element_type=F32) for t in terms)
         for j in range(x.shape[1] // bw)], axis=-1)


RWKV_HALO = 8


def _rwkv_pre_kernel(x_ref, xh_ref, sh_ref, g_ref, mu_ref, vec_ref, wr_ref, wk_ref, wv_ref, w1_ref, w2_ref,
                     a1_ref, a2_ref, g1_ref, g2_ref, bo_ref,
                     rp_ref, w_ref, k_ref, v_ref, a_ref, b_ref, yc_ref, bonus_ref, gate_ref, *, tm):
    i = pl.program_id(1)
    gain = g_ref[...]

    def norm(xf):
        return xf * lax.rsqrt(jnp.mean(xf * xf, axis=-1, keepdims=True) + NORM_EPS) * gain

    h = norm(x_ref[0])
    before = jnp.where(i > 0, norm(xh_ref[0])[RWKV_HALO - 1:RWKV_HALO, :], sh_ref[0])
    first = lax.broadcasted_iota(jnp.int32, (tm, 1), 0) == 0
    xx = jnp.where(first, before, pltpu.roll(h, 1, 0)) - h
    mu = mu_ref[...]
    vec = vec_ref[...]
    w0, a0, k_k, k_a, r_k = (vec[j:j + 1, :] for j in range(5))

    def mix(j):
        return (h + xx * mu[j:j + 1, :]).astype(BF16)

    dot = lambda a_, w_: jnp.dot(a_, w_[...], preferred_element_type=F32)
    r = dot(mix(0), wr_ref)
    wl = dot(jnp.tanh(dot(mix(1), w1_ref)).astype(BF16), w2_ref)
    k = dot(mix(2), wk_ref)
    v = dot(mix(3), wv_ref)
    al = dot(dot(mix(4), a1_ref).astype(BF16), a2_ref)
    gate_ref[0] = dot(jax.nn.sigmoid(dot(mix(5), g1_ref)).astype(BF16), g2_ref)
    z = -(w0 + wl)
    softplus = jnp.maximum(z, 0.0) + jnp.log(1.0 + jnp.exp(-jnp.abs(z)))
    decay = jnp.exp(-jnp.exp(-softplus - 0.5))
    a = jax.nn.sigmoid(a0 + al)
    kk = k * k_k
    kk = kk / jnp.maximum(jnp.sqrt(_head_sum(kk * kk, bo_ref)), 1e-12)
    k = k * (1.0 + (a - 1.0) * k_a)
    b = kk * a
    rp_ref[0] = decay * r - kk * _head_sum(b * r, bo_ref, split=False)
    w_ref[0] = decay
    k_ref[0] = k
    v_ref[0] = v
    a_ref[0] = -kk
    b_ref[0] = b
    yc_ref[0] = v * _head_sum(k * r, bo_ref, split=False)
    bonus_ref[0] = _head_sum(r * k * r_k, bo_ref, split=False) * v


def _rwkv_post_kernel(y_ref, yc_ref, bonus_ref, gate_ref, x_ref, ln_ref, wo_ref, bo_ref, o_ref):
    n = float(B_HEAD_DIM)
    y = y_ref[0] + yc_ref[0]
    d = y - _head_sum(y, bo_ref) / n
    var = _head_sum(d * d, bo_ref) / n
    ln = ln_ref[...]
    yn = d * lax.rsqrt(var + B_GN_EPS) * ln[0:1, :] + ln[1:2, :] + bonus_ref[0]
    o_ref[0] = x_ref[0] + jnp.dot((yn * gate_ref[0]).astype(BF16), wo_ref[...], preferred_element_type=F32)


def _block_ones():
    blk = jnp.arange(WKV_BW) // B_HEAD_DIM
    return (blk[:, None] == blk[None, :]).astype(BF16)


def _rwkv_pre(x, shift_prev, g, mu, vecs, ws, tm):
    B, T, D = x.shape
    H = RWKV_HALO
    assert T % tm == 0 and tm % H == 0
    nh = tm // H
    bo = _block_ones()
    row = pl.BlockSpec((1, tm, D), lambda b, i: (b, i, 0))
    const = lambda a: pl.BlockSpec(a.shape, lambda b, i: (0,) * a.ndim)
    pad8 = lambda a: jnp.pad(a, ((0, 8 - a.shape[0]), (0, 0)))
    mu8, vec8, g2 = pad8(mu), pad8(vecs), g.reshape(1, D)
    return pl.pallas_call(
        functools.partial(_rwkv_pre_kernel, tm=tm),
        grid=(B, T // tm),
        in_specs=[row, pl.BlockSpec((1, H, D), lambda b, i: (b, jnp.maximum(i * nh - 1, 0), 0)),
                  pl.BlockSpec((1, 1, D), lambda b, i: (b, 0, 0)), const(g2), const(mu8), const(vec8)]
                 + [const(a) for a in ws] + [const(bo)],
        out_specs=[row] * 9,
        out_shape=[jax.ShapeDtypeStruct((B, T, D), F32)] * 9,
        compiler_params=_cparams(("parallel", "parallel")),
        name="rwkv_pre",
    )(x, x, shift_prev.reshape(B, 1, D), g2, mu8, vec8, *ws, bo)


def _rwkv_post(y, yc, bonus, gate, x, ln, w_out, tm):
    B, T, D = x.shape
    bo = _block_ones()
    row = pl.BlockSpec((1, tm, D), lambda b, i: (b, i, 0))
    const = lambda a: pl.BlockSpec(a.shape, lambda b, i: (0,) * a.ndim)
    return pl.pallas_call(
        _rwkv_post_kernel,
        grid=(B, T // tm),
        in_specs=[row] * 5 + [const(ln), const(w_out), const(bo)],
        out_specs=row,
        out_shape=jax.ShapeDtypeStruct((B, T, D), F32),
        compiler_params=_cparams(("parallel", "parallel")),
        name="rwkv_post",
    )(y, yc, bonus, gate, x, ln, w_out, bo)


WKV_SUB = 16
WKV_NB = 8
WKV_TC = 64
WKV_BW = 2 * LANE


def _wkv_kernel(rp_ref, w_ref, k_ref, vh_ref, a_ref, b_ref, s0_ref, gs_ref, hs_ref, y_ref, sT_ref, S, *, tc, nb):
    c = pl.program_id(1)
    H = B_HEADS

    @pl.when(c == 0)
    def _():
        S[...] = s0_ref[...]

    def block(sc, carry):
        base = pl.multiple_of(sc * WKV_SUB, WKV_SUB)
        rows = pl.ds(base, WKV_SUB)
        gs = gs_ref[...]
        hsel = hs_ref[...]
        ins = [tuple(x[bi, rows, :] for x in (rp_ref, w_ref, k_ref, a_ref, b_ref)) for bi in range(nb)]
        for u in range(WKV_SUB):
            sas = []
            for bi in range(nb):
                r8, w8, k8, a8, b8 = ins[bi]
                s = S[bi]
                pa = (s * a8[u:u + 1, :]).astype(BF16)
                sas.append(jnp.dot(pa, gs, preferred_element_type=F32))
                pr = (s * r8[u:u + 1, :]).astype(BF16)
                y_ref[bi, base + u] = lax.dot_general(hsel, pr, (((1,), (1,)), ((), ())),
                                                      preferred_element_type=F32)
            deltas = []
            for bi in range(nb):
                r8, w8, k8, a8, b8 = ins[bi]
                v_t = jnp.transpose(vh_ref[bi, base + u])
                lhs = jnp.concatenate([sas[bi][:, :H], v_t], axis=1).astype(BF16)
                rhs = jnp.concatenate([hsel * b8[u:u + 1, :].astype(BF16), hsel * k8[u:u + 1, :].astype(BF16)],
                                      axis=0)
                deltas.append(jnp.dot(lhs, rhs, preferred_element_type=F32))
            for bi in range(nb):
                r8, w8, k8, a8, b8 = ins[bi]
                S[bi] = S[bi] * w8[u:u + 1, :] + deltas[bi]
        return carry

    lax.fori_loop(0, tc // WKV_SUB, block, 0)

    @pl.when(c == pl.num_programs(1) - 1)
    def _():
        sT_ref[...] = S[...]


def _wkv(rp, w, k, v, a, b, s0):
    B, T, D = rp.shape
    n = B_HEAD_DIM
    H = B_HEADS
    nb = _row_tile(B, WKV_NB)
    tc = _row_tile(T, WKV_TC)
    assert tc % WKV_SUB == 0 and H <= LANE
    s0t = jnp.transpose(s0, (0, 2, 1, 3)).reshape(B, n, D)
    hsel = (jnp.arange(H)[:, None] == (jnp.arange(D) // n)[None, :]).astype(BF16)
    gs = (jnp.arange(D)[:, None] // n == jnp.arange(LANE)[None, :]).astype(BF16)
    seq = pl.BlockSpec((nb, tc, D), lambda bi, c: (bi, c, 0))
    seqh = pl.BlockSpec((nb, tc, H, n), lambda bi, c: (bi, c, 0, 0))
    st = pl.BlockSpec((nb, n, D), lambda bi, c: (bi, 0, 0))
    const = lambda a_: pl.BlockSpec(a_.shape, lambda bi, c: (0, 0))
    y, sT = pl.pallas_call(
        functools.partial(_wkv_kernel, tc=tc, nb=nb),
        grid=(B // nb, T // tc),
        in_specs=[seq, seq, seq, seqh, seq, seq, st, const(gs), const(hsel)],
        out_specs=[seqh, st],
        out_shape=[jax.ShapeDtypeStruct((B, T, H, n), F32), jax.ShapeDtypeStruct((B, n, D), F32)],
        scratch_shapes=[pltpu.VMEM((nb, n, D), F32)],
        compiler_params=_cparams(("parallel", "arbitrary")),
        name="wkv",
    )(rp, w, k, v.reshape(B, T, H, n), a, b, s0t, gs, hsel)
    return y.reshape(B, T, D), jnp.transpose(sT.reshape(B, n, H, n), (0, 2, 1, 3))


FFN_HALO = SUBLANE_BF16


FFN_CW = 2 * LANE


def _ffn_body(x_ref, xh_ref, g_ref, wug_ref, wuv_ref, cg_ref, cv_ref, pg_ref, pv_ref, wd_ref, gf_ref,
              o_ref, st_ref, y_ref, hn, ug, uv, act, acc, *, tm, nb):
    i = pl.program_id(1)
    H = FFN_HALO
    cw_ = FFN_CW
    seg = tm + H

    def norm(xf, gain):
        return xf * lax.rsqrt(jnp.mean(xf * xf, axis=-1, keepdims=True) + NORM_EPS) * gain

    for s in range(nb):
        hn[s * seg:s * seg + H, :] = jnp.where(i > 0, norm(xh_ref[s], g_ref[...]), 0.0).astype(BF16)
        hn[s * seg + H:(s + 1) * seg, :] = norm(x_ref[s], g_ref[...]).astype(BF16)
    h = hn[...]
    first = jnp.where(i == 0, 1.0, 0.0)

    def conv(u, r0, taps):
        return (taps[3:4, :] + u[r0 - 2:r0 - 2 + tm, :] * taps[0:1, :] + u[r0 - 1:r0 - 1 + tm, :] * taps[1:2, :]
                + u[r0:r0 + tm, :] * taps[2:3, :])

    for c in range(D_FF // cw_):
        cols = slice(c * cw_, (c + 1) * cw_)
        for half, (u_scr, w_ref, p_ref) in enumerate(((ug, wug_ref, pg_ref), (uv, wuv_ref, pv_ref))):
            u = jnp.dot(h, w_ref[:, cols], preferred_element_type=F32)
            u_scr[c % 2] = u.astype(BF16)
            for s in range(nb):
                u_scr[c % 2, s * seg:s * seg + H, :] = (u[s * seg:s * seg + H, :]
                                                        + p_ref[s, :, cols] * first).astype(BF16)
                st_ref[s, 0, :, c * cw_ + half * D_FF:(c + 1) * cw_ + half * D_FF] = u[(s + 1) * seg - 8:(s + 1) * seg, :]
        for s in range(nb):
            gate = conv(ug.at[c % 2], s * seg + H, cg_ref[:, cols].astype(BF16))
            val = conv(uv.at[c % 2], s * seg + H, cv_ref[:, cols].astype(BF16))
            act[s * tm:(s + 1) * tm, :] = gate * jax.nn.sigmoid(gate) * val
        d = jnp.dot(act[...], wd_ref[cols, :], preferred_element_type=F32)
        if c == 0:
            acc[...] = d
        else:
            acc[...] += d
    for s in range(nb):
        out = x_ref[s] + acc[s * tm:(s + 1) * tm, :]
        o_ref[s] = out
        if gf_ref is not None:
            y_ref[s] = norm(out, gf_ref[...])


def _ffn_kernel(*refs, tm, nb, final):
    if final:
        (x_ref, xh_ref, g_ref, wug_ref, wuv_ref, cg_ref, cv_ref, pg_ref, pv_ref, wd_ref, gf_ref,
         o_ref, st_ref, y_ref, hn, ug, uv, act, acc) = refs
    else:
        (x_ref, xh_ref, g_ref, wug_ref, wuv_ref, cg_ref, cv_ref, pg_ref, pv_ref, wd_ref,
         o_ref, st_ref, hn, ug, uv, act, acc) = refs
        gf_ref = y_ref = None
    _ffn_body(x_ref, xh_ref, g_ref, wug_ref, wuv_ref, cg_ref, cv_ref, pg_ref, pv_ref, wd_ref, gf_ref,
              o_ref, st_ref, y_ref, hn, ug, uv, act, acc, tm=tm, nb=nb)


FFN_ROWS = 512


def _ffn(x, g, wug, wuv, cg, cv, prev, wd, final_g=None):
    B, T, D = x.shape
    F = D_FF
    H = FFN_HALO
    tm = min(FFN_ROWS, T)
    nb = _row_tile(B, max(1, FFN_ROWS // tm))
    assert T % tm == 0 and tm % H == 0 and F % FFN_CW == 0 and tm >= 8
    nh = tm // H
    nt = T // tm
    prev_h = jnp.pad(prev, ((0, 0), (H - prev.shape[1], 0), (0, 0)))
    pg, pv = prev_h[:, :, :F], prev_h[:, :, F:]
    const = lambda a: pl.BlockSpec(a.shape, lambda b, i: (0,) * a.ndim, pipeline_mode=pl.Buffered(1))
    row = pl.BlockSpec((nb, tm, D), lambda b, i: (b, i, 0))
    g2 = g.reshape(1, D)
    ins = [x, x, g2, wug, wuv, cg, cv, pg, pv, wd]
    specs = [row, pl.BlockSpec((nb, H, D), lambda b, i: (b, jnp.maximum(i * nh - 1, 0), 0)),
             const(g2), const(wug), const(wuv), const(cg), const(cv),
             pl.BlockSpec((nb, H, F), lambda b, i: (b, 0, 0)), pl.BlockSpec((nb, H, F), lambda b, i: (b, 0, 0)),
             const(wd)]
    outs = [row, pl.BlockSpec((nb, 1, 8, 2 * F), lambda b, i: (b, i, 0, 0))]
    shapes = [jax.ShapeDtypeStruct((B, T, D), F32), jax.ShapeDtypeStruct((B, nt, 8, 2 * F), F32)]
    if final_g is not None:
        gf = final_g.reshape(1, D)
        ins.append(gf)
        specs.append(const(gf))
        outs.append(row)
        shapes.append(jax.ShapeDtypeStruct((B, T, D), F32))
    res = pl.pallas_call(
        functools.partial(_ffn_kernel, tm=tm, nb=nb, final=final_g is not None),
        grid=(B // nb, nt),
        in_specs=specs,
        out_specs=outs,
        out_shape=shapes,
        scratch_shapes=[pltpu.VMEM((nb * (tm + H), D), BF16), pltpu.VMEM((2, nb * (tm + H), FFN_CW), BF16),
                        pltpu.VMEM((2, nb * (tm + H), FFN_CW), BF16), pltpu.VMEM((nb * tm, FFN_CW), BF16),
                        pltpu.VMEM((nb * tm, D), F32)],
        compiler_params=_cparams(("parallel", "parallel")),
        name="conv_ffn",
    )(*ins)
    state = res[1][:, nt - 1, 8 - (CONV_W - 1):, :]
    return (res[0], state) + tuple(res[2:])


def _pad_cols(w, n):
    return jnp.pad(w, ((0, 0), (0, n - w.shape[1])))


def _pad_keys(a, L):
    return jnp.pad(a, ((0, 0), (0, L - a.shape[1])) + ((0, 0),) * (a.ndim - 2))


def _dsa_layer(x, pos, past_k, past_v, past_ki, g, w_in, w_out):
    B, T, D = x.shape
    tq = min(256, T)
    qh, kf, kt, vf, vt, qi, tail, kib = _dsa_proj(x, g, w_in, _rope_tables(pos, A_HEAD_DIM, A_ROT), tq)
    P = past_k.shape[1]
    L = P + T
    Lp = _round_up(L, ATTN_KB)
    if Lp != T:
        past = lambda a: jnp.transpose(a, (0, 2, 1, 3)).astype(BF16)
        kt = jnp.pad(jnp.concatenate([past(past_k), kt], axis=2), ((0, 0), (0, 0), (0, Lp - L), (0, 0)))
        vt = jnp.pad(jnp.concatenate([past(past_v), vt], axis=2), ((0, 0), (0, 0), (0, Lp - L), (0, 0)))
        kib = _pad_keys(jnp.concatenate([past_ki.astype(BF16), kib], axis=1), Lp)
    x = _attn2(qh, kt, vt, x, w_out.astype(BF16), pos0=P, tq=tq, group=A_HEADS // A_KV_HEADS,
               idx=(qi, tail, kib), top=min(TOPK_MAX, L // 4), wi_off=IDX_DIM)
    kv_rows = lambda a: a.reshape(B, T, A_KV_HEADS, A_HEAD_DIM)
    return x, kv_rows(kf), kv_rows(vf), tail[..., :IDX_DIM]


def _rwkv_layer(x, shift_prev, S0, g, mu, w_rkv, w0, w1, w2, a0, a1, a2, g1, g2, k_k, k_a, r_k, ln_w, ln_b,
                w_out):
    B, T, D = x.shape
    tm = min(256, T)
    bf = lambda a: a.astype(BF16)
    vecs = jnp.stack([w0, a0, k_k, k_a, r_k.reshape(D)], axis=0)
    ws = [bf(w_rkv[0]), bf(w_rkv[1]), bf(w_rkv[2]), bf(w1), bf(w2), bf(a1), bf(a2), bf(g1), bf(g2)]
    rp, decay, k, v, a_vec, b_vec, yc, bonus, gate = _rwkv_pre(x, shift_prev, g, mu, vecs, ws, tm)
    y, S = _wkv(rp, decay, k, v, a_vec, b_vec, S0)
    ln = jnp.pad(jnp.stack([ln_w, ln_b], axis=0), ((0, 6), (0, 0)))
    x_new = _rwkv_post(y, yc, bonus, gate, x, ln, bf(w_out), tm)
    assert T >= 8
    shift = _norm(x[:, T - 8:].reshape(B * 8, D), g).reshape(B, 8, D)[:, -1]
    return x_new, shift, S


def _mla_layer(x, pos, past_lat, past_rope, g, w_in, g_q, g_kv, w_uq, w_ukv, w_out):
    B, T, D = x.shape
    tq = min(256, T)
    weights = _mla_weights(w_in, w_uq, w_ukv)
    tabs = _rope_tables(pos, LANE, C_ROPE, offset=C_NOPE)
    qh, kt, vt, lat, kpe = _mla_proj(x, g, g_q, g_kv, weights, tabs, tq)
    P = past_lat.shape[1]
    L = P + T
    Lp = _round_up(L, ATTN_KB)
    if Lp != T:
        past_slab = jnp.pad(past_rope, ((0, 0), (0, 0), (C_NOPE, LANE - C_NOPE - C_ROPE)))
        kt_p, vt_p = _mla_kv(past_lat, past_slab, weights[2], weights[3], _row_tile(P, 256))
        kt = jnp.pad(jnp.concatenate([kt_p, kt], axis=2), ((0, 0), (0, 0), (0, Lp - L), (0, 0)))
        vt = jnp.pad(jnp.concatenate([vt_p, vt], axis=2), ((0, 0), (0, 0), (0, Lp - L), (0, 0)))
    x = _attn2(qh, kt, vt, x, w_out.astype(BF16), pos0=P, tq=tq, group=1)
    return x, lat, kpe


def _ffn_layer(x, prev, g, w_up, w_conv, b_conv, w_down, final_g=None):
    F = D_FF
    taps = jnp.concatenate([w_conv, b_conv[None, :], jnp.zeros((8 - CONV_W - 1, 2 * F), F32)], axis=0)
    w_up = w_up.astype(BF16)
    return _ffn(x, g, w_up[:, :F], w_up[:, F:], taps[:, :F], taps[:, F:], prev, w_down.astype(BF16), final_g)


def _trunk(x, pos0, st, w):
    B, T, D = x.shape
    pos = pos0 + jnp.arange(T, dtype=jnp.int32)
    new = {name: [] for name in ('a_k', 'a_v', 'a_idx', 'b_wkv', 'b_shift', 'c_lat', 'c_rope', 'ffn')}
    for i in range(DEPTH):
        j = i // N_MIXERS
        kind = i % N_MIXERS
        if kind == 0:
            x, k, v, ki = _dsa_layer(x, pos, st['a_k'][j], st['a_v'][j], st['a_idx'][j], w['n_mix'][i],
                                     w['a_w_in'][j], w['a_w_out'][j])
            new['a_k'].append(k)
            new['a_v'].append(v)
            new['a_idx'].append(ki)
        elif kind == 1:
            x, shift, S = _rwkv_layer(x, st['b_shift'][j], st['b_wkv'][j], w['n_mix'][i], w['b_mu'][j],
                                      w['b_w_rkv'][j], w['b_w0'][j], w['b_w1'][j], w['b_w2'][j], w['b_a0'][j],
                                      w['b_a1'][j], w['b_a2'][j], w['b_g1'][j], w['b_g2'][j], w['b_k_k'][j],
                                      w['b_k_a'][j], w['b_r_k'][j], w['b_ln_w'][j], w['b_ln_b'][j],
                                      w['b_w_out'][j])
            new['b_shift'].append(shift)
            new['b_wkv'].append(S)
        else:
            x, lat, kpe = _mla_layer(x, pos, st['c_lat'][j], st['c_rope'][j], w['n_mix'][i], w['c_w_in'][j],
                                     w['c_g_q'][j], w['c_g_kv'][j], w['c_w_uq'][j], w['c_w_ukv'][j],
                                     w['c_w_out'][j])
            new['c_lat'].append(lat)
            new['c_rope'].append(kpe)
        last = i == DEPTH - 1
        res = _ffn_layer(x, st['ffn'][i], w['n_ffn'][i], w['f_w_up'][i], w['f_w_conv'][i],
                         w['f_b_conv'][i], w['f_w_down'][i], w['n_final'] if last else None)
        x = res[0]
        new['ffn'].append(res[1])
    return res[2], {name: jnp.stack(rows, axis=0) for name, rows in new.items()}


def kernel(x_prompt, x_sample, cache_a_k, cache_a_v, cache_a_idx, state_b_wkv, state_b_shift,
           cache_c_latent, cache_c_rope, state_ffn_conv, n_mix, n_ffn, n_final, a_w_in, a_w_out,
           b_mu, b_w_rkv, b_w0, b_w1, b_w2, b_a0, b_a1, b_a2, b_g1, b_g2, b_k_k, b_k_a, b_r_k,
           b_ln_w, b_ln_b, b_w_out, c_w_in, c_g_q, c_g_kv, c_w_uq, c_w_ukv, c_w_out,
           f_w_up, f_w_conv, f_b_conv, f_w_down):
    w = dict(n_mix=n_mix, n_ffn=n_ffn, n_final=n_final, a_w_in=a_w_in, a_w_out=a_w_out,
             b_mu=b_mu, b_w_rkv=b_w_rkv, b_w0=b_w0, b_w1=b_w1, b_w2=b_w2, b_a0=b_a0, b_a1=b_a1,
             b_a2=b_a2, b_g1=b_g1, b_g2=b_g2, b_k_k=b_k_k, b_k_a=b_k_a, b_r_k=b_r_k,
             b_ln_w=b_ln_w, b_ln_b=b_ln_b, b_w_out=b_w_out, c_w_in=c_w_in, c_g_q=c_g_q,
             c_g_kv=c_g_kv, c_w_uq=c_w_uq, c_w_ukv=c_w_ukv, c_w_out=c_w_out,
             f_w_up=f_w_up, f_w_conv=f_w_conv, f_b_conv=f_b_conv, f_w_down=f_w_down)
    Bp, Tp, D = x_prompt.shape
    n_a, n_b, n_c = cache_a_k.shape[0], state_b_wkv.shape[0], cache_c_latent.shape[0]
    st_prompt = dict(
        a_k=jnp.zeros((n_a, Bp, 0, A_KV_HEADS, A_HEAD_DIM), F32),
        a_v=jnp.zeros((n_a, Bp, 0, A_KV_HEADS, A_HEAD_DIM), F32),
        a_idx=jnp.zeros((n_a, Bp, 0, IDX_DIM), F32),
        b_wkv=jnp.zeros((n_b, Bp, B_HEADS, B_HEAD_DIM, B_HEAD_DIM), F32),
        b_shift=jnp.zeros((n_b, Bp, D), F32),
        c_lat=jnp.zeros((n_c, Bp, 0, C_KV_RANK), F32),
        c_rope=jnp.zeros((n_c, Bp, 0, C_ROPE), F32),
        ffn=jnp.zeros((DEPTH, Bp, CONV_W - 1, 2 * D_FF), F32))
    st_sample = dict(a_k=cache_a_k, a_v=cache_a_v, a_idx=cache_a_idx, b_wkv=state_b_wkv,
                     b_shift=state_b_shift, c_lat=cache_c_latent, c_rope=cache_c_rope,
                     ffn=state_ffn_conv)
    y_prompt, sp = _trunk(x_prompt, 0, st_prompt, w)
    y_sample, ss = _trunk(x_sample, cache_a_k.shape[2], st_sample, w)
    return (y_prompt, y_sample,
            sp['a_k'], ss['a_k'], sp['a_v'], ss['a_v'], sp['a_idx'], ss['a_idx'],
            sp['b_wkv'], ss['b_wkv'], sp['b_shift'], ss['b_shift'],
            sp['c_lat'], ss['c_lat'], sp['c_rope'], ss['c_rope'],
            sp['ffn'], ss['ffn'])
```

```python
import functools

import jax
import jax.numpy as jnp
from jax import lax
from jax.experimental import pallas as pl
from jax.experimental.pallas import tpu as pltpu

F32 = jnp.float32
BF16 = jnp.bfloat16

D_MODEL = 1024
DEPTH = 4
CHUNK = 64
N_MIXERS = 3
NORM_EPS = 1e-6
ROPE_THETA = 500000.0
A_HEADS, A_HEAD_DIM, A_KV_HEADS = 16, 64, 2
A_ROT = A_HEAD_DIM // 4
IDX_HEADS, IDX_DIM = 8, 64
IDX_ROT = IDX_DIM // 4
TOPK_MAX = 256
A_O_Q = A_HEADS * A_HEAD_DIM
A_O_K = A_O_Q + A_KV_HEADS * A_HEAD_DIM
A_O_V = A_O_K + A_KV_HEADS * A_HEAD_DIM
A_O_QI = A_O_V + IDX_HEADS * IDX_DIM
A_O_KI = A_O_QI + IDX_DIM
A_IN = A_O_KI + IDX_HEADS
B_HEAD_DIM = 64
B_HEADS = D_MODEL // B_HEAD_DIM
B_GN_EPS = 64e-5
C_HEADS, C_NOPE, C_ROPE, C_V = 16, 64, 32, 64
C_Q_RANK, C_KV_RANK = 512, 256
D_FF = 2816
CONV_W = 3

LANE = 128
SUBLANE_BF16 = 16
VMEM_LIMIT = 56 * 1024 * 1024
NEG_INF = float("-inf")
LOG2E = 1.4426950408889634


def _round_up(n, m):
    return (n + m - 1) // m * m


def _row_tile(M, pref):
    t = min(pref, M)
    while M % t:
        t //= 2
    return t


def _cparams(sem):
    return pltpu.CompilerParams(dimension_semantics=sem, vmem_limit_bytes=VMEM_LIMIT)


def _norm_kernel(x_ref, g_ref, o_ref):
    xf = x_ref[...]
    o_ref[...] = xf * lax.rsqrt(jnp.mean(xf * xf, axis=-1, keepdims=True) + NORM_EPS) * g_ref[...]


def _norm(x, g, tm=512):
    M, K = x.shape
    tm = _row_tile(M, tm)
    return pl.pallas_call(
        _norm_kernel,
        grid=(M // tm,),
        in_specs=[pl.BlockSpec((tm, K), lambda i: (i, 0)), pl.BlockSpec((1, K), lambda i: (0, 0))],
        out_specs=pl.BlockSpec((tm, K), lambda i: (i, 0)),
        out_shape=jax.ShapeDtypeStruct((M, K), F32),
        compiler_params=_cparams(("parallel",)),
        name="rmsnorm",
    )(x, g.reshape(1, K))


ATTN_KB = 512
ATTN_KB_SHIFT = ATTN_KB.bit_length() - 1
SELECT_ALL = 1e9
MANTISSA_BITS = 23
ATTN_ROWS_PER_ITER = 1024
TOPK_PART_ROWS = 64


def _for_key_blocks(nk, body, init):
    carry = lax.fori_loop(0, nk // 2, lambda i, c: body(2 * i + 1, body(2 * i, c)), init)
    return lax.cond(nk % 2 == 1, lambda c: body(nk - 1, c), lambda c: c, carry)


def _fold(x, op=jnp.add):
    acc = x[:, :LANE]
    for j in range(1, x.shape[1] // LANE):
        acc = op(acc, x[:, j * LANE:(j + 1) * LANE])
    return acc


def _topk_bias_t(tab_ref, qi_ref, wi_ref, ki_ref, SC, MS, *, nk, first, tq, top, wi_off):
    kb = ATTN_KB
    qi = qi_ref[0]
    w_t = jnp.transpose(wi_ref[0])[wi_off:wi_off + IDX_HEADS, :]
    part = TOPK_PART_ROWS
    zeros_p = jnp.zeros((part, tq), F32)
    zero = jnp.zeros((1, tq), F32)
    kf = float(top)
    pos_q = lax.broadcasted_iota(jnp.int32, (1, tq), 1) + first
    limit = (pos_q & ~(CHUNK - 1)) + CHUNK
    key_i = lax.broadcasted_iota(jnp.int32, (kb, tq), 0)

    def key_sum(body):
        def step(j, a):
            return a + jnp.sum(body(j).reshape(kb // part, part, tq), axis=0)

        return jnp.sum(_for_key_blocks(nk, step, zeros_p), axis=0, keepdims=True)

    def ones_where(c):
        return jnp.where(c, 1.0, 0.0)

    def score_block(j):
        kij = ki_ref[0, pl.ds(pl.multiple_of(j * kb, kb), kb), :]
        sc = jnp.zeros((kb, tq), F32)
        for h in range(IDX_HEADS):
            d = lax.dot_general(kij, qi[:, h * IDX_DIM:(h + 1) * IDX_DIM], (((1,), (1,)), ((), ())),
                                preferred_element_type=F32)
            sc = sc + w_t[h:h + 1, :] * jnp.maximum(d, 0.0)
        sc = jnp.where(key_i + j * kb < limit, sc, NEG_INF)
        SC[j] = sc
        return ones_where(sc >= 0.0)

    c0 = key_sum(score_block)
    neg = c0 < kf
    sgn = jnp.where(neg, -1.0, 1.0)
    kp = jnp.where(neg, kf, (nk * kb).astype(F32) - kf + 1.0)

    def flip(j, c):
        SC[j] = SC[j] * sgn
        return c

    _for_key_blocks(nk, flip, 0)

    def count_lt(cand):
        return key_sum(lambda j: ones_where(SC[j] < cand))

    def exp_step(i, carry):
        e_cur, t_cur = carry
        cand = jnp.where(e_cur == 0.0, tab_ref[0, i], t_cur * tab_ref[1, i])
        ok = count_lt(cand) < kp
        return jnp.where(ok, e_cur + tab_ref[2, i], e_cur), jnp.where(ok, cand, t_cur)

    _, t_pow = lax.fori_loop(0, 8, exp_step, (zero, zero))

    def man_step(i, carry):
        t_cur, frac = carry
        cand = t_cur + frac
        ok = count_lt(cand) < kp
        return jnp.where(ok, cand, t_cur), frac * 0.5

    t_cur, _ = lax.fori_loop(0, MANTISSA_BITS, man_step, (t_pow, t_pow * 0.5))
    thr = t_cur * sgn

    def score(j):
        return SC[j] * sgn

    def key_idx(j):
        return (key_i + j * kb).astype(F32)

    need = kf - key_sum(lambda j: ones_where(score(j) > thr))
    n_eq = key_sum(lambda j: ones_where(score(j) == thr))

    def index_cut():
        nbits = (SC.shape[0] * kb - 1).bit_length()

        def bit_step(i, carry):
            c_cur, bit = carry
            cand = c_cur + bit
            ok = key_sum(lambda j: ones_where((score(j) == thr) & (key_idx(j) < cand))) < need
            return jnp.where(ok, cand, c_cur), bit * 0.5

        c_cur, _ = lax.fori_loop(0, nbits, bit_step, (zero, jnp.full((1, tq), 2.0 ** (nbits - 1), F32)))
        return c_cur

    any_split = jnp.max(ones_where(n_eq > need)) > 0.0
    c_cut = lax.cond(any_split, index_cut, lambda: jnp.full((1, tq), SELECT_ALL, F32))

    def write_bias(j, c):
        s = score(j)
        sel = (s > thr) | ((s == thr) & (key_idx(j) <= c_cut))
        MS[j] = jnp.transpose(jnp.where(sel & (key_i + j * kb < limit), 0.0, NEG_INF))
        return c

    _for_key_blocks(nk, write_bias, 0)


def _attn2_kernel(*refs, n_kv, group, tq, pos0, top, indexer, wi_off):
    if indexer:
        (tab_ref, q_ref, k_ref, v_ref, x_ref, wo_ref, qi_ref, wi_ref, ki_ref, xo_ref,
         MS, LG, MACC, LACC, OACC, OH, OALL, SC) = refs
    else:
        q_ref, k_ref, v_ref, x_ref, wo_ref, xo_ref, MS, LG, MACC, LACC, OACC, OH, OALL = refs
    kb = ATTN_KB
    first = pos0 + pl.program_id(1) * tq
    last_limit = ((first + tq - 1) & ~(CHUNK - 1)) + CHUNK
    nk = (last_limit + (kb - 1)) >> ATTN_KB_SHIFT
    row = lax.broadcasted_iota(jnp.int32, (tq, 1), 0) + first
    limit = (row & ~(CHUNK - 1)) + CHUNK
    lane_i = lax.broadcasted_iota(jnp.int32, (tq, kb), 1)

    if indexer:
        _topk_bias_t(tab_ref, qi_ref, wi_ref, ki_ref, SC, MS, nk=nk, first=first, tq=tq, top=top, wi_off=wi_off)
    else:
        def causal_bias(j, c):
            MS[j] = jnp.where(lane_i + j * kb < limit, 0.0, NEG_INF)
            return c

        _for_key_blocks(nk, causal_bias, 0)

    hu = LG.shape[0]

    def per_kv_heads(gi, c):
        heads = [gi * hu + u for u in range(hu)]
        qs = [q_ref[0, g, 0] for g in heads]
        MACC[...] = jnp.full(MACC.shape, NEG_INF, F32)
        LACC[...] = jnp.zeros(LACC.shape, F32)
        OACC[...] = jnp.zeros(OACC.shape, F32)

        def logits_block(j, c_):
            keys = pl.ds(pl.multiple_of(j * kb, kb), kb)
            bias = MS[j][None]
            for u, g in enumerate(heads):
                lg = lax.dot_general(qs[u], k_ref[0, g, keys, :], (((1,), (1,)), ((), ())),
                                     preferred_element_type=F32)
                lg = (lg.reshape(group, tq, kb) + bias).reshape(group * tq, kb)
                LG[u, j] = lg
                MACC[u] = jnp.maximum(MACC[u], _fold(lg, jnp.maximum))
            return c_

        _for_key_blocks(nk, logits_block, 0)
        ms = [jnp.max(MACC[u], axis=-1, keepdims=True) for u in range(hu)]

        def value_block(j, c_):
            keys = pl.ds(pl.multiple_of(j * kb, kb), kb)
            for u, g in enumerate(heads):
                p = jnp.exp2(LG[u, j] - ms[u])
                LACC[u] += _fold(p)
                OACC[u] += jnp.dot(p.astype(BF16), v_ref[0, g, keys, :], preferred_element_type=F32)
            return c_

        _for_key_blocks(nk, value_block, 0)
        for u, g in enumerate(heads):
            OH[g] = (OACC[u] / jnp.sum(LACC[u], axis=-1, keepdims=True)).astype(OH.dtype)
        return c

    lax.fori_loop(0, n_kv // hu, per_kv_heads, 0)
    dv = OH.shape[2]
    for h in range(n_kv * group):
        OALL[:, h * dv:(h + 1) * dv] = OH[h // group, (h % group) * tq:(h % group + 1) * tq, :]
    xo_ref[0] = x_ref[0] + jnp.dot(OALL[...], wo_ref[...], preferred_element_type=F32)


def _attn2(qg, kt, vt, x, w_out, *, pos0, tq, group, idx=None, top=0, wi_off=0):
    B, n_kv, nq, rows, dq = qg.shape
    L, dv = kt.shape[2], vt.shape[3]
    T, D = x.shape[1], x.shape[2]
    kb = ATTN_KB
    assert nq * tq == T and rows == group * tq and L % kb == 0 and (idx is None or top <= kb)
    ins = [qg, kt, vt, x, w_out]
    specs = [pl.BlockSpec((1, n_kv, 1, rows, dq), lambda b, i: (b, 0, i, 0, 0)),
             pl.BlockSpec((1, n_kv, L, dq), lambda b, i: (b, 0, 0, 0)),
             pl.BlockSpec((1, n_kv, L, dv), lambda b, i: (b, 0, 0, 0)),
             pl.BlockSpec((1, tq, D), lambda b, i: (b, i, 0)),
             pl.BlockSpec(w_out.shape, lambda b, i: (0, 0))]
    hu = max(1, min(n_kv, ATTN_ROWS_PER_ITER // rows))
    assert n_kv % hu == 0
    scratch = [pltpu.VMEM((L // kb, tq, kb), F32), pltpu.VMEM((hu, L // kb, rows, kb), F32),
               pltpu.VMEM((hu, rows, LANE), F32), pltpu.VMEM((hu, rows, LANE), F32),
               pltpu.VMEM((hu, rows, dv), F32), pltpu.VMEM((n_kv, rows, dv), BF16),
               pltpu.VMEM((tq, n_kv * group * dv), BF16)]
    if idx is not None:
        qi, wi, ki = idx
        steps = [2 ** b for b in range(7, -1, -1)]
        tab = jnp.array([[2.0 ** (s - 127) for s in steps], [2.0 ** s if s < 128 else 1.0 for s in steps],
                         [float(s) for s in steps]], F32)
        ins = [tab] + ins + [qi, wi, ki]
        specs = ([pl.BlockSpec(memory_space=pltpu.SMEM)] + specs
                 + [pl.BlockSpec((1, tq, qi.shape[2]), lambda b, i: (b, i, 0)),
                    pl.BlockSpec((1, tq, wi.shape[2]), lambda b, i: (b, i, 0)),
                    pl.BlockSpec((1, L, ki.shape[2]), lambda b, i: (b, 0, 0))])
        scratch.append(pltpu.VMEM((L // kb, kb, tq), F32))
    return pl.pallas_call(
        functools.partial(_attn2_kernel, n_kv=n_kv, group=group, tq=tq, pos0=pos0, top=top,
                          indexer=idx is not None, wi_off=wi_off),
        grid=(B, nq),
        in_specs=specs,
        out_specs=pl.BlockSpec((1, tq, D), lambda b, i: (b, i, 0)),
        out_shape=jax.ShapeDtypeStruct((B, T, D), F32),
        scratch_shapes=scratch,
        compiler_params=_cparams(("parallel", "parallel")),
        name="dsa_attn" if idx is not None else "mla_attn",
    )(*ins)


def _rope_tables(pos, dh, rot, offset=0):
    half = rot // 2
    inv = ROPE_THETA ** (-jnp.arange(half, dtype=F32) / half)
    ang = pos.astype(F32)[:, None] * inv[None, :]
    cos, sin = jnp.cos(ang), jnp.sin(ang)
    T = pos.shape[0]
    pad = lambda n, v: jnp.full((T, n), v, F32)
    zh = pad(half, 0.0)
    lo, hi = offset, dh - offset - rot
    c = jnp.concatenate([pad(lo, 1.0), cos, cos, pad(hi, 1.0)], axis=1)
    s1 = jnp.concatenate([pad(lo, 0.0), -sin, zh, pad(hi, 0.0)], axis=1)
    s2 = jnp.concatenate([pad(lo, 0.0), zh, sin, pad(hi, 0.0)], axis=1)
    return tuple(jnp.tile(t, (1, LANE // dh)) for t in (c, s1, s2))


def _rope_lanes(x, c, s1, s2, half):
    return x * c + pltpu.roll(x, LANE - half, 1) * s1 + pltpu.roll(x, half, 1) * s2


def _dsa_proj_kernel(x_ref, g_ref, w_ref, c_ref, s1_ref, s2_ref, ts_ref, qh_ref, kf_ref, kt_ref, vf_ref, vt_ref,
                     qi_ref, tail_ref, kib_ref, *, tm):
    xf = x_ref[0]
    h = xf * lax.rsqrt(jnp.mean(xf * xf, axis=-1, keepdims=True) + NORM_EPS) * g_ref[...]
    acc = jnp.dot(h.astype(BF16), w_ref[...], preferred_element_type=F32)
    c, s1, s2 = c_ref[...], s1_ref[...], s2_ref[...]
    half = A_ROT // 2
    hd = A_HEAD_DIM
    group = A_HEADS // A_KV_HEADS
    rope = lambda xs: _rope_lanes(xs, c, s1, s2, half)
    slab = lambda off: acc[:, off:off + LANE]
    for s in range(A_O_Q // LANE):
        qs = (rope(slab(s * LANE)) * (hd ** -0.5 * LOG2E)).astype(BF16)
        for e in range(LANE // hd):
            head = s * (LANE // hd) + e
            u = head % group
            qh_ref[0, head // group, 0, u * tm:(u + 1) * tm, :] = qs[:, e * hd:(e + 1) * hd]
    ks = rope(slab(A_O_Q))
    vs = slab(A_O_K)
    kf_ref[0] = ks
    vf_ref[0] = vs
    for e in range(A_KV_HEADS):
        kt_ref[0, e] = ks[:, e * hd:(e + 1) * hd].astype(BF16)
        vt_ref[0, e] = vs[:, e * hd:(e + 1) * hd].astype(BF16)
    for s in range(IDX_HEADS * IDX_DIM // LANE):
        qi_ref[0, :, s * LANE:(s + 1) * LANE] = rope(slab(A_O_V + s * LANE)).astype(BF16)
    is_key = lax.broadcasted_iota(jnp.int32, (tm, LANE), 1) < IDX_DIM
    tl = _rope_lanes(slab(A_O_QI), jnp.where(is_key, c, 1.0), jnp.where(is_key, s1, 0.0),
                     jnp.where(is_key, s2, 0.0), half) * ts_ref[...]
    tail_ref[0] = tl
    kib_ref[0] = tl[:, :IDX_DIM].astype(BF16)


def _dsa_proj(x, g, w_in, tabs, tm):
    B, T, D = x.shape
    assert (A_HEAD_DIM, A_ROT) == (IDX_DIM, IDX_ROT) and A_KV_HEADS * A_HEAD_DIM == LANE
    assert A_O_QI % LANE == 0 and IDX_DIM + IDX_HEADS <= LANE and T % tm == 0
    n_in = _round_up(A_IN, LANE)
    group = A_HEADS // A_KV_HEADS
    hd = A_HEAD_DIM
    nq = T // tm
    lanes = jnp.arange(LANE)
    tail_scale = jnp.where(lanes < IDX_DIM, 1.0, jnp.where(lanes < IDX_DIM + IDX_HEADS,
                                                            (IDX_HEADS * IDX_DIM) ** -0.5, 0.0)).astype(F32)
    row = lambda n: pl.BlockSpec((1, tm, n), lambda b, i: (b, i, 0))
    tab = pl.BlockSpec((tm, LANE), lambda b, i: (i, 0))
    const = lambda a: pl.BlockSpec(a.shape, lambda b, i: (0,) * a.ndim)
    kvh = pl.BlockSpec((1, A_KV_HEADS, tm, hd), lambda b, i: (b, 0, i, 0))
    w = _pad_cols(w_in, n_in).astype(BF16)
    g2 = g.reshape(1, D)
    ts = tail_scale.reshape(1, LANE)
    return pl.pallas_call(
        functools.partial(_dsa_proj_kernel, tm=tm),
        grid=(B, nq),
        in_specs=[row(D), const(g2), const(w), tab, tab, tab, const(ts)],
        out_specs=[pl.BlockSpec((1, A_KV_HEADS, 1, group * tm, hd), lambda b, i: (b, 0, i, 0, 0)),
                   row(LANE), kvh, row(LANE), kvh, row(IDX_HEADS * IDX_DIM), row(LANE), row(IDX_DIM)],
        out_shape=[jax.ShapeDtypeStruct((B, A_KV_HEADS, nq, group * tm, hd), BF16),
                   jax.ShapeDtypeStruct((B, T, LANE), F32), jax.ShapeDtypeStruct((B, A_KV_HEADS, T, hd), BF16),
                   jax.ShapeDtypeStruct((B, T, LANE), F32), jax.ShapeDtypeStruct((B, A_KV_HEADS, T, hd), BF16),
                   jax.ShapeDtypeStruct((B, T, IDX_HEADS * IDX_DIM), BF16),
                   jax.ShapeDtypeStruct((B, T, LANE), F32), jax.ShapeDtypeStruct((B, T, IDX_DIM), BF16)],
        compiler_params=_cparams(("parallel", "parallel")),
        name="dsa_proj",
    )(x, g2, w, *tabs, ts)


def _rms(xf, gain):
    return xf * lax.rsqrt(jnp.mean(xf * xf, axis=-1, keepdims=True) + NORM_EPS) * gain


def _mla_write_kv(lat, kpe_slab, wuk_ref, wuv_ref, kt_ref, vt_ref):
    lb = lat.astype(BF16)
    kn = jnp.dot(lb, wuk_ref[...], preferred_element_type=F32)
    vv = jnp.dot(lb, wuv_ref[...], preferred_element_type=F32)
    for h in range(C_HEADS):
        kt_ref[0, h] = (kn[:, h * LANE:(h + 1) * LANE] + kpe_slab).astype(BF16)
        vt_ref[0, h] = vv[:, h * C_V:(h + 1) * C_V].astype(BF16)


def _mla_proj_kernel(x_ref, g_ref, win_ref, gq_ref, gkv_ref, wuq_ref, wuk_ref, wuv_ref, c_ref, s1_ref, s2_ref,
                     qh_ref, kt_ref, vt_ref, lat_ref, kpe_ref):
    h = _rms(x_ref[0], g_ref[...]).astype(BF16)
    proj = jnp.dot(h, win_ref[...], preferred_element_type=F32)
    c, s1, s2 = c_ref[...], s1_ref[...], s2_ref[...]
    half = C_ROPE // 2
    q = jnp.dot(_rms(proj[:, :C_Q_RANK], gq_ref[...]).astype(BF16), wuq_ref[...], preferred_element_type=F32)
    scale = (C_NOPE + C_ROPE) ** -0.5 * LOG2E
    for hd in range(C_HEADS):
        qh_ref[0, hd, 0] = (_rope_lanes(q[:, hd * LANE:(hd + 1) * LANE], c, s1, s2, half) * scale).astype(BF16)
    lat = _rms(proj[:, C_Q_RANK:C_Q_RANK + C_KV_RANK], gkv_ref[...])
    lat_ref[0] = lat
    kpe_slab = _rope_lanes(proj[:, C_Q_RANK + C_KV_RANK:], c, s1, s2, half)
    kpe_ref[0] = kpe_slab[:, C_NOPE:C_NOPE + C_ROPE]
    _mla_write_kv(lat, kpe_slab, wuk_ref, wuv_ref, kt_ref, vt_ref)


def _mla_kv_kernel(lat_ref, kpe_ref, wuk_ref, wuv_ref, kt_ref, vt_ref):
    _mla_write_kv(lat_ref[0], kpe_ref[0], wuk_ref, wuv_ref, kt_ref, vt_ref)


def _mla_weights(w_in, w_uq, w_ukv):
    D = w_in.shape[0]
    zc = lambda rows, n: jnp.zeros((rows, n), w_in.dtype)
    w_in2 = jnp.concatenate([w_in[:, :C_Q_RANK + C_KV_RANK], zc(D, C_NOPE), w_in[:, C_Q_RANK + C_KV_RANK:],
                             zc(D, LANE - C_NOPE - C_ROPE)], axis=1)
    pad_heads = lambda w, d: jnp.pad(w.reshape(w.shape[0], C_HEADS, d), ((0, 0), (0, 0), (0, LANE - d))).reshape(
        w.shape[0], C_HEADS * LANE)
    w_uq2 = pad_heads(w_uq, C_NOPE + C_ROPE)
    ukv = w_ukv.reshape(C_KV_RANK, C_HEADS, C_NOPE + C_V)
    w_uk2 = pad_heads(ukv[..., :C_NOPE].reshape(C_KV_RANK, C_HEADS * C_NOPE), C_NOPE)
    w_uv2 = ukv[..., C_NOPE:].reshape(C_KV_RANK, C_HEADS * C_V)
    return tuple(a.astype(BF16) for a in (w_in2, w_uq2, w_uk2, w_uv2))


def _mla_proj(x, g, g_q, g_kv, weights, tabs, tm):
    B, T, D = x.shape
    w_in2, w_uq2, w_uk2, w_uv2 = weights
    nq = T // tm
    row = lambda n: pl.BlockSpec((1, tm, n), lambda b, i: (b, i, 0))
    tab = pl.BlockSpec((tm, LANE), lambda b, i: (i, 0))
    const = lambda a: pl.BlockSpec(a.shape, lambda b, i: (0,) * a.ndim)
    heads = lambda d: pl.BlockSpec((1, C_HEADS, tm, d), lambda b, i: (b, 0, i, 0))
    vec = lambda a: a.reshape(1, a.shape[0])
    return pl.pallas_call(
        _mla_proj_kernel,
        grid=(B, nq),
        in_specs=[row(D), const(vec(g)), const(w_in2), const(vec(g_q)), const(vec(g_kv)), const(w_uq2),
                  const(w_uk2), const(w_uv2), tab, tab, tab],
        out_specs=[pl.BlockSpec((1, C_HEADS, 1, tm, LANE), lambda b, i: (b, 0, i, 0, 0)), heads(LANE), heads(C_V),
                   row(C_KV_RANK), row(C_ROPE)],
        out_shape=[jax.ShapeDtypeStruct((B, C_HEADS, nq, tm, LANE), BF16),
                   jax.ShapeDtypeStruct((B, C_HEADS, T, LANE), BF16), jax.ShapeDtypeStruct((B, C_HEADS, T, C_V), BF16),
                   jax.ShapeDtypeStruct((B, T, C_KV_RANK), F32), jax.ShapeDtypeStruct((B, T, C_ROPE), F32)],
        compiler_params=_cparams(("parallel", "parallel")),
        name="mla_proj",
    )(x, vec(g), w_in2, vec(g_q), vec(g_kv), w_uq2, w_uk2, w_uv2, *tabs)


def _mla_kv(lat, kpe_slab, w_uk2, w_uv2, tm):
    B, P, _ = lat.shape
    row = lambda n: pl.BlockSpec((1, tm, n), lambda b, i: (b, i, 0))
    const = lambda a: pl.BlockSpec(a.shape, lambda b, i: (0,) * a.ndim)
    heads = lambda d: pl.BlockSpec((1, C_HEADS, tm, d), lambda b, i: (b, 0, i, 0))
    return pl.pallas_call(
        _mla_kv_kernel,
        grid=(B, P // tm),
        in_specs=[row(C_KV_RANK), row(LANE), const(w_uk2), const(w_uv2)],
        out_specs=[heads(LANE), heads(C_V)],
        out_shape=[jax.ShapeDtypeStruct((B, C_HEADS, P, LANE), BF16), jax.ShapeDtypeStruct((B, C_HEADS, P, C_V), BF16)],
        compiler_params=_cparams(("parallel", "parallel")),
        name="mla_kv",
    )(lat, kpe_slab, w_uk2, w_uv2)


def _head_sum(x, bo_ref, split=True):
    bw = bo_ref.shape[0]
    bo = bo_ref[...]
    hi = x.astype(BF16)
    terms = [hi, (x - hi.astype(F32)).astype(BF16)] if split else [hi]
    return jnp.concatenate(
        [sum(jnp.dot(t[:, j * bw:(j + 1) * bw], bo, preferred_element_type=F32) for t in terms)
         for j in range(x.shape[1] // bw)], axis=-1)


RWKV_HALO = 8


def _rwkv_pre_kernel(x_ref, xh_ref, sh_ref, g_ref, mu_ref, vec_ref, wr_ref, wk_ref, wv_ref, w1_ref, w2_ref,
                     a1_ref, a2_ref, g1_ref, g2_ref, bo_ref,
                     rp_ref, w_ref, k_ref, v_ref, a_ref, b_ref, yc_ref, bonus_ref, gate_ref, *, tm):
    i = pl.program_id(1)
    gain = g_ref[...]

    def norm(xf):
        return xf * lax.rsqrt(jnp.mean(xf * xf, axis=-1, keepdims=True) + NORM_EPS) * gain

    h = norm(x_ref[0])
    before = jnp.where(i > 0, norm(xh_ref[0])[RWKV_HALO - 1:RWKV_HALO, :], sh_ref[0])
    first = lax.broadcasted_iota(jnp.int32, (tm, 1), 0) == 0
    xx = jnp.where(first, before, pltpu.roll(h, 1, 0)) - h
    mu = mu_ref[...]
    vec = vec_ref[...]
    w0, a0, k_k, k_a, r_k = (vec[j:j + 1, :] for j in range(5))

    def mix(j):
        return (h + xx * mu[j:j + 1, :]).astype(BF16)

    dot = lambda a_, w_: jnp.dot(a_, w_[...], preferred_element_type=F32)
    r = dot(mix(0), wr_ref)
    wl = dot(jnp.tanh(dot(mix(1), w1_ref)).astype(BF16), w2_ref)
    k = dot(mix(2), wk_ref)
    v = dot(mix(3), wv_ref)
    al = dot(dot(mix(4), a1_ref).astype(BF16), a2_ref)
    gate_ref[0] = dot(jax.nn.sigmoid(dot(mix(5), g1_ref)).astype(BF16), g2_ref)
    z = -(w0 + wl)
    softplus = jnp.maximum(z, 0.0) + jnp.log(1.0 + jnp.exp(-jnp.abs(z)))
    decay = jnp.exp(-jnp.exp(-softplus - 0.5))
    a = jax.nn.sigmoid(a0 + al)
    kk = k * k_k
    kk = kk / jnp.maximum(jnp.sqrt(_head_sum(kk * kk, bo_ref)), 1e-12)
    k = k * (1.0 + (a - 1.0) * k_a)
    b = kk * a
    rp_ref[0] = decay * r - kk * _head_sum(b * r, bo_ref, split=False)
    w_ref[0] = decay
    k_ref[0] = k
    for hh in range(B_HEADS):
        v_ref[0, :, hh, :] = v[:, hh * B_HEAD_DIM:(hh + 1) * B_HEAD_DIM]
    a_ref[0] = -kk
    b_ref[0] = b
    yc_ref[0] = v * _head_sum(k * r, bo_ref, split=False)
    bonus_ref[0] = _head_sum(r * k * r_k, bo_ref, split=False) * v


def _rwkv_post_kernel(y_ref, yc_ref, bonus_ref, gate_ref, x_ref, ln_ref, wo_ref, bo_ref, o_ref):
    n = float(B_HEAD_DIM)
    y = y_ref[0] + yc_ref[0]
    d = y - _head_sum(y, bo_ref) / n
    var = _head_sum(d * d, bo_ref) / n
    ln = ln_ref[...]
    yn = d * lax.rsqrt(var + B_GN_EPS) * ln[0:1, :] + ln[1:2, :] + bonus_ref[0]
    o_ref[0] = x_ref[0] + jnp.dot((yn * gate_ref[0]).astype(BF16), wo_ref[...], preferred_element_type=F32)


def _block_ones():
    blk = jnp.arange(WKV_BW) // B_HEAD_DIM
    return (blk[:, None] == blk[None, :]).astype(BF16)


def _rwkv_pre(x, shift_prev, g, mu, vecs, ws, tm):
    B, T, D = x.shape
    H = RWKV_HALO
    assert T % tm == 0 and tm % H == 0
    nh = tm // H
    bo = _block_ones()
    row = pl.BlockSpec((1, tm, D), lambda b, i: (b, i, 0))
    const = lambda a: pl.BlockSpec(a.shape, lambda b, i: (0,) * a.ndim)
    pad8 = lambda a: jnp.pad(a, ((0, 8 - a.shape[0]), (0, 0)))
    mu8, vec8, g2 = pad8(mu), pad8(vecs), g.reshape(1, D)
    return pl.pallas_call(
        functools.partial(_rwkv_pre_kernel, tm=tm),
        grid=(B, T // tm),
        in_specs=[row, pl.BlockSpec((1, H, D), lambda b, i: (b, jnp.maximum(i * nh - 1, 0), 0)),
                  pl.BlockSpec((1, 1, D), lambda b, i: (b, 0, 0)), const(g2), const(mu8), const(vec8)]
                 + [const(a) for a in ws] + [const(bo)],
        out_specs=[row] * 3 + [pl.BlockSpec((1, tm, B_HEADS, B_HEAD_DIM), lambda b, i: (b, i, 0, 0))] + [row] * 5,
        out_shape=[jax.ShapeDtypeStruct((B, T, D), F32)] * 3
                  + [jax.ShapeDtypeStruct((B, T, B_HEADS, B_HEAD_DIM), F32)] + [jax.ShapeDtypeStruct((B, T, D), F32)] * 5,
        compiler_params=_cparams(("parallel", "parallel")),
        name="rwkv_pre",
    )(x, x, shift_prev.reshape(B, 1, D), g2, mu8, vec8, *ws, bo)


def _rwkv_post(y, yc, bonus, gate, x, ln, w_out, tm):
    B, T, D = x.shape
    bo = _block_ones()
    row = pl.BlockSpec((1, tm, D), lambda b, i: (b, i, 0))
    const = lambda a: pl.BlockSpec(a.shape, lambda b, i: (0,) * a.ndim)
    return pl.pallas_call(
        _rwkv_post_kernel,
        grid=(B, T // tm),
        in_specs=[row] * 5 + [const(ln), const(w_out), const(bo)],
        out_specs=row,
        out_shape=jax.ShapeDtypeStruct((B, T, D), F32),
        compiler_params=_cparams(("parallel", "parallel")),
        name="rwkv_post",
    )(y, yc, bonus, gate, x, ln, w_out, bo)


WKV_SUB = 16
WKV_NB = 8
WKV_TC = 64
WKV_BW = 2 * LANE


def _wkv_kernel(rp_ref, w_ref, k_ref, vh_ref, a_ref, b_ref, s0_ref, gs_ref, hs_ref, y_ref, sT_ref, S, *, tc, nb):
    c = pl.program_id(1)
    H = B_HEADS

    @pl.when(c == 0)
    def _():
        S[...] = s0_ref[...]

    def block(sc, carry):
        base = pl.multiple_of(sc * WKV_SUB, WKV_SUB)
        rows = pl.ds(base, WKV_SUB)
        gs = gs_ref[...]
        hsel = hs_ref[...]
        ins = [tuple(x[bi, rows, :] for x in (rp_ref, w_ref, k_ref, a_ref, b_ref)) for bi in range(nb)]
        for u in range(WKV_SUB):
            sas = []
            for bi in range(nb):
                r8, w8, k8, a8, b8 = ins[bi]
                s = S[bi]
                pa = (s * a8[u:u + 1, :]).astype(BF16)
                sas.append(jnp.dot(pa, gs, preferred_element_type=F32))
                pr = (s * r8[u:u + 1, :]).astype(BF16)
                y_ref[bi, base + u] = lax.dot_general(hsel, pr, (((1,), (1,)), ((), ())),
                                                      preferred_element_type=F32)
            deltas = []
            for bi in range(nb):
                r8, w8, k8, a8, b8 = ins[bi]
                v_t = jnp.transpose(vh_ref[bi, base + u])
                lhs = jnp.concatenate([sas[bi][:, :H], v_t], axis=1).astype(BF16)
                rhs = jnp.concatenate([hsel * b8[u:u + 1, :].astype(BF16), hsel * k8[u:u + 1, :].astype(BF16)],
                                      axis=0)
                deltas.append(jnp.dot(lhs, rhs, preferred_element_type=F32))
            for bi in range(nb):
                r8, w8, k8, a8, b8 = ins[bi]
                S[bi] = S[bi] * w8[u:u + 1, :] + deltas[bi]
        return carry

    lax.fori_loop(0, tc // WKV_SUB, block, 0)

    @pl.when(c == pl.num_programs(1) - 1)
    def _():
        sT_ref[...] = S[...]


def _wkv(rp, w, k, v, a, b, s0):
    B, T, D = rp.shape
    n = B_HEAD_DIM
    H = B_HEADS
    nb = _row_tile(B, WKV_NB)
    tc = _row_tile(T, WKV_TC)
    assert tc % WKV_SUB == 0 and H <= LANE
    s0t = jnp.transpose(s0, (0, 2, 1, 3)).reshape(B, n, D)
    hsel = (jnp.arange(H)[:, None] == (jnp.arange(D) // n)[None, :]).astype(BF16)
    gs = (jnp.arange(D)[:, None] // n == jnp.arange(LANE)[None, :]).astype(BF16)
    seq = pl.BlockSpec((nb, tc, D), lambda bi, c: (bi, c, 0))
    seqh = pl.BlockSpec((nb, tc, H, n), lambda bi, c: (bi, c, 0, 0))
    st = pl.BlockSpec((nb, n, D), lambda bi, c: (bi, 0, 0))
    const = lambda a_: pl.BlockSpec(a_.shape, lambda bi, c: (0, 0))
    y, sT = pl.pallas_call(
        functools.partial(_wkv_kernel, tc=tc, nb=nb),
        grid=(B // nb, T // tc),
        in_specs=[seq, seq, seq, seqh, seq, seq, st, const(gs), const(hsel)],
        out_specs=[seqh, st],
        out_shape=[jax.ShapeDtypeStruct((B, T, H, n), F32), jax.ShapeDtypeStruct((B, n, D), F32)],
        scratch_shapes=[pltpu.VMEM((nb, n, D), F32)],
        compiler_params=_cparams(("parallel", "arbitrary")),
        name="wkv",
    )(rp, w, k, v, a, b, s0t, gs, hsel)
    return y.reshape(B, T, D), jnp.transpose(sT.reshape(B, n, H, n), (0, 2, 1, 3))


FFN_HALO = SUBLANE_BF16


FFN_CW = 2 * LANE


def _ffn_body(x_ref, xh_ref, g_ref, wug_ref, wuv_ref, cg_ref, cv_ref, pg_ref, pv_ref, wd_ref, gf_ref,
              o_ref, st_ref, y_ref, hn, ug, uv, act, acc, *, tm, nb):
    i = pl.program_id(1)
    H = FFN_HALO
    cw_ = FFN_CW
    seg = tm + H

    def norm(xf, gain):
        return xf * lax.rsqrt(jnp.mean(xf * xf, axis=-1, keepdims=True) + NORM_EPS) * gain

    for s in range(nb):
        hn[s * seg:s * seg + H, :] = jnp.where(i > 0, norm(xh_ref[s], g_ref[...]), 0.0).astype(BF16)
        hn[s * seg + H:(s + 1) * seg, :] = norm(x_ref[s], g_ref[...]).astype(BF16)
    h = hn[...]
    first = jnp.where(i == 0, 1.0, 0.0)

    def conv(u, r0, taps):
        return (taps[3:4, :] + u[r0 - 2:r0 - 2 + tm, :] * taps[0:1, :] + u[r0 - 1:r0 - 1 + tm, :] * taps[1:2, :]
                + u[r0:r0 + tm, :] * taps[2:3, :])

    for c in range(D_FF // cw_):
        cols = slice(c * cw_, (c + 1) * cw_)
        for half, (u_scr, w_ref, p_ref) in enumerate(((ug, wug_ref, pg_ref), (uv, wuv_ref, pv_ref))):
            u = jnp.dot(h, w_ref[:, cols], preferred_element_type=F32)
            u_scr[c % 2] = u.astype(BF16)
            for s in range(nb):
                u_scr[c % 2, s * seg:s * seg + H, :] = (u[s * seg:s * seg + H, :]
                                                        + p_ref[s, :, cols] * first).astype(BF16)
                st_ref[s, 0, :, c * cw_ + half * D_FF:(c + 1) * cw_ + half * D_FF] = u[(s + 1) * seg - 8:(s + 1) * seg, :]
        for s in range(nb):
            gate = conv(ug.at[c % 2], s * seg + H, cg_ref[:, cols].astype(BF16))
            val = conv(uv.at[c % 2], s * seg + H, cv_ref[:, cols].astype(BF16))
            act[s * tm:(s + 1) * tm, :] = gate * jax.nn.sigmoid(gate) * val
        d = jnp.dot(act[...], wd_ref[cols, :], preferred_element_type=F32)
        if c == 0:
            acc[...] = d
        else:
            acc[...] += d
    for s in range(nb):
        out = x_ref[s] + acc[s * tm:(s + 1) * tm, :]
        o_ref[s] = out
        if gf_ref is not None:
            y_ref[s] = norm(out, gf_ref[...])


def _ffn_kernel(*refs, tm, nb, final):
    if final:
        (x_ref, xh_ref, g_ref, wug_ref, wuv_ref, cg_ref, cv_ref, pg_ref, pv_ref, wd_ref, gf_ref,
         o_ref, st_ref, y_ref, hn, ug, uv, act, acc) = refs
    else:
        (x_ref, xh_ref, g_ref, wug_ref, wuv_ref, cg_ref, cv_ref, pg_ref, pv_ref, wd_ref,
         o_ref, st_ref, hn, ug, uv, act, acc) = refs
        gf_ref = y_ref = None
    _ffn_body(x_ref, xh_ref, g_ref, wug_ref, wuv_ref, cg_ref, cv_ref, pg_ref, pv_ref, wd_ref, gf_ref,
              o_ref, st_ref, y_ref, hn, ug, uv, act, acc, tm=tm, nb=nb)


FFN_ROWS = 512


def _ffn(x, g, wug, wuv, cg, cv, prev, wd, final_g=None):
    B, T, D = x.shape
    F = D_FF
    H = FFN_HALO
    tm = min(FFN_ROWS, T)
    nb = _row_tile(B, max(1, FFN_ROWS // tm))
    assert T % tm == 0 and tm % H == 0 and F % FFN_CW == 0 and tm >= 8
    nh = tm // H
    nt = T // tm
    prev_h = jnp.pad(prev, ((0, 0), (H - prev.shape[1], 0), (0, 0)))
    pg, pv = prev_h[:, :, :F], prev_h[:, :, F:]
    const = lambda a: pl.BlockSpec(a.shape, lambda b, i: (0,) * a.ndim, pipeline_mode=pl.Buffered(1))
    row = pl.BlockSpec((nb, tm, D), lambda b, i: (b, i, 0))
    g2 = g.reshape(1, D)
    ins = [x, x, g2, wug, wuv, cg, cv, pg, pv, wd]
    specs = [row, pl.BlockSpec((nb, H, D), lambda b, i: (b, jnp.maximum(i * nh - 1, 0), 0)),
             const(g2), const(wug), const(wuv), const(cg), const(cv),
             pl.BlockSpec((nb, H, F), lambda b, i: (b, 0, 0)), pl.BlockSpec((nb, H, F), lambda b, i: (b, 0, 0)),
             const(wd)]
    outs = [row, pl.BlockSpec((nb, 1, 8, 2 * F), lambda b, i: (b, i, 0, 0))]
    shapes = [jax.ShapeDtypeStruct((B, T, D), F32), jax.ShapeDtypeStruct((B, nt, 8, 2 * F), F32)]
    if final_g is not None:
        gf = final_g.reshape(1, D)
        ins.append(gf)
        specs.append(const(gf))
        outs.append(row)
        shapes.append(jax.ShapeDtypeStruct((B, T, D), F32))
    res = pl.pallas_call(
        functools.partial(_ffn_kernel, tm=tm, nb=nb, final=final_g is not None),
        grid=(B // nb, nt),
        in_specs=specs,
        out_specs=outs,
        out_shape=shapes,
        scratch_shapes=[pltpu.VMEM((nb * (tm + H), D), BF16), pltpu.VMEM((2, nb * (tm + H), FFN_CW), BF16),
                        pltpu.VMEM((2, nb * (tm + H), FFN_CW), BF16), pltpu.VMEM((nb * tm, FFN_CW), BF16),
                        pltpu.VMEM((nb * tm, D), F32)],
        compiler_params=_cparams(("parallel", "parallel")),
        name="conv_ffn",
    )(*ins)
    state = res[1][:, nt - 1, 8 - (CONV_W - 1):, :]
    return (res[0], state) + tuple(res[2:])


def _pad_cols(w, n):
    return jnp.pad(w, ((0, 0), (0, n - w.shape[1])))


def _pad_keys(a, L):
    return jnp.pad(a, ((0, 0), (0, L - a.shape[1])) + ((0, 0),) * (a.ndim - 2))


def _dsa_layer(x, pos, past_k, past_v, past_ki, g, w_in, w_out):
    B, T, D = x.shape
    tq = min(256, T)
    qh, kf, kt, vf, vt, qi, tail, kib = _dsa_proj(x, g, w_in, _rope_tables(pos, A_HEAD_DIM, A_ROT), tq)
    P = past_k.shape[1]
    L = P + T
    Lp = _round_up(L, ATTN_KB)
    if Lp != T:
        past = lambda a: jnp.transpose(a, (0, 2, 1, 3)).astype(BF16)
        kt = jnp.pad(jnp.concatenate([past(past_k), kt], axis=2), ((0, 0), (0, 0), (0, Lp - L), (0, 0)))
        vt = jnp.pad(jnp.concatenate([past(past_v), vt], axis=2), ((0, 0), (0, 0), (0, Lp - L), (0, 0)))
        kib = _pad_keys(jnp.concatenate([past_ki.astype(BF16), kib], axis=1), Lp)
    x = _attn2(qh, kt, vt, x, w_out.astype(BF16), pos0=P, tq=tq, group=A_HEADS // A_KV_HEADS,
               idx=(qi, tail, kib), top=min(TOPK_MAX, L // 4), wi_off=IDX_DIM)
    kv_rows = lambda a: a.reshape(B, T, A_KV_HEADS, A_HEAD_DIM)
    return x, kv_rows(kf), kv_rows(vf), tail[..., :IDX_DIM]


def _rwkv_layer(x, shift_prev, S0, g, mu, w_rkv, w0, w1, w2, a0, a1, a2, g1, g2, k_k, k_a, r_k, ln_w, ln_b,
                w_out):
    B, T, D = x.shape
    tm = min(256, T)
    bf = lambda a: a.astype(BF16)
    vecs = jnp.stack([w0, a0, k_k, k_a, r_k.reshape(D)], axis=0)
    ws = [bf(w_rkv[0]), bf(w_rkv[1]), bf(w_rkv[2]), bf(w1), bf(w2), bf(a1), bf(a2), bf(g1), bf(g2)]
    rp, decay, k, v, a_vec, b_vec, yc, bonus, gate = _rwkv_pre(x, shift_prev, g, mu, vecs, ws, tm)
    y, S = _wkv(rp, decay, k, v, a_vec, b_vec, S0)
    ln = jnp.pad(jnp.stack([ln_w, ln_b], axis=0), ((0, 6), (0, 0)))
    x_new = _rwkv_post(y, yc, bonus, gate, x, ln, bf(w_out), tm)
    assert T >= 8
    shift = _norm(x[:, T - 8:].reshape(B * 8, D), g).reshape(B, 8, D)[:, -1]
    return x_new, shift, S


def _mla_layer(x, pos, past_lat, past_rope, g, w_in, g_q, g_kv, w_uq, w_ukv, w_out):
    B, T, D = x.shape
    tq = min(256, T)
    weights = _mla_weights(w_in, w_uq, w_ukv)
    tabs = _rope_tables(pos, LANE, C_ROPE, offset=C_NOPE)
    qh, kt, vt, lat, kpe = _mla_proj(x, g, g_q, g_kv, weights, tabs, tq)
    P = past_lat.shape[1]
    L = P + T
    Lp = _round_up(L, ATTN_KB)
    if Lp != T:
        past_slab = jnp.pad(past_rope, ((0, 0), (0, 0), (C_NOPE, LANE - C_NOPE - C_ROPE)))
        kt_p, vt_p = _mla_kv(past_lat, past_slab, weights[2], weights[3], _row_tile(P, 256))
        kt = jnp.pad(jnp.concatenate([kt_p, kt], axis=2), ((0, 0), (0, 0), (0, Lp - L), (0, 0)))
        vt = jnp.pad(jnp.concatenate([vt_p, vt], axis=2), ((0, 0), (0, 0), (0, Lp - L), (0, 0)))
    x = _attn2(qh, kt, vt, x, w_out.astype(BF16), pos0=P, tq=tq, group=1)
    return x, lat, kpe


def _ffn_layer(x, prev, g, w_up, w_conv, b_conv, w_down, final_g=None):
    F = D_FF
    taps = jnp.concatenate([w_conv, b_conv[None, :], jnp.zeros((8 - CONV_W - 1, 2 * F), F32)], axis=0)
    w_up = w_up.astype(BF16)
    return _ffn(x, g, w_up[:, :F], w_up[:, F:], taps[:, :F], taps[:, F:], prev, w_down.astype(BF16), final_g)


def _trunk(x, pos0, st, w):
    B, T, D = x.shape
    pos = pos0 + jnp.arange(T, dtype=jnp.int32)
    new = {name: [] for name in ('a_k', 'a_v', 'a_idx', 'b_wkv', 'b_shift', 'c_lat', 'c_rope', 'ffn')}
    for i in range(DEPTH):
        j = i // N_MIXERS
        kind = i % N_MIXERS
        if kind == 0:
            x, k, v, ki = _dsa_layer(x, pos, st['a_k'][j], st['a_v'][j], st['a_idx'][j], w['n_mix'][i],
                                     w['a_w_in'][j], w['a_w_out'][j])
            new['a_k'].append(k)
            new['a_v'].append(v)
            new['a_idx'].append(ki)
        elif kind == 1:
            x, shift, S = _rwkv_layer(x, st['b_shift'][j], st['b_wkv'][j], w['n_mix'][i], w['b_mu'][j],
                                      w['b_w_rkv'][j], w['b_w0'][j], w['b_w1'][j], w['b_w2'][j], w['b_a0'][j],
                                      w['b_a1'][j], w['b_a2'][j], w['b_g1'][j], w['b_g2'][j], w['b_k_k'][j],
                                      w['b_k_a'][j], w['b_r_k'][j], w['b_ln_w'][j], w['b_ln_b'][j],
                                      w['b_w_out'][j])
            new['b_shift'].append(shift)
            new['b_wkv'].append(S)
        else:
            x, lat, kpe = _mla_layer(x, pos, st['c_lat'][j], st['c_rope'][j], w['n_mix'][i], w['c_w_in'][j],
                                     w['c_g_q'][j], w['c_g_kv'][j], w['c_w_uq'][j], w['c_w_ukv'][j],
                                     w['c_w_out'][j])
            new['c_lat'].append(lat)
            new['c_rope'].append(kpe)
        last = i == DEPTH - 1
        res = _ffn_layer(x, st['ffn'][i], w['n_ffn'][i], w['f_w_up'][i], w['f_w_conv'][i],
                         w['f_b_conv'][i], w['f_w_down'][i], w['n_final'] if last else None)
        x = res[0]
        new['ffn'].append(res[1])
    return res[2], {name: jnp.stack(rows, axis=0) for name, rows in new.items()}


def kernel(x_prompt, x_sample, cache_a_k, cache_a_v, cache_a_idx, state_b_wkv, state_b_shift,
           cache_c_latent, cache_c_rope, state_ffn_conv, n_mix, n_ffn, n_final, a_w_in, a_w_out,
           b_mu, b_w_rkv, b_w0, b_w1, b_w2, b_a0, b_a1, b_a2, b_g1, b_g2, b_k_k, b_k_a, b_r_k,
           b_ln_w, b_ln_b, b_w_out, c_w_in, c_g_q, c_g_kv, c_w_uq, c_w_ukv, c_w_out,
           f_w_up, f_w_conv, f_b_conv, f_w_down):
    w = dict(n_mix=n_mix, n_ffn=n_ffn, n_final=n_final, a_w_in=a_w_in, a_w_out=a_w_out,
             b_mu=b_mu, b_w_rkv=b_w_rkv, b_w0=b_w0, b_w1=b_w1, b_w2=b_w2, b_a0=b_a0, b_a1=b_a1,
             b_a2=b_a2, b_g1=b_g1, b_g2=b_g2, b_k_k=b_k_k, b_k_a=b_k_a, b_r_k=b_r_k,
             b_ln_w=b_ln_w, b_ln_b=b_ln_b, b_w_out=b_w_out, c_w_in=c_w_in, c_g_q=c_g_q,
             c_g_kv=c_g_kv, c_w_uq=c_w_uq, c_w_ukv=c_w_ukv, c_w_out=c_w_out,
             f_w_up=f_w_up, f_w_conv=f_w_conv, f_b_conv=f_b_conv, f_w_down=f_w_down)
    Bp, Tp, D = x_prompt.shape
    n_a, n_b, n_c = cache_a_k.shape[0], state_b_wkv.shape[0], cache_c_latent.shape[0]
    st_prompt = dict(
        a_k=jnp.zeros((n_a, Bp, 0, A_KV_HEADS, A_HEAD_DIM), F32),
        a_v=jnp.zeros((n_a, Bp, 0, A_KV_HEADS, A_HEAD_DIM), F32),
        a_idx=jnp.zeros((n_a, Bp, 0, IDX_DIM), F32),
        b_wkv=jnp.zeros((n_b, Bp, B_HEADS, B_HEAD_DIM, B_HEAD_DIM), F32),
        b_shift=jnp.zeros((n_b, Bp, D), F32),
        c_lat=jnp.zeros((n_c, Bp, 0, C_KV_RANK), F32),
        c_rope=jnp.zeros((n_c, Bp, 0, C_ROPE), F32),
        ffn=jnp.zeros((DEPTH, Bp, CONV_W - 1, 2 * D_FF), F32))
    st_sample = dict(a_k=cache_a_k, a_v=cache_a_v, a_idx=cache_a_idx, b_wkv=state_b_wkv,
                     b_shift=state_b_shift, c_lat=cache_c_latent, c_rope=cache_c_rope,
                     ffn=state_ffn_conv)
    y_prompt, sp = _trunk(x_prompt, 0, st_prompt, w)
    y_sample, ss = _trunk(x_sample, cache_a_k.shape[2], st_sample, w)
    return (y_prompt, y_sample,
            sp['a_k'], ss['a_k'], sp['a_v'], ss['a_v'], sp['a_idx'], ss['a_idx'],
            sp['b_wkv'], ss['b_wkv'], sp['b_shift'], ss['b_shift'],
            sp['c_lat'], ss['c_lat'], sp['c_rope'], ss['c_rope'],
            sp['ffn'], ss['ffn'])
```

```python
import functools

import jax
import jax.numpy as jnp
from jax import lax
from jax.experimental import pallas as pl
from jax.experimental.pallas import tpu as pltpu

F32 = jnp.float32
BF16 = jnp.bfloat16

D_MODEL = 1024
DEPTH = 4
CHUNK = 64
N_MIXERS = 3
NORM_EPS = 1e-6
ROPE_THETA = 500000.0
A_HEADS, A_HEAD_DIM, A_KV_HEADS = 16, 64, 2
A_ROT = A_HEAD_DIM // 4
IDX_HEADS, IDX_DIM = 8, 64
IDX_ROT = IDX_DIM // 4
TOPK_MAX = 256
A_O_Q = A_HEADS * A_HEAD_DIM
A_O_K = A_O_Q + A_KV_HEADS * A_HEAD_DIM
A_O_V = A_O_K + A_KV_HEADS * A_HEAD_DIM
A_O_QI = A_O_V + IDX_HEADS * IDX_DIM
A_O_KI = A_O_QI + IDX_DIM
A_IN = A_O_KI + IDX_HEADS
B_HEAD_DIM = 64
B_HEADS = D_MODEL // B_HEAD_DIM
B_GN_EPS = 64e-5
C_HEADS, C_NOPE, C_ROPE, C_V = 16, 64, 32, 64
C_Q_RANK, C_KV_RANK = 512, 256
D_FF = 2816
CONV_W = 3

LANE = 128
SUBLANE_BF16 = 16
VMEM_LIMIT = 56 * 1024 * 1024
NEG_INF = float("-inf")
LOG2E = 1.4426950408889634


def _round_up(n, m):
    return (n + m - 1) // m * m


def _row_tile(M, pref):
    t = min(pref, M)
    while M % t:
        t //= 2
    return t


def _cparams(sem):
    return pltpu.CompilerParams(dimension_semantics=sem, vmem_limit_bytes=VMEM_LIMIT)


def _norm_kernel(x_ref, g_ref, o_ref):
    xf = x_ref[...]
    o_ref[...] = xf * lax.rsqrt(jnp.mean(xf * xf, axis=-1, keepdims=True) + NORM_EPS) * g_ref[...]


def _norm(x, g, tm=512):
    M, K = x.shape
    tm = _row_tile(M, tm)
    return pl.pallas_call(
        _norm_kernel,
        grid=(M // tm,),
        in_specs=[pl.BlockSpec((tm, K), lambda i: (i, 0)), pl.BlockSpec((1, K), lambda i: (0, 0))],
        out_specs=pl.BlockSpec((tm, K), lambda i: (i, 0)),
        out_shape=jax.ShapeDtypeStruct((M, K), F32),
        compiler_params=_cparams(("parallel",)),
        name="rmsnorm",
    )(x, g.reshape(1, K))


ATTN_KB = 512
ATTN_KB_SHIFT = ATTN_KB.bit_length() - 1
SELECT_ALL = 1e9
MANTISSA_BITS = 23
ATTN_ROWS_PER_ITER = 1024
TOPK_PART_ROWS = 64


def _for_key_blocks(nk, body, init):
    carry = lax.fori_loop(0, nk // 2, lambda i, c: body(2 * i + 1, body(2 * i, c)), init)
    return lax.cond(nk % 2 == 1, lambda c: body(nk - 1, c), lambda c: c, carry)


def _fold(x, op=jnp.add):
    acc = x[:, :LANE]
    for j in range(1, x.shape[1] // LANE):
        acc = op(acc, x[:, j * LANE:(j + 1) * LANE])
    return acc


def _topk_bias_t(tab_ref, qi_ref, wi_ref, ki_ref, SC, MS, *, nk, first, tq, top, wi_off):
    kb = ATTN_KB
    qi = qi_ref[0]
    w_t = jnp.transpose(wi_ref[0])[wi_off:wi_off + IDX_HEADS, :]
    part = TOPK_PART_ROWS
    zeros_p = jnp.zeros((part, tq), F32)
    zero = jnp.zeros((1, tq), F32)
    kf = float(top)
    pos_q = lax.broadcasted_iota(jnp.int32, (1, tq), 1) + first
    limit = (pos_q & ~(CHUNK - 1)) + CHUNK
    key_i = lax.broadcasted_iota(jnp.int32, (kb, tq), 0)

    def key_sum(body):
        def step(j, a):
            return a + jnp.sum(body(j).reshape(kb // part, part, tq), axis=0)

        return jnp.sum(_for_key_blocks(nk, step, zeros_p), axis=0, keepdims=True)

    def ones_where(c):
        return jnp.where(c, 1.0, 0.0)

    def score_block(j):
        kij = ki_ref[0, pl.ds(pl.multiple_of(j * kb, kb), kb), :]
        sc = jnp.zeros((kb, tq), F32)
        for h in range(IDX_HEADS):
            d = lax.dot_general(kij, qi[:, h * IDX_DIM:(h + 1) * IDX_DIM], (((1,), (1,)), ((), ())),
                                preferred_element_type=F32)
            sc = sc + w_t[h:h + 1, :] * jnp.maximum(d, 0.0)
        sc = jnp.where(key_i + j * kb < limit, sc, NEG_INF)
        SC[j] = sc
        return ones_where(sc >= 0.0)

    c0 = key_sum(score_block)
    neg = c0 < kf
    sgn = jnp.where(neg, -1.0, 1.0)
    kp = jnp.where(neg, kf, (nk * kb).astype(F32) - kf + 1.0)

    def flip(j, c):
        SC[j] = SC[j] * sgn
        return c

    _for_key_blocks(nk, flip, 0)

    def count_lt(cand):
        return key_sum(lambda j: ones_where(SC[j] < cand))

    def exp_step(i, carry):
        e_cur, t_cur = carry
        cand = jnp.where(e_cur == 0.0, tab_ref[0, i], t_cur * tab_ref[1, i])
        ok = count_lt(cand) < kp
        return jnp.where(ok, e_cur + tab_ref[2, i], e_cur), jnp.where(ok, cand, t_cur)

    _, t_pow = lax.fori_loop(0, 8, exp_step, (zero, zero))

    def man_step(i, carry):
        t_cur, frac = carry
        cand = t_cur + frac
        ok = count_lt(cand) < kp
        return jnp.where(ok, cand, t_cur), frac * 0.5

    t_cur, _ = lax.fori_loop(0, MANTISSA_BITS, man_step, (t_pow, t_pow * 0.5))
    thr = t_cur * sgn

    def score(j):
        return SC[j] * sgn

    def key_idx(j):
        return (key_i + j * kb).astype(F32)

    need = kf - key_sum(lambda j: ones_where(score(j) > thr))
    n_eq = key_sum(lambda j: ones_where(score(j) == thr))

    def index_cut():
        nbits = (SC.shape[0] * kb - 1).bit_length()

        def bit_step(i, carry):
            c_cur, bit = carry
            cand = c_cur + bit
            ok = key_sum(lambda j: ones_where((score(j) == thr) & (key_idx(j) < cand))) < need
            return jnp.where(ok, cand, c_cur), bit * 0.5

        c_cur, _ = lax.fori_loop(0, nbits, bit_step, (zero, jnp.full((1, tq), 2.0 ** (nbits - 1), F32)))
        return c_cur

    any_split = jnp.max(ones_where(n_eq > need)) > 0.0
    c_cut = lax.cond(any_split, index_cut, lambda: jnp.full((1, tq), SELECT_ALL, F32))

    def write_bias(j, c):
        s = score(j)
        sel = (s > thr) | ((s == thr) & (key_idx(j) <= c_cut))
        MS[j] = jnp.transpose(jnp.where(sel & (key_i + j * kb < limit), 0.0, NEG_INF))
        return c

    _for_key_blocks(nk, write_bias, 0)


def _attn2_kernel(*refs, n_kv, group, tq, pos0, top, indexer, wi_off):
    if indexer:
        (tab_ref, q_ref, k_ref, v_ref, x_ref, wo_ref, qi_ref, wi_ref, ki_ref, xo_ref,
         MS, LG, MACC, LACC, OACC, OH, OALL, SC) = refs
    else:
        q_ref, k_ref, v_ref, x_ref, wo_ref, xo_ref, MS, LG, MACC, LACC, OACC, OH, OALL = refs
    kb = ATTN_KB
    first = pos0 + pl.program_id(1) * tq
    last_limit = ((first + tq - 1) & ~(CHUNK - 1)) + CHUNK
    nk = (last_limit + (kb - 1)) >> ATTN_KB_SHIFT
    row = lax.broadcasted_iota(jnp.int32, (tq, 1), 0) + first
    limit = (row & ~(CHUNK - 1)) + CHUNK
    lane_i = lax.broadcasted_iota(jnp.int32, (tq, kb), 1)

    def causal_bias(j, c):
        MS[j] = jnp.where(lane_i + j * kb < limit, 0.0, NEG_INF)
        return c

    if indexer:
        @pl.when(last_limit > top)
        def _():
            _topk_bias_t(tab_ref, qi_ref, wi_ref, ki_ref, SC, MS, nk=nk, first=first, tq=tq, top=top, wi_off=wi_off)

        @pl.when(last_limit <= top)
        def _():
            _for_key_blocks(nk, causal_bias, 0)
    else:
        _for_key_blocks(nk, causal_bias, 0)

    hu = LG.shape[0]

    def per_kv_heads(gi, c):
        heads = [gi * hu + u for u in range(hu)]
        qs = [q_ref[0, g, 0] for g in heads]
        MACC[...] = jnp.full(MACC.shape, NEG_INF, F32)
        LACC[...] = jnp.zeros(LACC.shape, F32)
        OACC[...] = jnp.zeros(OACC.shape, F32)

        def logits_block(j, c_):
            keys = pl.ds(pl.multiple_of(j * kb, kb), kb)
            bias = MS[j][None]
            for u, g in enumerate(heads):
                lg = lax.dot_general(qs[u], k_ref[0, g, keys, :], (((1,), (1,)), ((), ())),
                                     preferred_element_type=F32)
                lg = (lg.reshape(group, tq, kb) + bias).reshape(group * tq, kb)
                LG[u, j] = lg
                MACC[u] = jnp.maximum(MACC[u], _fold(lg, jnp.maximum))
            return c_

        _for_key_blocks(nk, logits_block, 0)
        ms = [jnp.max(MACC[u], axis=-1, keepdims=True) for u in range(hu)]

        def value_block(j, c_):
            keys = pl.ds(pl.multiple_of(j * kb, kb), kb)
            for u, g in enumerate(heads):
                p = jnp.exp2(LG[u, j] - ms[u])
                LACC[u] += _fold(p)
                OACC[u] += jnp.dot(p.astype(BF16), v_ref[0, g, keys, :], preferred_element_type=F32)
            return c_

        _for_key_blocks(nk, value_block, 0)
        for u, g in enumerate(heads):
            OH[g] = (OACC[u] / jnp.sum(LACC[u], axis=-1, keepdims=True)).astype(OH.dtype)
        return c

    lax.fori_loop(0, n_kv // hu, per_kv_heads, 0)
    dv = OH.shape[2]
    for h in range(n_kv * group):
        OALL[:, h * dv:(h + 1) * dv] = OH[h // group, (h % group) * tq:(h % group + 1) * tq, :]
    xo_ref[0] = x_ref[0] + jnp.dot(OALL[...], wo_ref[...], preferred_element_type=F32)


def _attn2(qg, kt, vt, x, w_out, *, pos0, tq, group, idx=None, top=0, wi_off=0):
    B, n_kv, nq, rows, dq = qg.shape
    L, dv = kt.shape[2], vt.shape[3]
    T, D = x.shape[1], x.shape[2]
    kb = ATTN_KB
    assert nq * tq == T and rows == group * tq and L % kb == 0 and (idx is None or top <= kb)
    ins = [qg, kt, vt, x, w_out]
    specs = [pl.BlockSpec((1, n_kv, 1, rows, dq), lambda b, i: (b, 0, i, 0, 0)),
             pl.BlockSpec((1, n_kv, L, dq), lambda b, i: (b, 0, 0, 0)),
             pl.BlockSpec((1, n_kv, L, dv), lambda b, i: (b, 0, 0, 0)),
             pl.BlockSpec((1, tq, D), lambda b, i: (b, i, 0)),
             pl.BlockSpec(w_out.shape, lambda b, i: (0, 0))]
    hu = max(1, min(n_kv, ATTN_ROWS_PER_ITER // rows))
    assert n_kv % hu == 0
    scratch = [pltpu.VMEM((L // kb, tq, kb), F32), pltpu.VMEM((hu, L // kb, rows, kb), F32),
               pltpu.VMEM((hu, rows, LANE), F32), pltpu.VMEM((hu, rows, LANE), F32),
               pltpu.VMEM((hu, rows, dv), F32), pltpu.VMEM((n_kv, rows, dv), BF16),
               pltpu.VMEM((tq, n_kv * group * dv), BF16)]
    if idx is not None:
        qi, wi, ki = idx
        steps = [2 ** b for b in range(7, -1, -1)]
        tab = jnp.array([[2.0 ** (s - 127) for s in steps], [2.0 ** s if s < 128 else 1.0 for s in steps],
                         [float(s) for s in steps]], F32)
        ins = [tab] + ins + [qi, wi, ki]
        specs = ([pl.BlockSpec(memory_space=pltpu.SMEM)] + specs
                 + [pl.BlockSpec((1, tq, qi.shape[2]), lambda b, i: (b, i, 0)),
                    pl.BlockSpec((1, tq, wi.shape[2]), lambda b, i: (b, i, 0)),
                    pl.BlockSpec((1, L, ki.shape[2]), lambda b, i: (b, 0, 0))])
        scratch.append(pltpu.VMEM((L // kb, kb, tq), F32))
    return pl.pallas_call(
        functools.partial(_attn2_kernel, n_kv=n_kv, group=group, tq=tq, pos0=pos0, top=top,
                          indexer=idx is not None, wi_off=wi_off),
        grid=(B, nq),
        in_specs=specs,
        out_specs=pl.BlockSpec((1, tq, D), lambda b, i: (b, i, 0)),
        out_shape=jax.ShapeDtypeStruct((B, T, D), F32),
        scratch_shapes=scratch,
        compiler_params=_cparams(("parallel", "parallel")),
        name="dsa_attn" if idx is not None else "mla_attn",
    )(*ins)


def _rope_tables(pos, dh, rot, offset=0):
    half = rot // 2
    inv = ROPE_THETA ** (-jnp.arange(half, dtype=F32) / half)
    ang = pos.astype(F32)[:, None] * inv[None, :]
    cos, sin = jnp.cos(ang), jnp.sin(ang)
    T = pos.shape[0]
    pad = lambda n, v: jnp.full((T, n), v, F32)
    zh = pad(half, 0.0)
    lo, hi = offset, dh - offset - rot
    c = jnp.concatenate([pad(lo, 1.0), cos, cos, pad(hi, 1.0)], axis=1)
    s1 = jnp.concatenate([pad(lo, 0.0), -sin, zh, pad(hi, 0.0)], axis=1)
    s2 = jnp.concatenate([pad(lo, 0.0), zh, sin, pad(hi, 0.0)], axis=1)
    return tuple(jnp.tile(t, (1, LANE // dh)) for t in (c, s1, s2))


def _rope_lanes(x, c, s1, s2, half):
    return x * c + pltpu.roll(x, LANE - half, 1) * s1 + pltpu.roll(x, half, 1) * s2


def _dsa_proj_kernel(x_ref, g_ref, w_ref, c_ref, s1_ref, s2_ref, ts_ref, qh_ref, kf_ref, kt_ref, vf_ref, vt_ref,
                     qi_ref, tail_ref, kib_ref, *, tm):
    xf = x_ref[0]
    h = xf * lax.rsqrt(jnp.mean(xf * xf, axis=-1, keepdims=True) + NORM_EPS) * g_ref[...]
    acc = jnp.dot(h.astype(BF16), w_ref[...], preferred_element_type=F32)
    c, s1, s2 = c_ref[...], s1_ref[...], s2_ref[...]
    half = A_ROT // 2
    hd = A_HEAD_DIM
    group = A_HEADS // A_KV_HEADS
    rope = lambda xs: _rope_lanes(xs, c, s1, s2, half)
    slab = lambda off: acc[:, off:off + LANE]
    for s in range(A_O_Q // LANE):
        qs = (rope(slab(s * LANE)) * (hd ** -0.5 * LOG2E)).astype(BF16)
        for e in range(LANE // hd):
            head = s * (LANE // hd) + e
            u = head % group
            qh_ref[0, head // group, 0, u * tm:(u + 1) * tm, :] = qs[:, e * hd:(e + 1) * hd]
    ks = rope(slab(A_O_Q))
    vs = slab(A_O_K)
    kf_ref[0] = ks
    vf_ref[0] = vs
    for e in range(A_KV_HEADS):
        kt_ref[0, e] = ks[:, e * hd:(e + 1) * hd].astype(BF16)
        vt_ref[0, e] = vs[:, e * hd:(e + 1) * hd].astype(BF16)
    for s in range(IDX_HEADS * IDX_DIM // LANE):
        qi_ref[0, :, s * LANE:(s + 1) * LANE] = rope(slab(A_O_V + s * LANE)).astype(BF16)
    is_key = lax.broadcasted_iota(jnp.int32, (tm, LANE), 1) < IDX_DIM
    tl = _rope_lanes(slab(A_O_QI), jnp.where(is_key, c, 1.0), jnp.where(is_key, s1, 0.0),
                     jnp.where(is_key, s2, 0.0), half) * ts_ref[...]
    tail_ref[0] = tl
    kib_ref[0] = tl[:, :IDX_DIM].astype(BF16)


def _dsa_proj(x, g, w_in, tabs, tm):
    B, T, D = x.shape
    assert (A_HEAD_DIM, A_ROT) == (IDX_DIM, IDX_ROT) and A_KV_HEADS * A_HEAD_DIM == LANE
    assert A_O_QI % LANE == 0 and IDX_DIM + IDX_HEADS <= LANE and T % tm == 0
    n_in = _round_up(A_IN, LANE)
    group = A_HEADS // A_KV_HEADS
    hd = A_HEAD_DIM
    nq = T // tm
    lanes = jnp.arange(LANE)
    tail_scale = jnp.where(lanes < IDX_DIM, 1.0, jnp.where(lanes < IDX_DIM + IDX_HEADS,
                                                            (IDX_HEADS * IDX_DIM) ** -0.5, 0.0)).astype(F32)
    row = lambda n: pl.BlockSpec((1, tm, n), lambda b, i: (b, i, 0))
    tab = pl.BlockSpec((tm, LANE), lambda b, i: (i, 0))
    const = lambda a: pl.BlockSpec(a.shape, lambda b, i: (0,) * a.ndim)
    kvh = pl.BlockSpec((1, A_KV_HEADS, tm, hd), lambda b, i: (b, 0, i, 0))
    w = _pad_cols(w_in, n_in).astype(BF16)
    g2 = g.reshape(1, D)
    ts = tail_scale.reshape(1, LANE)
    return pl.pallas_call(
        functools.partial(_dsa_proj_kernel, tm=tm),
        grid=(B, nq),
        in_specs=[row(D), const(g2), const(w), tab, tab, tab, const(ts)],
        out_specs=[pl.BlockSpec((1, A_KV_HEADS, 1, group * tm, hd), lambda b, i: (b, 0, i, 0, 0)),
                   row(LANE), kvh, row(LANE), kvh, row(IDX_HEADS * IDX_DIM), row(LANE), row(IDX_DIM)],
        out_shape=[jax.ShapeDtypeStruct((B, A_KV_HEADS, nq, group * tm, hd), BF16),
                   jax.ShapeDtypeStruct((B, T, LANE), F32), jax.ShapeDtypeStruct((B, A_KV_HEADS, T, hd), BF16),
                   jax.ShapeDtypeStruct((B, T, LANE), F32), jax.ShapeDtypeStruct((B, A_KV_HEADS, T, hd), BF16),
                   jax.ShapeDtypeStruct((B, T, IDX_HEADS * IDX_DIM), BF16),
                   jax.ShapeDtypeStruct((B, T, LANE), F32), jax.ShapeDtypeStruct((B, T, IDX_DIM), BF16)],
        compiler_params=_cparams(("parallel", "parallel")),
        name="dsa_proj",
    )(x, g2, w, *tabs, ts)


def _rms(xf, gain):
    return xf * lax.rsqrt(jnp.mean(xf * xf, axis=-1, keepdims=True) + NORM_EPS) * gain


def _mla_write_kv(lat, kpe_slab, wuk_ref, wuv_ref, kt_ref, vt_ref):
    lb = lat.astype(BF16)
    kn = jnp.dot(lb, wuk_ref[...], preferred_element_type=F32)
    vv = jnp.dot(lb, wuv_ref[...], preferred_element_type=F32)
    for h in range(C_HEADS):
        kt_ref[0, h] = (kn[:, h * LANE:(h + 1) * LANE] + kpe_slab).astype(BF16)
        vt_ref[0, h] = vv[:, h * C_V:(h + 1) * C_V].astype(BF16)


def _mla_proj_kernel(x_ref, g_ref, win_ref, gq_ref, gkv_ref, wuq_ref, wuk_ref, wuv_ref, c_ref, s1_ref, s2_ref,
                     qh_ref, kt_ref, vt_ref, lat_ref, kpe_ref):
    h = _rms(x_ref[0], g_ref[...]).astype(BF16)
    proj = jnp.dot(h, win_ref[...], preferred_element_type=F32)
    c, s1, s2 = c_ref[...], s1_ref[...], s2_ref[...]
    half = C_ROPE // 2
    q = jnp.dot(_rms(proj[:, :C_Q_RANK], gq_ref[...]).astype(BF16), wuq_ref[...], preferred_element_type=F32)
    scale = (C_NOPE + C_ROPE) ** -0.5 * LOG2E
    for hd in range(C_HEADS):
        qh_ref[0, hd, 0] = (_rope_lanes(q[:, hd * LANE:(hd + 1) * LANE], c, s1, s2, half) * scale).astype(BF16)
    lat = _rms(proj[:, C_Q_RANK:C_Q_RANK + C_KV_RANK], gkv_ref[...])
    lat_ref[0] = lat
    kpe_slab = _rope_lanes(proj[:, C_Q_RANK + C_KV_RANK:], c, s1, s2, half)
    kpe_ref[0] = kpe_slab[:, C_NOPE:C_NOPE + C_ROPE]
    _mla_write_kv(lat, kpe_slab, wuk_ref, wuv_ref, kt_ref, vt_ref)


def _mla_kv_kernel(lat_ref, kpe_ref, wuk_ref, wuv_ref, kt_ref, vt_ref):
    _mla_write_kv(lat_ref[0], kpe_ref[0], wuk_ref, wuv_ref, kt_ref, vt_ref)


def _mla_weights(w_in, w_uq, w_ukv):
    D = w_in.shape[0]
    zc = lambda rows, n: jnp.zeros((rows, n), w_in.dtype)
    w_in2 = jnp.concatenate([w_in[:, :C_Q_RANK + C_KV_RANK], zc(D, C_NOPE), w_in[:, C_Q_RANK + C_KV_RANK:],
                             zc(D, LANE - C_NOPE - C_ROPE)], axis=1)
    pad_heads = lambda w, d: jnp.pad(w.reshape(w.shape[0], C_HEADS, d), ((0, 0), (0, 0), (0, LANE - d))).reshape(
        w.shape[0], C_HEADS * LANE)
    w_uq2 = pad_heads(w_uq, C_NOPE + C_ROPE)
    ukv = w_ukv.reshape(C_KV_RANK, C_HEADS, C_NOPE + C_V)
    w_uk2 = pad_heads(ukv[..., :C_NOPE].reshape(C_KV_RANK, C_HEADS * C_NOPE), C_NOPE)
    w_uv2 = ukv[..., C_NOPE:].reshape(C_KV_RANK, C_HEADS * C_V)
    return tuple(a.astype(BF16) for a in (w_in2, w_uq2, w_uk2, w_uv2))


def _mla_proj(x, g, g_q, g_kv, weights, tabs, tm):
    B, T, D = x.shape
    w_in2, w_uq2, w_uk2, w_uv2 = weights
    nq = T // tm
    row = lambda n: pl.BlockSpec((1, tm, n), lambda b, i: (b, i, 0))
    tab = pl.BlockSpec((tm, LANE), lambda b, i: (i, 0))
    const = lambda a: pl.BlockSpec(a.shape, lambda b, i: (0,) * a.ndim)
    heads = lambda d: pl.BlockSpec((1, C_HEADS, tm, d), lambda b, i: (b, 0, i, 0))
    vec = lambda a: a.reshape(1, a.shape[0])
    return pl.pallas_call(
        _mla_proj_kernel,
        grid=(B, nq),
        in_specs=[row(D), const(vec(g)), const(w_in2), const(vec(g_q)), const(vec(g_kv)), const(w_uq2),
                  const(w_uk2), const(w_uv2), tab, tab, tab],
        out_specs=[pl.BlockSpec((1, C_HEADS, 1, tm, LANE), lambda b, i: (b, 0, i, 0, 0)), heads(LANE), heads(C_V),
                   row(C_KV_RANK), row(C_ROPE)],
        out_shape=[jax.ShapeDtypeStruct((B, C_HEADS, nq, tm, LANE), BF16),
                   jax.ShapeDtypeStruct((B, C_HEADS, T, LANE), BF16), jax.ShapeDtypeStruct((B, C_HEADS, T, C_V), BF16),
                   jax.ShapeDtypeStruct((B, T, C_KV_RANK), F32), jax.ShapeDtypeStruct((B, T, C_ROPE), F32)],
        compiler_params=_cparams(("parallel", "parallel")),
        name="mla_proj",
    )(x, vec(g), w_in2, vec(g_q), vec(g_kv), w_uq2, w_uk2, w_uv2, *tabs)


def _mla_kv(lat, kpe_slab, w_uk2, w_uv2, tm):
    B, P, _ = lat.shape
    row = lambda n: pl.BlockSpec((1, tm, n), lambda b, i: (b, i, 0))
    const = lambda a: pl.BlockSpec(a.shape, lambda b, i: (0,) * a.ndim)
    heads = lambda d: pl.BlockSpec((1, C_HEADS, tm, d), lambda b, i: (b, 0, i, 0))
    return pl.pallas_call(
        _mla_kv_kernel,
        grid=(B, P // tm),
        in_specs=[row(C_KV_RANK), row(LANE), const(w_uk2), const(w_uv2)],
        out_specs=[heads(LANE), heads(C_V)],
        out_shape=[jax.ShapeDtypeStruct((B, C_HEADS, P, LANE), BF16), jax.ShapeDtypeStruct((B, C_HEADS, P, C_V), BF16)],
        compiler_params=_cparams(("parallel", "parallel")),
        name="mla_kv",
    )(lat, kpe_slab, w_uk2, w_uv2)


def _head_sum(x, bo_ref, split=True):
    bw = bo_ref.shape[0]
    bo = bo_ref[...]
    hi = x.astype(BF16)
    terms = [hi, (x - hi.astype(F32)).astype(BF16)] if split else [hi]
    return jnp.concatenate(
        [sum(jnp.dot(t[:, j * bw:(j + 1) * bw], bo, preferred_element_type=F32) for t in terms)
         for j in range(x.shape[1] // bw)], axis=-1)


RWKV_HALO = 8


def _rwkv_pre_kernel(x_ref, xh_ref, sh_ref, g_ref, mu_ref, vec_ref, wr_ref, wk_ref, wv_ref, w1_ref, w2_ref,
                     a1_ref, a2_ref, g1_ref, g2_ref, bo_ref,
                     rp_ref, w_ref, k_ref, v_ref, a_ref, b_ref, yc_ref, bonus_ref, gate_ref, *, tm):
    i = pl.program_id(1)
    gain = g_ref[...]

    def norm(xf):
        return xf * lax.rsqrt(jnp.mean(xf * xf, axis=-1, keepdims=True) + NORM_EPS) * gain

    h = norm(x_ref[0])
    before = jnp.where(i > 0, norm(xh_ref[0])[RWKV_HALO - 1:RWKV_HALO, :], sh_ref[0])
    first = lax.broadcasted_iota(jnp.int32, (tm, 1), 0) == 0
    xx = jnp.where(first, before, pltpu.roll(h, 1, 0)) - h
    mu = mu_ref[...]
    vec = vec_ref[...]
    w0, a0, k_k, k_a, r_k = (vec[j:j + 1, :] for j in range(5))

    def mix(j):
        return (h + xx * mu[j:j + 1, :]).astype(BF16)

    dot = lambda a_, w_: jnp.dot(a_, w_[...], preferred_element_type=F32)
    r = dot(mix(0), wr_ref)
    wl = dot(jnp.tanh(dot(mix(1), w1_ref)).astype(BF16), w2_ref)
    k = dot(mix(2), wk_ref)
    v = dot(mix(3), wv_ref)
    al = dot(dot(mix(4), a1_ref).astype(BF16), a2_ref)
    gate_ref[0] = dot(jax.nn.sigmoid(dot(mix(5), g1_ref)).astype(BF16), g2_ref)
    z = -(w0 + wl)
    softplus = jnp.maximum(z, 0.0) + jnp.log(1.0 + jnp.exp(-jnp.abs(z)))
    decay = jnp.exp(-jnp.exp(-softplus - 0.5))
    a = jax.nn.sigmoid(a0 + al)
    kk = k * k_k
    kk = kk / jnp.maximum(jnp.sqrt(_head_sum(kk * kk, bo_ref)), 1e-12)
    k = k * (1.0 + (a - 1.0) * k_a)
    b = kk * a
    rp_ref[0] = decay * r - kk * _head_sum(b * r, bo_ref, split=False)
    w_ref[0] = decay
    k_ref[0] = k
    for hh in range(B_HEADS):
        v_ref[0, :, hh, :] = v[:, hh * B_HEAD_DIM:(hh + 1) * B_HEAD_DIM]
    a_ref[0] = -kk
    b_ref[0] = b
    yc_ref[0] = v * _head_sum(k * r, bo_ref, split=False)
    bonus_ref[0] = _head_sum(r * k * r_k, bo_ref, split=False) * v


def _rwkv_post_kernel(y_ref, yc_ref, bonus_ref, gate_ref, x_ref, ln_ref, wo_ref, bo_ref, o_ref):
    n = float(B_HEAD_DIM)
    y = y_ref[0] + yc_ref[0]
    d = y - _head_sum(y, bo_ref) / n
    var = _head_sum(d * d, bo_ref, split=False) / n
    ln = ln_ref[...]
    yn = d * lax.rsqrt(var + B_GN_EPS) * ln[0:1, :] + ln[1:2, :] + bonus_ref[0]
    o_ref[0] = x_ref[0] + jnp.dot((yn * gate_ref[0]).astype(BF16), wo_ref[...], preferred_element_type=F32)


def _block_ones():
    blk = jnp.arange(WKV_BW) // B_HEAD_DIM
    return (blk[:, None] == blk[None, :]).astype(BF16)


def _rwkv_pre(x, shift_prev, g, mu, vecs, ws, tm):
    B, T, D = x.shape
    H = RWKV_HALO
    assert T % tm == 0 and tm % H == 0
    nh = tm // H
    bo = _block_ones()
    row = pl.BlockSpec((1, tm, D), lambda b, i: (b, i, 0))
    const = lambda a: pl.BlockSpec(a.shape, lambda b, i: (0,) * a.ndim)
    pad8 = lambda a: jnp.pad(a, ((0, 8 - a.shape[0]), (0, 0)))
    mu8, vec8, g2 = pad8(mu), pad8(vecs), g.reshape(1, D)
    return pl.pallas_call(
        functools.partial(_rwkv_pre_kernel, tm=tm),
        grid=(B, T // tm),
        in_specs=[row, pl.BlockSpec((1, H, D), lambda b, i: (b, jnp.maximum(i * nh - 1, 0), 0)),
                  pl.BlockSpec((1, 1, D), lambda b, i: (b, 0, 0)), const(g2), const(mu8), const(vec8)]
                 + [const(a) for a in ws] + [const(bo)],
        out_specs=[row] * 3 + [pl.BlockSpec((1, tm, B_HEADS, B_HEAD_DIM), lambda b, i: (b, i, 0, 0))] + [row] * 5,
        out_shape=[jax.ShapeDtypeStruct((B, T, D), F32)] * 3
                  + [jax.ShapeDtypeStruct((B, T, B_HEADS, B_HEAD_DIM), F32)] + [jax.ShapeDtypeStruct((B, T, D), F32)] * 5,
        compiler_params=_cparams(("parallel", "parallel")),
        name="rwkv_pre",
    )(x, x, shift_prev.reshape(B, 1, D), g2, mu8, vec8, *ws, bo)


def _rwkv_post(y, yc, bonus, gate, x, ln, w_out, tm):
    B, T, D = x.shape
    bo = _block_ones()
    row = pl.BlockSpec((1, tm, D), lambda b, i: (b, i, 0))
    const = lambda a: pl.BlockSpec(a.shape, lambda b, i: (0,) * a.ndim)
    return pl.pallas_call(
        _rwkv_post_kernel,
        grid=(B, T // tm),
        in_specs=[row] * 5 + [const(ln), const(w_out), const(bo)],
        out_specs=row,
        out_shape=jax.ShapeDtypeStruct((B, T, D), F32),
        compiler_params=_cparams(("parallel", "parallel")),
        name="rwkv_post",
    )(y, yc, bonus, gate, x, ln, w_out, bo)


WKV_SUB = 16
WKV_NB = 8
WKV_TC = 64
WKV_BW = 2 * LANE


def _wkv_kernel(rp_ref, w_ref, k_ref, vh_ref, a_ref, b_ref, s0_ref, gs_ref, hs_ref, y_ref, sT_ref, S, *, tc, nb):
    c = pl.program_id(1)
    H = B_HEADS

    @pl.when(c == 0)
    def _():
        S[...] = s0_ref[...]

    def block(sc, carry):
        base = pl.multiple_of(sc * WKV_SUB, WKV_SUB)
        rows = pl.ds(base, WKV_SUB)
        gs = gs_ref[...]
        hsel = hs_ref[...]
        ins = [tuple(x[bi, rows, :] for x in (rp_ref, w_ref, k_ref, a_ref, b_ref)) for bi in range(nb)]
        for u in range(WKV_SUB):
            sas = []
            for bi in range(nb):
                r8, w8, k8, a8, b8 = ins[bi]
                s = S[bi]
                pa = (s * a8[u:u + 1, :]).astype(BF16)
                sas.append(jnp.dot(pa, gs, preferred_element_type=F32))
                pr = (s * r8[u:u + 1, :]).astype(BF16)
                y_ref[bi, base + u] = lax.dot_general(hsel, pr, (((1,), (1,)), ((), ())),
                                                      preferred_element_type=F32)
            deltas = []
            for bi in range(nb):
                r8, w8, k8, a8, b8 = ins[bi]
                v_t = jnp.transpose(vh_ref[bi, base + u])
                lhs = jnp.concatenate([sas[bi][:, :H], v_t], axis=1).astype(BF16)
                rhs = jnp.concatenate([hsel * b8[u:u + 1, :].astype(BF16), hsel * k8[u:u + 1, :].astype(BF16)],
                                      axis=0)
                deltas.append(jnp.dot(lhs, rhs, preferred_element_type=F32))
            for bi in range(nb):
                r8, w8, k8, a8, b8 = ins[bi]
                S[bi] = S[bi] * w8[u:u + 1, :] + deltas[bi]
        return carry

    lax.fori_loop(0, tc // WKV_SUB, block, 0)

    @pl.when(c == pl.num_programs(1) - 1)
    def _():
        sT_ref[...] = S[...]


def _wkv(rp, w, k, v, a, b, s0):
    B, T, D = rp.shape
    n = B_HEAD_DIM
    H = B_HEADS
    nb = _row_tile(B, WKV_NB)
    tc = _row_tile(T, WKV_TC)
    assert tc % WKV_SUB == 0 and H <= LANE
    s0t = jnp.transpose(s0, (0, 2, 1, 3)).reshape(B, n, D)
    hsel = (jnp.arange(H)[:, None] == (jnp.arange(D) // n)[None, :]).astype(BF16)
    gs = (jnp.arange(D)[:, None] // n == jnp.arange(LANE)[None, :]).astype(BF16)
    seq = pl.BlockSpec((nb, tc, D), lambda bi, c: (bi, c, 0))
    seqh = pl.BlockSpec((nb, tc, H, n), lambda bi, c: (bi, c, 0, 0))
    st = pl.BlockSpec((nb, n, D), lambda bi, c: (bi, 0, 0))
    const = lambda a_: pl.BlockSpec(a_.shape, lambda bi, c: (0, 0))
    y, sT = pl.pallas_call(
        functools.partial(_wkv_kernel, tc=tc, nb=nb),
        grid=(B // nb, T // tc),
        in_specs=[seq, seq, seq, seqh, seq, seq, st, const(gs), const(hsel)],
        out_specs=[seqh, st],
        out_shape=[jax.ShapeDtypeStruct((B, T, H, n), F32), jax.ShapeDtypeStruct((B, n, D), F32)],
        scratch_shapes=[pltpu.VMEM((nb, n, D), F32)],
        compiler_params=_cparams(("parallel", "arbitrary")),
        name="wkv",
    )(rp, w, k, v, a, b, s0t, gs, hsel)
    return y.reshape(B, T, D), jnp.transpose(sT.reshape(B, n, H, n), (0, 2, 1, 3))


FFN_HALO = SUBLANE_BF16


FFN_CW = 2 * LANE


def _ffn_body(x_ref, xh_ref, g_ref, wug_ref, wuv_ref, cg_ref, cv_ref, pg_ref, pv_ref, wd_ref, gf_ref,
              o_ref, st_ref, y_ref, hn, ug, uv, act, acc, *, tm, nb):
    i = pl.program_id(1)
    H = FFN_HALO
    cw_ = FFN_CW
    seg = tm + H

    def norm(xf, gain):
        return xf * lax.rsqrt(jnp.mean(xf * xf, axis=-1, keepdims=True) + NORM_EPS) * gain

    for s in range(nb):
        hn[s * seg:s * seg + H, :] = jnp.where(i > 0, norm(xh_ref[s], g_ref[...]), 0.0).astype(BF16)
        hn[s * seg + H:(s + 1) * seg, :] = norm(x_ref[s], g_ref[...]).astype(BF16)
    h = hn[...]
    first = jnp.where(i == 0, 1.0, 0.0)

    def conv(u, r0, taps):
        return (taps[3:4, :] + u[r0 - 2:r0 - 2 + tm, :] * taps[0:1, :] + u[r0 - 1:r0 - 1 + tm, :] * taps[1:2, :]
                + u[r0:r0 + tm, :] * taps[2:3, :])

    for c in range(D_FF // cw_):
        cols = slice(c * cw_, (c + 1) * cw_)
        for half, (u_scr, w_ref, p_ref) in enumerate(((ug, wug_ref, pg_ref), (uv, wuv_ref, pv_ref))):
            u = jnp.dot(h, w_ref[:, cols], preferred_element_type=F32)
            u_scr[c % 2] = u.astype(BF16)
            for s in range(nb):
                u_scr[c % 2, s * seg:s * seg + H, :] = (u[s * seg:s * seg + H, :]
                                                        + p_ref[s, :, cols] * first).astype(BF16)
                st_ref[s, 0, :, c * cw_ + half * D_FF:(c + 1) * cw_ + half * D_FF] = u[(s + 1) * seg - 8:(s + 1) * seg, :]
        for s in range(nb):
            gate = conv(ug.at[c % 2], s * seg + H, cg_ref[:, cols].astype(BF16))
            val = conv(uv.at[c % 2], s * seg + H, cv_ref[:, cols].astype(BF16))
            act[s * tm:(s + 1) * tm, :] = gate * jax.nn.sigmoid(gate) * val
        d = jnp.dot(act[...], wd_ref[cols, :], preferred_element_type=F32)
        if c == 0:
            acc[...] = d
        else:
            acc[...] += d
    for s in range(nb):
        out = x_ref[s] + acc[s * tm:(s + 1) * tm, :]
        o_ref[s] = out
        if gf_ref is not None:
            y_ref[s] = norm(out, gf_ref[...])


def _ffn_kernel(*refs, tm, nb, final):
    if final:
        (x_ref, xh_ref, g_ref, wug_ref, wuv_ref, cg_ref, cv_ref, pg_ref, pv_ref, wd_ref, gf_ref,
         o_ref, st_ref, y_ref, hn, ug, uv, act, acc) = refs
    else:
        (x_ref, xh_ref, g_ref, wug_ref, wuv_ref, cg_ref, cv_ref, pg_ref, pv_ref, wd_ref,
         o_ref, st_ref, hn, ug, uv, act, acc) = refs
        gf_ref = y_ref = None
    _ffn_body(x_ref, xh_ref, g_ref, wug_ref, wuv_ref, cg_ref, cv_ref, pg_ref, pv_ref, wd_ref, gf_ref,
              o_ref, st_ref, y_ref, hn, ug, uv, act, acc, tm=tm, nb=nb)


FFN_ROWS = 512


def _ffn(x, g, wug, wuv, cg, cv, prev, wd, final_g=None):
    B, T, D = x.shape
    F = D_FF
    H = FFN_HALO
    tm = min(FFN_ROWS, T)
    nb = _row_tile(B, max(1, FFN_ROWS // tm))
    assert T % tm == 0 and tm % H == 0 and F % FFN_CW == 0 and tm >= 8
    nh = tm // H
    nt = T // tm
    prev_h = jnp.pad(prev, ((0, 0), (H - prev.shape[1], 0), (0, 0)))
    pg, pv = prev_h[:, :, :F], prev_h[:, :, F:]
    const = lambda a: pl.BlockSpec(a.shape, lambda b, i: (0,) * a.ndim, pipeline_mode=pl.Buffered(1))
    row = pl.BlockSpec((nb, tm, D), lambda b, i: (b, i, 0))
    g2 = g.reshape(1, D)
    ins = [x, x, g2, wug, wuv, cg, cv, pg, pv, wd]
    specs = [row, pl.BlockSpec((nb, H, D), lambda b, i: (b, jnp.maximum(i * nh - 1, 0), 0)),
             const(g2), const(wug), const(wuv), const(cg), const(cv),
             pl.BlockSpec((nb, H, F), lambda b, i: (b, 0, 0)), pl.BlockSpec((nb, H, F), lambda b, i: (b, 0, 0)),
             const(wd)]
    outs = [row, pl.BlockSpec((nb, 1, 8, 2 * F), lambda b, i: (b, i, 0, 0))]
    shapes = [jax.ShapeDtypeStruct((B, T, D), F32), jax.ShapeDtypeStruct((B, nt, 8, 2 * F), F32)]
    if final_g is not None:
        gf = final_g.reshape(1, D)
        ins.append(gf)
        specs.append(const(gf))
        outs.append(row)
        shapes.append(jax.ShapeDtypeStruct((B, T, D), F32))
    res = pl.pallas_call(
        functools.partial(_ffn_kernel, tm=tm, nb=nb, final=final_g is not None),
        grid=(B // nb, nt),
        in_specs=specs,
        out_specs=outs,
        out_shape=shapes,
        scratch_shapes=[pltpu.VMEM((nb * (tm + H), D), BF16), pltpu.VMEM((2, nb * (tm + H), FFN_CW), BF16),
                        pltpu.VMEM((2, nb * (tm + H), FFN_CW), BF16), pltpu.VMEM((nb * tm, FFN_CW), BF16),
                        pltpu.VMEM((nb * tm, D), F32)],
        compiler_params=_cparams(("parallel", "parallel")),
        name="conv_ffn",
    )(*ins)
    state = res[1][:, nt - 1, 8 - (CONV_W - 1):, :]
    return (res[0], state) + tuple(res[2:])


def _pad_cols(w, n):
    return jnp.pad(w, ((0, 0), (0, n - w.shape[1])))


def _pad_keys(a, L):
    return jnp.pad(a, ((0, 0), (0, L - a.shape[1])) + ((0, 0),) * (a.ndim - 2))


def _dsa_layer(x, pos, past_k, past_v, past_ki, g, w_in, w_out):
    B, T, D = x.shape
    tq = min(256, T)
    qh, kf, kt, vf, vt, qi, tail, kib = _dsa_proj(x, g, w_in, _rope_tables(pos, A_HEAD_DIM, A_ROT), tq)
    P = past_k.shape[1]
    L = P + T
    Lp = _round_up(L, ATTN_KB)
    if Lp != T:
        past = lambda a: jnp.transpose(a, (0, 2, 1, 3)).astype(BF16)
        kt = jnp.pad(jnp.concatenate([past(past_k), kt], axis=2), ((0, 0), (0, 0), (0, Lp - L), (0, 0)))
        vt = jnp.pad(jnp.concatenate([past(past_v), vt], axis=2), ((0, 0), (0, 0), (0, Lp - L), (0, 0)))
        kib = _pad_keys(jnp.concatenate([past_ki.astype(BF16), kib], axis=1), Lp)
    x = _attn2(qh, kt, vt, x, w_out.astype(BF16), pos0=P, tq=tq, group=A_HEADS // A_KV_HEADS,
               idx=(qi, tail, kib), top=min(TOPK_MAX, L // 4), wi_off=IDX_DIM)
    kv_rows = lambda a: a.reshape(B, T, A_KV_HEADS, A_HEAD_DIM)
    return x, kv_rows(kf), kv_rows(vf), tail[..., :IDX_DIM]


def _rwkv_layer(x, shift_prev, S0, g, mu, w_rkv, w0, w1, w2, a0, a1, a2, g1, g2, k_k, k_a, r_k, ln_w, ln_b,
                w_out):
    B, T, D = x.shape
    tm = min(256, T)
    bf = lambda a: a.astype(BF16)
    vecs = jnp.stack([w0, a0, k_k, k_a, r_k.reshape(D)], axis=0)
    ws = [bf(w_rkv[0]), bf(w_rkv[1]), bf(w_rkv[2]), bf(w1), bf(w2), bf(a1), bf(a2), bf(g1), bf(g2)]
    rp, decay, k, v, a_vec, b_vec, yc, bonus, gate = _rwkv_pre(x, shift_prev, g, mu, vecs, ws, tm)
    y, S = _wkv(rp, decay, k, v, a_vec, b_vec, S0)
    ln = jnp.pad(jnp.stack([ln_w, ln_b], axis=0), ((0, 6), (0, 0)))
    x_new = _rwkv_post(y, yc, bonus, gate, x, ln, bf(w_out), tm)
    assert T >= 8
    shift = _norm(x[:, T - 8:].reshape(B * 8, D), g).reshape(B, 8, D)[:, -1]
    return x_new, shift, S


def _mla_layer(x, pos, past_lat, past_rope, g, w_in, g_q, g_kv, w_uq, w_ukv, w_out):
    B, T, D = x.shape
    tq = min(256, T)
    weights = _mla_weights(w_in, w_uq, w_ukv)
    tabs = _rope_tables(pos, LANE, C_ROPE, offset=C_NOPE)
    qh, kt, vt, lat, kpe = _mla_proj(x, g, g_q, g_kv, weights, tabs, tq)
    P = past_lat.shape[1]
    L = P + T
    Lp = _round_up(L, ATTN_KB)
    if Lp != T:
        past_slab = jnp.pad(past_rope, ((0, 0), (0, 0), (C_NOPE, LANE - C_NOPE - C_ROPE)))
        kt_p, vt_p = _mla_kv(past_lat, past_slab, weights[2], weights[3], _row_tile(P, 256))
        kt = jnp.pad(jnp.concatenate([kt_p, kt], axis=2), ((0, 0), (0, 0), (0, Lp - L), (0, 0)))
        vt = jnp.pad(jnp.concatenate([vt_p, vt], axis=2), ((0, 0), (0, 0), (0, Lp - L), (0, 0)))
    x = _attn2(qh, kt, vt, x, w_out.astype(BF16), pos0=P, tq=tq, group=1)
    return x, lat, kpe


def _ffn_layer(x, prev, g, w_up, w_conv, b_conv, w_down, final_g=None):
    F = D_FF
    taps = jnp.concatenate([w_conv, b_conv[None, :], jnp.zeros((8 - CONV_W - 1, 2 * F), F32)], axis=0)
    w_up = w_up.astype(BF16)
    return _ffn(x, g, w_up[:, :F], w_up[:, F:], taps[:, :F], taps[:, F:], prev, w_down.astype(BF16), final_g)


def _trunk(x, pos0, st, w):
    B, T, D = x.shape
    pos = pos0 + jnp.arange(T, dtype=jnp.int32)
    new = {name: [] for name in ('a_k', 'a_v', 'a_idx', 'b_wkv', 'b_shift', 'c_lat', 'c_rope', 'ffn')}
    for i in range(DEPTH):
        j = i // N_MIXERS
        kind = i % N_MIXERS
        if kind == 0:
            x, k, v, ki = _dsa_layer(x, pos, st['a_k'][j], st['a_v'][j], st['a_idx'][j], w['n_mix'][i],
                                     w['a_w_in'][j], w['a_w_out'][j])
            new['a_k'].append(k)
            new['a_v'].append(v)
            new['a_idx'].append(ki)
        elif kind == 1:
            x, shift, S = _rwkv_layer(x, st['b_shift'][j], st['b_wkv'][j], w['n_mix'][i], w['b_mu'][j],
                                      w['b_w_rkv'][j], w['b_w0'][j], w['b_w1'][j], w['b_w2'][j], w['b_a0'][j],
                                      w['b_a1'][j], w['b_a2'][j], w['b_g1'][j], w['b_g2'][j], w['b_k_k'][j],
                                      w['b_k_a'][j], w['b_r_k'][j], w['b_ln_w'][j], w['b_ln_b'][j],
                                      w['b_w_out'][j])
            new['b_shift'].append(shift)
            new['b_wkv'].append(S)
        else:
            x, lat, kpe = _mla_layer(x, pos, st['c_lat'][j], st['c_rope'][j], w['n_mix'][i], w['c_w_in'][j],
                                     w['c_g_q'][j], w['c_g_kv'][j], w['c_w_uq'][j], w['c_w_ukv'][j],
                                     w['c_w_out'][j])
            new['c_lat'].append(lat)
            new['c_rope'].append(kpe)
        last = i == DEPTH - 1
        res = _ffn_layer(x, st['ffn'][i], w['n_ffn'][i], w['f_w_up'][i], w['f_w_conv'][i],
                         w['f_b_conv'][i], w['f_w_down'][i], w['n_final'] if last else None)
        x = res[0]
        new['ffn'].append(res[1])
    return res[2], {name: jnp.stack(rows, axis=0) for name, rows in new.items()}


def kernel(x_prompt, x_sample, cache_a_k, cache_a_v, cache_a_idx, state_b_wkv, state_b_shift,
           cache_c_latent, cache_c_rope, state_ffn_conv, n_mix, n_ffn, n_final, a_w_in, a_w_out,
           b_mu, b_w_rkv, b_w0, b_w1, b_w2, b_a0, b_a1, b_a2, b_g1, b_g2, b_k_k, b_k_a, b_r_k,
           b_ln_w, b_ln_b, b_w_out, c_w_in, c_g_q, c_g_kv, c_w_uq, c_w_ukv, c_w_out,
           f_w_up, f_w_conv, f_b_conv, f_w_down):
    w = dict(n_mix=n_mix, n_ffn=n_ffn, n_final=n_final, a_w_in=a_w_in, a_w_out=a_w_out,
             b_mu=b_mu, b_w_rkv=b_w_rkv, b_w0=b_w0, b_w1=b_w1, b_w2=b_w2, b_a0=b_a0, b_a1=b_a1,
             b_a2=b_a2, b_g1=b_g1, b_g2=b_g2, b_k_k=b_k_k, b_k_a=b_k_a, b_r_k=b_r_k,
             b_ln_w=b_ln_w, b_ln_b=b_ln_b, b_w_out=b_w_out, c_w_in=c_w_in, c_g_q=c_g_q,
             c_g_kv=c_g_kv, c_w_uq=c_w_uq, c_w_ukv=c_w_ukv, c_w_out=c_w_out,
             f_w_up=f_w_up, f_w_conv=f_w_conv, f_b_conv=f_b_conv, f_w_down=f_w_down)
    Bp, Tp, D = x_prompt.shape
    n_a, n_b, n_c = cache_a_k.shape[0], state_b_wkv.shape[0], cache_c_latent.shape[0]
    st_prompt = dict(
        a_k=jnp.zeros((n_a, Bp, 0, A_KV_HEADS, A_HEAD_DIM), F32),
        a_v=jnp.zeros((n_a, Bp, 0, A_KV_HEADS, A_HEAD_DIM), F32),
        a_idx=jnp.zeros((n_a, Bp, 0, IDX_DIM), F32),
        b_wkv=jnp.zeros((n_b, Bp, B_HEADS, B_HEAD_DIM, B_HEAD_DIM), F32),
        b_shift=jnp.zeros((n_b, Bp, D), F32),
        c_lat=jnp.zeros((n_c, Bp, 0, C_KV_RANK), F32),
        c_rope=jnp.zeros((n_c, Bp, 0, C_ROPE), F32),
        ffn=jnp.zeros((DEPTH, Bp, CONV_W - 1, 2 * D_FF), F32))
    st_sample = dict(a_k=cache_a_k, a_v=cache_a_v, a_idx=cache_a_idx, b_wkv=state_b_wkv,
                     b_shift=state_b_shift, c_lat=cache_c_latent, c_rope=cache_c_rope,
                     ffn=state_ffn_conv)
    y_prompt, sp = _trunk(x_prompt, 0, st_prompt, w)
    y_sample, ss = _trunk(x_sample, cache_a_k.shape[2], st_sample, w)
    return (y_prompt, y_sample,
            sp['a_k'], ss['a_k'], sp['a_v'], ss['a_v'], sp['a_idx'], ss['a_idx'],
            sp['b_wkv'], ss['b_wkv'], sp['b_shift'], ss['b_shift'],
            sp['c_lat'], ss['c_lat'], sp['c_rope'], ss['c_rope'],
            sp['ffn'], ss['ffn'])
```

```python
import functools

import jax
import jax.numpy as jnp
from jax import lax
from jax.experimental import pallas as pl
from jax.experimental.pallas import tpu as pltpu

F32 = jnp.float32
BF16 = jnp.bfloat16

D_MODEL = 1024
DEPTH = 4
CHUNK = 64
N_MIXERS = 3
NORM_EPS = 1e-6
ROPE_THETA = 500000.0
A_HEADS, A_HEAD_DIM, A_KV_HEADS = 16, 64, 2
A_ROT = A_HEAD_DIM // 4
IDX_HEADS, IDX_DIM = 8, 64
IDX_ROT = IDX_DIM // 4
TOPK_MAX = 256
A_O_Q = A_HEADS * A_HEAD_DIM
A_O_K = A_O_Q + A_KV_HEADS * A_HEAD_DIM
A_O_V = A_O_K + A_KV_HEADS * A_HEAD_DIM
A_O_QI = A_O_V + IDX_HEADS * IDX_DIM
A_O_KI = A_O_QI + IDX_DIM
A_IN = A_O_KI + IDX_HEADS
B_HEAD_DIM = 64
B_HEADS = D_MODEL // B_HEAD_DIM
B_GN_EPS = 64e-5
C_HEADS, C_NOPE, C_ROPE, C_V = 16, 64, 32, 64
C_Q_RANK, C_KV_RANK = 512, 256
D_FF = 2816
CONV_W = 3

LANE = 128
SUBLANE_BF16 = 16
VMEM_LIMIT = 56 * 1024 * 1024
NEG_INF = float("-inf")
LOG2E = 1.4426950408889634


def _round_up(n, m):
    return (n + m - 1) // m * m


def _row_tile(M, pref):
    t = min(pref, M)
    while M % t:
        t //= 2
    return t


def _cparams(sem):
    return pltpu.CompilerParams(dimension_semantics=sem, vmem_limit_bytes=VMEM_LIMIT)


def _norm_kernel(x_ref, g_ref, o_ref):
    xf = x_ref[...]
    o_ref[...] = xf * lax.rsqrt(jnp.mean(xf * xf, axis=-1, keepdims=True) + NORM_EPS) * g_ref[...]


def _norm(x, g, tm=512):
    M, K = x.shape
    tm = _row_tile(M, tm)
    return pl.pallas_call(
        _norm_kernel,
        grid=(M // tm,),
        in_specs=[pl.BlockSpec((tm, K), lambda i: (i, 0)), pl.BlockSpec((1, K), lambda i: (0, 0))],
        out_specs=pl.BlockSpec((tm, K), lambda i: (i, 0)),
        out_shape=jax.ShapeDtypeStruct((M, K), F32),
        compiler_params=_cparams(("parallel",)),
        name="rmsnorm",
    )(x, g.reshape(1, K))


ATTN_KB = 512
ATTN_KB_SHIFT = ATTN_KB.bit_length() - 1
SELECT_ALL = 1e9
MANTISSA_BITS = 23
ATTN_ROWS_PER_ITER = 1024
TOPK_PART_ROWS = 64


def _for_key_blocks(nk, body, init):
    carry = lax.fori_loop(0, nk // 2, lambda i, c: body(2 * i + 1, body(2 * i, c)), init)
    return lax.cond(nk % 2 == 1, lambda c: body(nk - 1, c), lambda c: c, carry)


def _fold(x, op=jnp.add):
    acc = x[:, :LANE]
    for j in range(1, x.shape[1] // LANE):
        acc = op(acc, x[:, j * LANE:(j + 1) * LANE])
    return acc


def _topk_bias_t(tab_ref, qi_ref, wi_ref, ki_ref, SC, MS, *, nk, first, tq, top, wi_off):
    kb = ATTN_KB
    qi = qi_ref[0]
    w_t = jnp.transpose(wi_ref[0])[wi_off:wi_off + IDX_HEADS, :]
    part = TOPK_PART_ROWS
    zeros_p = jnp.zeros((part, tq), F32)
    zero = jnp.zeros((1, tq), F32)
    kf = float(top)
    pos_q = lax.broadcasted_iota(jnp.int32, (1, tq), 1) + first
    limit = (pos_q & ~(CHUNK - 1)) + CHUNK
    key_i = lax.broadcasted_iota(jnp.int32, (kb, tq), 0)

    def key_sum(body):
        def step(j, a):
            return a + jnp.sum(body(j).reshape(kb // part, part, tq), axis=0)

        return jnp.sum(_for_key_blocks(nk, step, zeros_p), axis=0, keepdims=True)

    def ones_where(c):
        return jnp.where(c, 1.0, 0.0)

    def score_block(j):
        kij = ki_ref[0, pl.ds(pl.multiple_of(j * kb, kb), kb), :]
        sc = jnp.zeros((kb, tq), F32)
        for h in range(IDX_HEADS):
            d = lax.dot_general(kij, qi[:, h * IDX_DIM:(h + 1) * IDX_DIM], (((1,), (1,)), ((), ())),
                                preferred_element_type=F32)
            sc = sc + w_t[h:h + 1, :] * jnp.maximum(d, 0.0)
        sc = jnp.where(key_i + j * kb < limit, sc, NEG_INF)
        SC[j] = sc
        return ones_where(sc >= 0.0)

    c0 = key_sum(score_block)
    neg = c0 < kf
    sgn = jnp.where(neg, -1.0, 1.0)
    kp = jnp.where(neg, kf, (nk * kb).astype(F32) - kf + 1.0)

    def flip(j, c):
        SC[j] = SC[j] * sgn
        return c

    _for_key_blocks(nk, flip, 0)

    def count_lt(cand):
        return key_sum(lambda j: ones_where(SC[j] < cand))

    def exp_step(i, carry):
        e_cur, t_cur = carry
        cand = jnp.where(e_cur == 0.0, tab_ref[0, i], t_cur * tab_ref[1, i])
        ok = count_lt(cand) < kp
        return jnp.where(ok, e_cur + tab_ref[2, i], e_cur), jnp.where(ok, cand, t_cur)

    _, t_pow = lax.fori_loop(0, 8, exp_step, (zero, zero))

    def man_step(i, carry):
        t_cur, frac = carry
        cand = t_cur + frac
        ok = count_lt(cand) < kp
        return jnp.where(ok, cand, t_cur), frac * 0.5

    t_cur, _ = lax.fori_loop(0, MANTISSA_BITS, man_step, (t_pow, t_pow * 0.5))
    thr = t_cur * sgn

    def score(j):
        return SC[j] * sgn

    def key_idx(j):
        return (key_i + j * kb).astype(F32)

    need = kf - key_sum(lambda j: ones_where(score(j) > thr))
    n_eq = key_sum(lambda j: ones_where(score(j) == thr))

    def index_cut():
        nbits = (SC.shape[0] * kb - 1).bit_length()

        def bit_step(i, carry):
            c_cur, bit = carry
            cand = c_cur + bit
            ok = key_sum(lambda j: ones_where((score(j) == thr) & (key_idx(j) < cand))) < need
            return jnp.where(ok, cand, c_cur), bit * 0.5

        c_cur, _ = lax.fori_loop(0, nbits, bit_step, (zero, jnp.full((1, tq), 2.0 ** (nbits - 1), F32)))
        return c_cur

    any_split = jnp.max(ones_where(n_eq > need)) > 0.0
    c_cut = lax.cond(any_split, index_cut, lambda: jnp.full((1, tq), SELECT_ALL, F32))

    def write_bias(j, c):
        s = score(j)
        sel = (s > thr) | ((s == thr) & (key_idx(j) <= c_cut))
        MS[j] = jnp.transpose(jnp.where(sel & (key_i + j * kb < limit), 0.0, NEG_INF))
        return c

    _for_key_blocks(nk, write_bias, 0)


def _attn2_kernel(*refs, n_kv, group, tq, pos0, top, indexer, wi_off):
    if indexer:
        (tab_ref, q_ref, k_ref, v_ref, x_ref, wo_ref, qi_ref, wi_ref, ki_ref, xo_ref,
         MS, LG, MACC, LACC, OACC, OH, OALL, SC) = refs
    else:
        q_ref, k_ref, v_ref, x_ref, wo_ref, xo_ref, MS, LG, MACC, LACC, OACC, OH, OALL = refs
    kb = ATTN_KB
    first = pos0 + pl.program_id(1) * tq
    last_limit = ((first + tq - 1) & ~(CHUNK - 1)) + CHUNK
    nk = (last_limit + (kb - 1)) >> ATTN_KB_SHIFT
    row = lax.broadcasted_iota(jnp.int32, (tq, 1), 0) + first
    limit = (row & ~(CHUNK - 1)) + CHUNK
    lane_i = lax.broadcasted_iota(jnp.int32, (tq, kb), 1)

    def causal_bias(j, c):
        MS[j] = jnp.where(lane_i + j * kb < limit, 0.0, NEG_INF)
        return c

    if indexer:
        @pl.when(last_limit > top)
        def _():
            _topk_bias_t(tab_ref, qi_ref, wi_ref, ki_ref, SC, MS, nk=nk, first=first, tq=tq, top=top, wi_off=wi_off)

        @pl.when(last_limit <= top)
        def _():
            _for_key_blocks(nk, causal_bias, 0)
    else:
        _for_key_blocks(nk, causal_bias, 0)

    hu = LG.shape[0]

    def per_kv_heads(gi, c):
        heads = [gi * hu + u for u in range(hu)]
        qs = [q_ref[0, g, 0] for g in heads]
        MACC[...] = jnp.full(MACC.shape, NEG_INF, F32)
        LACC[...] = jnp.zeros(LACC.shape, F32)
        OACC[...] = jnp.zeros(OACC.shape, F32)

        def logits_block(j, c_):
            keys = pl.ds(pl.multiple_of(j * kb, kb), kb)
            bias = MS[j][None]
            for u, g in enumerate(heads):
                lg = lax.dot_general(qs[u], k_ref[0, g, keys, :], (((1,), (1,)), ((), ())),
                                     preferred_element_type=F32)
                lg = (lg.reshape(group, tq, kb) + bias).reshape(group * tq, kb)
                LG[u, j] = lg
                MACC[u] = jnp.maximum(MACC[u], _fold(lg, jnp.maximum))
            return c_

        _for_key_blocks(nk, logits_block, 0)
        ms = [jnp.max(MACC[u], axis=-1, keepdims=True) for u in range(hu)]

        def value_block(j, c_):
            keys = pl.ds(pl.multiple_of(j * kb, kb), kb)
            for u, g in enumerate(heads):
                p = jnp.exp2(LG[u, j] - ms[u])
                LACC[u] += _fold(p)
                OACC[u] += jnp.dot(p.astype(BF16), v_ref[0, g, keys, :], preferred_element_type=F32)
            return c_

        _for_key_blocks(nk, value_block, 0)
        for u, g in enumerate(heads):
            OH[g] = (OACC[u] / jnp.sum(LACC[u], axis=-1, keepdims=True)).astype(OH.dtype)
        return c

    lax.fori_loop(0, n_kv // hu, per_kv_heads, 0)
    dv = OH.shape[2]
    for h in range(n_kv * group):
        OALL[:, h * dv:(h + 1) * dv] = OH[h // group, (h % group) * tq:(h % group + 1) * tq, :]
    xo_ref[0] = x_ref[0] + jnp.dot(OALL[...], wo_ref[...], preferred_element_type=F32)


def _attn2(qg, kt, vt, x, w_out, *, pos0, tq, group, idx=None, top=0, wi_off=0):
    B, n_kv, nq, rows, dq = qg.shape
    L, dv = kt.shape[2], vt.shape[3]
    T, D = x.shape[1], x.shape[2]
    kb = ATTN_KB
    assert nq * tq == T and rows == group * tq and L % kb == 0 and (idx is None or top <= kb)
    ins = [qg, kt, vt, x, w_out]
    specs = [pl.BlockSpec((1, n_kv, 1, rows, dq), lambda b, i: (b, 0, i, 0, 0)),
             pl.BlockSpec((1, n_kv, L, dq), lambda b, i: (b, 0, 0, 0)),
             pl.BlockSpec((1, n_kv, L, dv), lambda b, i: (b, 0, 0, 0)),
             pl.BlockSpec((1, tq, D), lambda b, i: (b, i, 0)),
             pl.BlockSpec(w_out.shape, lambda b, i: (0, 0))]
    hu = max(1, min(n_kv, ATTN_ROWS_PER_ITER // rows))
    assert n_kv % hu == 0
    scratch = [pltpu.VMEM((L // kb, tq, kb), F32), pltpu.VMEM((hu, L // kb, rows, kb), F32),
               pltpu.VMEM((hu, rows, LANE), F32), pltpu.VMEM((hu, rows, LANE), F32),
               pltpu.VMEM((hu, rows, dv), F32), pltpu.VMEM((n_kv, rows, dv), BF16),
               pltpu.VMEM((tq, n_kv * group * dv), BF16)]
    if idx is not None:
        qi, wi, ki = idx
        steps = [2 ** b for b in range(7, -1, -1)]
        tab = jnp.array([[2.0 ** (s - 127) for s in steps], [2.0 ** s if s < 128 else 1.0 for s in steps],
                         [float(s) for s in steps]], F32)
        ins = [tab] + ins + [qi, wi, ki]
        specs = ([pl.BlockSpec(memory_space=pltpu.SMEM)] + specs
                 + [pl.BlockSpec((1, tq, qi.shape[2]), lambda b, i: (b, i, 0)),
                    pl.BlockSpec((1, tq, wi.shape[2]), lambda b, i: (b, i, 0)),
                    pl.BlockSpec((1, L, ki.shape[2]), lambda b, i: (b, 0, 0))])
        scratch.append(pltpu.VMEM((L // kb, kb, tq), F32))
    return pl.pallas_call(
        functools.partial(_attn2_kernel, n_kv=n_kv, group=group, tq=tq, pos0=pos0, top=top,
                          indexer=idx is not None, wi_off=wi_off),
        grid=(B, nq),
        in_specs=specs,
        out_specs=pl.BlockSpec((1, tq, D), lambda b, i: (b, i, 0)),
        out_shape=jax.ShapeDtypeStruct((B, T, D), F32),
        scratch_shapes=scratch,
        compiler_params=_cparams(("parallel", "parallel")),
        name="dsa_attn" if idx is not None else "mla_attn",
    )(*ins)


def _rope_tables(pos, dh, rot, offset=0):
    half = rot // 2
    inv = ROPE_THETA ** (-jnp.arange(half, dtype=F32) / half)
    ang = pos.astype(F32)[:, None] * inv[None, :]
    cos, sin = jnp.cos(ang), jnp.sin(ang)
    T = pos.shape[0]
    pad = lambda n, v: jnp.full((T, n), v, F32)
    zh = pad(half, 0.0)
    lo, hi = offset, dh - offset - rot
    c = jnp.concatenate([pad(lo, 1.0), cos, cos, pad(hi, 1.0)], axis=1)
    s1 = jnp.concatenate([pad(lo, 0.0), -sin, zh, pad(hi, 0.0)], axis=1)
    s2 = jnp.concatenate([pad(lo, 0.0), zh, sin, pad(hi, 0.0)], axis=1)
    return tuple(jnp.tile(t, (1, LANE // dh)) for t in (c, s1, s2))


def _rope_lanes(x, c, s1, s2, half):
    return x * c + pltpu.roll(x, LANE - half, 1) * s1 + pltpu.roll(x, half, 1) * s2


def _dsa_proj_kernel(x_ref, g_ref, w_ref, c_ref, s1_ref, s2_ref, ts_ref, qh_ref, kf_ref, kt_ref, vf_ref, vt_ref,
                     qi_ref, tail_ref, kib_ref, *, tm):
    xf = x_ref[0]
    h = xf * lax.rsqrt(jnp.mean(xf * xf, axis=-1, keepdims=True) + NORM_EPS) * g_ref[...]
    acc = jnp.dot(h.astype(BF16), w_ref[...], preferred_element_type=F32)
    c, s1, s2 = c_ref[...], s1_ref[...], s2_ref[...]
    half = A_ROT // 2
    hd = A_HEAD_DIM
    group = A_HEADS // A_KV_HEADS
    rope = lambda xs: _rope_lanes(xs, c, s1, s2, half)
    slab = lambda off: acc[:, off:off + LANE]
    for s in range(A_O_Q // LANE):
        qs = (rope(slab(s * LANE)) * (hd ** -0.5 * LOG2E)).astype(BF16)
        for e in range(LANE // hd):
            head = s * (LANE // hd) + e
            u = head % group
            qh_ref[0, head // group, 0, u * tm:(u + 1) * tm, :] = qs[:, e * hd:(e + 1) * hd]
    ks = rope(slab(A_O_Q))
    vs = slab(A_O_K)
    kf_ref[0] = ks
    vf_ref[0] = vs
    for e in range(A_KV_HEADS):
        kt_ref[0, e] = ks[:, e * hd:(e + 1) * hd].astype(BF16)
        vt_ref[0, e] = vs[:, e * hd:(e + 1) * hd].astype(BF16)
    for s in range(IDX_HEADS * IDX_DIM // LANE):
        qi_ref[0, :, s * LANE:(s + 1) * LANE] = rope(slab(A_O_V + s * LANE)).astype(BF16)
    is_key = lax.broadcasted_iota(jnp.int32, (tm, LANE), 1) < IDX_DIM
    tl = _rope_lanes(slab(A_O_QI), jnp.where(is_key, c, 1.0), jnp.where(is_key, s1, 0.0),
                     jnp.where(is_key, s2, 0.0), half) * ts_ref[...]
    tail_ref[0] = tl
    kib_ref[0] = tl[:, :IDX_DIM].astype(BF16)


def _dsa_proj(x, g, w_in, tabs, tm):
    B, T, D = x.shape
    assert (A_HEAD_DIM, A_ROT) == (IDX_DIM, IDX_ROT) and A_KV_HEADS * A_HEAD_DIM == LANE
    assert A_O_QI % LANE == 0 and IDX_DIM + IDX_HEADS <= LANE and T % tm == 0
    n_in = _round_up(A_IN, LANE)
    group = A_HEADS // A_KV_HEADS
    hd = A_HEAD_DIM
    nq = T // tm
    lanes = jnp.arange(LANE)
    tail_scale = jnp.where(lanes < IDX_DIM, 1.0, jnp.where(lanes < IDX_DIM + IDX_HEADS,
                                                            (IDX_HEADS * IDX_DIM) ** -0.5, 0.0)).astype(F32)
    row = lambda n: pl.BlockSpec((1, tm, n), lambda b, i: (b, i, 0))
    tab = pl.BlockSpec((tm, LANE), lambda b, i: (i, 0))
    const = lambda a: pl.BlockSpec(a.shape, lambda b, i: (0,) * a.ndim)
    kvh = pl.BlockSpec((1, A_KV_HEADS, tm, hd), lambda b, i: (b, 0, i, 0))
    w = _pad_cols(w_in, n_in).astype(BF16)
    g2 = g.reshape(1, D)
    ts = tail_scale.reshape(1, LANE)
    return pl.pallas_call(
        functools.partial(_dsa_proj_kernel, tm=tm),
        grid=(B, nq),
        in_specs=[row(D), const(g2), const(w), tab, tab, tab, const(ts)],
        out_specs=[pl.BlockSpec((1, A_KV_HEADS, 1, group * tm, hd), lambda b, i: (b, 0, i, 0, 0)),
                   row(LANE), kvh, row(LANE), kvh, row(IDX_HEADS * IDX_DIM), row(LANE), row(IDX_DIM)],
        out_shape=[jax.ShapeDtypeStruct((B, A_KV_HEADS, nq, group * tm, hd), BF16),
                   jax.ShapeDtypeStruct((B, T, LANE), F32), jax.ShapeDtypeStruct((B, A_KV_HEADS, T, hd), BF16),
                   jax.ShapeDtypeStruct((B, T, LANE), F32), jax.ShapeDtypeStruct((B, A_KV_HEADS, T, hd), BF16),
                   jax.ShapeDtypeStruct((B, T, IDX_HEADS * IDX_DIM), BF16),
                   jax.ShapeDtypeStruct((B, T, LANE), F32), jax.ShapeDtypeStruct((B, T, IDX_DIM), BF16)],
        compiler_params=_cparams(("parallel", "parallel")),
        name="dsa_proj",
    )(x, g2, w, *tabs, ts)


def _rms(xf, gain):
    return xf * lax.rsqrt(jnp.mean(xf * xf, axis=-1, keepdims=True) + NORM_EPS) * gain


def _mla_write_kv(lat, kpe_slab, wuk_ref, wuv_ref, kt_ref, vt_ref):
    lb = lat.astype(BF16)
    kn = jnp.dot(lb, wuk_ref[...], preferred_element_type=F32)
    vv = jnp.dot(lb, wuv_ref[...], preferred_element_type=F32)
    for h in range(C_HEADS):
        kt_ref[0, h] = (kn[:, h * LANE:(h + 1) * LANE] + kpe_slab).astype(BF16)
        vt_ref[0, h] = vv[:, h * C_V:(h + 1) * C_V].astype(BF16)


def _mla_proj_kernel(x_ref, g_ref, win_ref, gq_ref, gkv_ref, wuq_ref, wuk_ref, wuv_ref, c_ref, s1_ref, s2_ref,
                     qh_ref, kt_ref, vt_ref, lat_ref, kpe_ref):
    h = _rms(x_ref[0], g_ref[...]).astype(BF16)
    proj = jnp.dot(h, win_ref[...], preferred_element_type=F32)
    c, s1, s2 = c_ref[...], s1_ref[...], s2_ref[...]
    half = C_ROPE // 2
    q = jnp.dot(_rms(proj[:, :C_Q_RANK], gq_ref[...]).astype(BF16), wuq_ref[...], preferred_element_type=F32)
    scale = (C_NOPE + C_ROPE) ** -0.5 * LOG2E
    for hd in range(C_HEADS):
        qh_ref[0, hd, 0] = (_rope_lanes(q[:, hd * LANE:(hd + 1) * LANE], c, s1, s2, half) * scale).astype(BF16)
    lat = _rms(proj[:, C_Q_RANK:C_Q_RANK + C_KV_RANK], gkv_ref[...])
    lat_ref[0] = lat
    kpe_slab = _rope_lanes(proj[:, C_Q_RANK + C_KV_RANK:], c, s1, s2, half)
    kpe_ref[0] = kpe_slab[:, C_NOPE:C_NOPE + C_ROPE]
    _mla_write_kv(lat, kpe_slab, wuk_ref, wuv_ref, kt_ref, vt_ref)


def _mla_kv_kernel(lat_ref, kpe_ref, wuk_ref, wuv_ref, kt_ref, vt_ref):
    _mla_write_kv(lat_ref[0], kpe_ref[0], wuk_ref, wuv_ref, kt_ref, vt_ref)


def _mla_weights(w_in, w_uq, w_ukv):
    D = w_in.shape[0]
    zc = lambda rows, n: jnp.zeros((rows, n), w_in.dtype)
    w_in2 = jnp.concatenate([w_in[:, :C_Q_RANK + C_KV_RANK], zc(D, C_NOPE), w_in[:, C_Q_RANK + C_KV_RANK:],
                             zc(D, LANE - C_NOPE - C_ROPE)], axis=1)
    pad_heads = lambda w, d: jnp.pad(w.reshape(w.shape[0], C_HEADS, d), ((0, 0), (0, 0), (0, LANE - d))).reshape(
        w.shape[0], C_HEADS * LANE)
    w_uq2 = pad_heads(w_uq, C_NOPE + C_ROPE)
    ukv = w_ukv.reshape(C_KV_RANK, C_HEADS, C_NOPE + C_V)
    w_uk2 = pad_heads(ukv[..., :C_NOPE].reshape(C_KV_RANK, C_HEADS * C_NOPE), C_NOPE)
    w_uv2 = ukv[..., C_NOPE:].reshape(C_KV_RANK, C_HEADS * C_V)
    return tuple(a.astype(BF16) for a in (w_in2, w_uq2, w_uk2, w_uv2))


def _mla_proj(x, g, g_q, g_kv, weights, tabs, tm):
    B, T, D = x.shape
    w_in2, w_uq2, w_uk2, w_uv2 = weights
    nq = T // tm
    row = lambda n: pl.BlockSpec((1, tm, n), lambda b, i: (b, i, 0))
    tab = pl.BlockSpec((tm, LANE), lambda b, i: (i, 0))
    const = lambda a: pl.BlockSpec(a.shape, lambda b, i: (0,) * a.ndim)
    heads = lambda d: pl.BlockSpec((1, C_HEADS, tm, d), lambda b, i: (b, 0, i, 0))
    vec = lambda a: a.reshape(1, a.shape[0])
    return pl.pallas_call(
        _mla_proj_kernel,
        grid=(B, nq),
        in_specs=[row(D), const(vec(g)), const(w_in2), const(vec(g_q)), const(vec(g_kv)), const(w_uq2),
                  const(w_uk2), const(w_uv2), tab, tab, tab],
        out_specs=[pl.BlockSpec((1, C_HEADS, 1, tm, LANE), lambda b, i: (b, 0, i, 0, 0)), heads(LANE), heads(C_V),
                   row(C_KV_RANK), row(C_ROPE)],
        out_shape=[jax.ShapeDtypeStruct((B, C_HEADS, nq, tm, LANE), BF16),
                   jax.ShapeDtypeStruct((B, C_HEADS, T, LANE), BF16), jax.ShapeDtypeStruct((B, C_HEADS, T, C_V), BF16),
                   jax.ShapeDtypeStruct((B, T, C_KV_RANK), F32), jax.ShapeDtypeStruct((B, T, C_ROPE), F32)],
        compiler_params=_cparams(("parallel", "parallel")),
        name="mla_proj",
    )(x, vec(g), w_in2, vec(g_q), vec(g_kv), w_uq2, w_uk2, w_uv2, *tabs)


def _mla_kv(lat, kpe_slab, w_uk2, w_uv2, tm):
    B, P, _ = lat.shape
    row = lambda n: pl.BlockSpec((1, tm, n), lambda b, i: (b, i, 0))
    const = lambda a: pl.BlockSpec(a.shape, lambda b, i: (0,) * a.ndim)
    heads = lambda d: pl.BlockSpec((1, C_HEADS, tm, d), lambda b, i: (b, 0, i, 0))
    return pl.pallas_call(
        _mla_kv_kernel,
        grid=(B, P // tm),
        in_specs=[row(C_KV_RANK), row(LANE), const(w_uk2), const(w_uv2)],
        out_specs=[heads(LANE), heads(C_V)],
        out_shape=[jax.ShapeDtypeStruct((B, C_HEADS, P, LANE), BF16), jax.ShapeDtypeStruct((B, C_HEADS, P, C_V), BF16)],
        compiler_params=_cparams(("parallel", "parallel")),
        name="mla_kv",
    )(lat, kpe_slab, w_uk2, w_uv2)


def _head_sum(x, bo_ref, split=True):
    bw = bo_ref.shape[0]
    bo = bo_ref[...]
    hi = x.astype(BF16)
    terms = [hi, (x - hi.astype(F32)).astype(BF16)] if split else [hi]
    return jnp.concatenate(
        [sum(jnp.dot(t[:, j * bw:(j + 1) * bw], bo, preferred_element_type=F32) for t in terms)
         for j in range(x.shape[1] // bw)], axis=-1)


RWKV_HALO = 8


def _rwkv_pre_kernel(x_ref, xh_ref, sh_ref, g_ref, mu_ref, vec_ref, wr_ref, wk_ref, wv_ref, w1_ref, w2_ref,
                     a1_ref, a2_ref, g1_ref, g2_ref, bo_ref,
                     rp_ref, w_ref, k_ref, v_ref, a_ref, b_ref, yc_ref, bonus_ref, gate_ref, *, tm):
    i = pl.program_id(1)
    gain = g_ref[...]

    def norm(xf):
        return xf * lax.rsqrt(jnp.mean(xf * xf, axis=-1, keepdims=True) + NORM_EPS) * gain

    h = norm(x_ref[0])
    before = jnp.where(i > 0, norm(xh_ref[0])[RWKV_HALO - 1:RWKV_HALO, :], sh_ref[0])
    first = lax.broadcasted_iota(jnp.int32, (tm, 1), 0) == 0
    xx = jnp.where(first, before, pltpu.roll(h, 1, 0)) - h
    mu = mu_ref[...]
    vec = vec_ref[...]
    w0, a0, k_k, k_a, r_k = (vec[j:j + 1, :] for j in range(5))

    def mix(j):
        return (h + xx * mu[j:j + 1, :]).astype(BF16)

    dot = lambda a_, w_: jnp.dot(a_, w_[...], preferred_element_type=F32)
    r = dot(mix(0), wr_ref)
    wl = dot(jnp.tanh(dot(mix(1), w1_ref)).astype(BF16), w2_ref)
    k = dot(mix(2), wk_ref)
    v = dot(mix(3), wv_ref)
    al = dot(dot(mix(4), a1_ref).astype(BF16), a2_ref)
    gate_ref[0] = dot(jax.nn.sigmoid(dot(mix(5), g1_ref)).astype(BF16), g2_ref)
    z = -(w0 + wl)
    softplus = jnp.maximum(z, 0.0) + jnp.log(1.0 + jnp.exp(-jnp.abs(z)))
    decay = jnp.exp(-jnp.exp(-softplus - 0.5))
    a = jax.nn.sigmoid(a0 + al)
    kk = k * k_k
    kk = kk / jnp.maximum(jnp.sqrt(_head_sum(kk * kk, bo_ref)), 1e-12)
    k = k * (1.0 + (a - 1.0) * k_a)
    b = kk * a
    rp_ref[0] = decay * r - kk * _head_sum(b * r, bo_ref, split=False)
    w_ref[0] = decay
    k_ref[0] = k
    for hh in range(B_HEADS):
        v_ref[0, :, hh, :] = v[:, hh * B_HEAD_DIM:(hh + 1) * B_HEAD_DIM]
    a_ref[0] = -kk
    b_ref[0] = b
    yc_ref[0] = v * _head_sum(k * r, bo_ref, split=False)
    bonus_ref[0] = _head_sum(r * k * r_k, bo_ref, split=False) * v


def _rwkv_post_kernel(y_ref, yc_ref, bonus_ref, gate_ref, x_ref, ln_ref, wo_ref, bo_ref, o_ref):
    n = float(B_HEAD_DIM)
    y = y_ref[0] + yc_ref[0]
    d = y - _head_sum(y, bo_ref) / n
    var = _head_sum(d * d, bo_ref, split=False) / n
    ln = ln_ref[...]
    yn = d * lax.rsqrt(var + B_GN_EPS) * ln[0:1, :] + ln[1:2, :] + bonus_ref[0]
    o_ref[0] = x_ref[0] + jnp.dot((yn * gate_ref[0]).astype(BF16), wo_ref[...], preferred_element_type=F32)


def _block_ones():
    blk = jnp.arange(WKV_BW) // B_HEAD_DIM
    return (blk[:, None] == blk[None, :]).astype(BF16)


def _rwkv_pre(x, shift_prev, g, mu, vecs, ws, tm):
    B, T, D = x.shape
    H = RWKV_HALO
    assert T % tm == 0 and tm % H == 0
    nh = tm // H
    bo = _block_ones()
    row = pl.BlockSpec((1, tm, D), lambda b, i: (b, i, 0))
    const = lambda a: pl.BlockSpec(a.shape, lambda b, i: (0,) * a.ndim)
    pad8 = lambda a: jnp.pad(a, ((0, 8 - a.shape[0]), (0, 0)))
    mu8, vec8, g2 = pad8(mu), pad8(vecs), g.reshape(1, D)
    return pl.pallas_call(
        functools.partial(_rwkv_pre_kernel, tm=tm),
        grid=(B, T // tm),
        in_specs=[row, pl.BlockSpec((1, H, D), lambda b, i: (b, jnp.maximum(i * nh - 1, 0), 0)),
                  pl.BlockSpec((1, 1, D), lambda b, i: (b, 0, 0)), const(g2), const(mu8), const(vec8)]
                 + [const(a) for a in ws] + [const(bo)],
        out_specs=[row] * 3 + [pl.BlockSpec((1, tm, B_HEADS, B_HEAD_DIM), lambda b, i: (b, i, 0, 0))] + [row] * 5,
        out_shape=[jax.ShapeDtypeStruct((B, T, D), F32)] * 3
                  + [jax.ShapeDtypeStruct((B, T, B_HEADS, B_HEAD_DIM), F32)] + [jax.ShapeDtypeStruct((B, T, D), F32)] * 5,
        compiler_params=_cparams(("parallel", "parallel")),
        name="rwkv_pre",
    )(x, x, shift_prev.reshape(B, 1, D), g2, mu8, vec8, *ws, bo)


def _rwkv_post(y, yc, bonus, gate, x, ln, w_out, tm):
    B, T, D = x.shape
    bo = _block_ones()
    row = pl.BlockSpec((1, tm, D), lambda b, i: (b, i, 0))
    const = lambda a: pl.BlockSpec(a.shape, lambda b, i: (0,) * a.ndim)
    return pl.pallas_call(
        _rwkv_post_kernel,
        grid=(B, T // tm),
        in_specs=[row] * 5 + [const(ln), const(w_out), const(bo)],
        out_specs=row,
        out_shape=jax.ShapeDtypeStruct((B, T, D), F32),
        compiler_params=_cparams(("parallel", "parallel")),
        name="rwkv_post",
    )(y, yc, bonus, gate, x, ln, w_out, bo)


WKV_SUB = 16
WKV_NB = 8
WKV_TC = 64
WKV_BW = 2 * LANE


def _wkv_kernel(rp_ref, w_ref, k_ref, vh_ref, a_ref, b_ref, s0_ref, gs_ref, hs_ref, y_ref, sT_ref, S, *, tc, nb):
    c = pl.program_id(1)
    H = B_HEADS

    @pl.when(c == 0)
    def _():
        S[...] = s0_ref[...]

    def block(sc, carry):
        base = pl.multiple_of(sc * WKV_SUB, WKV_SUB)
        rows = pl.ds(base, WKV_SUB)
        gs = gs_ref[...]
        hsel = hs_ref[...]
        ins = [tuple(x[bi, rows, :] for x in (rp_ref, w_ref, k_ref, a_ref, b_ref)) for bi in range(nb)]
        for u in range(WKV_SUB):
            sas = []
            for bi in range(nb):
                r8, w8, k8, a8, b8 = ins[bi]
                s = S[bi]
                pa = (s * a8[u:u + 1, :]).astype(BF16)
                sas.append(jnp.dot(pa, gs, preferred_element_type=F32))
                pr = (s * r8[u:u + 1, :]).astype(BF16)
                y_ref[bi, base + u] = lax.dot_general(hsel, pr, (((1,), (1,)), ((), ())),
                                                      preferred_element_type=F32)
            deltas = []
            for bi in range(nb):
                r8, w8, k8, a8, b8 = ins[bi]
                v_t = jnp.transpose(vh_ref[bi, base + u])
                lhs = jnp.concatenate([sas[bi][:, :H], v_t], axis=1).astype(BF16)
                rhs = jnp.concatenate([hsel * b8[u:u + 1, :].astype(BF16), hsel * k8[u:u + 1, :].astype(BF16)],
                                      axis=0)
                deltas.append(jnp.dot(lhs, rhs, preferred_element_type=F32))
            for bi in range(nb):
                r8, w8, k8, a8, b8 = ins[bi]
                S[bi] = S[bi] * w8[u:u + 1, :] + deltas[bi]
        return carry

    lax.fori_loop(0, tc // WKV_SUB, block, 0)

    @pl.when(c == pl.num_programs(1) - 1)
    def _():
        sT_ref[...] = S[...]


def _wkv(rp, w, k, v, a, b, s0):
    B, T, D = rp.shape
    n = B_HEAD_DIM
    H = B_HEADS
    nb = _row_tile(B, WKV_NB)
    tc = _row_tile(T, WKV_TC)
    assert tc % WKV_SUB == 0 and H <= LANE
    s0t = jnp.transpose(s0, (0, 2, 1, 3)).reshape(B, n, D)
    hsel = (jnp.arange(H)[:, None] == (jnp.arange(D) // n)[None, :]).astype(BF16)
    gs = (jnp.arange(D)[:, None] // n == jnp.arange(LANE)[None, :]).astype(BF16)
    seq = pl.BlockSpec((nb, tc, D), lambda bi, c: (bi, c, 0))
    seqh = pl.BlockSpec((nb, tc, H, n), lambda bi, c: (bi, c, 0, 0))
    st = pl.BlockSpec((nb, n, D), lambda bi, c: (bi, 0, 0))
    const = lambda a_: pl.BlockSpec(a_.shape, lambda bi, c: (0, 0))
    y, sT = pl.pallas_call(
        functools.partial(_wkv_kernel, tc=tc, nb=nb),
        grid=(B // nb, T // tc),
        in_specs=[seq, seq, seq, seqh, seq, seq, st, const(gs), const(hsel)],
        out_specs=[seqh, st],
        out_shape=[jax.ShapeDtypeStruct((B, T, H, n), F32), jax.ShapeDtypeStruct((B, n, D), F32)],
        scratch_shapes=[pltpu.VMEM((nb, n, D), F32)],
        compiler_params=_cparams(("parallel", "arbitrary")),
        name="wkv",
    )(rp, w, k, v, a, b, s0t, gs, hsel)
    return y.reshape(B, T, D), jnp.transpose(sT.reshape(B, n, H, n), (0, 2, 1, 3))


FFN_HALO = SUBLANE_BF16


FFN_CW = 2 * LANE


def _ffn_body(x_ref, xh_ref, g_ref, wug_ref, wuv_ref, cg_ref, cv_ref, pg_ref, pv_ref, wd_ref, gf_ref,
              o_ref, st_ref, y_ref, hn, ug, uv, act, acc, *, tm, nb):
    i = pl.program_id(1)
    H = FFN_HALO
    cw_ = FFN_CW
    seg = tm + H

    def norm(xf, gain):
        return xf * lax.rsqrt(jnp.mean(xf * xf, axis=-1, keepdims=True) + NORM_EPS) * gain

    for s in range(nb):
        hn[s * seg:s * seg + H, :] = jnp.where(i > 0, norm(xh_ref[s], g_ref[...]), 0.0).astype(BF16)
        hn[s * seg + H:(s + 1) * seg, :] = norm(x_ref[s], g_ref[...]).astype(BF16)
    h = hn[...]
    first = jnp.where(i == 0, 1.0, 0.0)

    def conv(u, r0, taps):
        return (taps[3:4, :] + u[r0 - 2:r0 - 2 + tm, :] * taps[0:1, :] + u[r0 - 1:r0 - 1 + tm, :] * taps[1:2, :]
                + u[r0:r0 + tm, :] * taps[2:3, :])

    def up(c):
        cols = slice(c * cw_, (c + 1) * cw_)
        for half, (u_scr, w_ref, p_ref) in enumerate(((ug, wug_ref, pg_ref), (uv, wuv_ref, pv_ref))):
            u = jnp.dot(h, w_ref[:, cols], preferred_element_type=F32)
            u_scr[c % 2] = u.astype(BF16)
            for s in range(nb):
                u_scr[c % 2, s * seg:s * seg + H, :] = (u[s * seg:s * seg + H, :]
                                                        + p_ref[s, :, cols] * first).astype(BF16)
                st_ref[s, 0, :, c * cw_ + half * D_FF:(c + 1) * cw_ + half * D_FF] = u[(s + 1) * seg - 8:(s + 1) * seg, :]

    n_steps = D_FF // cw_
    up(0)
    for c in range(n_steps):
        cols = slice(c * cw_, (c + 1) * cw_)
        if c + 1 < n_steps:
            up(c + 1)
        for s in range(nb):
            gate = conv(ug.at[c % 2], s * seg + H, cg_ref[:, cols].astype(BF16))
            val = conv(uv.at[c % 2], s * seg + H, cv_ref[:, cols].astype(BF16))
            act[s * tm:(s + 1) * tm, :] = gate * jax.nn.sigmoid(gate) * val
        d = jnp.dot(act[...], wd_ref[cols, :], preferred_element_type=F32)
        if c == 0:
            acc[...] = d
        else:
            acc[...] += d
    for s in range(nb):
        out = x_ref[s] + acc[s * tm:(s + 1) * tm, :]
        o_ref[s] = out
        if gf_ref is not None:
            y_ref[s] = norm(out, gf_ref[...])


def _ffn_kernel(*refs, tm, nb, final):
    if final:
        (x_ref, xh_ref, g_ref, wug_ref, wuv_ref, cg_ref, cv_ref, pg_ref, pv_ref, wd_ref, gf_ref,
         o_ref, st_ref, y_ref, hn, ug, uv, act, acc) = refs
    else:
        (x_ref, xh_ref, g_ref, wug_ref, wuv_ref, cg_ref, cv_ref, pg_ref, pv_ref, wd_ref,
         o_ref, st_ref, hn, ug, uv, act, acc) = refs
        gf_ref = y_ref = None
    _ffn_body(x_ref, xh_ref, g_ref, wug_ref, wuv_ref, cg_ref, cv_ref, pg_ref, pv_ref, wd_ref, gf_ref,
              o_ref, st_ref, y_ref, hn, ug, uv, act, acc, tm=tm, nb=nb)


FFN_ROWS = 512


def _ffn(x, g, wug, wuv, cg, cv, prev, wd, final_g=None):
    B, T, D = x.shape
    F = D_FF
    H = FFN_HALO
    tm = min(FFN_ROWS, T)
    nb = _row_tile(B, max(1, FFN_ROWS // tm))
    assert T % tm == 0 and tm % H == 0 and F % FFN_CW == 0 and tm >= 8
    nh = tm // H
    nt = T // tm
    prev_h = jnp.pad(prev, ((0, 0), (H - prev.shape[1], 0), (0, 0)))
    pg, pv = prev_h[:, :, :F], prev_h[:, :, F:]
    const = lambda a: pl.BlockSpec(a.shape, lambda b, i: (0,) * a.ndim, pipeline_mode=pl.Buffered(1))
    row = pl.BlockSpec((nb, tm, D), lambda b, i: (b, i, 0))
    g2 = g.reshape(1, D)
    ins = [x, x, g2, wug, wuv, cg, cv, pg, pv, wd]
    specs = [row, pl.BlockSpec((nb, H, D), lambda b, i: (b, jnp.maximum(i * nh - 1, 0), 0)),
             const(g2), const(wug), const(wuv), const(cg), const(cv),
             pl.BlockSpec((nb, H, F), lambda b, i: (b, 0, 0)), pl.BlockSpec((nb, H, F), lambda b, i: (b, 0, 0)),
             const(wd)]
    outs = [row, pl.BlockSpec((nb, 1, 8, 2 * F), lambda b, i: (b, i, 0, 0))]
    shapes = [jax.ShapeDtypeStruct((B, T, D), F32), jax.ShapeDtypeStruct((B, nt, 8, 2 * F), F32)]
    if final_g is not None:
        gf = final_g.reshape(1, D)
        ins.append(gf)
        specs.append(const(gf))
        outs.append(row)
        shapes.append(jax.ShapeDtypeStruct((B, T, D), F32))
    res = pl.pallas_call(
        functools.partial(_ffn_kernel, tm=tm, nb=nb, final=final_g is not None),
        grid=(B // nb, nt),
        in_specs=specs,
        out_specs=outs,
        out_shape=shapes,
        scratch_shapes=[pltpu.VMEM((nb * (tm + H), D), BF16), pltpu.VMEM((2, nb * (tm + H), FFN_CW), BF16),
                        pltpu.VMEM((2, nb * (tm + H), FFN_CW), BF16), pltpu.VMEM((nb * tm, FFN_CW), BF16),
                        pltpu.VMEM((nb * tm, D), F32)],
        compiler_params=_cparams(("parallel", "parallel")),
        name="conv_ffn",
    )(*ins)
    state = res[1][:, nt - 1, 8 - (CONV_W - 1):, :]
    return (res[0], state) + tuple(res[2:])


def _pad_cols(w, n):
    return jnp.pad(w, ((0, 0), (0, n - w.shape[1])))


def _pad_keys(a, L):
    return jnp.pad(a, ((0, 0), (0, L - a.shape[1])) + ((0, 0),) * (a.ndim - 2))


def _dsa_layer(x, pos, past_k, past_v, past_ki, g, w_in, w_out):
    B, T, D = x.shape
    tq = min(256, T)
    qh, kf, kt, vf, vt, qi, tail, kib = _dsa_proj(x, g, w_in, _rope_tables(pos, A_HEAD_DIM, A_ROT), tq)
    P = past_k.shape[1]
    L = P + T
    Lp = _round_up(L, ATTN_KB)
    if Lp != T:
        past = lambda a: jnp.transpose(a, (0, 2, 1, 3)).astype(BF16)
        kt = jnp.pad(jnp.concatenate([past(past_k), kt], axis=2), ((0, 0), (0, 0), (0, Lp - L), (0, 0)))
        vt = jnp.pad(jnp.concatenate([past(past_v), vt], axis=2), ((0, 0), (0, 0), (0, Lp - L), (0, 0)))
        kib = _pad_keys(jnp.concatenate([past_ki.astype(BF16), kib], axis=1), Lp)
    x = _attn2(qh, kt, vt, x, w_out.astype(BF16), pos0=P, tq=tq, group=A_HEADS // A_KV_HEADS,
               idx=(qi, tail, kib), top=min(TOPK_MAX, L // 4), wi_off=IDX_DIM)
    kv_rows = lambda a: a.reshape(B, T, A_KV_HEADS, A_HEAD_DIM)
    return x, kv_rows(kf), kv_rows(vf), tail[..., :IDX_DIM]


def _rwkv_layer(x, shift_prev, S0, g, mu, w_rkv, w0, w1, w2, a0, a1, a2, g1, g2, k_k, k_a, r_k, ln_w, ln_b,
                w_out):
    B, T, D = x.shape
    tm = min(256, T)
    bf = lambda a: a.astype(BF16)
    vecs = jnp.stack([w0, a0, k_k, k_a, r_k.reshape(D)], axis=0)
    ws = [bf(w_rkv[0]), bf(w_rkv[1]), bf(w_rkv[2]), bf(w1), bf(w2), bf(a1), bf(a2), bf(g1), bf(g2)]
    rp, decay, k, v, a_vec, b_vec, yc, bonus, gate = _rwkv_pre(x, shift_prev, g, mu, vecs, ws, tm)
    y, S = _wkv(rp, decay, k, v, a_vec, b_vec, S0)
    ln = jnp.pad(jnp.stack([ln_w, ln_b], axis=0), ((0, 6), (0, 0)))
    x_new = _rwkv_post(y, yc, bonus, gate, x, ln, bf(w_out), tm)
    assert T >= 8
    shift = _norm(x[:, T - 8:].reshape(B * 8, D), g).reshape(B, 8, D)[:, -1]
    return x_new, shift, S


def _mla_layer(x, pos, past_lat, past_rope, g, w_in, g_q, g_kv, w_uq, w_ukv, w_out):
    B, T, D = x.shape
    tq = min(256, T)
    weights = _mla_weights(w_in, w_uq, w_ukv)
    tabs = _rope_tables(pos, LANE, C_ROPE, offset=C_NOPE)
    qh, kt, vt, lat, kpe = _mla_proj(x, g, g_q, g_kv, weights, tabs, tq)
    P = past_lat.shape[1]
    L = P + T
    Lp = _round_up(L, ATTN_KB)
    if Lp != T:
        past_slab = jnp.pad(past_rope, ((0, 0), (0, 0), (C_NOPE, LANE - C_NOPE - C_ROPE)))
        kt_p, vt_p = _mla_kv(past_lat, past_slab, weights[2], weights[3], _row_tile(P, 256))
        kt = jnp.pad(jnp.concatenate([kt_p, kt], axis=2), ((0, 0), (0, 0), (0, Lp - L), (0, 0)))
        vt = jnp.pad(jnp.concatenate([vt_p, vt], axis=2), ((0, 0), (0, 0), (0, Lp - L), (0, 0)))
    x = _attn2(qh, kt, vt, x, w_out.astype(BF16), pos0=P, tq=tq, group=1)
    return x, lat, kpe


def _ffn_layer(x, prev, g, w_up, w_conv, b_conv, w_down, final_g=None):
    F = D_FF
    taps = jnp.concatenate([w_conv, b_conv[None, :], jnp.zeros((8 - CONV_W - 1, 2 * F), F32)], axis=0)
    w_up = w_up.astype(BF16)
    return _ffn(x, g, w_up[:, :F], w_up[:, F:], taps[:, :F], taps[:, F:], prev, w_down.astype(BF16), final_g)


def _trunk(x, pos0, st, w):
    B, T, D = x.shape
    pos = pos0 + jnp.arange(T, dtype=jnp.int32)
    new = {name: [] for name in ('a_k', 'a_v', 'a_idx', 'b_wkv', 'b_shift', 'c_lat', 'c_rope', 'ffn')}
    for i in range(DEPTH):
        j = i // N_MIXERS
        kind = i % N_MIXERS
        if kind == 0:
            x, k, v, ki = _dsa_layer(x, pos, st['a_k'][j], st['a_v'][j], st['a_idx'][j], w['n_mix'][i],
                                     w['a_w_in'][j], w['a_w_out'][j])
            new['a_k'].append(k)
            new['a_v'].append(v)
            new['a_idx'].append(ki)
        elif kind == 1:
            x, shift, S = _rwkv_layer(x, st['b_shift'][j], st['b_wkv'][j], w['n_mix'][i], w['b_mu'][j],
                                      w['b_w_rkv'][j], w['b_w0'][j], w['b_w1'][j], w['b_w2'][j], w['b_a0'][j],
                                      w['b_a1'][j], w['b_a2'][j], w['b_g1'][j], w['b_g2'][j], w['b_k_k'][j],
                                      w['b_k_a'][j], w['b_r_k'][j], w['b_ln_w'][j], w['b_ln_b'][j],
                                      w['b_w_out'][j])
            new['b_shift'].append(shift)
            new['b_wkv'].append(S)
        else:
            x, lat, kpe = _mla_layer(x, pos, st['c_lat'][j], st['c_rope'][j], w['n_mix'][i], w['c_w_in'][j],
                                     w['c_g_q'][j], w['c_g_kv'][j], w['c_w_uq'][j], w['c_w_ukv'][j],
                                     w['c_w_out'][j])
            new['c_lat'].append(lat)
            new['c_rope'].append(kpe)
        last = i == DEPTH - 1
        res = _ffn_layer(x, st['ffn'][i], w['n_ffn'][i], w['f_w_up'][i], w['f_w_conv'][i],
                         w['f_b_conv'][i], w['f_w_down'][i], w['n_final'] if last else None)
        x = res[0]
        new['ffn'].append(res[1])
    return res[2], {name: jnp.stack(rows, axis=0) for name, rows in new.items()}


def kernel(x_prompt, x_sample, cache_a_k, cache_a_v, cache_a_idx, state_b_wkv, state_b_shift,
           cache_c_latent, cache_c_rope, state_ffn_conv, n_mix, n_ffn, n_final, a_w_in, a_w_out,
           b_mu, b_w_rkv, b_w0, b_w1, b_w2, b_a0, b_a1, b_a2, b_g1, b_g2, b_k_k, b_k_a, b_r_k,
           b_ln_w, b_ln_b, b_w_out, c_w_in, c_g_q, c_g_kv, c_w_uq, c_w_ukv, c_w_out,
           f_w_up, f_w_conv, f_b_conv, f_w_down):
    w = dict(n_mix=n_mix, n_ffn=n_ffn, n_final=n_final, a_w_in=a_w_in, a_w_out=a_w_out,
             b_mu=b_mu, b_w_rkv=b_w_rkv, b_w0=b_w0, b_w1=b_w1, b_w2=b_w2, b_a0=b_a0, b_a1=b_a1,
             b_a2=b_a2, b_g1=b_g1, b_g2=b_g2, b_k_k=b_k_k, b_k_a=b_k_a, b_r_k=b_r_k,
             b_ln_w=b_ln_w, b_ln_b=b_ln_b, b_w_out=b_w_out, c_w_in=c_w_in, c_g_q=c_g_q,
             c_g_kv=c_g_kv, c_w_uq=c_w_uq, c_w_ukv=c_w_ukv, c_w_out=c_w_out,
             f_w_up=f_w_up, f_w_conv=f_w_conv, f_b_conv=f_b_conv, f_w_down=f_w_down)
    Bp, Tp, D = x_prompt.shape
    n_a, n_b, n_c = cache_a_k.shape[0], state_b_wkv.shape[0], cache_c_latent.shape[0]
    st_prompt = dict(
        a_k=jnp.zeros((n_a, Bp, 0, A_KV_HEADS, A_HEAD_DIM), F32),
        a_v=jnp.zeros((n_a, Bp, 0, A_KV_HEADS, A_HEAD_DIM), F32),
        a_idx=jnp.zeros((n_a, Bp, 0, IDX_DIM), F32),
        b_wkv=jnp.zeros((n_b, Bp, B_HEADS, B_HEAD_DIM, B_HEAD_DIM), F32),
        b_shift=jnp.zeros((n_b, Bp, D), F32),
        c_lat=jnp.zeros((n_c, Bp, 0, C_KV_RANK), F32),
        c_rope=jnp.zeros((n_c, Bp, 0, C_ROPE), F32),
        ffn=jnp.zeros((DEPTH, Bp, CONV_W - 1, 2 * D_FF), F32))
    st_sample = dict(a_k=cache_a_k, a_v=cache_a_v, a_idx=cache_a_idx, b_wkv=state_b_wkv,
                     b_shift=state_b_shift, c_lat=cache_c_latent, c_rope=cache_c_rope,
                     ffn=state_ffn_conv)
    y_prompt, sp = _trunk(x_prompt, 0, st_prompt, w)
    y_sample, ss = _trunk(x_sample, cache_a_k.shape[2], st_sample, w)
    return (y_prompt, y_sample,
            sp['a_k'], ss['a_k'], sp['a_v'], ss['a_v'], sp['a_idx'], ss['a_idx'],
            sp['b_wkv'], ss['b_wkv'], sp['b_shift'], ss['b_shift'],
            sp['c_lat'], ss['c_lat'], sp['c_rope'], ss['c_rope'],
            sp['ffn'], ss['ffn'])
```

```python
import functools

import jax
import jax.numpy as jnp
from jax import lax
from jax.experimental import pallas as pl
from jax.experimental.pallas import tpu as pltpu

F32 = jnp.float32
BF16 = jnp.bfloat16

D_MODEL = 1024
DEPTH = 4
CHUNK = 64
N_MIXERS = 3
NORM_EPS = 1e-6
ROPE_THETA = 500000.0
A_HEADS, A_HEAD_DIM, A_KV_HEADS = 16, 64, 2
A_ROT = A_HEAD_DIM // 4
IDX_HEADS, IDX_DIM = 8, 64
IDX_ROT = IDX_DIM // 4
TOPK_MAX = 256
A_O_Q = A_HEADS * A_HEAD_DIM
A_O_K = A_O_Q + A_KV_HEADS * A_HEAD_DIM
A_O_V = A_O_K + A_KV_HEADS * A_HEAD_DIM
A_O_QI = A_O_V + IDX_HEADS * IDX_DIM
A_O_KI = A_O_QI + IDX_DIM
A_IN = A_O_KI + IDX_HEADS
B_HEAD_DIM = 64
B_HEADS = D_MODEL // B_HEAD_DIM
B_GN_EPS = 64e-5
C_HEADS, C_NOPE, C_ROPE, C_V = 16, 64, 32, 64
C_Q_RANK, C_KV_RANK = 512, 256
D_FF = 2816
CONV_W = 3

LANE = 128
SUBLANE_BF16 = 16
VMEM_LIMIT = 56 * 1024 * 1024
NEG_INF = float("-inf")
LOG2E = 1.4426950408889634


def _round_up(n, m):
    return (n + m - 1) // m * m


def _row_tile(M, pref):
    t = min(pref, M)
    while M % t:
        t //= 2
    return t


def _cparams(sem):
    return pltpu.CompilerParams(dimension_semantics=sem, vmem_limit_bytes=VMEM_LIMIT)


def _norm_kernel(x_ref, g_ref, o_ref):
    xf = x_ref[...]
    o_ref[...] = xf * lax.rsqrt(jnp.mean(xf * xf, axis=-1, keepdims=True) + NORM_EPS) * g_ref[...]


def _norm(x, g, tm=512):
    M, K = x.shape
    tm = _row_tile(M, tm)
    return pl.pallas_call(
        _norm_kernel,
        grid=(M // tm,),
        in_specs=[pl.BlockSpec((tm, K), lambda i: (i, 0)), pl.BlockSpec((1, K), lambda i: (0, 0))],
        out_specs=pl.BlockSpec((tm, K), lambda i: (i, 0)),
        out_shape=jax.ShapeDtypeStruct((M, K), F32),
        compiler_params=_cparams(("parallel",)),
        name="rmsnorm",
    )(x, g.reshape(1, K))


ATTN_KB = 512
ATTN_KB_SHIFT = ATTN_KB.bit_length() - 1
SELECT_ALL = 1e9
MANTISSA_BITS = 23
ATTN_ROWS_PER_ITER = 1024
TOPK_PART_ROWS = 64


def _for_key_blocks(nk, body, init):
    carry = lax.fori_loop(0, nk // 2, lambda i, c: body(2 * i + 1, body(2 * i, c)), init)
    return lax.cond(nk % 2 == 1, lambda c: body(nk - 1, c), lambda c: c, carry)


def _fold(x, op=jnp.add):
    acc = x[:, :LANE]
    for j in range(1, x.shape[1] // LANE):
        acc = op(acc, x[:, j * LANE:(j + 1) * LANE])
    return acc


def _topk_bias_t(tab_ref, qi_ref, wi_ref, ki_ref, SC, MS, *, nk, first, tq, top, wi_off):
    kb = ATTN_KB
    qi = qi_ref[0]
    w_t = jnp.transpose(wi_ref[0])[wi_off:wi_off + IDX_HEADS, :]
    part = TOPK_PART_ROWS
    zeros_p = jnp.zeros((part, tq), F32)
    zero = jnp.zeros((1, tq), F32)
    kf = float(top)
    pos_q = lax.broadcasted_iota(jnp.int32, (1, tq), 1) + first
    limit = (pos_q & ~(CHUNK - 1)) + CHUNK
    key_i = lax.broadcasted_iota(jnp.int32, (kb, tq), 0)

    def key_sum(body):
        def step(j, a):
            return a + jnp.sum(body(j).reshape(kb // part, part, tq), axis=0)

        return jnp.sum(_for_key_blocks(nk, step, zeros_p), axis=0, keepdims=True)

    def ones_where(c):
        return jnp.where(c, 1.0, 0.0)

    def score_block(j):
        kij = ki_ref[0, pl.ds(pl.multiple_of(j * kb, kb), kb), :]
        sc = jnp.zeros((kb, tq), F32)
        for h in range(IDX_HEADS):
            d = lax.dot_general(kij, qi[:, h * IDX_DIM:(h + 1) * IDX_DIM], (((1,), (1,)), ((), ())),
                                preferred_element_type=F32)
            sc = sc + w_t[h:h + 1, :] * jnp.maximum(d, 0.0)
        sc = jnp.where(key_i + j * kb < limit, sc, NEG_INF)
        SC[j] = sc
        return ones_where(sc >= 0.0)

    c0 = key_sum(score_block)
    neg = c0 < kf
    sgn = jnp.where(neg, -1.0, 1.0)
    kp = jnp.where(neg, kf, (nk * kb).astype(F32) - kf + 1.0)

    def flip(j, c):
        SC[j] = SC[j] * sgn
        return c

    _for_key_blocks(nk, flip, 0)

    def count_lt(cand):
        return key_sum(lambda j: ones_where(SC[j] < cand))

    def exp_step(i, carry):
        e_cur, t_cur = carry
        cand = jnp.where(e_cur == 0.0, tab_ref[0, i], t_cur * tab_ref[1, i])
        ok = count_lt(cand) < kp
        return jnp.where(ok, e_cur + tab_ref[2, i], e_cur), jnp.where(ok, cand, t_cur)

    _, t_pow = lax.fori_loop(0, 8, exp_step, (zero, zero))

    def man_step(i, carry):
        t_cur, frac = carry
        cand = t_cur + frac
        ok = count_lt(cand) < kp
        return jnp.where(ok, cand, t_cur), frac * 0.5

    t_cur, _ = lax.fori_loop(0, MANTISSA_BITS, man_step, (t_pow, t_pow * 0.5))
    thr = t_cur * sgn

    def score(j):
        return SC[j] * sgn

    def key_idx(j):
        return (key_i + j * kb).astype(F32)

    need = kf - key_sum(lambda j: ones_where(score(j) > thr))
    n_eq = key_sum(lambda j: ones_where(score(j) == thr))

    def index_cut():
        nbits = (SC.shape[0] * kb - 1).bit_length()

        def bit_step(i, carry):
            c_cur, bit = carry
            cand = c_cur + bit
            ok = key_sum(lambda j: ones_where((score(j) == thr) & (key_idx(j) < cand))) < need
            return jnp.where(ok, cand, c_cur), bit * 0.5

        c_cur, _ = lax.fori_loop(0, nbits, bit_step, (zero, jnp.full((1, tq), 2.0 ** (nbits - 1), F32)))
        return c_cur

    any_split = jnp.max(ones_where(n_eq > need)) > 0.0
    c_cut = lax.cond(any_split, index_cut, lambda: jnp.full((1, tq), SELECT_ALL, F32))

    def write_bias(j, c):
        s = score(j)
        sel = (s > thr) | ((s == thr) & (key_idx(j) <= c_cut))
        MS[j] = jnp.transpose(jnp.where(sel & (key_i + j * kb < limit), 0.0, NEG_INF))
        return c

    _for_key_blocks(nk, write_bias, 0)


def _attn2_kernel(*refs, n_kv, group, tq, pos0, top, indexer, wi_off):
    if indexer:
        (tab_ref, q_ref, k_ref, v_ref, x_ref, wo_ref, qi_ref, wi_ref, ki_ref, xo_ref,
         MS, LG, MACC, LACC, OACC, OH, OALL, SC) = refs
    else:
        q_ref, k_ref, v_ref, x_ref, wo_ref, xo_ref, MS, LG, MACC, LACC, OACC, OH, OALL = refs
    kb = ATTN_KB
    first = pos0 + pl.program_id(1) * tq
    last_limit = ((first + tq - 1) & ~(CHUNK - 1)) + CHUNK
    nk = (last_limit + (kb - 1)) >> ATTN_KB_SHIFT
    row = lax.broadcasted_iota(jnp.int32, (tq, 1), 0) + first
    limit = (row & ~(CHUNK - 1)) + CHUNK
    lane_i = lax.broadcasted_iota(jnp.int32, (tq, kb), 1)

    def causal_bias(j, c):
        MS[j] = jnp.where(lane_i + j * kb < limit, 0.0, NEG_INF)
        return c

    if indexer:
        @pl.when(last_limit > top)
        def _():
            _topk_bias_t(tab_ref, qi_ref, wi_ref, ki_ref, SC, MS, nk=nk, first=first, tq=tq, top=top, wi_off=wi_off)

        @pl.when(last_limit <= top)
        def _():
            _for_key_blocks(nk, causal_bias, 0)
    else:
        _for_key_blocks(nk, causal_bias, 0)

    hu = LG.shape[0]

    def per_kv_heads(gi, c):
        heads = [gi * hu + u for u in range(hu)]
        qs = [q_ref[0, g, 0] for g in heads]
        MACC[...] = jnp.full(MACC.shape, NEG_INF, F32)
        LACC[...] = jnp.zeros(LACC.shape, F32)
        OACC[...] = jnp.zeros(OACC.shape, F32)

        def logits_block(j, c_):
            keys = pl.ds(pl.multiple_of(j * kb, kb), kb)
            bias = MS[j][None]
            for u, g in enumerate(heads):
                lg = lax.dot_general(qs[u], k_ref[0, g, keys, :], (((1,), (1,)), ((), ())),
                                     preferred_element_type=F32)
                lg = (lg.reshape(group, tq, kb) + bias).reshape(group * tq, kb)
                LG[u, j] = lg
                MACC[u] = jnp.maximum(MACC[u], _fold(lg, jnp.maximum))
            return c_

        _for_key_blocks(nk, logits_block, 0)
        ms = [jnp.max(MACC[u], axis=-1, keepdims=True) for u in range(hu)]

        def value_block(j, c_):
            keys = pl.ds(pl.multiple_of(j * kb, kb), kb)
            for u, g in enumerate(heads):
                p = jnp.exp2(LG[u, j] - ms[u])
                LACC[u] += _fold(p)
                OACC[u] += jnp.dot(p.astype(BF16), v_ref[0, g, keys, :], preferred_element_type=F32)
            return c_

        _for_key_blocks(nk, value_block, 0)
        for u, g in enumerate(heads):
            OH[g] = (OACC[u] / jnp.sum(LACC[u], axis=-1, keepdims=True)).astype(OH.dtype)
        return c

    lax.fori_loop(0, n_kv // hu, per_kv_heads, 0)
    dv = OH.shape[2]
    for h in range(n_kv * group):
        OALL[:, h * dv:(h + 1) * dv] = OH[h // group, (h % group) * tq:(h % group + 1) * tq, :]
    xo_ref[0] = x_ref[0] + jnp.dot(OALL[...], wo_ref[...], preferred_element_type=F32)


def _attn2(qg, kt, vt, x, w_out, *, pos0, tq, group, idx=None, top=0, wi_off=0):
    B, n_kv, nq, rows, dq = qg.shape
    L, dv = kt.shape[2], vt.shape[3]
    T, D = x.shape[1], x.shape[2]
    kb = ATTN_KB
    assert nq * tq == T and rows == group * tq and L % kb == 0 and (idx is None or top <= kb)
    ins = [qg, kt, vt, x, w_out]
    specs = [pl.BlockSpec((1, n_kv, 1, rows, dq), lambda b, i: (b, 0, i, 0, 0)),
             pl.BlockSpec((1, n_kv, L, dq), lambda b, i: (b, 0, 0, 0)),
             pl.BlockSpec((1, n_kv, L, dv), lambda b, i: (b, 0, 0, 0)),
             pl.BlockSpec((1, tq, D), lambda b, i: (b, i, 0)),
             pl.BlockSpec(w_out.shape, lambda b, i: (0, 0))]
    hu = max(1, min(n_kv, ATTN_ROWS_PER_ITER // rows))
    assert n_kv % hu == 0
    scratch = [pltpu.VMEM((L // kb, tq, kb), F32), pltpu.VMEM((hu, L // kb, rows, kb), F32),
               pltpu.VMEM((hu, rows, LANE), F32), pltpu.VMEM((hu, rows, LANE), F32),
               pltpu.VMEM((hu, rows, dv), F32), pltpu.VMEM((n_kv, rows, dv), BF16),
               pltpu.VMEM((tq, n_kv * group * dv), BF16)]
    if idx is not None:
        qi, wi, ki = idx
        steps = [2 ** b for b in range(7, -1, -1)]
        tab = jnp.array([[2.0 ** (s - 127) for s in steps], [2.0 ** s if s < 128 else 1.0 for s in steps],
                         [float(s) for s in steps]], F32)
        ins = [tab] + ins + [qi, wi, ki]
        specs = ([pl.BlockSpec(memory_space=pltpu.SMEM)] + specs
                 + [pl.BlockSpec((1, tq, qi.shape[2]), lambda b, i: (b, i, 0)),
                    pl.BlockSpec((1, tq, wi.shape[2]), lambda b, i: (b, i, 0)),
                    pl.BlockSpec((1, L, ki.shape[2]), lambda b, i: (b, 0, 0))])
        scratch.append(pltpu.VMEM((L // kb, kb, tq), F32))
    return pl.pallas_call(
        functools.partial(_attn2_kernel, n_kv=n_kv, group=group, tq=tq, pos0=pos0, top=top,
                          indexer=idx is not None, wi_off=wi_off),
        grid=(B, nq),
        in_specs=specs,
        out_specs=pl.BlockSpec((1, tq, D), lambda b, i: (b, i, 0)),
        out_shape=jax.ShapeDtypeStruct((B, T, D), F32),
        scratch_shapes=scratch,
        compiler_params=_cparams(("parallel", "parallel")),
        name="dsa_attn" if idx is not None else "mla_attn",
    )(*ins)


def _rope_tables(pos, dh, rot, offset=0):
    half = rot // 2
    inv = ROPE_THETA ** (-jnp.arange(half, dtype=F32) / half)
    ang = pos.astype(F32)[:, None] * inv[None, :]
    cos, sin = jnp.cos(ang), jnp.sin(ang)
    T = pos.shape[0]
    pad = lambda n, v: jnp.full((T, n), v, F32)
    zh = pad(half, 0.0)
    lo, hi = offset, dh - offset - rot
    c = jnp.concatenate([pad(lo, 1.0), cos, cos, pad(hi, 1.0)], axis=1)
    s1 = jnp.concatenate([pad(lo, 0.0), -sin, zh, pad(hi, 0.0)], axis=1)
    s2 = jnp.concatenate([pad(lo, 0.0), zh, sin, pad(hi, 0.0)], axis=1)
    return tuple(jnp.tile(t, (1, LANE // dh)) for t in (c, s1, s2))


def _rope_lanes(x, c, s1, s2, half):
    return x * c + pltpu.roll(x, LANE - half, 1) * s1 + pltpu.roll(x, half, 1) * s2


def _dsa_proj_kernel(x_ref, g_ref, w_ref, c_ref, s1_ref, s2_ref, ts_ref, qh_ref, kf_ref, kt_ref, vf_ref, vt_ref,
                     qi_ref, tail_ref, kib_ref, *, tm):
    xf = x_ref[0]
    h = xf * lax.rsqrt(jnp.mean(xf * xf, axis=-1, keepdims=True) + NORM_EPS) * g_ref[...]
    acc = jnp.dot(h.astype(BF16), w_ref[...], preferred_element_type=F32)
    c, s1, s2 = c_ref[...], s1_ref[...], s2_ref[...]
    half = A_ROT // 2
    hd = A_HEAD_DIM
    group = A_HEADS // A_KV_HEADS
    rope = lambda xs: _rope_lanes(xs, c, s1, s2, half)
    slab = lambda off: acc[:, off:off + LANE]
    for s in range(A_O_Q // LANE):
        qs = (rope(slab(s * LANE)) * (hd ** -0.5 * LOG2E)).astype(BF16)
        for e in range(LANE // hd):
            head = s * (LANE // hd) + e
            u = head % group
            qh_ref[0, head // group, 0, u * tm:(u + 1) * tm, :] = qs[:, e * hd:(e + 1) * hd]
    ks = rope(slab(A_O_Q))
    vs = slab(A_O_K)
    kf_ref[0] = ks
    vf_ref[0] = vs
    for e in range(A_KV_HEADS):
        kt_ref[0, e] = ks[:, e * hd:(e + 1) * hd].astype(BF16)
        vt_ref[0, e] = vs[:, e * hd:(e + 1) * hd].astype(BF16)
    for s in range(IDX_HEADS * IDX_DIM // LANE):
        qi_ref[0, :, s * LANE:(s + 1) * LANE] = rope(slab(A_O_V + s * LANE)).astype(BF16)
    is_key = lax.broadcasted_iota(jnp.int32, (tm, LANE), 1) < IDX_DIM
    tl = _rope_lanes(slab(A_O_QI), jnp.where(is_key, c, 1.0), jnp.where(is_key, s1, 0.0),
                     jnp.where(is_key, s2, 0.0), half) * ts_ref[...]
    tail_ref[0] = tl
    kib_ref[0] = tl[:, :IDX_DIM].astype(BF16)


def _dsa_proj(x, g, w_in, tabs, tm):
    B, T, D = x.shape
    assert (A_HEAD_DIM, A_ROT) == (IDX_DIM, IDX_ROT) and A_KV_HEADS * A_HEAD_DIM == LANE
    assert A_O_QI % LANE == 0 and IDX_DIM + IDX_HEADS <= LANE and T % tm == 0
    n_in = _round_up(A_IN, LANE)
    group = A_HEADS // A_KV_HEADS
    hd = A_HEAD_DIM
    nq = T // tm
    lanes = jnp.arange(LANE)
    tail_scale = jnp.where(lanes < IDX_DIM, 1.0, jnp.where(lanes < IDX_DIM + IDX_HEADS,
                                                            (IDX_HEADS * IDX_DIM) ** -0.5, 0.0)).astype(F32)
    row = lambda n: pl.BlockSpec((1, tm, n), lambda b, i: (b, i, 0))
    tab = pl.BlockSpec((tm, LANE), lambda b, i: (i, 0))
    const = lambda a: pl.BlockSpec(a.shape, lambda b, i: (0,) * a.ndim)
    kvh = pl.BlockSpec((1, A_KV_HEADS, tm, hd), lambda b, i: (b, 0, i, 0))
    w = _pad_cols(w_in, n_in).astype(BF16)
    g2 = g.reshape(1, D)
    ts = tail_scale.reshape(1, LANE)
    return pl.pallas_call(
        functools.partial(_dsa_proj_kernel, tm=tm),
        grid=(B, nq),
        in_specs=[row(D), const(g2), const(w), tab, tab, tab, const(ts)],
        out_specs=[pl.BlockSpec((1, A_KV_HEADS, 1, group * tm, hd), lambda b, i: (b, 0, i, 0, 0)),
                   row(LANE), kvh, row(LANE), kvh, row(IDX_HEADS * IDX_DIM), row(LANE), row(IDX_DIM)],
        out_shape=[jax.ShapeDtypeStruct((B, A_KV_HEADS, nq, group * tm, hd), BF16),
                   jax.ShapeDtypeStruct((B, T, LANE), F32), jax.ShapeDtypeStruct((B, A_KV_HEADS, T, hd), BF16),
                   jax.ShapeDtypeStruct((B, T, LANE), F32), jax.ShapeDtypeStruct((B, A_KV_HEADS, T, hd), BF16),
                   jax.ShapeDtypeStruct((B, T, IDX_HEADS * IDX_DIM), BF16),
                   jax.ShapeDtypeStruct((B, T, LANE), F32), jax.ShapeDtypeStruct((B, T, IDX_DIM), BF16)],
        compiler_params=_cparams(("parallel", "parallel")),
        name="dsa_proj",
    )(x, g2, w, *tabs, ts)


def _rms(xf, gain):
    return xf * lax.rsqrt(jnp.mean(xf * xf, axis=-1, keepdims=True) + NORM_EPS) * gain


def _mla_write_kv(lat, kpe_slab, wuk_ref, wuv_ref, kt_ref, vt_ref):
    lb = lat.astype(BF16)
    kn = jnp.dot(lb, wuk_ref[...], preferred_element_type=F32)
    vv = jnp.dot(lb, wuv_ref[...], preferred_element_type=F32)
    for h in range(C_HEADS):
        kt_ref[0, h] = (kn[:, h * LANE:(h + 1) * LANE] + kpe_slab).astype(BF16)
        vt_ref[0, h] = vv[:, h * C_V:(h + 1) * C_V].astype(BF16)


def _mla_proj_kernel(x_ref, g_ref, win_ref, gq_ref, gkv_ref, wuq_ref, wuk_ref, wuv_ref, c_ref, s1_ref, s2_ref,
                     qh_ref, kt_ref, vt_ref, lat_ref, kpe_ref):
    h = _rms(x_ref[0], g_ref[...]).astype(BF16)
    proj = jnp.dot(h, win_ref[...], preferred_element_type=F32)
    c, s1, s2 = c_ref[...], s1_ref[...], s2_ref[...]
    half = C_ROPE // 2
    q = jnp.dot(_rms(proj[:, :C_Q_RANK], gq_ref[...]).astype(BF16), wuq_ref[...], preferred_element_type=F32)
    scale = (C_NOPE + C_ROPE) ** -0.5 * LOG2E
    for hd in range(C_HEADS):
        qh_ref[0, hd, 0] = (_rope_lanes(q[:, hd * LANE:(hd + 1) * LANE], c, s1, s2, half) * scale).astype(BF16)
    lat = _rms(proj[:, C_Q_RANK:C_Q_RANK + C_KV_RANK], gkv_ref[...])
    lat_ref[0] = lat
    kpe_slab = _rope_lanes(proj[:, C_Q_RANK + C_KV_RANK:], c, s1, s2, half)
    kpe_ref[0] = kpe_slab[:, C_NOPE:C_NOPE + C_ROPE]
    _mla_write_kv(lat, kpe_slab, wuk_ref, wuv_ref, kt_ref, vt_ref)


def _mla_kv_kernel(lat_ref, kpe_ref, wuk_ref, wuv_ref, kt_ref, vt_ref):
    _mla_write_kv(lat_ref[0], kpe_ref[0], wuk_ref, wuv_ref, kt_ref, vt_ref)


def _mla_weights(w_in, w_uq, w_ukv):
    D = w_in.shape[0]
    zc = lambda rows, n: jnp.zeros((rows, n), w_in.dtype)
    w_in2 = jnp.concatenate([w_in[:, :C_Q_RANK + C_KV_RANK], zc(D, C_NOPE), w_in[:, C_Q_RANK + C_KV_RANK:],
                             zc(D, LANE - C_NOPE - C_ROPE)], axis=1)
    pad_heads = lambda w, d: jnp.pad(w.reshape(w.shape[0], C_HEADS, d), ((0, 0), (0, 0), (0, LANE - d))).reshape(
        w.shape[0], C_HEADS * LANE)
    w_uq2 = pad_heads(w_uq, C_NOPE + C_ROPE)
    ukv = w_ukv.reshape(C_KV_RANK, C_HEADS, C_NOPE + C_V)
    w_uk2 = pad_heads(ukv[..., :C_NOPE].reshape(C_KV_RANK, C_HEADS * C_NOPE), C_NOPE)
    w_uv2 = ukv[..., C_NOPE:].reshape(C_KV_RANK, C_HEADS * C_V)
    return tuple(a.astype(BF16) for a in (w_in2, w_uq2, w_uk2, w_uv2))


def _mla_proj(x, g, g_q, g_kv, weights, tabs, tm):
    B, T, D = x.shape
    w_in2, w_uq2, w_uk2, w_uv2 = weights
    nq = T // tm
    row = lambda n: pl.BlockSpec((1, tm, n), lambda b, i: (b, i, 0))
    tab = pl.BlockSpec((tm, LANE), lambda b, i: (i, 0))
    const = lambda a: pl.BlockSpec(a.shape, lambda b, i: (0,) * a.ndim)
    heads = lambda d: pl.BlockSpec((1, C_HEADS, tm, d), lambda b, i: (b, 0, i, 0))
    vec = lambda a: a.reshape(1, a.shape[0])
    return pl.pallas_call(
        _mla_proj_kernel,
        grid=(B, nq),
        in_specs=[row(D), const(vec(g)), const(w_in2), const(vec(g_q)), const(vec(g_kv)), const(w_uq2),
                  const(w_uk2), const(w_uv2), tab, tab, tab],
        out_specs=[pl.BlockSpec((1, C_HEADS, 1, tm, LANE), lambda b, i: (b, 0, i, 0, 0)), heads(LANE), heads(C_V),
                   row(C_KV_RANK), row(C_ROPE)],
        out_shape=[jax.ShapeDtypeStruct((B, C_HEADS, nq, tm, LANE), BF16),
                   jax.ShapeDtypeStruct((B, C_HEADS, T, LANE), BF16), jax.ShapeDtypeStruct((B, C_HEADS, T, C_V), BF16),
                   jax.ShapeDtypeStruct((B, T, C_KV_RANK), F32), jax.ShapeDtypeStruct((B, T, C_ROPE), F32)],
        compiler_params=_cparams(("parallel", "parallel")),
        name="mla_proj",
    )(x, vec(g), w_in2, vec(g_q), vec(g_kv), w_uq2, w_uk2, w_uv2, *tabs)


def _mla_kv(lat, kpe_slab, w_uk2, w_uv2, tm):
    B, P, _ = lat.shape
    row = lambda n: pl.BlockSpec((1, tm, n), lambda b, i: (b, i, 0))
    const = lambda a: pl.BlockSpec(a.shape, lambda b, i: (0,) * a.ndim)
    heads = lambda d: pl.BlockSpec((1, C_HEADS, tm, d), lambda b, i: (b, 0, i, 0))
    return pl.pallas_call(
        _mla_kv_kernel,
        grid=(B, P // tm),
        in_specs=[row(C_KV_RANK), row(LANE), const(w_uk2), const(w_uv2)],
        out_specs=[heads(LANE), heads(C_V)],
        out_shape=[jax.ShapeDtypeStruct((B, C_HEADS, P, LANE), BF16), jax.ShapeDtypeStruct((B, C_HEADS, P, C_V), BF16)],
        compiler_params=_cparams(("parallel", "parallel")),
        name="mla_kv",
    )(lat, kpe_slab, w_uk2, w_uv2)


def _head_sum(x, bo_ref, split=True):
    bw = bo_ref.shape[0]
    bo = bo_ref[...]
    hi = x.astype(BF16)
    terms = [hi, (x - hi.astype(F32)).astype(BF16)] if split else [hi]
    return jnp.concatenate(
        [sum(jnp.dot(t[:, j * bw:(j + 1) * bw], bo, preferred_element_type=F32) for t in terms)
         for j in range(x.shape[1] // bw)], axis=-1)


RWKV_HALO = 8


def _rwkv_pre_kernel(x_ref, xh_ref, sh_ref, g_ref, mu_ref, vec_ref, wr_ref, wk_ref, wv_ref, w1_ref, w2_ref,
                     a1_ref, a2_ref, g1_ref, g2_ref, bo_ref,
                     rp_ref, w_ref, k_ref, v_ref, a_ref, b_ref, yc_ref, bonus_ref, gate_ref, *, tm):
    i = pl.program_id(1)
    gain = g_ref[...]

    def norm(xf):
        return xf * lax.rsqrt(jnp.mean(xf * xf, axis=-1, keepdims=True) + NORM_EPS) * gain

    h = norm(x_ref[0])
    before = jnp.where(i > 0, norm(xh_ref[0])[RWKV_HALO - 1:RWKV_HALO, :], sh_ref[0])
    first = lax.broadcasted_iota(jnp.int32, (tm, 1), 0) == 0
    xx = jnp.where(first, before, pltpu.roll(h, 1, 0)) - h
    mu = mu_ref[...]
    vec = vec_ref[...]
    w0, a0, k_k, k_a, r_k = (vec[j:j + 1, :] for j in range(5))

    def mix(j):
        return (h + xx * mu[j:j + 1, :]).astype(BF16)

    dot = lambda a_, w_: jnp.dot(a_, w_[...], preferred_element_type=F32)
    r = dot(mix(0), wr_ref)
    wl = dot(jnp.tanh(dot(mix(1), w1_ref)).astype(BF16), w2_ref)
    k = dot(mix(2), wk_ref)
    v = dot(mix(3), wv_ref)
    al = dot(dot(mix(4), a1_ref).astype(BF16), a2_ref)
    gate_ref[0] = dot(jax.nn.sigmoid(dot(mix(5), g1_ref)).astype(BF16), g2_ref)
    z = -(w0 + wl)
    softplus = jnp.maximum(z, 0.0) + jnp.log(1.0 + jnp.exp(-jnp.abs(z)))
    decay = jnp.exp(-jnp.exp(-softplus - 0.5))
    a = jax.nn.sigmoid(a0 + al)
    kk = k * k_k
    kk = kk / jnp.maximum(jnp.sqrt(_head_sum(kk * kk, bo_ref)), 1e-12)
    k = k * (1.0 + (a - 1.0) * k_a)
    b = kk * a
    rp_ref[0] = decay * r - kk * _head_sum(b * r, bo_ref, split=False)
    w_ref[0] = decay
    k_ref[0] = k
    for hh in range(B_HEADS):
        v_ref[0, :, hh, :] = v[:, hh * B_HEAD_DIM:(hh + 1) * B_HEAD_DIM]
    a_ref[0] = -kk
    b_ref[0] = b
    yc_ref[0] = v * _head_sum(k * r, bo_ref, split=False)
    bonus_ref[0] = _head_sum(r * k * r_k, bo_ref, split=False) * v


def _rwkv_post_kernel(y_ref, yc_ref, bonus_ref, gate_ref, x_ref, ln_ref, wo_ref, bo_ref, o_ref):
    n = float(B_HEAD_DIM)
    y = y_ref[0] + yc_ref[0]
    d = y - _head_sum(y, bo_ref) / n
    var = _head_sum(d * d, bo_ref, split=False) / n
    ln = ln_ref[...]
    yn = d * lax.rsqrt(var + B_GN_EPS) * ln[0:1, :] + ln[1:2, :] + bonus_ref[0]
    o_ref[0] = x_ref[0] + jnp.dot((yn * gate_ref[0]).astype(BF16), wo_ref[...], preferred_element_type=F32)


def _block_ones():
    blk = jnp.arange(WKV_BW) // B_HEAD_DIM
    return (blk[:, None] == blk[None, :]).astype(BF16)


def _rwkv_pre(x, shift_prev, g, mu, vecs, ws, tm):
    B, T, D = x.shape
    H = RWKV_HALO
    assert T % tm == 0 and tm % H == 0
    nh = tm // H
    bo = _block_ones()
    row = pl.BlockSpec((1, tm, D), lambda b, i: (b, i, 0))
    const = lambda a: pl.BlockSpec(a.shape, lambda b, i: (0,) * a.ndim)
    pad8 = lambda a: jnp.pad(a, ((0, 8 - a.shape[0]), (0, 0)))
    mu8, vec8, g2 = pad8(mu), pad8(vecs), g.reshape(1, D)
    return pl.pallas_call(
        functools.partial(_rwkv_pre_kernel, tm=tm),
        grid=(B, T // tm),
        in_specs=[row, pl.BlockSpec((1, H, D), lambda b, i: (b, jnp.maximum(i * nh - 1, 0), 0)),
                  pl.BlockSpec((1, 1, D), lambda b, i: (b, 0, 0)), const(g2), const(mu8), const(vec8)]
                 + [const(a) for a in ws] + [const(bo)],
        out_specs=[row] * 3 + [pl.BlockSpec((1, tm, B_HEADS, B_HEAD_DIM), lambda b, i: (b, i, 0, 0))] + [row] * 5,
        out_shape=[jax.ShapeDtypeStruct((B, T, D), F32)] * 3
                  + [jax.ShapeDtypeStruct((B, T, B_HEADS, B_HEAD_DIM), F32)] + [jax.ShapeDtypeStruct((B, T, D), F32)] * 5,
        compiler_params=_cparams(("parallel", "parallel")),
        name="rwkv_pre",
    )(x, x, shift_prev.reshape(B, 1, D), g2, mu8, vec8, *ws, bo)


def _rwkv_post(y, yc, bonus, gate, x, ln, w_out, tm):
    B, T, D = x.shape
    bo = _block_ones()
    row = pl.BlockSpec((1, tm, D), lambda b, i: (b, i, 0))
    const = lambda a: pl.BlockSpec(a.shape, lambda b, i: (0,) * a.ndim)
    return pl.pallas_call(
        _rwkv_post_kernel,
        grid=(B, T // tm),
        in_specs=[row] * 5 + [const(ln), const(w_out), const(bo)],
        out_specs=row,
        out_shape=jax.ShapeDtypeStruct((B, T, D), F32),
        compiler_params=_cparams(("parallel", "parallel")),
        name="rwkv_post",
    )(y, yc, bonus, gate, x, ln, w_out, bo)


WKV_SUB = 16
WKV_NB = 8
WKV_TC = 64
WKV_BW = 2 * LANE


def _wkv_kernel(rp_ref, w_ref, k_ref, vh_ref, a_ref, b_ref, s0_ref, gs_ref, hs_ref, y_ref, sT_ref, S, *, tc, nb):
    c = pl.program_id(1)
    H = B_HEADS

    @pl.when(c == 0)
    def _():
        S[...] = s0_ref[...]

    def block(sc, carry):
        base = pl.multiple_of(sc * WKV_SUB, WKV_SUB)
        rows = pl.ds(base, WKV_SUB)
        gs = gs_ref[...]
        hsel = hs_ref[...]
        ins = [tuple(x[bi, rows, :] for x in (rp_ref, w_ref, k_ref, a_ref, b_ref)) for bi in range(nb)]
        for u in range(WKV_SUB):
            sas = []
            for bi in range(nb):
                r8, w8, k8, a8, b8 = ins[bi]
                s = S[bi]
                pa = (s * a8[u:u + 1, :]).astype(BF16)
                sas.append(jnp.dot(pa, gs, preferred_element_type=F32))
                pr = (s * r8[u:u + 1, :]).astype(BF16)
                y_ref[bi, base + u] = lax.dot_general(hsel, pr, (((1,), (1,)), ((), ())),
                                                      preferred_element_type=F32)
            deltas = []
            for bi in range(nb):
                r8, w8, k8, a8, b8 = ins[bi]
                v_t = jnp.transpose(vh_ref[bi, base + u])
                lhs = jnp.concatenate([sas[bi][:, :H], v_t], axis=1).astype(BF16)
                rhs = jnp.concatenate([hsel * b8[u:u + 1, :].astype(BF16), hsel * k8[u:u + 1, :].astype(BF16)],
                                      axis=0)
                deltas.append(jnp.dot(lhs, rhs, preferred_element_type=F32))
            for bi in range(nb):
                r8, w8, k8, a8, b8 = ins[bi]
                S[bi] = S[bi] * w8[u:u + 1, :] + deltas[bi]
        return carry

    lax.fori_loop(0, tc // WKV_SUB, block, 0)

    @pl.when(c == pl.num_programs(1) - 1)
    def _():
        sT_ref[...] = S[...]


def _wkv(rp, w, k, v, a, b, s0):
    B, T, D = rp.shape
    n = B_HEAD_DIM
    H = B_HEADS
    nb = _row_tile(B, WKV_NB)
    tc = _row_tile(T, WKV_TC)
    assert tc % WKV_SUB == 0 and H <= LANE
    s0t = jnp.transpose(s0, (0, 2, 1, 3)).reshape(B, n, D)
    hsel = (jnp.arange(H)[:, None] == (jnp.arange(D) // n)[None, :]).astype(BF16)
    gs = (jnp.arange(D)[:, None] // n == jnp.arange(LANE)[None, :]).astype(BF16)
    seq = pl.BlockSpec((nb, tc, D), lambda bi, c: (bi, c, 0))
    seqh = pl.BlockSpec((nb, tc, H, n), lambda bi, c: (bi, c, 0, 0))
    st = pl.BlockSpec((nb, n, D), lambda bi, c: (bi, 0, 0))
    const = lambda a_: pl.BlockSpec(a_.shape, lambda bi, c: (0, 0))
    y, sT = pl.pallas_call(
        functools.partial(_wkv_kernel, tc=tc, nb=nb),
        grid=(B // nb, T // tc),
        in_specs=[seq, seq, seq, seqh, seq, seq, st, const(gs), const(hsel)],
        out_specs=[seqh, st],
        out_shape=[jax.ShapeDtypeStruct((B, T, H, n), F32), jax.ShapeDtypeStruct((B, n, D), F32)],
        scratch_shapes=[pltpu.VMEM((nb, n, D), F32)],
        compiler_params=_cparams(("parallel", "arbitrary")),
        name="wkv",
    )(rp, w, k, v, a, b, s0t, gs, hsel)
    return y.reshape(B, T, D), jnp.transpose(sT.reshape(B, n, H, n), (0, 2, 1, 3))


FFN_HALO = SUBLANE_BF16


FFN_CW = 2 * LANE


def _ffn_body(x_ref, xh_ref, g_ref, wug_ref, wuv_ref, cg_ref, cv_ref, pg_ref, pv_ref, wd_ref, gf_ref,
              o_ref, st_ref, y_ref, hn, ug, uv, act, acc, *, tm, nb):
    i = pl.program_id(1)
    H = FFN_HALO
    cw_ = FFN_CW
    seg = tm + H

    def norm(xf, gain):
        return xf * lax.rsqrt(jnp.mean(xf * xf, axis=-1, keepdims=True) + NORM_EPS) * gain

    for s in range(nb):
        hn[s * seg:s * seg + H, :] = jnp.where(i > 0, norm(xh_ref[s], g_ref[...]), 0.0).astype(BF16)
        hn[s * seg + H:(s + 1) * seg, :] = norm(x_ref[s], g_ref[...]).astype(BF16)
    h = hn[...]
    first = jnp.where(i == 0, 1.0, 0.0)

    def conv(u, r0, taps):
        return (taps[3:4, :] + u[r0 - 2:r0 - 2 + tm, :] * taps[0:1, :] + u[r0 - 1:r0 - 1 + tm, :] * taps[1:2, :]
                + u[r0:r0 + tm, :] * taps[2:3, :])

    def up(c):
        cols = slice(c * cw_, (c + 1) * cw_)
        for half, (u_scr, w_ref, p_ref) in enumerate(((ug, wug_ref, pg_ref), (uv, wuv_ref, pv_ref))):
            u = jnp.dot(h, w_ref[:, cols], preferred_element_type=F32)
            u_scr[c % 2] = u.astype(BF16)
            for s in range(nb):
                u_scr[c % 2, s * seg:s * seg + H, :] = (u[s * seg:s * seg + H, :]
                                                        + p_ref[s, :, cols] * first).astype(BF16)
                st_ref[s, 0, :, c * cw_ + half * D_FF:(c + 1) * cw_ + half * D_FF] = u[(s + 1) * seg - 8:(s + 1) * seg, :]

    n_steps = D_FF // cw_

    def down(c):
        d = jnp.dot(act[c % 2], wd_ref[c * cw_:(c + 1) * cw_, :], preferred_element_type=F32)
        if c == 0:
            acc[...] = d
        else:
            acc[...] += d

    up(0)
    for c in range(n_steps):
        cols = slice(c * cw_, (c + 1) * cw_)
        if c + 1 < n_steps:
            up(c + 1)
        if c >= 1:
            down(c - 1)
        for s in range(nb):
            gate = conv(ug.at[c % 2], s * seg + H, cg_ref[:, cols].astype(BF16))
            val = conv(uv.at[c % 2], s * seg + H, cv_ref[:, cols].astype(BF16))
            act[c % 2, s * tm:(s + 1) * tm, :] = gate * jax.nn.sigmoid(gate) * val
    down(n_steps - 1)
    for s in range(nb):
        out = x_ref[s] + acc[s * tm:(s + 1) * tm, :]
        o_ref[s] = out
        if gf_ref is not None:
            y_ref[s] = norm(out, gf_ref[...])


def _ffn_kernel(*refs, tm, nb, final):
    if final:
        (x_ref, xh_ref, g_ref, wug_ref, wuv_ref, cg_ref, cv_ref, pg_ref, pv_ref, wd_ref, gf_ref,
         o_ref, st_ref, y_ref, hn, ug, uv, act, acc) = refs
    else:
        (x_ref, xh_ref, g_ref, wug_ref, wuv_ref, cg_ref, cv_ref, pg_ref, pv_ref, wd_ref,
         o_ref, st_ref, hn, ug, uv, act, acc) = refs
        gf_ref = y_ref = None
    _ffn_body(x_ref, xh_ref, g_ref, wug_ref, wuv_ref, cg_ref, cv_ref, pg_ref, pv_ref, wd_ref, gf_ref,
              o_ref, st_ref, y_ref, hn, ug, uv, act, acc, tm=tm, nb=nb)


FFN_ROWS = 512


def _ffn(x, g, wug, wuv, cg, cv, prev, wd, final_g=None):
    B, T, D = x.shape
    F = D_FF
    H = FFN_HALO
    tm = min(FFN_ROWS, T)
    nb = _row_tile(B, max(1, FFN_ROWS // tm))
    assert T % tm == 0 and tm % H == 0 and F % FFN_CW == 0 and tm >= 8
    nh = tm // H
    nt = T // tm
    prev_h = jnp.pad(prev, ((0, 0), (H - prev.shape[1], 0), (0, 0)))
    pg, pv = prev_h[:, :, :F], prev_h[:, :, F:]
    const = lambda a: pl.BlockSpec(a.shape, lambda b, i: (0,) * a.ndim, pipeline_mode=pl.Buffered(1))
    row = pl.BlockSpec((nb, tm, D), lambda b, i: (b, i, 0))
    g2 = g.reshape(1, D)
    ins = [x, x, g2, wug, wuv, cg, cv, pg, pv, wd]
    specs = [row, pl.BlockSpec((nb, H, D), lambda b, i: (b, jnp.maximum(i * nh - 1, 0), 0)),
             const(g2), const(wug), const(wuv), const(cg), const(cv),
             pl.BlockSpec((nb, H, F), lambda b, i: (b, 0, 0)), pl.BlockSpec((nb, H, F), lambda b, i: (b, 0, 0)),
             const(wd)]
    outs = [row, pl.BlockSpec((nb, 1, 8, 2 * F), lambda b, i: (b, i, 0, 0))]
    shapes = [jax.ShapeDtypeStruct((B, T, D), F32), jax.ShapeDtypeStruct((B, nt, 8, 2 * F), F32)]
    if final_g is not None:
        gf = final_g.reshape(1, D)
        ins.append(gf)
        specs.append(const(gf))
        outs.append(row)
        shapes.append(jax.ShapeDtypeStruct((B, T, D), F32))
    res = pl.pallas_call(
        functools.partial(_ffn_kernel, tm=tm, nb=nb, final=final_g is not None),
        grid=(B // nb, nt),
        in_specs=specs,
        out_specs=outs,
        out_shape=shapes,
        scratch_shapes=[pltpu.VMEM((nb * (tm + H), D), BF16), pltpu.VMEM((2, nb * (tm + H), FFN_CW), BF16),
                        pltpu.VMEM((2, nb * (tm + H), FFN_CW), BF16), pltpu.VMEM((2, nb * tm, FFN_CW), BF16),
                        pltpu.VMEM((nb * tm, D), F32)],
        compiler_params=_cparams(("parallel", "parallel")),
        name="conv_ffn",
    )(*ins)
    state = res[1][:, nt - 1, 8 - (CONV_W - 1):, :]
    return (res[0], state) + tuple(res[2:])


def _pad_cols(w, n):
    return jnp.pad(w, ((0, 0), (0, n - w.shape[1])))


def _pad_keys(a, L):
    return jnp.pad(a, ((0, 0), (0, L - a.shape[1])) + ((0, 0),) * (a.ndim - 2))


def _dsa_layer(x, pos, past_k, past_v, past_ki, g, w_in, w_out):
    B, T, D = x.shape
    tq = min(256, T)
    qh, kf, kt, vf, vt, qi, tail, kib = _dsa_proj(x, g, w_in, _rope_tables(pos, A_HEAD_DIM, A_ROT), tq)
    P = past_k.shape[1]
    L = P + T
    Lp = _round_up(L, ATTN_KB)
    if Lp != T:
        past = lambda a: jnp.transpose(a, (0, 2, 1, 3)).astype(BF16)
        kt = jnp.pad(jnp.concatenate([past(past_k), kt], axis=2), ((0, 0), (0, 0), (0, Lp - L), (0, 0)))
        vt = jnp.pad(jnp.concatenate([past(past_v), vt], axis=2), ((0, 0), (0, 0), (0, Lp - L), (0, 0)))
        kib = _pad_keys(jnp.concatenate([past_ki.astype(BF16), kib], axis=1), Lp)
    x = _attn2(qh, kt, vt, x, w_out.astype(BF16), pos0=P, tq=tq, group=A_HEADS // A_KV_HEADS,
               idx=(qi, tail, kib), top=min(TOPK_MAX, L // 4), wi_off=IDX_DIM)
    kv_rows = lambda a: a.reshape(B, T, A_KV_HEADS, A_HEAD_DIM)
    return x, kv_rows(kf), kv_rows(vf), tail[..., :IDX_DIM]


def _rwkv_layer(x, shift_prev, S0, g, mu, w_rkv, w0, w1, w2, a0, a1, a2, g1, g2, k_k, k_a, r_k, ln_w, ln_b,
                w_out):
    B, T, D = x.shape
    tm = min(256, T)
    bf = lambda a: a.astype(BF16)
    vecs = jnp.stack([w0, a0, k_k, k_a, r_k.reshape(D)], axis=0)
    ws = [bf(w_rkv[0]), bf(w_rkv[1]), bf(w_rkv[2]), bf(w1), bf(w2), bf(a1), bf(a2), bf(g1), bf(g2)]
    rp, decay, k, v, a_vec, b_vec, yc, bonus, gate = _rwkv_pre(x, shift_prev, g, mu, vecs, ws, tm)
    y, S = _wkv(rp, decay, k, v, a_vec, b_vec, S0)
    ln = jnp.pad(jnp.stack([ln_w, ln_b], axis=0), ((0, 6), (0, 0)))
    x_new = _rwkv_post(y, yc, bonus, gate, x, ln, bf(w_out), tm)
    assert T >= 8
    shift = _norm(x[:, T - 8:].reshape(B * 8, D), g).reshape(B, 8, D)[:, -1]
    return x_new, shift, S


def _mla_layer(x, pos, past_lat, past_rope, g, w_in, g_q, g_kv, w_uq, w_ukv, w_out):
    B, T, D = x.shape
    tq = min(256, T)
    weights = _mla_weights(w_in, w_uq, w_ukv)
    tabs = _rope_tables(pos, LANE, C_ROPE, offset=C_NOPE)
    qh, kt, vt, lat, kpe = _mla_proj(x, g, g_q, g_kv, weights, tabs, tq)
    P = past_lat.shape[1]
    L = P + T
    Lp = _round_up(L, ATTN_KB)
    if Lp != T:
        past_slab = jnp.pad(past_rope, ((0, 0), (0, 0), (C_NOPE, LANE - C_NOPE - C_ROPE)))
        kt_p, vt_p = _mla_kv(past_lat, past_slab, weights[2], weights[3], _row_tile(P, 256))
        kt = jnp.pad(jnp.concatenate([kt_p, kt], axis=2), ((0, 0), (0, 0), (0, Lp - L), (0, 0)))
        vt = jnp.pad(jnp.concatenate([vt_p, vt], axis=2), ((0, 0), (0, 0), (0, Lp - L), (0, 0)))
    x = _attn2(qh, kt, vt, x, w_out.astype(BF16), pos0=P, tq=tq, group=1)
    return x, lat, kpe


def _ffn_layer(x, prev, g, w_up, w_conv, b_conv, w_down, final_g=None):
    F = D_FF
    taps = jnp.concatenate([w_conv, b_conv[None, :], jnp.zeros((8 - CONV_W - 1, 2 * F), F32)], axis=0)
    w_up = w_up.astype(BF16)
    return _ffn(x, g, w_up[:, :F], w_up[:, F:], taps[:, :F], taps[:, F:], prev, w_down.astype(BF16), final_g)


def _trunk(x, pos0, st, w):
    B, T, D = x.shape
    pos = pos0 + jnp.arange(T, dtype=jnp.int32)
    new = {name: [] for name in ('a_k', 'a_v', 'a_idx', 'b_wkv', 'b_shift', 'c_lat', 'c_rope', 'ffn')}
    for i in range(DEPTH):
        j = i // N_MIXERS
        kind = i % N_MIXERS
        if kind == 0:
            x, k, v, ki = _dsa_layer(x, pos, st['a_k'][j], st['a_v'][j], st['a_idx'][j], w['n_mix'][i],
                                     w['a_w_in'][j], w['a_w_out'][j])
            new['a_k'].append(k)
            new['a_v'].append(v)
            new['a_idx'].append(ki)
        elif kind == 1:
            x, shift, S = _rwkv_layer(x, st['b_shift'][j], st['b_wkv'][j], w['n_mix'][i], w['b_mu'][j],
                                      w['b_w_rkv'][j], w['b_w0'][j], w['b_w1'][j], w['b_w2'][j], w['b_a0'][j],
                                      w['b_a1'][j], w['b_a2'][j], w['b_g1'][j], w['b_g2'][j], w['b_k_k'][j],
                                      w['b_k_a'][j], w['b_r_k'][j], w['b_ln_w'][j], w['b_ln_b'][j],
                                      w['b_w_out'][j])
            new['b_shift'].append(shift)
            new['b_wkv'].append(S)
        else:
            x, lat, kpe = _mla_layer(x, pos, st['c_lat'][j], st['c_rope'][j], w['n_mix'][i], w['c_w_in'][j],
                                     w['c_g_q'][j], w['c_g_kv'][j], w['c_w_uq'][j], w['c_w_ukv'][j],
                                     w['c_w_out'][j])
            new['c_lat'].append(lat)
            new['c_rope'].append(kpe)
        last = i == DEPTH - 1
        res = _ffn_layer(x, st['ffn'][i], w['n_ffn'][i], w['f_w_up'][i], w['f_w_conv'][i],
                         w['f_b_conv'][i], w['f_w_down'][i], w['n_final'] if last else None)
        x = res[0]
        new['ffn'].append(res[1])
    return res[2], {name: jnp.stack(rows, axis=0) for name, rows in new.items()}


def kernel(x_prompt, x_sample, cache_a_k, cache_a_v, cache_a_idx, state_b_wkv, state_b_shift,
           cache_c_latent, cache_c_rope, state_ffn_conv, n_mix, n_ffn, n_final, a_w_in, a_w_out,
           b_mu, b_w_rkv, b_w0, b_w1, b_w2, b_a0, b_a1, b_a2, b_g1, b_g2, b_k_k, b_k_a, b_r_k,
           b_ln_w, b_ln_b, b_w_out, c_w_in, c_g_q, c_g_kv, c_w_uq, c_w_ukv, c_w_out,
           f_w_up, f_w_conv, f_b_conv, f_w_down):
    w = dict(n_mix=n_mix, n_ffn=n_ffn, n_final=n_final, a_w_in=a_w_in, a_w_out=a_w_out,
             b_mu=b_mu, b_w_rkv=b_w_rkv, b_w0=b_w0, b_w1=b_w1, b_w2=b_w2, b_a0=b_a0, b_a1=b_a1,
             b_a2=b_a2, b_g1=b_g1, b_g2=b_g2, b_k_k=b_k_k, b_k_a=b_k_a, b_r_k=b_r_k,
             b_ln_w=b_ln_w, b_ln_b=b_ln_b, b_w_out=b_w_out, c_w_in=c_w_in, c_g_q=c_g_q,
             c_g_kv=c_g_kv, c_w_uq=c_w_uq, c_w_ukv=c_w_ukv, c_w_out=c_w_out,
             f_w_up=f_w_up, f_w_conv=f_w_conv, f_b_conv=f_b_conv, f_w_down=f_w_down)
    Bp, Tp, D = x_prompt.shape
    n_a, n_b, n_c = cache_a_k.shape[0], state_b_wkv.shape[0], cache_c_latent.shape[0]
    st_prompt = dict(
        a_k=jnp.zeros((n_a, Bp, 0, A_KV_HEADS, A_HEAD_DIM), F32),
        a_v=jnp.zeros((n_a, Bp, 0, A_KV_HEADS, A_HEAD_DIM), F32),
        a_idx=jnp.zeros((n_a, Bp, 0, IDX_DIM), F32),
        b_wkv=jnp.zeros((n_b, Bp, B_HEADS, B_HEAD_DIM, B_HEAD_DIM), F32),
        b_shift=jnp.zeros((n_b, Bp, D), F32),
        c_lat=jnp.zeros((n_c, Bp, 0, C_KV_RANK), F32),
        c_rope=jnp.zeros((n_c, Bp, 0, C_ROPE), F32),
        ffn=jnp.zeros((DEPTH, Bp, CONV_W - 1, 2 * D_FF), F32))
    st_sample = dict(a_k=cache_a_k, a_v=cache_a_v, a_idx=cache_a_idx, b_wkv=state_b_wkv,
                     b_shift=state_b_shift, c_lat=cache_c_latent, c_rope=cache_c_rope,
                     ffn=state_ffn_conv)
    y_prompt, sp = _trunk(x_prompt, 0, st_prompt, w)
    y_sample, ss = _trunk(x_sample, cache_a_k.shape[2], st_sample, w)
    return (y_prompt, y_sample,
            sp['a_k'], ss['a_k'], sp['a_v'], ss['a_v'], sp['a_idx'], ss['a_idx'],
            sp['b_wkv'], ss['b_wkv'], sp['b_shift'], ss['b_shift'],
            sp['c_lat'], ss['c_lat'], sp['c_rope'], ss['c_rope'],
            sp['ffn'], ss['ffn'])
```
